```python
import math
import jax, jax.numpy as jnp
from jax import lax
import numpy as np

D_MODEL = 1024
BATCH = 16
SEQ = 2048
DEPTH = 1

D_RNN = 1344
LRU_BLOCKS = 4
LRU_BLOCK_W = D_RNN // LRU_BLOCKS
CONV_W = 4
LRU_C = 8.0
N_HEADS = 16
HEAD_DIM = 64
N_KV_GROUPS = 4
HEADS_PER_GROUP = N_HEADS // N_KV_GROUPS
CMP_BLOCK = 32
CMP_STRIDE = 16
SEL_BLOCK = 64
N_SELECT = 16
WINDOW = 512
PHI_HIDDEN = 256
Q_BLOCK = 32
SEL_FORCED = 1e4
NEG_INF = -1e30
REL_BUCKETS = 32
REL_MAX_DIST = 128
D_FF = 4 * D_MODEL
NORM_EPS = 1e-6

Q_W = N_HEADS * HEAD_DIM
KV_W = N_KV_GROUPS * HEAD_DIM
SPLIT_SIZES = (D_RNN, D_RNN, Q_W, KV_W, KV_W, KV_W, KV_W, KV_W, KV_W, 3 * N_HEADS, D_MODEL, D_MODEL)
D_IN = 2 * D_RNN + Q_W + 6 * KV_W + 3 * N_HEADS + 2 * D_MODEL

kernel_name = "hybrid_rglru_nsa_sqrelu"


def rmsnorm(x, g):
    xf = x.astype(jnp.float32)
    y = xf * lax.rsqrt(jnp.mean(xf * xf, axis=-1, keepdims=True) + NORM_EPS)
    return y.astype(x.dtype) * g


def t5_bucket(dist):
    max_exact = REL_BUCKETS // 2
    d = jnp.maximum(dist, 0)
    df = jnp.maximum(d.astype(jnp.float32), 1.0)
    large = max_exact + (jnp.log(df / max_exact) / math.log(REL_MAX_DIST / max_exact)
                         * (REL_BUCKETS - max_exact)).astype(jnp.int32)
    large = jnp.minimum(large, REL_BUCKETS - 1)
    return jnp.where(d < max_exact, d, large)


def masked_softmax(s, mask):
    p = jax.nn.softmax(jnp.where(mask, s.astype(jnp.float32), NEG_INF), axis=-1)
    return jnp.where(mask, p, 0.0)


def causal_depthwise_conv(x, w, b):
    y = lax.conv_general_dilated(x, w[:, None, :], window_strides=(1,), padding=[(CONV_W - 1, 0)],
                                 dimension_numbers=("NWC", "WIO", "NWC"),
                                 feature_group_count=x.shape[-1])
    return y + b


def block_diag_linear(x, w, b):
    xb = x.reshape(x.shape[0], x.shape[1], LRU_BLOCKS, LRU_BLOCK_W)
    y = jnp.einsum("btni,nij->btnj", xb, w) + b
    return y.reshape(x.shape)


def rg_lru(x, w_a, b_a, w_x, b_x, lam):
    r = jax.nn.sigmoid(block_diag_linear(x, w_a, b_a)).astype(jnp.float32)
    i = jax.nn.sigmoid(block_diag_linear(x, w_x, b_x))
    log_a = -LRU_C * r * jax.nn.softplus(-lam.astype(jnp.float32))
    a = jnp.exp(log_a)
    mult = jnp.sqrt(-jnp.expm1(2.0 * log_a))
    first = (jnp.arange(x.shape[1]) == 0)[None, :, None]
    mult = jnp.where(first, 1.0, mult)
    u = mult * (i * x).astype(jnp.float32)

    def combine(c1, c2):
        a1, b1 = c1
        a2, b2 = c2
        return a1 * a2, a2 * b1 + b2

    _, h = lax.associative_scan(combine, (a, u), axis=1)
    return h.astype(x.dtype)


def compress_blocks(z, pe, w1, w2):
    b, t = z.shape[0], z.shape[1]
    n_c = (t - CMP_BLOCK) // CMP_STRIDE + 1
    idx = jnp.arange(n_c)[:, None] * CMP_STRIDE + jnp.arange(CMP_BLOCK)[None, :]
    blk = z[:, idx] + pe[None, None, :, None, :]
    flat = blk.transpose(0, 3, 1, 2, 4).reshape(b, N_KV_GROUPS, n_c, CMP_BLOCK * HEAD_DIM)
    return jax.nn.gelu(flat @ w1, approximate=True) @ w2


def nsa_mixer(q, k_c, v_c, k_s, v_s, k_w, v_w, g_nsa, rel_bias,
              phi_k_pe, phi_k_w1, phi_k_w2, phi_v_pe, phi_v_w1, phi_v_w2,
              q_norm, kc_norm, ks_norm, kw_norm):
    B, T, _ = q.shape
    G, R, hd = N_KV_GROUPS, HEADS_PER_GROUP, HEAD_DIM
    qh = rmsnorm(q.reshape(B, T, N_HEADS, hd), q_norm) * (hd ** -0.5)
    qh = qh.reshape(B, T, G, R, hd).transpose(0, 2, 3, 1, 4)
    kv = lambda z: z.reshape(B, T, G, hd)
    kc = rmsnorm(compress_blocks(kv(k_c), phi_k_pe, phi_k_w1, phi_k_w2), kc_norm)
    vc = compress_blocks(kv(v_c), phi_v_pe, phi_v_w1, phi_v_w2)
    n_c = kc.shape[2]
    n_sblk = T // SEL_BLOCK
    ks = rmsnorm(kv(k_s), ks_norm).transpose(0, 2, 1, 3).reshape(B, G, n_sblk, SEL_BLOCK, hd)
    vs = kv(v_s).transpose(0, 2, 1, 3).reshape(B, G, n_sblk, SEL_BLOCK, hd)
    pad = ((0, 0), (0, 0), (WINDOW, 0), (0, 0))
    kw = jnp.pad(rmsnorm(kv(k_w), kw_norm).transpose(0, 2, 1, 3), pad)
    vw = jnp.pad(kv(v_w).transpose(0, 2, 1, 3), pad)
    gates = jax.nn.sigmoid(g_nsa.reshape(B, T, G, R, 3).transpose(0, 2, 3, 1, 4))
    bias_gr = rel_bias.T.reshape(G, R, REL_BUCKETS)

    cstart = jnp.arange(n_c) * CMP_STRIDE
    cend = cstart + CMP_BLOCK - 1
    sj = jnp.arange(n_sblk)
    cover = ((cstart[:, None] < (sj[None, :] + 1) * SEL_BLOCK)
             & (cend[:, None] >= sj[None, :] * SEL_BLOCK)).astype(jnp.float32)
    n_top = min(N_SELECT, n_sblk)
    L = n_top * SEL_BLOCK
    bi = jnp.arange(B)[:, None, None, None]
    gi = jnp.arange(G)[None, :, None, None]
    gi5 = jnp.arange(G)[None, :, None, None, None]
    ri5 = jnp.arange(R)[None, None, :, None, None]

    def attend_block(blk):
        t0 = blk * Q_BLOCK
        tq = t0 + jnp.arange(Q_BLOCK)
        qb = lax.dynamic_slice_in_dim(qh, t0, Q_BLOCK, axis=3)
        gb = lax.dynamic_slice_in_dim(gates, t0, Q_BLOCK, axis=3)
        s_c = (jnp.einsum("bgrqd,bgcd->bgrqc", qb, kc).astype(jnp.float32)
               + bias_gr[:, :, t5_bucket(tq[:, None] - cend[None, :])])
        p_c = masked_softmax(s_c, cend[None, :] <= tq[:, None])
        o_c = jnp.einsum("bgrqc,bgcd->bgrqd", p_c.astype(vc.dtype), vc)
        imp = jnp.einsum("bgrqc,cj->bgqj", p_c, cover)
        qblk = tq // SEL_BLOCK
        causal_j = sj[None, :] <= qblk[:, None]
        forced = causal_j & ((sj[None, :] == 0) | (sj[None, :] >= qblk[:, None] - 1))
        score = jnp.where(forced, SEL_FORCED, jnp.where(causal_j, imp, -1.0))
        top_v, top_i = lax.top_k(score, n_top)
        k_sel = ks[bi, gi, top_i].reshape(B, G, Q_BLOCK, L, hd)
        v_sel = vs[bi, gi, top_i].reshape(B, G, Q_BLOCK, L, hd)
        pos = top_i[..., None] * SEL_BLOCK + jnp.arange(SEL_BLOCK)
        mask_s = ((top_v >= 0.0)[..., None] & (pos <= tq[:, None, None])).reshape(B, G, Q_BLOCK, L)
        dist_s = tq[:, None] - pos.reshape(B, G, Q_BLOCK, L)
        s_s = (jnp.einsum("bgrqd,bgqld->bgrql", qb, k_sel).astype(jnp.float32)
               + bias_gr[gi5, ri5, t5_bucket(dist_s)[:, :, None]])
        p_s = masked_softmax(s_s, mask_s[:, :, None])
        o_s = jnp.einsum("bgrql,bgqld->bgrqd", p_s.astype(v_sel.dtype), v_sel)
        kwb = lax.dynamic_slice_in_dim(kw, t0, WINDOW + Q_BLOCK, axis=2)
        vwb = lax.dynamic_slice_in_dim(vw, t0, WINDOW + Q_BLOCK, axis=2)
        sk = t0 - WINDOW + jnp.arange(WINDOW + Q_BLOCK)
        dist_w = tq[:, None] - sk[None, :]
        mask_w = (dist_w >= 0) & (dist_w < WINDOW) & (sk[None, :] >= 0)
        s_w = (jnp.einsum("bgrqd,bgsd->bgrqs", qb, kwb).astype(jnp.float32)
               + bias_gr[:, :, t5_bucket(dist_w)])
        p_w = masked_softmax(s_w, mask_w)
        o_w = jnp.einsum("bgrqs,bgsd->bgrqd", p_w.astype(vwb.dtype), vwb)
        return gb[..., 0:1] * o_c + gb[..., 1:2] * o_s + gb[..., 2:3] * o_w

    o = lax.map(attend_block, jnp.arange(T // Q_BLOCK))
    return o.transpose(1, 0, 4, 2, 3, 5).reshape(B, T, Q_W)


def hybrid_layer(x, rel_bias, norm_mix, w_in, conv_w, conv_b, gate_a_w, gate_a_b, gate_x_w, gate_x_b,
                 lru_lambda, phi_k_pe, phi_k_w1, phi_k_w2, phi_v_pe, phi_v_w1, phi_v_w2,
                 q_norm, kc_norm, ks_norm, kw_norm, proj_a, proj_b, w_out,
                 norm_mlp, w_mlp_in, w_mlp_out):
    xn = rmsnorm(x, norm_mix)
    cuts = [int(c) for c in np.cumsum(SPLIT_SIZES)[:-1]]
    (u_rnn, u_gate, q, k_c, v_c, k_s, v_s, k_w, v_w, g_nsa, g_a, g_b) = jnp.split(xn @ w_in, cuts, axis=-1)
    h_a = rg_lru(causal_depthwise_conv(u_rnn, conv_w, conv_b), gate_a_w, gate_a_b, gate_x_w, gate_x_b, lru_lambda)
    y_a = h_a * jax.nn.gelu(u_gate, approximate=True)
    y_b = nsa_mixer(q, k_c, v_c, k_s, v_s, k_w, v_w, g_nsa, rel_bias,
                    phi_k_pe, phi_k_w1, phi_k_w2, phi_v_pe, phi_v_w1, phi_v_w2,
                    q_norm, kc_norm, ks_norm, kw_norm)
    merged = jax.nn.sigmoid(g_a) * (y_a @ proj_a) + jax.nn.sigmoid(g_b) * (y_b @ proj_b)
    h = x + merged @ w_out
    z = rmsnorm(h, norm_mlp) @ w_mlp_in
    return h + jnp.square(jax.nn.relu(z)) @ w_mlp_out


def setup_inputs(seed: int = 0) -> dict:
    key = jax.random.key(seed)
    ks = jax.random.split(key, 32)
    nrm = lambda k, shape, scale: jax.random.normal(k, shape, jnp.float32) * scale
    gain = lambda k, shape: 1.0 + 0.05 * jax.random.normal(k, shape, jnp.float32)
    a0 = jax.random.uniform(ks[9], (DEPTH, D_RNN), jnp.float32, minval=0.9, maxval=0.999)
    s = a0 ** (1.0 / LRU_C)
    lam = jnp.log(s) - jnp.log1p(-s)
    cw = CMP_BLOCK * HEAD_DIM
    return {
        "x": nrm(ks[0], (BATCH, SEQ, D_MODEL), 1.0),
        "norm_mix": gain(ks[1], (DEPTH, D_MODEL)),
        "w_in": nrm(ks[2], (DEPTH, D_MODEL, D_IN), D_MODEL ** -0.5),
        "conv_w": nrm(ks[3], (DEPTH, CONV_W, D_RNN), CONV_W ** -0.5),
        "conv_b": nrm(ks[4], (DEPTH, D_RNN), 0.1),
        "gate_a_w": nrm(ks[5], (DEPTH, LRU_BLOCKS, LRU_BLOCK_W, LRU_BLOCK_W), LRU_BLOCK_W ** -0.5),
        "gate_a_b": nrm(ks[6], (DEPTH, LRU_BLOCKS, LRU_BLOCK_W), 0.1),
        "gate_x_w": nrm(ks[7], (DEPTH, LRU_BLOCKS, LRU_BLOCK_W, LRU_BLOCK_W), LRU_BLOCK_W ** -0.5),
        "gate_x_b": nrm(ks[8], (DEPTH, LRU_BLOCKS, LRU_BLOCK_W), 0.1),
        "lru_lambda": lam,
        "phi_k_pe": nrm(ks[10], (DEPTH, CMP_BLOCK, HEAD_DIM), 0.1),
        "phi_k_w1": nrm(ks[11], (DEPTH, cw, PHI_HIDDEN), cw ** -0.5),
        "phi_k_w2": nrm(ks[12], (DEPTH, PHI_HIDDEN, HEAD_DIM), PHI_HIDDEN ** -0.5),
        "phi_v_pe": nrm(ks[13], (DEPTH, CMP_BLOCK, HEAD_DIM), 0.1),
        "phi_v_w1": nrm(ks[14], (DEPTH, cw, PHI_HIDDEN), cw ** -0.5),
        "phi_v_w2": nrm(ks[15], (DEPTH, PHI_HIDDEN, HEAD_DIM), PHI_HIDDEN ** -0.5),
        "q_norm": gain(ks[16], (DEPTH, HEAD_DIM)),
        "kc_norm": gain(ks[17], (DEPTH, HEAD_DIM)),
        "ks_norm": gain(ks[18], (DEPTH, HEAD_DIM)),
        "kw_norm": gain(ks[19], (DEPTH, HEAD_DIM)),
        "rel_bias": nrm(ks[20], (REL_BUCKETS, N_HEADS), 0.5),
        "proj_a": nrm(ks[21], (DEPTH, D_RNN, D_MODEL), D_RNN ** -0.5),
        "proj_b": nrm(ks[22], (DEPTH, Q_W, D_MODEL), Q_W ** -0.5),
        "w_out": nrm(ks[23], (DEPTH, D_MODEL, D_MODEL), D_MODEL ** -0.5),
        "norm_mlp": gain(ks[24], (DEPTH, D_MODEL)),
        "w_mlp_in": nrm(ks[25], (DEPTH, D_MODEL, D_FF), D_MODEL ** -0.5),
        "w_mlp_out": nrm(ks[26], (DEPTH, D_FF, D_MODEL), D_FF ** -0.5),
    }


def reference(x, norm_mix, w_in, conv_w, conv_b, gate_a_w, gate_a_b, gate_x_w, gate_x_b, lru_lambda,
              phi_k_pe, phi_k_w1, phi_k_w2, phi_v_pe, phi_v_w1, phi_v_w2,
              q_norm, kc_norm, ks_norm, kw_norm, rel_bias, proj_a, proj_b, w_out,
              norm_mlp, w_mlp_in, w_mlp_out):
    h = x
    for l in range(DEPTH):
        h = hybrid_layer(h, rel_bias, norm_mix[l], w_in[l], conv_w[l], conv_b[l],
                         gate_a_w[l], gate_a_b[l], gate_x_w[l], gate_x_b[l], lru_lambda[l],
                         phi_k_pe[l], phi_k_w1[l], phi_k_w2[l], phi_v_pe[l], phi_v_w1[l], phi_v_w2[l],
                         q_norm[l], kc_norm[l], ks_norm[l], kw_norm[l],
                         proj_a[l], proj_b[l], w_out[l], norm_mlp[l], w_mlp_in[l], w_mlp_out[l])
    return h
```

```python
import functools
import math

import numpy as np
import jax
import jax.numpy as jnp
from jax import lax
from jax.experimental import pallas as pl
from jax.experimental.pallas import tpu as pltpu

F32 = jnp.float32
BF16 = jnp.bfloat16

D_MODEL = 1024
D_RNN = 1344
LRU_BLOCKS = 4
LRU_BLOCK_W = D_RNN // LRU_BLOCKS
CONV_W = 4
LRU_C = 8.0
N_HEADS = 16
HEAD_DIM = 64
N_KV_GROUPS = 4
HEADS_PER_GROUP = N_HEADS // N_KV_GROUPS
CMP_BLOCK = 32
CMP_STRIDE = 16
SEL_BLOCK = 64
N_SELECT = 16
WINDOW = 512
PHI_HIDDEN = 256
SEL_FORCED = 1e4
REL_BUCKETS = 32
REL_MAX_DIST = 128
D_FF = 4 * D_MODEL
NORM_EPS = 1e-6
Q_W = N_HEADS * HEAD_DIM
KV_W = N_KV_GROUPS * HEAD_DIM

LANES = 128
VMEM_LIMIT = 56 * 1024 * 1024

RNN_BW = 384
RNN_W = LRU_BLOCKS * RNN_BW
MASK_NEG = -1e30
SEL_NEG = -1e9
SEL_ROW0 = 64
MAX_SBLK = 32
FAR_DIST = 256

COL_U = 0
COL_GATE = RNN_W
COL_Q = 2 * RNN_W
COL_KS = COL_Q + Q_W
COL_VS = COL_KS + KV_W
COL_KW = COL_VS + KV_W
COL_VW = COL_KW + KV_W
COL_KC = COL_VW + KV_W
COL_VC = COL_KC + KV_W
COL_GN = COL_VC + KV_W
COL_GA = 6144
COL_GB = 7168
D_PROJ = 8192


def _dot(a, b):
    return jnp.dot(a, b, preferred_element_type=F32)


def _dot_nt(a, b):
    return lax.dot_general(a, b, (((1,), (1,)), ((), ())), preferred_element_type=F32)


def _gelu_tanh(x):
    return 0.5 * x * (1.0 + jnp.tanh(math.sqrt(2.0 / math.pi) * (x + 0.044715 * (x * x * x))))


def _sigmoid(x):
    return 1.0 / (1.0 + jnp.exp(-x))


def _seg_sum(x, seg_ones):
    hi = x.astype(BF16)
    lo = (x - hi.astype(F32)).astype(BF16)
    return _dot(hi, seg_ones) + _dot(lo, seg_ones)


def _params(sem):
    return pltpu.CompilerParams(dimension_semantics=sem, vmem_limit_bytes=VMEM_LIMIT)


def _norm_matmul_kernel(x_ref, g_ref, w_ref, o_ref, xn_ref):
    @pl.when(pl.program_id(1) == 0)
    def _():
        x = x_ref[...]
        y = x * lax.rsqrt(jnp.mean(x * x, axis=-1, keepdims=True) + NORM_EPS) * g_ref[...]
        xn_ref[...] = y.astype(BF16)

    o_ref[...] = _dot(xn_ref[...], w_ref[...]).astype(o_ref.dtype)


def _norm_matmul(x, gain, w, tm, tn):
    m, k = x.shape
    n = w.shape[1]
    return pl.pallas_call(
        _norm_matmul_kernel,
        grid=(m // tm, n // tn),
        in_specs=[
            pl.BlockSpec((tm, k), lambda i, j: (i, 0)),
            pl.BlockSpec((1, k), lambda i, j: (0, 0)),
            pl.BlockSpec((k, tn), lambda i, j: (0, j)),
        ],
        out_specs=pl.BlockSpec((tm, tn), lambda i, j: (i, j)),
        out_shape=jax.ShapeDtypeStruct((m, n), F32),
        scratch_shapes=[pltpu.VMEM((tm, k), BF16)],
        compiler_params=_params(("parallel", "arbitrary")),
    )(x, gain, w)


def _rglru_kernel(u_ref, ug_ref, cw_ref, cb_ref, wa_ref, ba_ref, wx_ref, bx_ref, lam_ref,
                  y_ref, ubuf, hcar, *, tc):
    t = pl.program_id(2)
    halo = 8

    @pl.when(t == 0)
    def _():
        ubuf[0:halo, :] = jnp.zeros((halo, RNN_BW), F32)
        hcar[...] = jnp.zeros_like(hcar)

    u = u_ref[...]
    ubuf[halo:halo + tc, :] = u
    xc = cb_ref[...]
    for k in range(CONV_W):
        start = halo - (CONV_W - 1) + k
        xc = xc + cw_ref[k:k + 1, :] * ubuf[start:start + tc, :]
    ubuf[0:halo, :] = u[tc - halo:tc, :]

    xb = xc.astype(BF16)
    r = _sigmoid(_dot(xb, wa_ref[...]) + ba_ref[...])
    i = _sigmoid(_dot(xb, wx_ref[...]) + bx_ref[...])
    z = -lam_ref[...]
    softplus = jnp.maximum(z, 0.0) + jnp.log(1.0 + jnp.exp(-jnp.abs(z)))
    log_a = (-LRU_C) * r * softplus
    a = jnp.exp(log_a)
    mult = jnp.sqrt(1.0 - jnp.exp(2.0 * log_a))
    row = lax.broadcasted_iota(jnp.int32, (tc, 1), 0)
    mult = jnp.where((row + t * tc) == 0, 1.0, mult)
    b = mult * (i * xc)

    d = 1
    while d < tc:
        keep = row >= d
        a_prev = pltpu.roll(a, d, 0)
        b_prev = pltpu.roll(b, d, 0)
        b = jnp.where(keep, a * b_prev, 0.0) + b
        a = jnp.where(keep, a * a_prev, a)
        d *= 2
    h = b + a * hcar[0:1, :]
    hcar[...] = jnp.broadcast_to(h[tc - 1:tc, :], hcar.shape)
    y_ref[...] = (h * _gelu_tanh(ug_ref[...])).astype(y_ref.dtype)


def _rglru(proj3, conv_w, conv_b, wa, ba, wx, bx, lam, tc):
    bsz, t, _ = proj3.shape
    nb = LRU_BLOCKS
    vec = pl.BlockSpec((1, RNN_BW), lambda b, n, s: (0, n))
    mat = pl.BlockSpec((None, RNN_BW, RNN_BW), lambda b, n, s: (n, 0, 0))
    return pl.pallas_call(
        functools.partial(_rglru_kernel, tc=tc),
        grid=(bsz, nb, t // tc),
        in_specs=[
            pl.BlockSpec((None, tc, RNN_BW), lambda b, n, s: (b, s, COL_U // RNN_BW + n)),
            pl.BlockSpec((None, tc, RNN_BW), lambda b, n, s: (b, s, COL_GATE // RNN_BW + n)),
            pl.BlockSpec((CONV_W, RNN_BW), lambda b, n, s: (0, n)),
            vec, mat, vec, mat, vec, vec,
        ],
        out_specs=pl.BlockSpec((None, tc, RNN_BW), lambda b, n, s: (b, s, n)),
        out_shape=jax.ShapeDtypeStruct((bsz, t, RNN_W), BF16),
        scratch_shapes=[pltpu.VMEM((tc + 8, RNN_BW), F32), pltpu.VMEM((8, RNN_BW), F32)],
        compiler_params=_params(("parallel", "parallel", "arbitrary")),
    )(proj3, proj3, conv_w, conv_b, wa, ba, wx, bx, lam)


def _compress_kernel(z_ref, pe_ref, w1_ref, w2_ref, norm_ref, seg_ref, o_ref, *, n_chunks):
    which = pl.program_id(0)
    z = z_ref[...]
    first = _dot((z + pe_ref[0:1, :]).astype(BF16), w1_ref[0])
    second = _dot((z + pe_ref[1:2, :]).astype(BF16), w1_ref[1])
    pre = first + pltpu.roll(second, n_chunks - 1, 0)
    out = _dot(_gelu_tanh(pre).astype(BF16), w2_ref[...])
    ssq = _seg_sum(out * out, seg_ref[...])
    normed = out * lax.rsqrt(ssq * (1.0 / HEAD_DIM) + NORM_EPS) * norm_ref[...]
    lane = lax.broadcasted_iota(jnp.int32, (n_chunks, LANES), 1)
    low = lane < HEAD_DIM
    is_key = which == 0
    for pair in range(N_KV_GROUPS // 2):
        sl = slice(pair * LANES, (pair + 1) * LANES)
        kblk, vblk = normed[:, sl], out[:, sl]
        kswap = pltpu.roll(kblk, HEAD_DIM, 1)
        vswap = pltpu.roll(vblk, HEAD_DIM, 1)
        even = jnp.where(is_key, jnp.where(low, kblk, 0.0), jnp.where(low, vblk, vswap))
        odd = jnp.where(is_key, jnp.where(low, kswap, 0.0), jnp.where(low, vswap, vblk))
        o_ref[2 * pair] = even.astype(o_ref.dtype)
        o_ref[2 * pair + 1] = odd.astype(o_ref.dtype)


def _compress(z, pe, w1, w2, norm, seg):
    _, bsz, n_chunks, width = z.shape
    hid = w1.shape[-1]
    return pl.pallas_call(
        functools.partial(_compress_kernel, n_chunks=n_chunks),
        grid=(2, bsz),
        in_specs=[
            pl.BlockSpec((None, None, n_chunks, width), lambda w, b: (w, b, 0, 0)),
            pl.BlockSpec((None, 2, width), lambda w, b: (w, 0, 0)),
            pl.BlockSpec((None, 2, width, hid), lambda w, b: (w, 0, 0, 0)),
            pl.BlockSpec((None, hid, KV_W), lambda w, b: (w, 0, 0)),
            pl.BlockSpec((None, 1, KV_W), lambda w, b: (w, 0, 0)),
            pl.BlockSpec((KV_W, KV_W), lambda w, b: (0, 0)),
        ],
        out_specs=pl.BlockSpec((None, None, N_KV_GROUPS, n_chunks, LANES), lambda w, b: (w, b, 0, 0, 0)),
        out_shape=jax.ShapeDtypeStruct((2, bsz, N_KV_GROUPS, n_chunks, LANES), BF16),
        compiler_params=_params(("arbitrary", "arbitrary")),
    )(z, pe, w1, w2, norm, seg)


def _nsa_prep_kernel(q_ref, ks_ref, vs_ref, kw_ref, vw_ref, qn_ref, ksn_ref, kwn_ref, seg_ref,
                     qo_ref, kso_ref, vso_ref, kwo_ref, vwo_ref, *, tt):
    t0 = pl.program_id(1) * tt
    lane = lax.broadcasted_iota(jnp.int32, (tt, LANES), 1)
    row = lax.broadcasted_iota(jnp.int32, (tt, LANES), 0) + t0
    low = lane < HEAD_DIM
    seg = seg_ref[...]
    onehot = jnp.where((lane - SEL_ROW0) == row // SEL_BLOCK, 1.0, 0.0)

    def normed(ref, gain_ref, blk, scale):
        x = ref[:, blk * LANES:(blk + 1) * LANES]
        ssq = _seg_sum(x * x, seg)
        y = x * lax.rsqrt(ssq * (1.0 / HEAD_DIM) + NORM_EPS) * gain_ref[:, blk * LANES:(blk + 1) * LANES]
        return y * scale if scale != 1.0 else y

    for blk in range(Q_W // LANES):
        y = normed(q_ref, qn_ref, blk, HEAD_DIM ** -0.5)
        ysw = pltpu.roll(y, HEAD_DIM, 1)
        qo_ref[2 * blk] = jnp.where(low, y, 0.0).astype(BF16)
        qo_ref[2 * blk + 1] = jnp.where(low, ysw, 0.0).astype(BF16)
    for blk in range(KV_W // LANES):
        y = normed(ks_ref, ksn_ref, blk, 1.0)
        ysw = pltpu.roll(y, HEAD_DIM, 1)
        kso_ref[2 * blk] = jnp.where(low, y, onehot).astype(BF16)
        kso_ref[2 * blk + 1] = jnp.where(low, ysw, onehot).astype(BF16)
        y = normed(kw_ref, kwn_ref, blk, 1.0)
        ysw = pltpu.roll(y, HEAD_DIM, 1)
        kwo_ref[2 * blk] = jnp.where(low, y, 0.0).astype(BF16)
        kwo_ref[2 * blk + 1] = jnp.where(low, ysw, 0.0).astype(BF16)
        for src, dst in ((vs_ref, vso_ref), (vw_ref, vwo_ref)):
            v = src[:, blk * LANES:(blk + 1) * LANES]
            vsw = pltpu.roll(v, HEAD_DIM, 1)
            dst[2 * blk] = jnp.where(low, v, vsw).astype(BF16)
            dst[2 * blk + 1] = jnp.where(low, vsw, v).astype(BF16)


def _nsa_prep(proj3, qn, ksn, kwn, seg, tt):
    bsz, t, _ = proj3.shape

    def col(width, offset):
        return pl.BlockSpec((None, tt, width), lambda b, s: (b, s, offset // width))

    def vec(width):
        return pl.BlockSpec((1, width), lambda b, s: (0, 0))

    def out(heads):
        return pl.BlockSpec((None, heads, tt, LANES), lambda b, s: (b, 0, s, 0))

    kv_shape = jax.ShapeDtypeStruct((bsz, N_KV_GROUPS, t, LANES), BF16)
    return pl.pallas_call(
        functools.partial(_nsa_prep_kernel, tt=tt),
        grid=(bsz, t // tt),
        in_specs=[col(Q_W, COL_Q), col(KV_W, COL_KS), col(KV_W, COL_VS), col(KV_W, COL_KW), col(KV_W, COL_VW),
                  vec(Q_W), vec(KV_W), vec(KV_W), pl.BlockSpec((LANES, LANES), lambda b, s: (0, 0))],
        out_specs=[out(N_HEADS), out(N_KV_GROUPS), out(N_KV_GROUPS), out(N_KV_GROUPS), out(N_KV_GROUPS)],
        out_shape=[jax.ShapeDtypeStruct((bsz, N_HEADS, t, LANES), BF16), kv_shape, kv_shape, kv_shape, kv_shape],
        compiler_params=_params(("parallel", "parallel")),
    )(proj3, proj3, proj3, proj3, proj3, qn, ksn, kwn, seg)


def _nsa_kernel(q_ref, kc_ref, vc_ref, ks_ref, vs_ref, kw_ref, vw_ref, bc_ref, bdiag_ref, bnear_ref,
                bwin_ref, cover_ref, gate_ref, o_ref, qs_ref, m_ref, l_ref, acc_ref, *, tq, n_sblk):
    qi = pl.program_id(2)
    t0 = qi * tq
    rg = HEADS_PER_GROUP
    rows = rg * tq

    kc = kc_ref[...]
    vc = vc_ref[...]
    cover = cover_ref[...]
    o_cmp = []
    imp_t = jnp.zeros((LANES, tq), F32)
    for r in range(rg):
        bias = bc_ref[r]
        s = _dot_nt(q_ref[r], kc) + bias
        visible = bias > 0.5 * MASK_NEG
        m = jnp.max(s, axis=-1, keepdims=True)
        p = jnp.exp(s - m)
        p = jnp.where(visible, p / jnp.sum(p, axis=-1, keepdims=True), 0.0)
        p_hi = p.astype(BF16)
        p_lo = (p - p_hi.astype(F32)).astype(BF16)
        o_cmp.append(_dot(p_hi, vc))
        imp_t = imp_t + _dot_nt(cover, p_hi) + _dot_nt(cover, p_lo)

    score = imp_t[SEL_ROW0:SEL_ROW0 + MAX_SBLK, :]
    jrow = lax.broadcasted_iota(jnp.int32, (MAX_SBLK, tq), 0)
    qblk = (lax.broadcasted_iota(jnp.int32, (MAX_SBLK, tq), 1) + t0) // SEL_BLOCK
    causal = jrow <= qblk
    forced = causal & ((jrow == 0) | (jrow >= qblk - 1))
    score = jnp.where(forced, SEL_FORCED, jnp.where(causal, score, -1.0))
    rank = jnp.zeros((MAX_SBLK, tq), F32)
    for j in range(n_sblk):
        other = score[j:j + 1, :]
        ahead = (other > score) | ((other == score) & (jrow > j))
        rank = rank + jnp.where(ahead, 1.0, 0.0)
    n_top = min(N_SELECT, n_sblk)
    selected = (rank < n_top) & (score >= 0.0)
    neg_t = jnp.where(selected, 0.0, SEL_NEG)
    neg_full = jnp.concatenate(
        [jnp.zeros((SEL_ROW0, tq), F32), neg_t, jnp.zeros((LANES - SEL_ROW0 - MAX_SBLK, tq), F32)], axis=0)
    neg = jnp.transpose(neg_full)
    for r in range(rg):
        qs_ref[r * tq:(r + 1) * tq, :] = (q_ref[r].astype(F32) + neg).astype(BF16)

    def start(q, k_ref, v_ref, kt, bias):
        s = _dot_nt(q, k_ref[pl.ds(kt * tq, tq), :]) + bias
        m = jnp.max(s, axis=-1, keepdims=True)
        p = jnp.exp(s - m)
        m_ref[...] = m
        l_ref[...] = jnp.sum(p, axis=-1, keepdims=True)
        acc_ref[...] = _dot(p.astype(BF16), v_ref[pl.ds(kt * tq, tq), :])

    def step(q, k_ref, v_ref, kt, bias):
        s = _dot_nt(q, k_ref[pl.ds(kt * tq, tq), :])
        if bias is not None:
            s = s + bias
        m_old = m_ref[...]
        m_new = jnp.maximum(m_old, jnp.max(s, axis=-1, keepdims=True))
        alpha = jnp.exp(m_old - m_new)
        p = jnp.exp(s - m_new)
        m_ref[...] = m_new
        l_ref[...] = alpha * l_ref[...] + jnp.sum(p, axis=-1, keepdims=True)
        acc_ref[...] = alpha * acc_ref[...] + _dot(p.astype(BF16), v_ref[pl.ds(kt * tq, tq), :])

    def finish():
        return acc_ref[...] / l_ref[...]

    bdiag = bdiag_ref[...].reshape(rows, tq)
    q_sel = qs_ref[...]
    start(q_sel, ks_ref, vs_ref, qi, bdiag)

    @pl.when(qi >= 1)
    def _():
        step(q_sel, ks_ref, vs_ref, qi - 1, bnear_ref[...].reshape(rows, tq))

    def far_body(kt, carry):
        step(q_sel, ks_ref, vs_ref, kt, None)
        return carry

    lax.fori_loop(0, jnp.maximum(qi - 1, 0), far_body, 0)
    o_sel = finish()

    q_win = q_ref[...].reshape(rows, LANES)
    start(q_win, kw_ref, vw_ref, qi, bdiag)

    @pl.when(qi >= 1)
    def _():
        step(q_win, kw_ref, vw_ref, qi - 1, bnear_ref[...].reshape(rows, tq))

    @pl.when(qi >= 2)
    def _():
        bwin = bwin_ref[...]
        step(q_win, kw_ref, vw_ref, qi - 2, jnp.concatenate([bwin] * rg, axis=0))

    o_win = finish()

    gates = _sigmoid(gate_ref[...])
    low = lax.broadcasted_iota(jnp.int32, (tq, LANES), 1) < HEAD_DIM
    outs = []
    for r in range(rg):
        sl = slice(r * tq, (r + 1) * tq)
        outs.append(gates[:, 3 * r:3 * r + 1] * o_cmp[r]
                    + gates[:, 3 * r + 1:3 * r + 2] * o_sel[sl]
                    + gates[:, 3 * r + 2:3 * r + 3] * o_win[sl])
    for pair in range(rg // 2):
        o_ref[:, pair * LANES:(pair + 1) * LANES] = jnp.where(low, outs[2 * pair], outs[2 * pair + 1]).astype(o_ref.dtype)


def _nsa_attention(qh, kc, vc, ks, vs, kw, vw, bias_c, bias_diag, bias_near, bias_win, cover, proj3, tq):
    bsz, _, t, _ = qh.shape
    n_chunks = kc.shape[2]
    rg = HEADS_PER_GROUP
    rows = rg * tq

    kv_small = pl.BlockSpec((None, None, n_chunks, LANES), lambda b, g, i: (b, g, 0, 0))
    kv_full = pl.BlockSpec((None, None, t, LANES), lambda b, g, i: (b, g, 0, 0))
    head_tile = pl.BlockSpec((rg, tq, tq), lambda b, g, i: (g, 0, 0))
    return pl.pallas_call(
        functools.partial(_nsa_kernel, tq=tq, n_sblk=t // SEL_BLOCK),
        grid=(bsz, N_KV_GROUPS, t // tq),
        in_specs=[
            pl.BlockSpec((None, rg, tq, LANES), lambda b, g, i: (b, g, i, 0)),
            kv_small, kv_small, kv_full, kv_full, kv_full, kv_full,
            pl.BlockSpec((rg, tq, LANES), lambda b, g, i: (g, i, 0)),
            head_tile, head_tile,
            pl.BlockSpec((tq, tq), lambda b, g, i: (0, 0)),
            pl.BlockSpec((LANES, LANES), lambda b, g, i: (0, 0)),
            pl.BlockSpec((None, tq, LANES), lambda b, g, i: (b, i, COL_GN // LANES + g)),
        ],
        out_specs=pl.BlockSpec((None, tq, rg * HEAD_DIM), lambda b, g, i: (b, i, g)),
        out_shape=jax.ShapeDtypeStruct((bsz, t, Q_W), BF16),
        scratch_shapes=[
            pltpu.VMEM((rows, LANES), BF16),
            pltpu.VMEM((rows, 1), F32),
            pltpu.VMEM((rows, 1), F32),
            pltpu.VMEM((rows, LANES), F32),
        ],
        compiler_params=_params(("parallel", "parallel", "arbitrary")),
    )(qh, kc, vc, ks, vs, kw, vw, bias_c, bias_diag, bias_near, bias_win, cover, proj3)


def _merge_kernel(ya_ref, yb_ref, ga_ref, gb_ref, x_ref, pa_ref, pb_ref, wo_ref, h_ref):
    merged = (_sigmoid(ga_ref[...]) * _dot(ya_ref[...], pa_ref[...])
              + _sigmoid(gb_ref[...]) * _dot(yb_ref[...], pb_ref[...]))
    h_ref[...] = x_ref[...] + _dot(merged.astype(BF16), wo_ref[...])


def _merge_out(ya, yb, proj, x, pa, pb, wo, tm):
    m = x.shape[0]

    def rows(width, offset=0):
        return pl.BlockSpec((tm, width), lambda i: (i, offset // width))

    def whole(a):
        return pl.BlockSpec(a.shape, lambda i: (0, 0))

    return pl.pallas_call(
        _merge_kernel,
        grid=(m // tm,),
        in_specs=[rows(RNN_W), rows(Q_W), rows(D_MODEL, COL_GA), rows(D_MODEL, COL_GB), rows(D_MODEL),
                  whole(pa), whole(pb), whole(wo)],
        out_specs=rows(D_MODEL),
        out_shape=jax.ShapeDtypeStruct((m, D_MODEL), F32),
        compiler_params=_params(("parallel",)),
    )(ya, yb, proj, proj, x, pa, pb, wo)


def _mlp_kernel(h_ref, g_ref, w1_ref, w2_ref, o_ref, hn_ref, acc_ref):
    j = pl.program_id(1)

    @pl.when(j == 0)
    def _():
        h = h_ref[...]
        y = h * lax.rsqrt(jnp.mean(h * h, axis=-1, keepdims=True) + NORM_EPS) * g_ref[...]
        hn_ref[...] = y.astype(BF16)
        acc_ref[...] = h

    z = jnp.maximum(_dot(hn_ref[...], w1_ref[...]), 0.0)
    acc_ref[...] += _dot((z * z).astype(BF16), w2_ref[...])

    @pl.when(j == pl.num_programs(1) - 1)
    def _():
        o_ref[...] = acc_ref[...]


def _mlp(h, gain, w1, w2, tm, tf):
    m, d = h.shape
    ff = w1.shape[1]
    return pl.pallas_call(
        _mlp_kernel,
        grid=(m // tm, ff // tf),
        in_specs=[
            pl.BlockSpec((tm, d), lambda i, j: (i, 0)),
            pl.BlockSpec((1, d), lambda i, j: (0, 0)),
            pl.BlockSpec((d, tf), lambda i, j: (0, j)),
            pl.BlockSpec((tf, d), lambda i, j: (j, 0)),
        ],
        out_specs=pl.BlockSpec((tm, d), lambda i, j: (i, 0)),
        out_shape=jax.ShapeDtypeStruct((m, d), F32),
        scratch_shapes=[pltpu.VMEM((tm, d), BF16), pltpu.VMEM((tm, d), F32)],
        compiler_params=_params(("parallel", "arbitrary")),
    )(h, gain, w1, w2)


def _t5_bucket_table():
    max_exact = REL_BUCKETS // 2
    d = np.arange(FAR_DIST)
    df = np.maximum(d.astype(np.float32), np.float32(1.0))
    large = max_exact + (np.log(df / np.float32(max_exact)) / np.float32(math.log(REL_MAX_DIST / max_exact))
                         * np.float32(REL_BUCKETS - max_exact)).astype(np.int32)
    large = np.minimum(large, REL_BUCKETS - 1)
    return np.where(d < max_exact, d, large).astype(np.int32)


def _pad_blocks(w, axis):
    shape = w.shape
    w = w.reshape(shape[:axis] + (LRU_BLOCKS, LRU_BLOCK_W) + shape[axis + 1:])
    pad = [(0, 0)] * w.ndim
    pad[axis + 1] = (0, RNN_BW - LRU_BLOCK_W)
    w = jnp.pad(w, pad)
    return w.reshape(shape[:axis] + (RNN_W,) + shape[axis + 1:])


def _in_proj_weight(w_in):
    cuts = np.cumsum((D_RNN, D_RNN, Q_W, KV_W, KV_W, KV_W, KV_W, KV_W, KV_W, 3 * N_HEADS, D_MODEL))
    (w_u, w_gate, w_q, w_kc, w_vc, w_ks, w_vs, w_kw, w_vw, w_gn, w_ga, w_gb) = jnp.split(w_in, cuts, axis=1)
    per_group = 3 * HEADS_PER_GROUP
    w_gn = jnp.pad(w_gn.reshape(D_MODEL, N_KV_GROUPS, per_group), ((0, 0), (0, 0), (0, LANES - per_group)))
    w_gn = w_gn.reshape(D_MODEL, N_KV_GROUPS * LANES)
    gap = jnp.zeros((D_MODEL, COL_GA - COL_GN - N_KV_GROUPS * LANES), w_in.dtype)
    w = jnp.concatenate([_pad_blocks(w_u, 1), _pad_blocks(w_gate, 1), w_q, w_ks, w_vs, w_kw, w_vw, w_kc, w_vc,
                         w_gn, gap, w_ga, w_gb], axis=1)
    assert w.shape[1] == D_PROJ
    return w.astype(BF16)


def _phi_weights(pe, w1, w2):
    half = CMP_BLOCK // 2
    eye = jnp.eye(N_KV_GROUPS, dtype=w1.dtype)
    w1h = w1.reshape(2, half, HEAD_DIM, PHI_HIDDEN)
    w1e = jnp.einsum("xldh,gk->xlgdkh", w1h, eye).reshape(2, half * KV_W, N_KV_GROUPS * PHI_HIDDEN)
    w2e = jnp.einsum("hd,gk->ghkd", w2, eye).reshape(N_KV_GROUPS * PHI_HIDDEN, KV_W)
    pee = jnp.broadcast_to(pe.reshape(2, half, 1, HEAD_DIM), (2, half, N_KV_GROUPS, HEAD_DIM)).reshape(2, half * KV_W)
    return pee, w1e.astype(BF16), w2e.astype(BF16)


def kernel(x, norm_mix, w_in, conv_w, conv_b, gate_a_w, gate_a_b, gate_x_w, gate_x_b, lru_lambda, phi_k_pe, phi_k_w1, phi_k_w2, phi_v_pe, phi_v_w1, phi_v_w2, q_norm, kc_norm, ks_norm, kw_norm, rel_bias, proj_a, proj_b, w_out, norm_mlp, w_mlp_in, w_mlp_out):
    bsz, t, d = x.shape
    assert d == D_MODEL and norm_mix.shape[0] == 1
    tq = 256
    assert t % tq == 0 and t // SEL_BLOCK <= MAX_SBLK and t % CMP_STRIDE == 0
    n_tok = bsz * t
    n_chunks = t // CMP_STRIDE
    assert n_chunks % 8 == 0 and n_chunks <= LANES
    x2 = x.reshape(n_tok, d)

    proj = _norm_matmul(x2, norm_mix, _in_proj_weight(w_in[0]), tm=1024 if n_tok % 1024 == 0 else tq, tn=512)
    proj3 = proj.reshape(bsz, t, D_PROJ)

    pad_w = lambda w: jnp.pad(w, ((0, 0), (0, RNN_BW - LRU_BLOCK_W), (0, RNN_BW - LRU_BLOCK_W))).astype(BF16)
    ya = _rglru(proj3, _pad_blocks(conv_w[0], 1), _pad_blocks(conv_b, 1),
                pad_w(gate_a_w[0]), _pad_blocks(gate_a_b.reshape(1, D_RNN), 1),
                pad_w(gate_x_w[0]), _pad_blocks(gate_x_b.reshape(1, D_RNN), 1),
                _pad_blocks(lru_lambda, 1), tc=512 if t % 512 == 0 else tq)

    zk = proj[:, COL_KC:COL_KC + KV_W].reshape(bsz, n_chunks, CMP_STRIDE * KV_W)
    zv = proj[:, COL_VC:COL_VC + KV_W].reshape(bsz, n_chunks, CMP_STRIDE * KV_W)
    pe_k, w1_k, w2_k = _phi_weights(phi_k_pe[0], phi_k_w1[0], phi_k_w2[0])
    pe_v, w1_v, w2_v = _phi_weights(phi_v_pe[0], phi_v_w1[0], phi_v_w2[0])
    seg256 = jnp.asarray(np.kron(np.eye(KV_W // HEAD_DIM), np.ones((HEAD_DIM, HEAD_DIM))), BF16)
    kcn = jnp.tile(kc_norm, (1, N_KV_GROUPS))
    cmp = _compress(jnp.stack([zk, zv]), jnp.stack([pe_k, pe_v]), jnp.stack([w1_k, w1_v]),
                    jnp.stack([w2_k, w2_v]), jnp.stack([kcn, jnp.ones_like(kcn)]), seg256)
    kc, vc = cmp[0], cmp[1]

    seg128 = jnp.asarray(np.kron(np.eye(LANES // HEAD_DIM), np.ones((HEAD_DIM, HEAD_DIM))), BF16)
    qh, ks, vs, kw, vw = _nsa_prep(proj3, jnp.tile(q_norm, (1, N_HEADS)), jnp.tile(ks_norm, (1, N_KV_GROUPS)),
                                   jnp.tile(kw_norm, (1, N_KV_GROUPS)), seg128, tt=tq)

    table = rel_bias.T[:, _t5_bucket_table()]
    shifted = table - table[:, FAR_DIST - 1:]
    ti = np.arange(tq)[:, None]
    tj = np.arange(tq)[None, :]
    bias_diag = jnp.where(ti >= tj, shifted[:, np.clip(ti - tj, 0, FAR_DIST - 1)], MASK_NEG)
    bias_near = shifted[:, np.clip(tq + ti - tj, 0, FAR_DIST - 1)]
    bias_win = jnp.asarray(np.where(tj > ti, 0.0, MASK_NEG), F32)
    assert WINDOW == 2 * tq
    dist_c = np.arange(t)[:, None] - (np.arange(LANES)[None, :] * CMP_STRIDE + CMP_BLOCK - 1)
    seen = (dist_c >= 0) & (np.arange(LANES)[None, :] < n_chunks - 1)
    bias_c = jnp.where(seen, table[:, np.clip(dist_c, 0, FAR_DIST - 1)], MASK_NEG)
    cstart = np.arange(LANES) * CMP_STRIDE
    sj = np.arange(MAX_SBLK)
    cov = ((cstart[None, :] < (sj[:, None] + 1) * SEL_BLOCK) & (cstart[None, :] + CMP_BLOCK - 1 >= sj[:, None] * SEL_BLOCK)
           & (np.arange(LANES)[None, :] < n_chunks - 1))
    cover = np.zeros((LANES, LANES), np.float32)
    cover[SEL_ROW0:SEL_ROW0 + MAX_SBLK] = cov
    yb = _nsa_attention(qh, kc, vc, ks, vs, kw, vw, bias_c.astype(F32), bias_diag.astype(F32),
                        bias_near.astype(F32), bias_win, jnp.asarray(cover, BF16), proj3, tq)

    pa = jnp.pad(proj_a[0].reshape(LRU_BLOCKS, LRU_BLOCK_W, D_MODEL), ((0, 0), (0, RNN_BW - LRU_BLOCK_W), (0, 0)))
    h = _merge_out(ya.reshape(n_tok, RNN_W), yb.reshape(n_tok, Q_W), proj, x2,
                   pa.reshape(RNN_W, D_MODEL).astype(BF16), proj_b[0].astype(BF16), w_out[0].astype(BF16),
                   tm=512 if n_tok % 512 == 0 else tq)

    out = _mlp(h, norm_mlp, w_mlp_in[0].astype(BF16), w_mlp_out[0].astype(BF16),
               tm=1024 if n_tok % 1024 == 0 else tq, tf=512)
    return out.reshape(bsz, t, d)
```

```python
import functools
import math

import numpy as np
import jax
import jax.numpy as jnp
from jax import lax
from jax.experimental import pallas as pl
from jax.experimental.pallas import tpu as pltpu

F32 = jnp.float32
BF16 = jnp.bfloat16

D_MODEL = 1024
D_RNN = 1344
LRU_BLOCKS = 4
LRU_BLOCK_W = D_RNN // LRU_BLOCKS
CONV_W = 4
LRU_C = 8.0
N_HEADS = 16
HEAD_DIM = 64
N_KV_GROUPS = 4
HEADS_PER_GROUP = N_HEADS // N_KV_GROUPS
CMP_BLOCK = 32
CMP_STRIDE = 16
SEL_BLOCK = 64
N_SELECT = 16
WINDOW = 512
PHI_HIDDEN = 256
SEL_FORCED = 1e4
REL_BUCKETS = 32
REL_MAX_DIST = 128
D_FF = 4 * D_MODEL
NORM_EPS = 1e-6
Q_W = N_HEADS * HEAD_DIM
KV_W = N_KV_GROUPS * HEAD_DIM

LANES = 128
VMEM_LIMIT = 56 * 1024 * 1024

RNN_BW = 384
RNN_W = LRU_BLOCKS * RNN_BW
MASK_NEG = -1e30
SEL_NEG = -1e9
SEL_ROW0 = 64
MAX_SBLK = 32
FAR_DIST = 256

COL_U = 0
COL_GATE = RNN_W
COL_Q = 2 * RNN_W
COL_KS = COL_Q + Q_W
COL_VS = COL_KS + KV_W
COL_KW = COL_VS + KV_W
COL_VW = COL_KW + KV_W
COL_KC = COL_VW + KV_W
COL_VC = COL_KC + KV_W
COL_GN = COL_VC + KV_W
COL_GA = 6144
COL_GB = 7168
D_PROJ = 8192


def _dot(a, b):
    return jnp.dot(a, b, preferred_element_type=F32)


def _dot_nt(a, b):
    return lax.dot_general(a, b, (((1,), (1,)), ((), ())), preferred_element_type=F32)


def _gelu_tanh(x):
    return 0.5 * x * (1.0 + jnp.tanh(math.sqrt(2.0 / math.pi) * (x + 0.044715 * (x * x * x))))


def _sigmoid(x):
    return 1.0 / (1.0 + jnp.exp(-x))


def _seg_sum(x, seg_ones):
    hi = x.astype(BF16)
    lo = (x - hi.astype(F32)).astype(BF16)
    return _dot(hi, seg_ones) + _dot(lo, seg_ones)


def _params(sem):
    return pltpu.CompilerParams(dimension_semantics=sem, vmem_limit_bytes=VMEM_LIMIT)


def _norm_matmul_kernel(x_ref, g_ref, w_ref, o_ref, xn_ref):
    @pl.when(pl.program_id(1) == 0)
    def _():
        x = x_ref[...]
        y = x * lax.rsqrt(jnp.mean(x * x, axis=-1, keepdims=True) + NORM_EPS) * g_ref[...]
        xn_ref[...] = y.astype(BF16)

    o_ref[...] = _dot(xn_ref[...], w_ref[...]).astype(o_ref.dtype)


def _norm_matmul(x, gain, w, tm, tn):
    m, k = x.shape
    n = w.shape[1]
    return pl.pallas_call(
        _norm_matmul_kernel,
        grid=(m // tm, n // tn),
        in_specs=[
            pl.BlockSpec((tm, k), lambda i, j: (i, 0)),
            pl.BlockSpec((1, k), lambda i, j: (0, 0)),
            pl.BlockSpec((k, tn), lambda i, j: (0, j)),
        ],
        out_specs=pl.BlockSpec((tm, tn), lambda i, j: (i, j)),
        out_shape=jax.ShapeDtypeStruct((m, n), F32),
        scratch_shapes=[pltpu.VMEM((tm, k), BF16)],
        compiler_params=_params(("parallel", "arbitrary")),
    )(x, gain, w)


def _rglru_kernel(u_ref, ug_ref, cw_ref, cb_ref, wa_ref, ba_ref, wx_ref, bx_ref, lam_ref,
                  y_ref, ubuf, hcar, *, tc):
    t = pl.program_id(2)
    halo = 8

    @pl.when(t == 0)
    def _():
        ubuf[0:halo, :] = jnp.zeros((halo, RNN_BW), F32)
        hcar[...] = jnp.zeros_like(hcar)

    u = u_ref[...]
    ubuf[halo:halo + tc, :] = u
    xc = cb_ref[...]
    for k in range(CONV_W):
        start = halo - (CONV_W - 1) + k
        xc = xc + cw_ref[k:k + 1, :] * ubuf[start:start + tc, :]
    ubuf[0:halo, :] = u[tc - halo:tc, :]

    xb = xc.astype(BF16)
    r = _sigmoid(_dot(xb, wa_ref[...]) + ba_ref[...])
    i = _sigmoid(_dot(xb, wx_ref[...]) + bx_ref[...])
    z = -lam_ref[...]
    softplus = jnp.maximum(z, 0.0) + jnp.log(1.0 + jnp.exp(-jnp.abs(z)))
    log_a = (-LRU_C) * r * softplus
    a = jnp.exp(log_a)
    mult = jnp.sqrt(1.0 - jnp.exp(2.0 * log_a))
    row = lax.broadcasted_iota(jnp.int32, (tc, 1), 0)
    mult = jnp.where((row + t * tc) == 0, 1.0, mult)
    b = mult * (i * xc)

    d = 1
    while d < tc:
        keep = row >= d
        a_prev = pltpu.roll(a, d, 0)
        b_prev = pltpu.roll(b, d, 0)
        b = jnp.where(keep, a * b_prev, 0.0) + b
        a = jnp.where(keep, a * a_prev, a)
        d *= 2
    h = b + a * hcar[0:1, :]
    hcar[...] = jnp.broadcast_to(h[tc - 1:tc, :], hcar.shape)
    y_ref[...] = (h * _gelu_tanh(ug_ref[...])).astype(y_ref.dtype)


def _rglru(proj3, conv_w, conv_b, wa, ba, wx, bx, lam, tc):
    bsz, t, _ = proj3.shape
    nb = LRU_BLOCKS
    vec = pl.BlockSpec((1, RNN_BW), lambda b, n, s: (0, n))
    mat = pl.BlockSpec((None, RNN_BW, RNN_BW), lambda b, n, s: (n, 0, 0))
    return pl.pallas_call(
        functools.partial(_rglru_kernel, tc=tc),
        grid=(bsz, nb, t // tc),
        in_specs=[
            pl.BlockSpec((None, tc, RNN_BW), lambda b, n, s: (b, s, COL_U // RNN_BW + n)),
            pl.BlockSpec((None, tc, RNN_BW), lambda b, n, s: (b, s, COL_GATE // RNN_BW + n)),
            pl.BlockSpec((CONV_W, RNN_BW), lambda b, n, s: (0, n)),
            vec, mat, vec, mat, vec, vec,
        ],
        out_specs=pl.BlockSpec((None, tc, RNN_BW), lambda b, n, s: (b, s, n)),
        out_shape=jax.ShapeDtypeStruct((bsz, t, RNN_W), BF16),
        scratch_shapes=[pltpu.VMEM((tc + 8, RNN_BW), F32), pltpu.VMEM((8, RNN_BW), F32)],
        compiler_params=_params(("parallel", "parallel", "arbitrary")),
    )(proj3, proj3, conv_w, conv_b, wa, ba, wx, bx, lam)


def _compress_kernel(z_ref, pe_ref, w1_ref, w2_ref, norm_ref, seg_ref, o_ref, *, n_chunks, is_key):
    z = z_ref[...]
    first = _dot((z + pe_ref[0:1, :]).astype(BF16), w1_ref[0])
    second = _dot((z + pe_ref[1:2, :]).astype(BF16), w1_ref[1])
    pre = first + pltpu.roll(second, n_chunks - 1, 0)
    out = _dot(_gelu_tanh(pre).astype(BF16), w2_ref[...])
    if is_key:
        ssq = _seg_sum(out * out, seg_ref[...])
        normed = out * lax.rsqrt(ssq * (1.0 / HEAD_DIM) + NORM_EPS) * norm_ref[...]
        low = lax.broadcasted_iota(jnp.int32, (n_chunks, LANES), 1) < HEAD_DIM
        for pair in range(N_KV_GROUPS // 2):
            blk = normed[:, pair * LANES:(pair + 1) * LANES]
            o_ref[2 * pair] = jnp.where(low, blk, 0.0).astype(o_ref.dtype)
            o_ref[2 * pair + 1] = jnp.where(low, pltpu.roll(blk, HEAD_DIM, 1), 0.0).astype(o_ref.dtype)
    else:
        out_t = jnp.transpose(out)
        for g in range(N_KV_GROUPS):
            o_ref[g] = out_t[g * HEAD_DIM:(g + 1) * HEAD_DIM, :].astype(o_ref.dtype)


def _compress(z, pe, w1, w2, norm, seg, is_key):
    bsz, n_chunks, width = z.shape
    hid = w1.shape[-1]
    out_block = (N_KV_GROUPS, n_chunks, LANES) if is_key else (N_KV_GROUPS, HEAD_DIM, n_chunks)
    return pl.pallas_call(
        functools.partial(_compress_kernel, n_chunks=n_chunks, is_key=is_key),
        grid=(bsz,),
        in_specs=[
            pl.BlockSpec((None, n_chunks, width), lambda b: (b, 0, 0)),
            pl.BlockSpec((2, width), lambda b: (0, 0)),
            pl.BlockSpec((2, width, hid), lambda b: (0, 0, 0)),
            pl.BlockSpec((hid, KV_W), lambda b: (0, 0)),
            pl.BlockSpec((1, KV_W), lambda b: (0, 0)),
            pl.BlockSpec((KV_W, KV_W), lambda b: (0, 0)),
        ],
        out_specs=pl.BlockSpec((None,) + out_block, lambda b: (b, 0, 0, 0)),
        out_shape=jax.ShapeDtypeStruct((bsz,) + out_block, BF16),
        compiler_params=_params(("parallel",)),
    )(z, pe, w1, w2, norm, seg)


def _nsa_prep_kernel(q_ref, ks_ref, vs_ref, kw_ref, vw_ref, qn_ref, ksn_ref, kwn_ref, seg_ref,
                     qo_ref, kso_ref, vso_ref, kwo_ref, vwo_ref, *, tt):
    t0 = pl.program_id(1) * tt
    lane = lax.broadcasted_iota(jnp.int32, (tt, LANES), 1)
    row = lax.broadcasted_iota(jnp.int32, (tt, LANES), 0) + t0
    low = lane < HEAD_DIM
    seg = seg_ref[...]
    onehot = jnp.where((lane - SEL_ROW0) == row // SEL_BLOCK, 1.0, 0.0)

    def normed(ref, gain_ref, blk, scale):
        x = ref[:, blk * LANES:(blk + 1) * LANES]
        ssq = _seg_sum(x * x, seg)
        y = x * lax.rsqrt(ssq * (1.0 / HEAD_DIM) + NORM_EPS) * gain_ref[:, blk * LANES:(blk + 1) * LANES]
        return y * scale if scale != 1.0 else y

    for blk in range(Q_W // LANES):
        y = normed(q_ref, qn_ref, blk, HEAD_DIM ** -0.5)
        ysw = pltpu.roll(y, HEAD_DIM, 1)
        qo_ref[2 * blk] = jnp.where(low, y, 0.0).astype(BF16)
        qo_ref[2 * blk + 1] = jnp.where(low, ysw, 0.0).astype(BF16)
    for blk in range(KV_W // LANES):
        y = normed(ks_ref, ksn_ref, blk, 1.0)
        ysw = pltpu.roll(y, HEAD_DIM, 1)
        kso_ref[2 * blk] = jnp.where(low, y, onehot).astype(BF16)
        kso_ref[2 * blk + 1] = jnp.where(low, ysw, onehot).astype(BF16)
        y = normed(kw_ref, kwn_ref, blk, 1.0)
        ysw = pltpu.roll(y, HEAD_DIM, 1)
        kwo_ref[2 * blk] = jnp.where(low, y, 0.0).astype(BF16)
        kwo_ref[2 * blk + 1] = jnp.where(low, ysw, 0.0).astype(BF16)
        for src, dst in ((vs_ref, vso_ref), (vw_ref, vwo_ref)):
            v_t = jnp.transpose(src[:, blk * LANES:(blk + 1) * LANES])
            dst[2 * blk] = v_t[0:HEAD_DIM, :].astype(BF16)
            dst[2 * blk + 1] = v_t[HEAD_DIM:LANES, :].astype(BF16)


def _nsa_prep(proj3, qn, ksn, kwn, seg, tt):
    bsz, t, _ = proj3.shape

    def col(width, offset):
        return pl.BlockSpec((None, tt, width), lambda b, s: (b, s, offset // width))

    def vec(width):
        return pl.BlockSpec((1, width), lambda b, s: (0, 0))

    def out(heads):
        return pl.BlockSpec((None, heads, tt, LANES), lambda b, s: (b, 0, s, 0))

    out_t = pl.BlockSpec((None, N_KV_GROUPS, HEAD_DIM, tt), lambda b, s: (b, 0, 0, s))
    k_shape = jax.ShapeDtypeStruct((bsz, N_KV_GROUPS, t, LANES), BF16)
    v_shape = jax.ShapeDtypeStruct((bsz, N_KV_GROUPS, HEAD_DIM, t), BF16)
    return pl.pallas_call(
        functools.partial(_nsa_prep_kernel, tt=tt),
        grid=(bsz, t // tt),
        in_specs=[col(Q_W, COL_Q), col(KV_W, COL_KS), col(KV_W, COL_VS), col(KV_W, COL_KW), col(KV_W, COL_VW),
                  vec(Q_W), vec(KV_W), vec(KV_W), pl.BlockSpec((LANES, LANES), lambda b, s: (0, 0))],
        out_specs=[out(N_HEADS), out(N_KV_GROUPS), out_t, out(N_KV_GROUPS), out_t],
        out_shape=[jax.ShapeDtypeStruct((bsz, N_HEADS, t, LANES), BF16), k_shape, v_shape, k_shape, v_shape],
        compiler_params=_params(("parallel", "parallel")),
    )(proj3, proj3, proj3, proj3, proj3, qn, ksn, kwn, seg)


def _nsa_kernel(q_ref, kc_ref, vc_ref, ks_ref, vs_ref, kw_ref, vw_ref, bc_ref, bdiag_ref, bnear_ref,
                bwin_ref, cover_ref, gate_ref, o_ref, qs_ref, m_ref, l_ref, acc_ref, *, tq, n_sblk):
    qi = pl.program_id(2)
    t0 = qi * tq
    rg = HEADS_PER_GROUP


    kc = kc_ref[...]
    vc_t = vc_ref[...]
    cover = cover_ref[...]
    o_cmp = []
    imp_t = jnp.zeros((LANES, tq), F32)
    for r in range(rg):
        bias = bc_ref[r]
        s = _dot_nt(kc, q_ref[r]) + bias
        visible = bias > 0.5 * MASK_NEG
        m = jnp.max(s, axis=0, keepdims=True)
        p = jnp.exp(s - m)
        p = jnp.where(visible, p / jnp.sum(p, axis=0, keepdims=True), 0.0)
        p_hi = p.astype(BF16)
        p_lo = (p - p_hi.astype(F32)).astype(BF16)
        o_cmp.append(_dot(vc_t, p_hi))
        imp_t = imp_t + _dot(cover, p_hi) + _dot(cover, p_lo)

    score = imp_t[SEL_ROW0:SEL_ROW0 + MAX_SBLK, :]
    jrow = lax.broadcasted_iota(jnp.int32, (MAX_SBLK, tq), 0)
    qblk = (lax.broadcasted_iota(jnp.int32, (MAX_SBLK, tq), 1) + t0) // SEL_BLOCK
    causal = jrow <= qblk
    forced = causal & ((jrow == 0) | (jrow >= qblk - 1))
    score = jnp.where(forced, SEL_FORCED, jnp.where(causal, score, -1.0))
    rank = jnp.zeros((MAX_SBLK, tq), F32)
    for j in range(n_sblk):
        other = score[j:j + 1, :]
        ahead = (other > score) | ((other == score) & (jrow > j))
        rank = rank + jnp.where(ahead, 1.0, 0.0)
    n_top = min(N_SELECT, n_sblk)
    selected = (rank < n_top) & (score >= 0.0)
    neg_t = jnp.where(selected, 0.0, SEL_NEG)
    neg_full = jnp.concatenate(
        [jnp.zeros((SEL_ROW0, tq), F32), neg_t, jnp.zeros((LANES - SEL_ROW0 - MAX_SBLK, tq), F32)], axis=0)
    neg = jnp.transpose(neg_full)
    for r in range(rg):
        qs_ref[r] = (q_ref[r].astype(F32) + neg).astype(BF16)

    def start(q_of, k_ref, v_ref, kt, bias_of):
        k = k_ref[pl.ds(kt * tq, tq), :]
        v_t = v_ref[:, pl.ds(kt * tq, tq)]
        for r in range(rg):
            s = _dot_nt(k, q_of(r)) + bias_of(r)
            m = jnp.max(s, axis=0, keepdims=True)
            p = jnp.exp(s - m)
            m_ref[r] = m
            l_ref[r] = jnp.sum(p, axis=0, keepdims=True)
            acc_ref[r] = _dot(v_t, p.astype(BF16))

    def step(q_of, k_ref, v_ref, kt, bias_of):
        k = k_ref[pl.ds(kt * tq, tq), :]
        v_t = v_ref[:, pl.ds(kt * tq, tq)]
        m_old = [m_ref[r] for r in range(rg)]
        l_old = [l_ref[r] for r in range(rg)]
        acc_old = [acc_ref[r] for r in range(rg)]
        scores = [_dot_nt(k, q_of(r)) for r in range(rg)]
        m_out, l_out, acc_out = [], [], []
        for r in range(rg):
            s = scores[r]
            if bias_of is not None:
                s = s + bias_of(r)
            m_new = jnp.maximum(m_old[r], jnp.max(s, axis=0, keepdims=True))
            alpha = jnp.exp(m_old[r] - m_new)
            p = jnp.exp(s - m_new)
            m_out.append(m_new)
            l_out.append(alpha * l_old[r] + jnp.sum(p, axis=0, keepdims=True))
            acc_out.append(alpha * acc_old[r] + _dot(v_t, p.astype(BF16)))
        for r in range(rg):
            m_ref[r] = m_out[r]
            l_ref[r] = l_out[r]
            acc_ref[r] = acc_out[r]

    def finish():
        return [acc_ref[r] / l_ref[r] for r in range(rg)]

    diag_of = lambda r: bdiag_ref[r]
    near_of = lambda r: bnear_ref[r]
    sel_of = lambda r: qs_ref[r]
    start(sel_of, ks_ref, vs_ref, qi, diag_of)

    @pl.when(qi >= 1)
    def _():
        step(sel_of, ks_ref, vs_ref, qi - 1, near_of)

    def far_body(kt, carry):
        step(sel_of, ks_ref, vs_ref, kt, None)
        return carry

    lax.fori_loop(0, jnp.maximum(qi - 1, 0), far_body, 0)
    o_sel = finish()

    win_of = lambda r: q_ref[r]
    start(win_of, kw_ref, vw_ref, qi, diag_of)

    @pl.when(qi >= 1)
    def _():
        step(win_of, kw_ref, vw_ref, qi - 1, near_of)

    @pl.when(qi >= 2)
    def _():
        step(win_of, kw_ref, vw_ref, qi - 2, lambda r: bwin_ref[...])

    o_win = finish()

    gates_t = jnp.transpose(_sigmoid(gate_ref[...]))
    outs = []
    for r in range(rg):
        outs.append(gates_t[3 * r:3 * r + 1, :] * o_cmp[r]
                    + gates_t[3 * r + 1:3 * r + 2, :] * o_sel[r]
                    + gates_t[3 * r + 2:3 * r + 3, :] * o_win[r])
    o_ref[...] = jnp.transpose(jnp.concatenate(outs, axis=0)).astype(o_ref.dtype)


def _nsa_attention(qh, kc, vc, ks, vs, kw, vw, bias_c, bias_diag, bias_near, bias_win, cover, proj3, tq):
    bsz, _, t, _ = qh.shape
    n_chunks = kc.shape[2]
    rg = HEADS_PER_GROUP

    k_full = pl.BlockSpec((None, None, t, LANES), lambda b, g, i: (b, g, 0, 0))
    v_full = pl.BlockSpec((None, None, HEAD_DIM, t), lambda b, g, i: (b, g, 0, 0))
    head_tile = pl.BlockSpec((rg, tq, tq), lambda b, g, i: (g, 0, 0))
    return pl.pallas_call(
        functools.partial(_nsa_kernel, tq=tq, n_sblk=t // SEL_BLOCK),
        grid=(bsz, N_KV_GROUPS, t // tq),
        in_specs=[
            pl.BlockSpec((None, rg, tq, LANES), lambda b, g, i: (b, g, i, 0)),
            pl.BlockSpec((None, None, n_chunks, LANES), lambda b, g, i: (b, g, 0, 0)),
            pl.BlockSpec((None, None, HEAD_DIM, n_chunks), lambda b, g, i: (b, g, 0, 0)),
            k_full, v_full, k_full, v_full,
            pl.BlockSpec((rg, n_chunks, tq), lambda b, g, i: (g, 0, i)),
            head_tile, head_tile,
            pl.BlockSpec((tq, tq), lambda b, g, i: (0, 0)),
            pl.BlockSpec((LANES, n_chunks), lambda b, g, i: (0, 0)),
            pl.BlockSpec((None, tq, LANES), lambda b, g, i: (b, i, COL_GN // LANES + g)),
        ],
        out_specs=pl.BlockSpec((None, tq, rg * HEAD_DIM), lambda b, g, i: (b, i, g)),
        out_shape=jax.ShapeDtypeStruct((bsz, t, Q_W), BF16),
        scratch_shapes=[
            pltpu.VMEM((rg, tq, LANES), BF16),
            pltpu.VMEM((rg, 1, tq), F32),
            pltpu.VMEM((rg, 1, tq), F32),
            pltpu.VMEM((rg, HEAD_DIM, tq), F32),
        ],
        compiler_params=_params(("parallel", "parallel", "arbitrary")),
    )(qh, kc, vc, ks, vs, kw, vw, bias_c, bias_diag, bias_near, bias_win, cover, proj3)


def _merge_kernel(ya_ref, yb_ref, ga_ref, gb_ref, x_ref, pa_ref, pb_ref, wo_ref, h_ref):
    merged = (_sigmoid(ga_ref[...]) * _dot(ya_ref[...], pa_ref[...])
              + _sigmoid(gb_ref[...]) * _dot(yb_ref[...], pb_ref[...]))
    h_ref[...] = x_ref[...] + _dot(merged.astype(BF16), wo_ref[...])


def _merge_out(ya, yb, proj, x, pa, pb, wo, tm):
    m = x.shape[0]

    def rows(width, offset=0):
        return pl.BlockSpec((tm, width), lambda i: (i, offset // width))

    def whole(a):
        return pl.BlockSpec(a.shape, lambda i: (0, 0))

    return pl.pallas_call(
        _merge_kernel,
        grid=(m // tm,),
        in_specs=[rows(RNN_W), rows(Q_W), rows(D_MODEL, COL_GA), rows(D_MODEL, COL_GB), rows(D_MODEL),
                  whole(pa), whole(pb), whole(wo)],
        out_specs=rows(D_MODEL),
        out_shape=jax.ShapeDtypeStruct((m, D_MODEL), F32),
        compiler_params=_params(("parallel",)),
    )(ya, yb, proj, proj, x, pa, pb, wo)


def _mlp_kernel(h_ref, g_ref, w1_ref, w2_ref, o_ref, hn_ref, acc_ref):
    j = pl.program_id(1)

    @pl.when(j == 0)
    def _():
        h = h_ref[...]
        y = h * lax.rsqrt(jnp.mean(h * h, axis=-1, keepdims=True) + NORM_EPS) * g_ref[...]
        hn_ref[...] = y.astype(BF16)
        acc_ref[...] = h

    z = jnp.maximum(_dot(hn_ref[...], w1_ref[...]), 0.0)
    acc_ref[...] += _dot((z * z).astype(BF16), w2_ref[...])

    @pl.when(j == pl.num_programs(1) - 1)
    def _():
        o_ref[...] = acc_ref[...]


def _mlp(h, gain, w1, w2, tm, tf):
    m, d = h.shape
    ff = w1.shape[1]
    return pl.pallas_call(
        _mlp_kernel,
        grid=(m // tm, ff // tf),
        in_specs=[
            pl.BlockSpec((tm, d), lambda i, j: (i, 0)),
            pl.BlockSpec((1, d), lambda i, j: (0, 0)),
            pl.BlockSpec((d, tf), lambda i, j: (0, j)),
            pl.BlockSpec((tf, d), lambda i, j: (j, 0)),
        ],
        out_specs=pl.BlockSpec((tm, d), lambda i, j: (i, 0)),
        out_shape=jax.ShapeDtypeStruct((m, d), F32),
        scratch_shapes=[pltpu.VMEM((tm, d), BF16), pltpu.VMEM((tm, d), F32)],
        compiler_params=_params(("parallel", "arbitrary")),
    )(h, gain, w1, w2)


def _t5_bucket_table():
    max_exact = REL_BUCKETS // 2
    d = np.arange(FAR_DIST)
    df = np.maximum(d.astype(np.float32), np.float32(1.0))
    large = max_exact + (np.log(df / np.float32(max_exact)) / np.float32(math.log(REL_MAX_DIST / max_exact))
                         * np.float32(REL_BUCKETS - max_exact)).astype(np.int32)
    large = np.minimum(large, REL_BUCKETS - 1)
    return np.where(d < max_exact, d, large).astype(np.int32)


def _pad_blocks(w, axis):
    shape = w.shape
    w = w.reshape(shape[:axis] + (LRU_BLOCKS, LRU_BLOCK_W) + shape[axis + 1:])
    pad = [(0, 0)] * w.ndim
    pad[axis + 1] = (0, RNN_BW - LRU_BLOCK_W)
    w = jnp.pad(w, pad)
    return w.reshape(shape[:axis] + (RNN_W,) + shape[axis + 1:])


def _in_proj_weight(w_in):
    cuts = np.cumsum((D_RNN, D_RNN, Q_W, KV_W, KV_W, KV_W, KV_W, KV_W, KV_W, 3 * N_HEADS, D_MODEL))
    (w_u, w_gate, w_q, w_kc, w_vc, w_ks, w_vs, w_kw, w_vw, w_gn, w_ga, w_gb) = jnp.split(w_in, cuts, axis=1)
    per_group = 3 * HEADS_PER_GROUP
    w_gn = jnp.pad(w_gn.reshape(D_MODEL, N_KV_GROUPS, per_group), ((0, 0), (0, 0), (0, LANES - per_group)))
    w_gn = w_gn.reshape(D_MODEL, N_KV_GROUPS * LANES)
    gap = jnp.zeros((D_MODEL, COL_GA - COL_GN - N_KV_GROUPS * LANES), w_in.dtype)
    w = jnp.concatenate([_pad_blocks(w_u, 1), _pad_blocks(w_gate, 1), w_q, w_ks, w_vs, w_kw, w_vw, w_kc, w_vc,
                         w_gn, gap, w_ga, w_gb], axis=1)
    assert w.shape[1] == D_PROJ
    return w.astype(BF16)


def _phi_weights(pe, w1, w2):
    half = CMP_BLOCK // 2
    eye = jnp.eye(N_KV_GROUPS, dtype=w1.dtype)
    w1h = w1.reshape(2, half, HEAD_DIM, PHI_HIDDEN)
    w1e = jnp.einsum("xldh,gk->xlgdkh", w1h, eye).reshape(2, half * KV_W, N_KV_GROUPS * PHI_HIDDEN)
    w2e = jnp.einsum("hd,gk->ghkd", w2, eye).reshape(N_KV_GROUPS * PHI_HIDDEN, KV_W)
    pee = jnp.broadcast_to(pe.reshape(2, half, 1, HEAD_DIM), (2, half, N_KV_GROUPS, HEAD_DIM)).reshape(2, half * KV_W)
    return pee, w1e.astype(BF16), w2e.astype(BF16)


def kernel(x, norm_mix, w_in, conv_w, conv_b, gate_a_w, gate_a_b, gate_x_w, gate_x_b, lru_lambda, phi_k_pe, phi_k_w1, phi_k_w2, phi_v_pe, phi_v_w1, phi_v_w2, q_norm, kc_norm, ks_norm, kw_norm, rel_bias, proj_a, proj_b, w_out, norm_mlp, w_mlp_in, w_mlp_out):
    bsz, t, d = x.shape
    assert d == D_MODEL and norm_mix.shape[0] == 1
    tq = 256
    assert t % tq == 0 and t // SEL_BLOCK <= MAX_SBLK and t % CMP_STRIDE == 0
    n_tok = bsz * t
    n_chunks = t // CMP_STRIDE
    assert n_chunks % 8 == 0 and n_chunks <= LANES
    x2 = x.reshape(n_tok, d)

    proj = _norm_matmul(x2, norm_mix, _in_proj_weight(w_in[0]), tm=1024 if n_tok % 1024 == 0 else tq, tn=512)
    proj3 = proj.reshape(bsz, t, D_PROJ)

    pad_w = lambda w: jnp.pad(w, ((0, 0), (0, RNN_BW - LRU_BLOCK_W), (0, RNN_BW - LRU_BLOCK_W))).astype(BF16)
    ya = _rglru(proj3, _pad_blocks(conv_w[0], 1), _pad_blocks(conv_b, 1),
                pad_w(gate_a_w[0]), _pad_blocks(gate_a_b.reshape(1, D_RNN), 1),
                pad_w(gate_x_w[0]), _pad_blocks(gate_x_b.reshape(1, D_RNN), 1),
                _pad_blocks(lru_lambda, 1), tc=512 if t % 512 == 0 else tq)

    zk = proj[:, COL_KC:COL_KC + KV_W].reshape(bsz, n_chunks, CMP_STRIDE * KV_W)
    zv = proj[:, COL_VC:COL_VC + KV_W].reshape(bsz, n_chunks, CMP_STRIDE * KV_W)
    pe_k, w1_k, w2_k = _phi_weights(phi_k_pe[0], phi_k_w1[0], phi_k_w2[0])
    pe_v, w1_v, w2_v = _phi_weights(phi_v_pe[0], phi_v_w1[0], phi_v_w2[0])
    seg256 = jnp.asarray(np.kron(np.eye(KV_W // HEAD_DIM), np.ones((HEAD_DIM, HEAD_DIM))), BF16)
    kcn = jnp.tile(kc_norm, (1, N_KV_GROUPS))
    kc = _compress(zk, pe_k, w1_k, w2_k, kcn, seg256, is_key=True)
    vc = _compress(zv, pe_v, w1_v, w2_v, kcn, seg256, is_key=False)

    seg128 = jnp.asarray(np.kron(np.eye(LANES // HEAD_DIM), np.ones((HEAD_DIM, HEAD_DIM))), BF16)
    qh, ks, vs, kw, vw = _nsa_prep(proj3, jnp.tile(q_norm, (1, N_HEADS)), jnp.tile(ks_norm, (1, N_KV_GROUPS)),
                                   jnp.tile(kw_norm, (1, N_KV_GROUPS)), seg128, tt=tq)

    bucket_of = _t5_bucket_table()
    far = rel_bias[REL_BUCKETS - 1][:, None, None]

    def bias_table(dist, valid, shift):
        buckets = jnp.asarray(bucket_of[np.clip(dist, 0, FAR_DIST - 1)].astype(np.int8))
        onehot = (buckets[None] == jnp.arange(REL_BUCKETS, dtype=jnp.int8).reshape(-1, 1, 1)).astype(F32)
        vals = jnp.einsum("kh,kji->hji", rel_bias, onehot, precision=lax.Precision.HIGHEST)
        if shift:
            vals = vals - far
        return jnp.where(jnp.asarray(valid), vals, MASK_NEG).astype(F32)

    kj = np.arange(tq)[:, None]
    qi_ = np.arange(tq)[None, :]
    bias_diag = bias_table(qi_ - kj, qi_ >= kj, True)
    bias_near = bias_table(tq + qi_ - kj, np.ones((tq, tq), bool), True)
    bias_win = jnp.asarray(np.where(kj > qi_, 0.0, MASK_NEG), F32)
    assert WINDOW == 2 * tq
    cidx = np.arange(n_chunks)[:, None]
    dist_c = np.arange(t)[None, :] - (cidx * CMP_STRIDE + CMP_BLOCK - 1)
    bias_c = bias_table(dist_c, (dist_c >= 0) & (cidx < n_chunks - 1), False)
    cstart = np.arange(n_chunks) * CMP_STRIDE
    sj = np.arange(MAX_SBLK)
    cov = ((cstart[None, :] < (sj[:, None] + 1) * SEL_BLOCK) & (cstart[None, :] + CMP_BLOCK - 1 >= sj[:, None] * SEL_BLOCK)
           & (np.arange(n_chunks)[None, :] < n_chunks - 1))
    cover = np.zeros((LANES, n_chunks), np.float32)
    cover[SEL_ROW0:SEL_ROW0 + MAX_SBLK] = cov
    yb = _nsa_attention(qh, kc, vc, ks, vs, kw, vw, bias_c, bias_diag, bias_near, bias_win,
                        jnp.asarray(cover, BF16), proj3, tq)

    pa = jnp.pad(proj_a[0].reshape(LRU_BLOCKS, LRU_BLOCK_W, D_MODEL), ((0, 0), (0, RNN_BW - LRU_BLOCK_W), (0, 0)))
    h = _merge_out(ya.reshape(n_tok, RNN_W), yb.reshape(n_tok, Q_W), proj, x2,
                   pa.reshape(RNN_W, D_MODEL).astype(BF16), proj_b[0].astype(BF16), w_out[0].astype(BF16),
                   tm=512 if n_tok % 512 == 0 else tq)

    out = _mlp(h, norm_mlp, w_mlp_in[0].astype(BF16), w_mlp_out[0].astype(BF16),
               tm=1024 if n_tok % 1024 == 0 else tq, tf=512)
    return out.reshape(bsz, t, d)
```

```python
import functools
import math

import numpy as np
import jax
import jax.numpy as jnp
from jax import lax
from jax.experimental import pallas as pl
from jax.experimental.pallas import tpu as pltpu

F32 = jnp.float32
BF16 = jnp.bfloat16

D_MODEL = 1024
D_RNN = 1344
LRU_BLOCKS = 4
LRU_BLOCK_W = D_RNN // LRU_BLOCKS
CONV_W = 4
LRU_C = 8.0
N_HEADS = 16
HEAD_DIM = 64
N_KV_GROUPS = 4
HEADS_PER_GROUP = N_HEADS // N_KV_GROUPS
CMP_BLOCK = 32
CMP_STRIDE = 16
SEL_BLOCK = 64
N_SELECT = 16
WINDOW = 512
PHI_HIDDEN = 256
SEL_FORCED = 1e4
REL_BUCKETS = 32
REL_MAX_DIST = 128
D_FF = 4 * D_MODEL
NORM_EPS = 1e-6
Q_W = N_HEADS * HEAD_DIM
KV_W = N_KV_GROUPS * HEAD_DIM

LANES = 128
VMEM_LIMIT = 56 * 1024 * 1024

RNN_BW = 384
RNN_W = LRU_BLOCKS * RNN_BW
MASK_NEG = -1e30
SEL_NEG = -1e9
SEL_ROW0 = 64
MAX_SBLK = 32
FAR_DIST = 256
LOG2E = math.log2(math.e)
V_ROWS = HEAD_DIM + 16
BIAS_ZERO, BIAS_DIAG, BIAS_NEAR, BIAS_WIN, BIAS_MASKED = range(5)

COL_U = 0
COL_GATE = RNN_W
COL_Q = 2 * RNN_W
COL_KS = COL_Q + Q_W
COL_VS = COL_KS + KV_W
COL_KW = COL_VS + KV_W
COL_VW = COL_KW + KV_W
COL_KC = COL_VW + KV_W
COL_VC = COL_KC + KV_W
COL_GN = COL_VC + KV_W
COL_GA = 6144
COL_GB = 7168
D_PROJ = 8192


def _dot(a, b):
    return jnp.dot(a, b, preferred_element_type=F32)


def _dot_nt(a, b):
    return lax.dot_general(a, b, (((1,), (1,)), ((), ())), preferred_element_type=F32)


def _gelu_tanh(x):
    return 0.5 * x * (1.0 + jnp.tanh(math.sqrt(2.0 / math.pi) * (x + 0.044715 * (x * x * x))))


def _sigmoid(x):
    return 1.0 / (1.0 + jnp.exp(-x))


def _seg_sum(x, seg_ones):
    hi = x.astype(BF16)
    lo = (x - hi.astype(F32)).astype(BF16)
    return _dot(hi, seg_ones) + _dot(lo, seg_ones)


def _params(sem, flags=None):
    return pltpu.CompilerParams(dimension_semantics=sem, vmem_limit_bytes=VMEM_LIMIT, flags=flags)


def _norm_matmul_kernel(x_ref, g_ref, w_ref, o_ref, xn_ref):
    @pl.when(pl.program_id(1) == 0)
    def _():
        x = x_ref[...]
        y = x * lax.rsqrt(jnp.mean(x * x, axis=-1, keepdims=True) + NORM_EPS) * g_ref[...]
        xn_ref[...] = y.astype(BF16)

    o_ref[...] = _dot(xn_ref[...], w_ref[...]).astype(o_ref.dtype)


def _norm_matmul(x, gain, w, tm, tn):
    m, k = x.shape
    n = w.shape[1]
    return pl.pallas_call(
        _norm_matmul_kernel,
        grid=(m // tm, n // tn),
        in_specs=[
            pl.BlockSpec((tm, k), lambda i, j: (i, 0)),
            pl.BlockSpec((1, k), lambda i, j: (0, 0)),
            pl.BlockSpec((k, tn), lambda i, j: (0, j)),
        ],
        out_specs=pl.BlockSpec((tm, tn), lambda i, j: (i, j)),
        out_shape=jax.ShapeDtypeStruct((m, n), BF16),
        scratch_shapes=[pltpu.VMEM((tm, k), BF16)],
        compiler_params=_params(("parallel", "arbitrary")),
    )(x, gain, w)


def _rglru_kernel(u_ref, ug_ref, cw_ref, cb_ref, wa_ref, ba_ref, wx_ref, bx_ref, lam_ref,
                  y_ref, ubuf, hcar, *, tc):
    t = pl.program_id(2)
    halo = 8

    @pl.when(t == 0)
    def _():
        ubuf[0:halo, :] = jnp.zeros((halo, RNN_BW), F32)
        hcar[...] = jnp.zeros_like(hcar)

    u = u_ref[...].astype(F32)
    ubuf[halo:halo + tc, :] = u
    xc = cb_ref[...]
    for k in range(CONV_W):
        start = halo - (CONV_W - 1) + k
        xc = xc + cw_ref[k:k + 1, :] * ubuf[start:start + tc, :]
    ubuf[0:halo, :] = u[tc - halo:tc, :]

    xb = xc.astype(BF16)
    r = _sigmoid(_dot(xb, wa_ref[...]) + ba_ref[...])
    i = _sigmoid(_dot(xb, wx_ref[...]) + bx_ref[...])
    z = -lam_ref[...]
    softplus = jnp.maximum(z, 0.0) + jnp.log(1.0 + jnp.exp(-jnp.abs(z)))
    log_a = (-LRU_C) * r * softplus
    a = jnp.exp(log_a)
    mult = jnp.sqrt(1.0 - jnp.exp(2.0 * log_a))
    row = lax.broadcasted_iota(jnp.int32, (tc, 1), 0)
    mult = jnp.where((row + t * tc) == 0, 1.0, mult)
    b = mult * (i * xc)

    d = 1
    while d < tc:
        keep = row >= d
        a_prev = pltpu.roll(a, d, 0)
        b_prev = pltpu.roll(b, d, 0)
        b = jnp.where(keep, a * b_prev, 0.0) + b
        a = jnp.where(keep, a * a_prev, a)
        d *= 2
    h = b + a * hcar[0:1, :]
    hcar[...] = jnp.broadcast_to(h[tc - 1:tc, :], hcar.shape)
    y_ref[...] = (h * _gelu_tanh(ug_ref[...].astype(F32))).astype(y_ref.dtype)


def _rglru(proj3, conv_w, conv_b, wa, ba, wx, bx, lam, tc):
    bsz, t, _ = proj3.shape
    nb = LRU_BLOCKS
    vec = pl.BlockSpec((1, RNN_BW), lambda b, n, s: (0, n))
    mat = pl.BlockSpec((None, RNN_BW, RNN_BW), lambda b, n, s: (n, 0, 0))
    return pl.pallas_call(
        functools.partial(_rglru_kernel, tc=tc),
        grid=(bsz, nb, t // tc),
        in_specs=[
            pl.BlockSpec((None, tc, RNN_BW), lambda b, n, s: (b, s, COL_U // RNN_BW + n)),
            pl.BlockSpec((None, tc, RNN_BW), lambda b, n, s: (b, s, COL_GATE // RNN_BW + n)),
            pl.BlockSpec((CONV_W, RNN_BW), lambda b, n, s: (0, n)),
            vec, mat, vec, mat, vec, vec,
        ],
        out_specs=pl.BlockSpec((None, tc, RNN_BW), lambda b, n, s: (b, s, n)),
        out_shape=jax.ShapeDtypeStruct((bsz, t, RNN_W), BF16),
        scratch_shapes=[pltpu.VMEM((tc + 8, RNN_BW), F32), pltpu.VMEM((8, RNN_BW), F32)],
        compiler_params=_params(("parallel", "parallel", "arbitrary")),
    )(proj3, proj3, conv_w, conv_b, wa, ba, wx, bx, lam)


def _compress_kernel(z_ref, pe_ref, w1_ref, w2_ref, norm_ref, seg_ref, o_ref, *, n_chunks, is_key):
    z = z_ref[...].astype(F32)
    first = _dot((z + pe_ref[0:1, :]).astype(BF16), w1_ref[0])
    second = _dot((z + pe_ref[1:2, :]).astype(BF16), w1_ref[1])
    pre = first + pltpu.roll(second, n_chunks - 1, 0)
    out = _dot(_gelu_tanh(pre).astype(BF16), w2_ref[...])
    if is_key:
        ssq = _seg_sum(out * out, seg_ref[...])
        normed = out * lax.rsqrt(ssq * (1.0 / HEAD_DIM) + NORM_EPS) * norm_ref[...]
        low = lax.broadcasted_iota(jnp.int32, (n_chunks, LANES), 1) < HEAD_DIM
        for pair in range(N_KV_GROUPS // 2):
            blk = normed[:, pair * LANES:(pair + 1) * LANES]
            o_ref[2 * pair] = jnp.where(low, blk, 0.0).astype(o_ref.dtype)
            o_ref[2 * pair + 1] = jnp.where(low, pltpu.roll(blk, HEAD_DIM, 1), 0.0).astype(o_ref.dtype)
    else:
        out_t = jnp.transpose(out)
        for g in range(N_KV_GROUPS):
            o_ref[g] = out_t[g * HEAD_DIM:(g + 1) * HEAD_DIM, :].astype(o_ref.dtype)


def _compress(z, pe, w1, w2, norm, seg, is_key):
    bsz, n_chunks, width = z.shape
    hid = w1.shape[-1]
    out_block = (N_KV_GROUPS, n_chunks, LANES) if is_key else (N_KV_GROUPS, HEAD_DIM, n_chunks)
    return pl.pallas_call(
        functools.partial(_compress_kernel, n_chunks=n_chunks, is_key=is_key),
        grid=(bsz,),
        in_specs=[
            pl.BlockSpec((None, n_chunks, width), lambda b: (b, 0, 0)),
            pl.BlockSpec((2, width), lambda b: (0, 0)),
            pl.BlockSpec((2, width, hid), lambda b: (0, 0, 0)),
            pl.BlockSpec((hid, KV_W), lambda b: (0, 0)),
            pl.BlockSpec((1, KV_W), lambda b: (0, 0)),
            pl.BlockSpec((KV_W, KV_W), lambda b: (0, 0)),
        ],
        out_specs=pl.BlockSpec((None,) + out_block, lambda b: (b, 0, 0, 0)),
        out_shape=jax.ShapeDtypeStruct((bsz,) + out_block, BF16),
        compiler_params=_params(("parallel",)),
    )(z, pe, w1, w2, norm, seg)


def _nsa_prep_kernel(q_ref, ks_ref, vs_ref, kw_ref, vw_ref, qn_ref, ksn_ref, kwn_ref, seg_ref,
                     qo_ref, ko_ref, vo_ref, *, tt):
    t0 = pl.program_id(1) * tt
    lane = lax.broadcasted_iota(jnp.int32, (tt, LANES), 1)
    row = lax.broadcasted_iota(jnp.int32, (tt, LANES), 0) + t0
    low = lane < HEAD_DIM
    seg = seg_ref[...]
    onehot = jnp.where((lane - SEL_ROW0) == row // SEL_BLOCK, 1.0, 0.0)

    def normed(ref, gain_ref, blk, scale):
        x = ref[:, blk * LANES:(blk + 1) * LANES].astype(F32)
        ssq = _seg_sum(x * x, seg)
        y = x * lax.rsqrt(ssq * (1.0 / HEAD_DIM) + NORM_EPS) * gain_ref[:, blk * LANES:(blk + 1) * LANES]
        return y * scale if scale != 1.0 else y

    for blk in range(Q_W // LANES):
        y = normed(q_ref, qn_ref, blk, HEAD_DIM ** -0.5 * LOG2E)
        ysw = pltpu.roll(y, HEAD_DIM, 1)
        qo_ref[2 * blk] = jnp.where(low, y, 0.0).astype(BF16)
        qo_ref[2 * blk + 1] = jnp.where(low, ysw, 0.0).astype(BF16)
    ones_row = jnp.where(lax.broadcasted_iota(jnp.int32, (V_ROWS - HEAD_DIM, tt), 0) == 0, 1.0, 0.0).astype(BF16)
    for blk in range(KV_W // LANES):
        y = normed(ks_ref, ksn_ref, blk, 1.0)
        ysw = pltpu.roll(y, HEAD_DIM, 1)
        ko_ref[2 * blk, 0] = jnp.where(low, y, onehot).astype(BF16)
        ko_ref[2 * blk + 1, 0] = jnp.where(low, ysw, onehot).astype(BF16)
        y = normed(kw_ref, kwn_ref, blk, 1.0)
        ysw = pltpu.roll(y, HEAD_DIM, 1)
        ko_ref[2 * blk, 1] = jnp.where(low, y, 0.0).astype(BF16)
        ko_ref[2 * blk + 1, 1] = jnp.where(low, ysw, 0.0).astype(BF16)
        for branch, src in enumerate((vs_ref, vw_ref)):
            v_t = jnp.transpose(src[:, blk * LANES:(blk + 1) * LANES].astype(F32))
            for half in range(2):
                g = 2 * blk + half
                vo_ref[g, branch, 0:HEAD_DIM, :] = v_t[half * HEAD_DIM:(half + 1) * HEAD_DIM, :].astype(BF16)
                vo_ref[g, branch, HEAD_DIM:V_ROWS, :] = ones_row


def _nsa_prep(proj3, qn, ksn, kwn, seg, tt):
    bsz, t, _ = proj3.shape

    def col(width, offset):
        return pl.BlockSpec((None, tt, width), lambda b, s: (b, s, offset // width))

    def vec(width):
        return pl.BlockSpec((1, width), lambda b, s: (0, 0))

    g = N_KV_GROUPS
    return pl.pallas_call(
        functools.partial(_nsa_prep_kernel, tt=tt),
        grid=(bsz, t // tt),
        in_specs=[col(Q_W, COL_Q), col(KV_W, COL_KS), col(KV_W, COL_VS), col(KV_W, COL_KW), col(KV_W, COL_VW),
                  vec(Q_W), vec(KV_W), vec(KV_W), pl.BlockSpec((LANES, LANES), lambda b, s: (0, 0))],
        out_specs=[pl.BlockSpec((None, N_HEADS, tt, LANES), lambda b, s: (b, 0, s, 0)),
                   pl.BlockSpec((None, g, 2, tt, LANES), lambda b, s: (b, 0, 0, s, 0)),
                   pl.BlockSpec((None, g, 2, V_ROWS, tt), lambda b, s: (b, 0, 0, 0, s))],
        out_shape=[jax.ShapeDtypeStruct((bsz, N_HEADS, t, LANES), BF16),
                   jax.ShapeDtypeStruct((bsz, g, 2, t, LANES), BF16),
                   jax.ShapeDtypeStruct((bsz, g, 2, V_ROWS, t), BF16)],
        compiler_params=_params(("parallel", "parallel")),
    )(proj3, proj3, proj3, proj3, proj3, qn, ksn, kwn, seg)


def _nsa_kernel(q_ref, kc_ref, vc_ref, k_ref, v_ref, bc_ref, bias_ref, cover_ref, gate_ref, o_ref,
                qq_ref, s0_ref, s1_ref, smax0_ref, smax1_ref, p0_ref, p1_ref, *state_refs, tq, n_sblk):
    qi = pl.program_id(2)
    t0 = qi * tq
    rg = HEADS_PER_GROUP
    m_refs, acc_refs = state_refs[:rg], state_refs[rg:]
    sel, win = 0, 1


    kc = kc_ref[...]
    vc_t = vc_ref[...]
    cover = cover_ref[...]
    o_cmp = []
    imp_t = jnp.zeros((LANES, tq), F32)
    for r in range(rg):
        bias = bc_ref[r]
        s = _dot_nt(kc, q_ref[r]) + bias
        visible = bias > 0.5 * MASK_NEG
        m = jnp.max(s, axis=0, keepdims=True)
        p = jnp.exp2(s - m)
        p = jnp.where(visible, p / jnp.sum(p, axis=0, keepdims=True), 0.0)
        p_hi = p.astype(BF16)
        p_lo = (p - p_hi.astype(F32)).astype(BF16)
        o_cmp.append(_dot(vc_t, p_hi))
        imp_t = imp_t + _dot(cover, p_hi) + _dot(cover, p_lo)

    score = imp_t[SEL_ROW0:SEL_ROW0 + MAX_SBLK, :]
    jrow = lax.broadcasted_iota(jnp.int32, (MAX_SBLK, tq), 0)
    qblk = (lax.broadcasted_iota(jnp.int32, (MAX_SBLK, tq), 1) + t0) // SEL_BLOCK
    causal = jrow <= qblk
    forced = causal & ((jrow == 0) | (jrow >= qblk - 1))
    score = jnp.where(forced, SEL_FORCED, jnp.where(causal, score, -1.0))
    rank = jnp.zeros((MAX_SBLK, tq), F32)
    for j in range(n_sblk):
        other = score[j:j + 1, :]
        ahead = (other > score) | ((other == score) & (jrow > j))
        rank = rank + jnp.where(ahead, 1.0, 0.0)
    n_top = min(N_SELECT, n_sblk)
    selected = (rank < n_top) & (score >= 0.0)
    neg_t = jnp.where(selected, 0.0, SEL_NEG)
    neg_full = jnp.concatenate(
        [jnp.zeros((SEL_ROW0, tq), F32), neg_t, jnp.zeros((LANES - SEL_ROW0 - MAX_SBLK, tq), F32)], axis=0)
    neg = jnp.transpose(neg_full)
    for r in range(rg):
        qq_ref[sel, r] = (q_ref[r].astype(F32) + neg).astype(BF16)
        qq_ref[win, r] = q_ref[r]

    n_near = WINDOW // tq + 1

    def tile(branch, a, kind):
        real = a <= qi
        return branch, pl.multiple_of(jnp.where(real, qi - a, 0) * tq, tq), jnp.where(real, kind, BIAS_MASKED)

    def near(a):
        return (tile(sel, a, (BIAS_DIAG, BIAS_NEAR, BIAS_ZERO)[a]), tile(win, a, (BIAS_DIAG, BIAS_NEAR, BIAS_WIN)[a]))

    def far(j):
        return tile(sel, n_near + 2 * j, BIAS_ZERO), tile(sel, n_near + 2 * j + 1, BIAS_ZERO)

    slots = ((s0_ref, smax0_ref, p0_ref), (s1_ref, smax1_ref, p1_ref))

    def scores(pair):
        for (branch, k0, kind), (s_ref, smax_ref, _) in zip(pair, slots):
            k = k_ref[branch, pl.ds(k0, tq), :]
            for r in range(rg):
                s = _dot_nt(k, qq_ref[branch, r]) + bias_ref[r, kind]
                s_ref[r] = s
                smax_ref[r] = jnp.max(s, axis=0, keepdims=True)

    def softmax(pair):
        branch_a, branch_b = pair[0][0], pair[1][0]
        for r in range(rg):
            m_ref, acc_ref = m_refs[r], acc_refs[r]
            if branch_a == branch_b:
                m_old = m_ref[branch_a]
                m_new = jnp.maximum(m_old, jnp.maximum(smax0_ref[r], smax1_ref[r]))
                p0_ref[r] = jnp.exp2(s0_ref[r] - m_new).astype(BF16)
                p1_ref[r] = jnp.exp2(s1_ref[r] - m_new).astype(BF16)
                m_ref[branch_a] = m_new
                acc_ref[branch_a] = jnp.exp2(m_old - m_new) * acc_ref[branch_a]
            else:
                for (branch, _, _), (s_ref, smax_ref, p_ref) in zip(pair, slots):
                    m_old = m_ref[branch]
                    m_new = jnp.maximum(m_old, smax_ref[r])
                    p_ref[r] = jnp.exp2(s_ref[r] - m_new).astype(BF16)
                    m_ref[branch] = m_new
                    acc_ref[branch] = jnp.exp2(m_old - m_new) * acc_ref[branch]

    def values(pair):
        for (branch, k0, _), (_, _, p_ref) in zip(pair, slots):
            v_t = v_ref[branch, :, pl.ds(k0, tq)]
            for r in range(rg):
                acc_refs[r][branch] += _dot(v_t, p_ref[r])

    for r in range(rg):
        m_refs[r][...] = jnp.full(m_refs[r].shape, MASK_NEG, F32)
        acc_refs[r][...] = jnp.zeros(acc_refs[r].shape, F32)
    n_far = jnp.maximum(qi + 1 - n_near + 1, 0) // 2
    scores(near(0))
    softmax(near(0))
    scores(near(1))
    values(near(0))
    softmax(near(1))
    scores(near(2))
    values(near(1))
    softmax(near(2))

    @pl.when(n_far == 0)
    def _():
        values(near(2))

    @pl.when(n_far == 1)
    def _():
        scores(far(0))
        values(near(2))
        softmax(far(0))
        values(far(0))

    @pl.when(n_far >= 2)
    def _():
        scores(far(0))
        values(near(2))
        softmax(far(0))
        scores(far(1))

        def trip(j, carry):
            values(far(j - 1))
            softmax(far(j))
            scores(far(j + 1))
            return carry

        lax.fori_loop(1, n_far - 1, trip, 0)
        values(far(n_far - 2))
        softmax(far(n_far - 1))
        values(far(n_far - 1))

    def finish(branch):
        return [acc_refs[r][branch, 0:HEAD_DIM, :] / acc_refs[r][branch, HEAD_DIM:HEAD_DIM + 1, :] for r in range(rg)]

    o_sel = finish(sel)
    o_win = finish(win)

    gates_t = jnp.transpose(_sigmoid(gate_ref[...].astype(F32)))
    outs = []
    for r in range(rg):
        outs.append(gates_t[3 * r:3 * r + 1, :] * o_cmp[r]
                    + gates_t[3 * r + 1:3 * r + 2, :] * o_sel[r]
                    + gates_t[3 * r + 2:3 * r + 3, :] * o_win[r])
    o_ref[...] = jnp.transpose(jnp.concatenate(outs, axis=0)).astype(o_ref.dtype)


def _nsa_attention(qh, kc, vc, k_all, v_all, bias_c, bias_tiles, cover, proj3, tq):
    bsz, _, t, _ = qh.shape
    n_chunks = kc.shape[2]
    rg = HEADS_PER_GROUP
    n_kinds = bias_tiles.shape[1]
    return pl.pallas_call(
        functools.partial(_nsa_kernel, tq=tq, n_sblk=t // SEL_BLOCK),
        grid=(bsz, N_KV_GROUPS, t // tq),
        in_specs=[
            pl.BlockSpec((None, rg, tq, LANES), lambda b, g, i: (b, g, i, 0)),
            pl.BlockSpec((None, None, n_chunks, LANES), lambda b, g, i: (b, g, 0, 0)),
            pl.BlockSpec((None, None, HEAD_DIM, n_chunks), lambda b, g, i: (b, g, 0, 0)),
            pl.BlockSpec((None, None, 2, t, LANES), lambda b, g, i: (b, g, 0, 0, 0)),
            pl.BlockSpec((None, None, 2, V_ROWS, t), lambda b, g, i: (b, g, 0, 0, 0)),
            pl.BlockSpec((rg, n_chunks, tq), lambda b, g, i: (g, 0, i)),
            pl.BlockSpec((rg, n_kinds, tq, tq), lambda b, g, i: (g, 0, 0, 0)),
            pl.BlockSpec((LANES, n_chunks), lambda b, g, i: (0, 0)),
            pl.BlockSpec((None, tq, LANES), lambda b, g, i: (b, i, COL_GN // LANES + g)),
        ],
        out_specs=pl.BlockSpec((None, tq, rg * HEAD_DIM), lambda b, g, i: (b, i, g)),
        out_shape=jax.ShapeDtypeStruct((bsz, t, Q_W), BF16),
        scratch_shapes=[
            pltpu.VMEM((2, rg, tq, LANES), BF16),
            pltpu.VMEM((rg, tq, tq), F32), pltpu.VMEM((rg, tq, tq), F32),
            pltpu.VMEM((rg, 1, tq), F32), pltpu.VMEM((rg, 1, tq), F32),
            pltpu.VMEM((rg, tq, tq), BF16), pltpu.VMEM((rg, tq, tq), BF16),
        ] + [pltpu.VMEM((2, 1, tq), F32)] * rg
          + [pltpu.VMEM((2, V_ROWS, tq), F32)] * rg,
        compiler_params=_params(("parallel", "parallel", "arbitrary")),
    )(qh, kc, vc, k_all, v_all, bias_c, bias_tiles, cover, proj3)


def _merge_kernel(ya_ref, yb_ref, ga_ref, gb_ref, x_ref, pa_ref, pb_ref, wo_ref, h_ref):
    merged = (_sigmoid(ga_ref[...].astype(F32)) * _dot(ya_ref[...], pa_ref[...])
              + _sigmoid(gb_ref[...].astype(F32)) * _dot(yb_ref[...], pb_ref[...]))
    h_ref[...] = x_ref[...] + _dot(merged.astype(BF16), wo_ref[...])


def _merge_out(ya, yb, proj, x, pa, pb, wo, tm):
    m = x.shape[0]

    def rows(width, offset=0):
        return pl.BlockSpec((tm, width), lambda i: (i, offset // width))

    def whole(a):
        return pl.BlockSpec(a.shape, lambda i: (0, 0))

    return pl.pallas_call(
        _merge_kernel,
        grid=(m // tm,),
        in_specs=[rows(RNN_W), rows(Q_W), rows(D_MODEL, COL_GA), rows(D_MODEL, COL_GB), rows(D_MODEL),
                  whole(pa), whole(pb), whole(wo)],
        out_specs=rows(D_MODEL),
        out_shape=jax.ShapeDtypeStruct((m, D_MODEL), F32),
        compiler_params=_params(("parallel",)),
    )(ya, yb, proj, proj, x, pa, pb, wo)


def _mlp_kernel(h_ref, g_ref, w1_ref, w2_ref, o_ref, hn_ref, acc_ref):
    j = pl.program_id(1)

    @pl.when(j == 0)
    def _():
        h = h_ref[...]
        y = h * lax.rsqrt(jnp.mean(h * h, axis=-1, keepdims=True) + NORM_EPS) * g_ref[...]
        hn_ref[...] = y.astype(BF16)
        acc_ref[...] = h

    z = jnp.maximum(_dot(hn_ref[...], w1_ref[...]), 0.0)
    acc_ref[...] += _dot((z * z).astype(BF16), w2_ref[...])

    @pl.when(j == pl.num_programs(1) - 1)
    def _():
        o_ref[...] = acc_ref[...]


def _mlp(h, gain, w1, w2, tm, tf):
    m, d = h.shape
    ff = w1.shape[1]
    return pl.pallas_call(
        _mlp_kernel,
        grid=(m // tm, ff // tf),
        in_specs=[
            pl.BlockSpec((tm, d), lambda i, j: (i, 0)),
            pl.BlockSpec((1, d), lambda i, j: (0, 0)),
            pl.BlockSpec((d, tf), lambda i, j: (0, j)),
            pl.BlockSpec((tf, d), lambda i, j: (j, 0)),
        ],
        out_specs=pl.BlockSpec((tm, d), lambda i, j: (i, 0)),
        out_shape=jax.ShapeDtypeStruct((m, d), F32),
        scratch_shapes=[pltpu.VMEM((tm, d), BF16), pltpu.VMEM((tm, d), F32)],
        compiler_params=_params(("parallel", "arbitrary")),
    )(h, gain, w1, w2)


def _t5_bucket_table():
    max_exact = REL_BUCKETS // 2
    d = np.arange(FAR_DIST)
    df = np.maximum(d.astype(np.float32), np.float32(1.0))
    large = max_exact + (np.log(df / np.float32(max_exact)) / np.float32(math.log(REL_MAX_DIST / max_exact))
                         * np.float32(REL_BUCKETS - max_exact)).astype(np.int32)
    large = np.minimum(large, REL_BUCKETS - 1)
    return np.where(d < max_exact, d, large).astype(np.int32)


def _pad_blocks(w, axis):
    shape = w.shape
    w = w.reshape(shape[:axis] + (LRU_BLOCKS, LRU_BLOCK_W) + shape[axis + 1:])
    pad = [(0, 0)] * w.ndim
    pad[axis + 1] = (0, RNN_BW - LRU_BLOCK_W)
    w = jnp.pad(w, pad)
    return w.reshape(shape[:axis] + (RNN_W,) + shape[axis + 1:])


def _in_proj_weight(w_in):
    cuts = np.cumsum((D_RNN, D_RNN, Q_W, KV_W, KV_W, KV_W, KV_W, KV_W, KV_W, 3 * N_HEADS, D_MODEL))
    (w_u, w_gate, w_q, w_kc, w_vc, w_ks, w_vs, w_kw, w_vw, w_gn, w_ga, w_gb) = jnp.split(w_in, cuts, axis=1)
    per_group = 3 * HEADS_PER_GROUP
    w_gn = jnp.pad(w_gn.reshape(D_MODEL, N_KV_GROUPS, per_group), ((0, 0), (0, 0), (0, LANES - per_group)))
    w_gn = w_gn.reshape(D_MODEL, N_KV_GROUPS * LANES)
    gap = jnp.zeros((D_MODEL, COL_GA - COL_GN - N_KV_GROUPS * LANES), w_in.dtype)
    w = jnp.concatenate([_pad_blocks(w_u, 1), _pad_blocks(w_gate, 1), w_q, w_ks, w_vs, w_kw, w_vw, w_kc, w_vc,
                         w_gn, gap, w_ga, w_gb], axis=1)
    assert w.shape[1] == D_PROJ
    return w.astype(BF16)


def _phi_weights(pe, w1, w2):
    half = CMP_BLOCK // 2
    eye = jnp.eye(N_KV_GROUPS, dtype=w1.dtype)
    w1h = w1.reshape(2, half, HEAD_DIM, PHI_HIDDEN)
    w1e = jnp.einsum("xldh,gk->xlgdkh", w1h, eye).reshape(2, half * KV_W, N_KV_GROUPS * PHI_HIDDEN)
    w2e = jnp.einsum("hd,gk->ghkd", w2, eye).reshape(N_KV_GROUPS * PHI_HIDDEN, KV_W)
    pee = jnp.broadcast_to(pe.reshape(2, half, 1, HEAD_DIM), (2, half, N_KV_GROUPS, HEAD_DIM)).reshape(2, half * KV_W)
    return pee, w1e.astype(BF16), w2e.astype(BF16)


def kernel(x, norm_mix, w_in, conv_w, conv_b, gate_a_w, gate_a_b, gate_x_w, gate_x_b, lru_lambda, phi_k_pe, phi_k_w1, phi_k_w2, phi_v_pe, phi_v_w1, phi_v_w2, q_norm, kc_norm, ks_norm, kw_norm, rel_bias, proj_a, proj_b, w_out, norm_mlp, w_mlp_in, w_mlp_out):
    bsz, t, d = x.shape
    assert d == D_MODEL and norm_mix.shape[0] == 1
    tq = 256
    assert t % tq == 0 and t // SEL_BLOCK <= MAX_SBLK and t % CMP_STRIDE == 0
    n_tok = bsz * t
    n_chunks = t // CMP_STRIDE
    assert n_chunks % 8 == 0 and n_chunks <= LANES
    x2 = x.reshape(n_tok, d)

    proj = _norm_matmul(x2, norm_mix, _in_proj_weight(w_in[0]), tm=1024 if n_tok % 1024 == 0 else tq, tn=512)
    proj3 = proj.reshape(bsz, t, D_PROJ)

    pad_w = lambda w: jnp.pad(w, ((0, 0), (0, RNN_BW - LRU_BLOCK_W), (0, RNN_BW - LRU_BLOCK_W))).astype(BF16)
    ya = _rglru(proj3, _pad_blocks(conv_w[0], 1), _pad_blocks(conv_b, 1),
                pad_w(gate_a_w[0]), _pad_blocks(gate_a_b.reshape(1, D_RNN), 1),
                pad_w(gate_x_w[0]), _pad_blocks(gate_x_b.reshape(1, D_RNN), 1),
                _pad_blocks(lru_lambda, 1), tc=512 if t % 512 == 0 else tq)

    zk = proj[:, COL_KC:COL_KC + KV_W].reshape(bsz, n_chunks, CMP_STRIDE * KV_W)
    zv = proj[:, COL_VC:COL_VC + KV_W].reshape(bsz, n_chunks, CMP_STRIDE * KV_W)
    pe_k, w1_k, w2_k = _phi_weights(phi_k_pe[0], phi_k_w1[0], phi_k_w2[0])
    pe_v, w1_v, w2_v = _phi_weights(phi_v_pe[0], phi_v_w1[0], phi_v_w2[0])
    seg256 = jnp.asarray(np.kron(np.eye(KV_W // HEAD_DIM), np.ones((HEAD_DIM, HEAD_DIM))), BF16)
    kcn = jnp.tile(kc_norm, (1, N_KV_GROUPS))
    kc = _compress(zk, pe_k, w1_k, w2_k, kcn, seg256, is_key=True)
    vc = _compress(zv, pe_v, w1_v, w2_v, kcn, seg256, is_key=False)

    seg128 = jnp.asarray(np.kron(np.eye(LANES // HEAD_DIM), np.ones((HEAD_DIM, HEAD_DIM))), BF16)
    qh, k_all, v_all = _nsa_prep(proj3, jnp.tile(q_norm, (1, N_HEADS)), jnp.tile(ks_norm, (1, N_KV_GROUPS)),
                                 jnp.tile(kw_norm, (1, N_KV_GROUPS)), seg128, tt=tq)

    bucket_of = _t5_bucket_table()
    far = rel_bias[REL_BUCKETS - 1][:, None, None]

    def bias_table(dist, valid, shift):
        buckets = jnp.asarray(bucket_of[np.clip(dist, 0, FAR_DIST - 1)].astype(np.int8))
        onehot = (buckets[None] == jnp.arange(REL_BUCKETS, dtype=jnp.int8).reshape(-1, 1, 1)).astype(F32)
        vals = jnp.einsum("kh,kji->hji", rel_bias, onehot, precision=lax.Precision.HIGHEST)
        if shift:
            vals = vals - far
        return jnp.where(jnp.asarray(valid), vals * LOG2E, MASK_NEG).astype(F32)

    kj = np.arange(tq)[:, None]
    qi_ = np.arange(tq)[None, :]
    bias_diag = bias_table(qi_ - kj, qi_ >= kj, True)
    bias_near = bias_table(tq + qi_ - kj, np.ones((tq, tq), bool), True)
    shared = jnp.asarray(np.stack([np.zeros((tq, tq)), np.where(kj > qi_, 0.0, MASK_NEG), np.full((tq, tq), MASK_NEG)]), F32)
    shared = jnp.broadcast_to(shared[None], (N_HEADS, 3, tq, tq))
    bias_tiles = jnp.concatenate([shared[:, 0:1], bias_diag[:, None], bias_near[:, None], shared[:, 1:3]], axis=1)
    assert bias_tiles.shape[1] == BIAS_MASKED + 1 and WINDOW == 2 * tq
    cidx = np.arange(n_chunks)[:, None]
    dist_c = np.arange(t)[None, :] - (cidx * CMP_STRIDE + CMP_BLOCK - 1)
    bias_c = bias_table(dist_c, (dist_c >= 0) & (cidx < n_chunks - 1), False)
    cstart = np.arange(n_chunks) * CMP_STRIDE
    sj = np.arange(MAX_SBLK)
    cov = ((cstart[None, :] < (sj[:, None] + 1) * SEL_BLOCK) & (cstart[None, :] + CMP_BLOCK - 1 >= sj[:, None] * SEL_BLOCK)
           & (np.arange(n_chunks)[None, :] < n_chunks - 1))
    cover = np.zeros((LANES, n_chunks), np.float32)
    cover[SEL_ROW0:SEL_ROW0 + MAX_SBLK] = cov
    yb = _nsa_attention(qh, kc, vc, k_all, v_all, bias_c, bias_tiles, jnp.asarray(cover, BF16), proj3, tq)

    pa = jnp.pad(proj_a[0].reshape(LRU_BLOCKS, LRU_BLOCK_W, D_MODEL), ((0, 0), (0, RNN_BW - LRU_BLOCK_W), (0, 0)))
    h = _merge_out(ya.reshape(n_tok, RNN_W), yb.reshape(n_tok, Q_W), proj, x2,
                   pa.reshape(RNN_W, D_MODEL).astype(BF16), proj_b[0].astype(BF16), w_out[0].astype(BF16),
                   tm=512 if n_tok % 512 == 0 else tq)

    out = _mlp(h, norm_mlp, w_mlp_in[0].astype(BF16), w_mlp_out[0].astype(BF16),
               tm=1024 if n_tok % 1024 == 0 else tq, tf=512)
    return out.reshape(bsz, t, d)
```

```python
import functools
import math

import numpy as np
import jax
import jax.numpy as jnp
from jax import lax
from jax.experimental import pallas as pl
from jax.experimental.pallas import tpu as pltpu

F32 = jnp.float32
BF16 = jnp.bfloat16

D_MODEL = 1024
D_RNN = 1344
LRU_BLOCKS = 4
LRU_BLOCK_W = D_RNN // LRU_BLOCKS
CONV_W = 4
LRU_C = 8.0
N_HEADS = 16
HEAD_DIM = 64
N_KV_GROUPS = 4
HEADS_PER_GROUP = N_HEADS // N_KV_GROUPS
CMP_BLOCK = 32
CMP_STRIDE = 16
SEL_BLOCK = 64
N_SELECT = 16
WINDOW = 512
PHI_HIDDEN = 256
SEL_FORCED = 1e4
REL_BUCKETS = 32
REL_MAX_DIST = 128
D_FF = 4 * D_MODEL
NORM_EPS = 1e-6
Q_W = N_HEADS * HEAD_DIM
KV_W = N_KV_GROUPS * HEAD_DIM

LANES = 128
VMEM_LIMIT = 56 * 1024 * 1024

RNN_BW = 384
RNN_W = LRU_BLOCKS * RNN_BW
MASK_NEG = -1e30
SEL_NEG = -1e9
SEL_ROW0 = 64
MAX_SBLK = 32
FAR_DIST = 256
LOG2E = math.log2(math.e)
V_ROWS = HEAD_DIM + 16
BIAS_ZERO, BIAS_DIAG, BIAS_NEAR, BIAS_WIN, BIAS_MASKED = range(5)
JOBS_PER_TRIP = 2

COL_U = 0
COL_GATE = RNN_W
COL_Q = 2 * RNN_W
COL_KS = COL_Q + Q_W
COL_VS = COL_KS + KV_W
COL_KW = COL_VS + KV_W
COL_VW = COL_KW + KV_W
COL_KC = COL_VW + KV_W
COL_VC = COL_KC + KV_W
COL_GN = COL_VC + KV_W
COL_GA = 6144
COL_GB = 7168
D_PROJ = 8192


def _dot(a, b):
    return jnp.dot(a, b, preferred_element_type=F32)


def _dot_nt(a, b):
    return lax.dot_general(a, b, (((1,), (1,)), ((), ())), preferred_element_type=F32)


def _gelu_tanh(x):
    return 0.5 * x * (1.0 + jnp.tanh(math.sqrt(2.0 / math.pi) * (x + 0.044715 * (x * x * x))))


def _sigmoid(x):
    return 1.0 / (1.0 + jnp.exp(-x))


def _seg_sum(x, seg_ones):
    hi = x.astype(BF16)
    lo = (x - hi.astype(F32)).astype(BF16)
    return _dot(hi, seg_ones) + _dot(lo, seg_ones)


def _params(sem, flags=None):
    return pltpu.CompilerParams(dimension_semantics=sem, vmem_limit_bytes=VMEM_LIMIT, flags=flags)


def _norm_matmul_kernel(x_ref, g_ref, w_ref, o_ref, xn_ref):
    @pl.when(pl.program_id(1) == 0)
    def _():
        x = x_ref[...]
        y = x * lax.rsqrt(jnp.mean(x * x, axis=-1, keepdims=True) + NORM_EPS) * g_ref[...]
        xn_ref[...] = y.astype(BF16)

    o_ref[...] = _dot(xn_ref[...], w_ref[...]).astype(o_ref.dtype)


def _norm_matmul(x, gain, w, bsz, tm, tn):
    m, k = x.shape
    n = w.shape[1]
    t = m // bsz
    tiles = t // tm
    col_tiles = n // tn
    return pl.pallas_call(
        _norm_matmul_kernel,
        grid=(m // tm, n // tn),
        in_specs=[
            pl.BlockSpec((tm, k), lambda i, j: (i, 0)),
            pl.BlockSpec((1, k), lambda i, j: (0, 0)),
            pl.BlockSpec((k, tn), lambda i, j: (0, j)),
        ],
        out_specs=pl.BlockSpec((tm, tn), lambda i, j: (i % tiles, (i // tiles) * col_tiles + j)),
        out_shape=jax.ShapeDtypeStruct((t, bsz * n), BF16),
        scratch_shapes=[pltpu.VMEM((tm, k), BF16)],
        compiler_params=_params(("parallel", "arbitrary")),
    )(x, gain, w)


def _rglru_kernel(u_ref, ug_ref, cw_ref, cb_ref, wa_ref, ba_ref, wx_ref, bx_ref, lam_ref,
                  y_ref, ubuf, a_scr, b_scr, hcar, *, tc, bsz):
    t = pl.program_id(1)
    rows = tc * bsz
    halo = (CONV_W - 1) * bsz

    @pl.when(t == 0)
    def _():
        ubuf[0:halo, :] = jnp.zeros((halo, RNN_BW), F32)
        hcar[...] = jnp.zeros_like(hcar)

    u = u_ref[...].astype(F32).reshape(rows, RNN_BW)
    ubuf[halo:halo + rows, :] = u
    xc = cb_ref[...]
    for k in range(CONV_W):
        xc = xc + cw_ref[k:k + 1, :] * ubuf[k * bsz:k * bsz + rows, :]
    ubuf[0:halo, :] = u[rows - halo:rows, :]

    xb = xc.astype(BF16)
    r = _sigmoid(_dot(xb, wa_ref[...]) + ba_ref[...])
    i = _sigmoid(_dot(xb, wx_ref[...]) + bx_ref[...])
    z = -lam_ref[...]
    softplus = jnp.maximum(z, 0.0) + jnp.log(1.0 + jnp.exp(-jnp.abs(z)))
    log_a = (-LRU_C) * r * softplus
    mult = jnp.sqrt(1.0 - jnp.exp(2.0 * log_a))
    row = lax.broadcasted_iota(jnp.int32, (rows, 1), 0)
    mult = jnp.where((row < bsz) & (t == 0), 1.0, mult)
    a_scr[...] = jnp.exp(log_a)
    b_scr[...] = mult * (i * xc)

    def step(s, h):
        at = pl.ds(pl.multiple_of(s * bsz, bsz), bsz)
        h = a_scr[at, :] * h + b_scr[at, :]
        b_scr[at, :] = h
        return h

    hcar[...] = lax.fori_loop(0, tc, step, hcar[...], unroll=8)
    y = b_scr[...] * _gelu_tanh(ug_ref[...].astype(F32).reshape(rows, RNN_BW))
    y_ref[...] = y.reshape(tc, bsz, RNN_BW).astype(y_ref.dtype)


def _rglru(proj3, conv_w, conv_b, wa, ba, wx, bx, lam, tc):
    t, bsz, _ = proj3.shape
    assert bsz % 8 == 0, "the recurrence advances whole sublane groups of sequences"
    nb = LRU_BLOCKS
    rows = tc * bsz
    vec = pl.BlockSpec((1, RNN_BW), lambda n, s: (0, n))
    mat = pl.BlockSpec((None, RNN_BW, RNN_BW), lambda n, s: (n, 0, 0))
    return pl.pallas_call(
        functools.partial(_rglru_kernel, tc=tc, bsz=bsz),
        grid=(nb, t // tc),
        in_specs=[
            pl.BlockSpec((tc, bsz, RNN_BW), lambda n, s: (s, 0, COL_U // RNN_BW + n)),
            pl.BlockSpec((tc, bsz, RNN_BW), lambda n, s: (s, 0, COL_GATE // RNN_BW + n)),
            pl.BlockSpec((CONV_W, RNN_BW), lambda n, s: (0, n)),
            vec, mat, vec, mat, vec, vec,
        ],
        out_specs=pl.BlockSpec((tc, bsz, RNN_BW), lambda n, s: (s, 0, n)),
        out_shape=jax.ShapeDtypeStruct((t, bsz, RNN_W), BF16),
        scratch_shapes=[pltpu.VMEM((rows + (CONV_W - 1) * bsz, RNN_BW), F32),
                        pltpu.VMEM((rows, RNN_BW), F32), pltpu.VMEM((rows, RNN_BW), F32),
                        pltpu.VMEM((bsz, RNN_BW), F32)],
        compiler_params=_params(("parallel", "arbitrary")),
    )(proj3, proj3, conv_w, conv_b, wa, ba, wx, bx, lam)


def _compress_kernel(z_ref, pe_ref, w1_ref, w2_ref, norm_ref, seg_ref, o_ref, *, n_chunks, is_key):
    z = z_ref[...].astype(F32)
    first = _dot((z + pe_ref[0:1, :]).astype(BF16), w1_ref[0])
    second = _dot((z + pe_ref[1:2, :]).astype(BF16), w1_ref[1])
    pre = first + pltpu.roll(second, n_chunks - 1, 0)
    out = _dot(_gelu_tanh(pre).astype(BF16), w2_ref[...])
    if is_key:
        ssq = _seg_sum(out * out, seg_ref[...])
        normed = out * lax.rsqrt(ssq * (1.0 / HEAD_DIM) + NORM_EPS) * norm_ref[...]
        low = lax.broadcasted_iota(jnp.int32, (n_chunks, LANES), 1) < HEAD_DIM
        for pair in range(N_KV_GROUPS // 2):
            blk = normed[:, pair * LANES:(pair + 1) * LANES]
            o_ref[2 * pair] = jnp.where(low, blk, 0.0).astype(o_ref.dtype)
            o_ref[2 * pair + 1] = jnp.where(low, pltpu.roll(blk, HEAD_DIM, 1), 0.0).astype(o_ref.dtype)
    else:
        out_t = jnp.transpose(out)
        for g in range(N_KV_GROUPS):
            o_ref[g] = out_t[g * HEAD_DIM:(g + 1) * HEAD_DIM, :].astype(o_ref.dtype)


def _compress(z, pe, w1, w2, norm, seg, is_key):
    bsz, n_chunks, width = z.shape
    hid = w1.shape[-1]
    out_block = (N_KV_GROUPS, n_chunks, LANES) if is_key else (N_KV_GROUPS, HEAD_DIM, n_chunks)
    return pl.pallas_call(
        functools.partial(_compress_kernel, n_chunks=n_chunks, is_key=is_key),
        grid=(bsz,),
        in_specs=[
            pl.BlockSpec((None, n_chunks, width), lambda b: (b, 0, 0)),
            pl.BlockSpec((2, width), lambda b: (0, 0)),
            pl.BlockSpec((2, width, hid), lambda b: (0, 0, 0)),
            pl.BlockSpec((hid, KV_W), lambda b: (0, 0)),
            pl.BlockSpec((1, KV_W), lambda b: (0, 0)),
            pl.BlockSpec((KV_W, KV_W), lambda b: (0, 0)),
        ],
        out_specs=pl.BlockSpec((None,) + out_block, lambda b: (b, 0, 0, 0)),
        out_shape=jax.ShapeDtypeStruct((bsz,) + out_block, BF16),
        compiler_params=_params(("parallel",)),
    )(z, pe, w1, w2, norm, seg)


def _nsa_prep_kernel(q_ref, ks_ref, vs_ref, kw_ref, vw_ref, qn_ref, ksn_ref, kwn_ref, seg_ref,
                     qo_ref, ko_ref, vo_ref, *, tt):
    t0 = pl.program_id(1) * tt
    lane = lax.broadcasted_iota(jnp.int32, (tt, LANES), 1)
    row = lax.broadcasted_iota(jnp.int32, (tt, LANES), 0) + t0
    low = lane < HEAD_DIM
    seg = seg_ref[...]
    onehot = jnp.where((lane - SEL_ROW0) == row // SEL_BLOCK, 1.0, 0.0)

    def normed(ref, gain_ref, blk, scale):
        x = ref[:, blk * LANES:(blk + 1) * LANES].astype(F32)
        ssq = _seg_sum(x * x, seg)
        y = x * lax.rsqrt(ssq * (1.0 / HEAD_DIM) + NORM_EPS) * gain_ref[:, blk * LANES:(blk + 1) * LANES]
        return y * scale if scale != 1.0 else y

    q_pad = jnp.zeros((LANES - HEAD_DIM, tt), BF16)
    for blk in range(Q_W // LANES):
        y_t = jnp.transpose(normed(q_ref, qn_ref, blk, HEAD_DIM ** -0.5 * LOG2E)).astype(BF16)
        for half in range(2):
            qo_ref[2 * blk + half, 0:HEAD_DIM, :] = y_t[half * HEAD_DIM:(half + 1) * HEAD_DIM, :]
            qo_ref[2 * blk + half, HEAD_DIM:LANES, :] = q_pad
    ones_row = jnp.where(lax.broadcasted_iota(jnp.int32, (V_ROWS - HEAD_DIM, tt), 0) == 0, 1.0, 0.0).astype(BF16)
    for blk in range(KV_W // LANES):
        y = normed(ks_ref, ksn_ref, blk, 1.0)
        ysw = pltpu.roll(y, HEAD_DIM, 1)
        ko_ref[2 * blk, 0] = jnp.where(low, y, onehot).astype(BF16)
        ko_ref[2 * blk + 1, 0] = jnp.where(low, ysw, onehot).astype(BF16)
        y = normed(kw_ref, kwn_ref, blk, 1.0)
        ysw = pltpu.roll(y, HEAD_DIM, 1)
        ko_ref[2 * blk, 1] = jnp.where(low, y, 0.0).astype(BF16)
        ko_ref[2 * blk + 1, 1] = jnp.where(low, ysw, 0.0).astype(BF16)
        for branch, src in enumerate((vs_ref, vw_ref)):
            v_t = jnp.transpose(src[:, blk * LANES:(blk + 1) * LANES].astype(F32))
            for half in range(2):
                g = 2 * blk + half
                vo_ref[g, branch, 0:HEAD_DIM, :] = v_t[half * HEAD_DIM:(half + 1) * HEAD_DIM, :].astype(BF16)
                vo_ref[g, branch, HEAD_DIM:V_ROWS, :] = ones_row


def _nsa_prep(proj_t, bsz, qn, ksn, kwn, seg, tt):
    t = proj_t.shape[0]

    def col(width, offset):
        return pl.BlockSpec((tt, width), lambda b, s: (s, (b * D_PROJ + offset) // width))

    def vec(width):
        return pl.BlockSpec((1, width), lambda b, s: (0, 0))

    g = N_KV_GROUPS
    return pl.pallas_call(
        functools.partial(_nsa_prep_kernel, tt=tt),
        grid=(bsz, t // tt),
        in_specs=[col(Q_W, COL_Q), col(KV_W, COL_KS), col(KV_W, COL_VS), col(KV_W, COL_KW), col(KV_W, COL_VW),
                  vec(Q_W), vec(KV_W), vec(KV_W), pl.BlockSpec((LANES, LANES), lambda b, s: (0, 0))],
        out_specs=[pl.BlockSpec((None, N_HEADS, LANES, tt), lambda b, s: (b, 0, 0, s)),
                   pl.BlockSpec((None, g, 2, tt, LANES), lambda b, s: (b, 0, 0, s, 0)),
                   pl.BlockSpec((None, g, 2, V_ROWS, tt), lambda b, s: (b, 0, 0, 0, s))],
        out_shape=[jax.ShapeDtypeStruct((bsz, N_HEADS, LANES, t), BF16),
                   jax.ShapeDtypeStruct((bsz, g, 2, t, LANES), BF16),
                   jax.ShapeDtypeStruct((bsz, g, 2, V_ROWS, t), BF16)],
        compiler_params=_params(("parallel", "parallel")),
    )(proj_t, proj_t, proj_t, proj_t, proj_t, qn, ksn, kwn, seg)


def _nsa_kernel(q_ref, kc_ref, vc_ref, k_ref, v_ref, bc_ref, bias_ref, cover_ref, gate_ref, o_ref,
                qq_ref, *scratch, tq, n_q, n_sblk):
    qi = pl.program_id(2)
    t0 = qi * tq
    rg = HEADS_PER_GROUP
    jpt = JOBS_PER_TRIP
    s_refs, smax_refs, p_refs = ((scratch[k:k + jpt], scratch[k + jpt:k + 2 * jpt]) for k in (0, 2 * jpt, 4 * jpt))
    m_refs, acc_refs = scratch[6 * jpt:6 * jpt + rg], scratch[6 * jpt + rg:]
    sel, win = 0, 1


    bias = jnp.concatenate([bc_ref[r] for r in range(rg)], axis=1)
    s = _dot(kc_ref[...], jnp.concatenate([q_ref[r] for r in range(rg)], axis=1)) + bias
    visible = bias > 0.5 * MASK_NEG
    p = jnp.exp2(s - jnp.max(s, axis=0, keepdims=True))
    p = jnp.where(visible, p / jnp.sum(p, axis=0, keepdims=True), 0.0)
    o_cmp_all = _dot(vc_ref[...], p.astype(BF16))
    o_cmp = [o_cmp_all[:, r * tq:(r + 1) * tq] for r in range(rg)]
    p_sum = p[:, 0:tq]
    for r in range(1, rg):
        p_sum = p_sum + p[:, r * tq:(r + 1) * tq]
    p_hi = p_sum.astype(BF16)
    p_lo = (p_sum - p_hi.astype(F32)).astype(BF16)
    imp_t = _dot(cover_ref[...], p_hi) + _dot(cover_ref[...], p_lo)

    score = imp_t[SEL_ROW0:SEL_ROW0 + MAX_SBLK, :]
    jrow = lax.broadcasted_iota(jnp.int32, (MAX_SBLK, tq), 0)
    qblk = (lax.broadcasted_iota(jnp.int32, (MAX_SBLK, tq), 1) + t0) // SEL_BLOCK
    causal = jrow <= qblk
    forced = causal & ((jrow == 0) | (jrow >= qblk - 1))
    score = jnp.where(forced, SEL_FORCED, jnp.where(causal, score, -1.0))
    rank = jnp.zeros((MAX_SBLK, tq), F32)
    for j in range(n_sblk):
        other = score[j:j + 1, :]
        ahead = (other > score) | ((other == score) & (jrow > j))
        rank = rank + jnp.where(ahead, 1.0, 0.0)
    n_top = min(N_SELECT, n_sblk)
    selected = (rank < n_top) & (score >= 0.0)
    neg = jnp.where(selected, 0.0, SEL_NEG).astype(BF16)
    for r in range(rg):
        qq_ref[r, 0:SEL_ROW0, :] = q_ref[r, 0:SEL_ROW0, :]
        qq_ref[r, SEL_ROW0:SEL_ROW0 + MAX_SBLK, :] = neg
        qq_ref[r, SEL_ROW0 + MAX_SBLK:LANES, :] = q_ref[r, SEL_ROW0 + MAX_SBLK:LANES, :]

    near_max = WINDOW // tq + 1
    plan = [(branch, a) for a in range(min(near_max, n_q)) for branch in (sel, win)]
    plan += [(sel, a) for a in range(near_max, n_q)]
    plan += [(sel, n_q)] * (-len(plan) % JOBS_PER_TRIP)
    groups = [plan[g:g + JOBS_PER_TRIP] for g in range(0, len(plan), JOBS_PER_TRIP)]
    n_groups = sum((qi >= group[0][1]).astype(jnp.int32) for group in groups)

    def jobs_of(i):
        jobs = []
        for branch, a in groups[i]:
            kind = (BIAS_DIAG, BIAS_NEAR, BIAS_WIN if branch == win else BIAS_ZERO)[min(a, 2)]
            real = a <= qi
            jobs.append((branch, pl.multiple_of(jnp.where(real, qi - a, 0) * tq, tq),
                         jnp.where(real, kind, BIAS_MASKED)))
        return jobs

    def scores(i):
        for (branch, k0, kind), s_ref, smax_ref in zip(jobs_of(i), s_refs[i % 2], smax_refs[i % 2]):
            k = k_ref[branch, pl.ds(k0, tq), :]
            for r in range(rg):
                q_t = qq_ref[r] if branch == sel else q_ref[r]
                s = _dot(k, q_t) + bias_ref[r, kind]
                s_ref[r] = s
                smax_ref[r] = jnp.max(s, axis=0, keepdims=True)

    def softmax(i):
        for r in range(rg):
            for branch in sorted({b for b, _ in groups[i]}):
                mine = [j for j, (b, _) in enumerate(groups[i]) if b == branch]
                m_old = m_refs[r][branch]
                m_new = m_old
                for j in mine:
                    m_new = jnp.maximum(m_new, smax_refs[i % 2][j][r])
                for j in mine:
                    p_refs[i % 2][j][r] = jnp.exp2(s_refs[i % 2][j][r] - m_new).astype(BF16)
                m_refs[r][branch] = m_new
                acc_refs[r][branch] = jnp.exp2(m_old - m_new) * acc_refs[r][branch]

    def values(i):
        for (branch, k0, _), p_ref in zip(jobs_of(i), p_refs[i % 2]):
            v_t = v_ref[branch, :, pl.ds(k0, tq)]
            for r in range(rg):
                acc_refs[r][branch] += _dot(v_t, p_ref[r])

    for r in range(rg):
        m_refs[r][...] = jnp.full(m_refs[r].shape, MASK_NEG, F32)
        acc_refs[r][...] = jnp.zeros(acc_refs[r].shape, F32)
    scores(0)
    for i in range(len(groups)):
        if i + 1 < len(groups):
            @pl.when(i + 1 < n_groups)
            def _(i=i):
                if i > 0:
                    values(i - 1)
                softmax(i)
                scores(i + 1)

        @pl.when(i + 1 == n_groups)
        def _(i=i):
            if i > 0:
                values(i - 1)
            softmax(i)
            values(i)

    def finish(branch):
        return [acc_refs[r][branch, 0:HEAD_DIM, :] / acc_refs[r][branch, HEAD_DIM:HEAD_DIM + 1, :] for r in range(rg)]

    o_sel = finish(sel)
    o_win = finish(win)

    gates_t = jnp.transpose(_sigmoid(gate_ref[...].astype(F32)))
    outs = []
    for r in range(rg):
        outs.append(gates_t[3 * r:3 * r + 1, :] * o_cmp[r]
                    + gates_t[3 * r + 1:3 * r + 2, :] * o_sel[r]
                    + gates_t[3 * r + 2:3 * r + 3, :] * o_win[r])
    o_ref[...] = jnp.transpose(jnp.concatenate(outs, axis=0)).astype(o_ref.dtype)


def _nsa_attention(qh, kc, vc, k_all, v_all, bias_c, bias_tiles, cover, proj3, tq):
    bsz, _, _, t = qh.shape
    n_chunks = kc.shape[2]
    rg = HEADS_PER_GROUP
    n_kinds = bias_tiles.shape[1]
    return pl.pallas_call(
        functools.partial(_nsa_kernel, tq=tq, n_q=t // tq, n_sblk=t // SEL_BLOCK),
        grid=(bsz, N_KV_GROUPS, t // tq),
        in_specs=[
            pl.BlockSpec((None, rg, LANES, tq), lambda b, g, i: (b, g, 0, i)),
            pl.BlockSpec((None, None, n_chunks, LANES), lambda b, g, i: (b, g, 0, 0)),
            pl.BlockSpec((None, None, HEAD_DIM, n_chunks), lambda b, g, i: (b, g, 0, 0)),
            pl.BlockSpec((None, None, 2, t, LANES), lambda b, g, i: (b, g, 0, 0, 0)),
            pl.BlockSpec((None, None, 2, V_ROWS, t), lambda b, g, i: (b, g, 0, 0, 0)),
            pl.BlockSpec((rg, n_chunks, tq), lambda b, g, i: (g, 0, i)),
            pl.BlockSpec((rg, n_kinds, tq, tq), lambda b, g, i: (g, 0, 0, 0)),
            pl.BlockSpec((LANES, n_chunks), lambda b, g, i: (0, 0)),
            pl.BlockSpec((tq, LANES), lambda b, g, i: (i, (b * D_PROJ + COL_GN) // LANES + g)),
        ],
        out_specs=pl.BlockSpec((None, tq, rg * HEAD_DIM), lambda b, g, i: (b, i, g)),
        out_shape=jax.ShapeDtypeStruct((bsz, t, Q_W), BF16),
        scratch_shapes=[
            pltpu.VMEM((rg, LANES, tq), BF16),
        ] + [pltpu.VMEM((rg, tq, tq), F32)] * (2 * JOBS_PER_TRIP)
          + [pltpu.VMEM((rg, 1, tq), F32)] * (2 * JOBS_PER_TRIP)
          + [pltpu.VMEM((rg, tq, tq), BF16)] * (2 * JOBS_PER_TRIP)
          + [pltpu.VMEM((2, 1, tq), F32)] * rg
          + [pltpu.VMEM((2, V_ROWS, tq), F32)] * rg,
        compiler_params=_params(("parallel", "parallel", "arbitrary")),
    )(qh, kc, vc, k_all, v_all, bias_c, bias_tiles, cover, proj3)


def _merge_kernel(ya_ref, yb_ref, ga_ref, gb_ref, x_ref, pa_ref, pb_ref, wo_ref, h_ref):
    merged = (_sigmoid(ga_ref[...].astype(F32)) * _dot(ya_ref[...], pa_ref[...])
              + _sigmoid(gb_ref[...].astype(F32)) * _dot(yb_ref[...], pb_ref[...]))
    h_ref[...] = x_ref[...] + _dot(merged.astype(BF16), wo_ref[...])


def _merge_out(ya_t, yb, proj_t, x, pa, pb, wo, bsz, tm):
    t = ya_t.shape[0]
    tiles = t // tm

    def rows(width):
        return pl.BlockSpec((tm, width), lambda b, s: (b * tiles + s, 0))

    def time_major(width, per_batch, offset=0):
        return pl.BlockSpec((tm, width), lambda b, s: (s, (b * per_batch + offset) // width))

    def whole(a):
        return pl.BlockSpec(a.shape, lambda b, s: (0, 0))

    return pl.pallas_call(
        _merge_kernel,
        grid=(bsz, tiles),
        in_specs=[time_major(RNN_W, RNN_W), rows(Q_W), time_major(D_MODEL, D_PROJ, COL_GA),
                  time_major(D_MODEL, D_PROJ, COL_GB), rows(D_MODEL), whole(pa), whole(pb), whole(wo)],
        out_specs=rows(D_MODEL),
        out_shape=jax.ShapeDtypeStruct((bsz * t, D_MODEL), F32),
        compiler_params=_params(("parallel", "parallel")),
    )(ya_t, yb, proj_t, proj_t, x, pa, pb, wo)


def _mlp_kernel(h_ref, g_ref, w1_ref, w2_ref, o_ref, hn_ref, acc_ref):
    j = pl.program_id(1)

    @pl.when(j == 0)
    def _():
        h = h_ref[...]
        y = h * lax.rsqrt(jnp.mean(h * h, axis=-1, keepdims=True) + NORM_EPS) * g_ref[...]
        hn_ref[...] = y.astype(BF16)
        acc_ref[...] = h

    z = jnp.maximum(_dot(hn_ref[...], w1_ref[...]), 0.0)
    acc_ref[...] += _dot((z * z).astype(BF16), w2_ref[...])

    @pl.when(j == pl.num_programs(1) - 1)
    def _():
        o_ref[...] = acc_ref[...]


def _mlp(h, gain, w1, w2, tm, tf):
    m, d = h.shape
    ff = w1.shape[1]
    return pl.pallas_call(
        _mlp_kernel,
        grid=(m // tm, ff // tf),
        in_specs=[
            pl.BlockSpec((tm, d), lambda i, j: (i, 0)),
            pl.BlockSpec((1, d), lambda i, j: (0, 0)),
            pl.BlockSpec((d, tf), lambda i, j: (0, j)),
            pl.BlockSpec((tf, d), lambda i, j: (j, 0)),
        ],
        out_specs=pl.BlockSpec((tm, d), lambda i, j: (i, 0)),
        out_shape=jax.ShapeDtypeStruct((m, d), F32),
        scratch_shapes=[pltpu.VMEM((tm, d), BF16), pltpu.VMEM((tm, d), F32)],
        compiler_params=_params(("parallel", "arbitrary")),
    )(h, gain, w1, w2)


def _t5_bucket_table():
    max_exact = REL_BUCKETS // 2
    d = np.arange(FAR_DIST)
    df = np.maximum(d.astype(np.float32), np.float32(1.0))
    large = max_exact + (np.log(df / np.float32(max_exact)) / np.float32(math.log(REL_MAX_DIST / max_exact))
                         * np.float32(REL_BUCKETS - max_exact)).astype(np.int32)
    large = np.minimum(large, REL_BUCKETS - 1)
    return np.where(d < max_exact, d, large).astype(np.int32)


def _pad_blocks(w, axis):
    shape = w.shape
    w = w.reshape(shape[:axis] + (LRU_BLOCKS, LRU_BLOCK_W) + shape[axis + 1:])
    pad = [(0, 0)] * w.ndim
    pad[axis + 1] = (0, RNN_BW - LRU_BLOCK_W)
    w = jnp.pad(w, pad)
    return w.reshape(shape[:axis] + (RNN_W,) + shape[axis + 1:])


def _in_proj_weight(w_in):
    cuts = np.cumsum((D_RNN, D_RNN, Q_W, KV_W, KV_W, KV_W, KV_W, KV_W, KV_W, 3 * N_HEADS, D_MODEL))
    (w_u, w_gate, w_q, w_kc, w_vc, w_ks, w_vs, w_kw, w_vw, w_gn, w_ga, w_gb) = jnp.split(w_in, cuts, axis=1)
    per_group = 3 * HEADS_PER_GROUP
    w_gn = jnp.pad(w_gn.reshape(D_MODEL, N_KV_GROUPS, per_group), ((0, 0), (0, 0), (0, LANES - per_group)))
    w_gn = w_gn.reshape(D_MODEL, N_KV_GROUPS * LANES)
    gap = jnp.zeros((D_MODEL, COL_GA - COL_GN - N_KV_GROUPS * LANES), w_in.dtype)
    w = jnp.concatenate([_pad_blocks(w_u, 1), _pad_blocks(w_gate, 1), w_q, w_ks, w_vs, w_kw, w_vw, w_kc, w_vc,
                         w_gn, gap, w_ga, w_gb], axis=1)
    assert w.shape[1] == D_PROJ
    return w.astype(BF16)


def _phi_weights(pe, w1, w2):
    half = CMP_BLOCK // 2
    eye = jnp.eye(N_KV_GROUPS, dtype=w1.dtype)
    w1h = w1.reshape(2, half, HEAD_DIM, PHI_HIDDEN)
    w1e = jnp.einsum("xldh,gk->xlgdkh", w1h, eye).reshape(2, half * KV_W, N_KV_GROUPS * PHI_HIDDEN)
    w2e = jnp.einsum("hd,gk->ghkd", w2, eye).reshape(N_KV_GROUPS * PHI_HIDDEN, KV_W)
    pee = jnp.broadcast_to(pe.reshape(2, half, 1, HEAD_DIM), (2, half, N_KV_GROUPS, HEAD_DIM)).reshape(2, half * KV_W)
    return pee, w1e.astype(BF16), w2e.astype(BF16)


def kernel(x, norm_mix, w_in, conv_w, conv_b, gate_a_w, gate_a_b, gate_x_w, gate_x_b, lru_lambda, phi_k_pe, phi_k_w1, phi_k_w2, phi_v_pe, phi_v_w1, phi_v_w2, q_norm, kc_norm, ks_norm, kw_norm, rel_bias, proj_a, proj_b, w_out, norm_mlp, w_mlp_in, w_mlp_out):
    bsz, t, d = x.shape
    assert d == D_MODEL and norm_mix.shape[0] == 1
    tq = 256
    assert t % tq == 0 and t // SEL_BLOCK <= MAX_SBLK and t % CMP_STRIDE == 0
    n_tok = bsz * t
    n_chunks = t // CMP_STRIDE
    assert n_chunks % 8 == 0 and n_chunks <= LANES
    x2 = x.reshape(n_tok, d)

    proj_t = _norm_matmul(x2, norm_mix, _in_proj_weight(w_in[0]), bsz, tm=1024 if t % 1024 == 0 else tq, tn=512)
    proj3 = proj_t.reshape(t, bsz, D_PROJ)

    pad_w = lambda w: jnp.pad(w, ((0, 0), (0, RNN_BW - LRU_BLOCK_W), (0, RNN_BW - LRU_BLOCK_W))).astype(BF16)
    ya_t = _rglru(proj3, _pad_blocks(conv_w[0], 1), _pad_blocks(conv_b, 1),
                  pad_w(gate_a_w[0]), _pad_blocks(gate_a_b.reshape(1, D_RNN), 1),
                  pad_w(gate_x_w[0]), _pad_blocks(gate_x_b.reshape(1, D_RNN), 1),
                  _pad_blocks(lru_lambda, 1), tc=128)

    def chunks(col):
        z = jnp.swapaxes(proj3[:, :, col:col + KV_W], 0, 1)
        return z.reshape(bsz, n_chunks, CMP_STRIDE * KV_W)

    zk, zv = chunks(COL_KC), chunks(COL_VC)
    pe_k, w1_k, w2_k = _phi_weights(phi_k_pe[0], phi_k_w1[0], phi_k_w2[0])
    pe_v, w1_v, w2_v = _phi_weights(phi_v_pe[0], phi_v_w1[0], phi_v_w2[0])
    seg256 = jnp.asarray(np.kron(np.eye(KV_W // HEAD_DIM), np.ones((HEAD_DIM, HEAD_DIM))), BF16)
    kcn = jnp.tile(kc_norm, (1, N_KV_GROUPS))
    kc = _compress(zk, pe_k, w1_k, w2_k, kcn, seg256, is_key=True)
    vc = _compress(zv, pe_v, w1_v, w2_v, kcn, seg256, is_key=False)

    seg128 = jnp.asarray(np.kron(np.eye(LANES // HEAD_DIM), np.ones((HEAD_DIM, HEAD_DIM))), BF16)
    qh, k_all, v_all = _nsa_prep(proj_t, bsz, jnp.tile(q_norm, (1, N_HEADS)), jnp.tile(ks_norm, (1, N_KV_GROUPS)),
                                 jnp.tile(kw_norm, (1, N_KV_GROUPS)), seg128, tt=tq)

    bucket_of = _t5_bucket_table()
    far = rel_bias[REL_BUCKETS - 1][:, None, None]

    def bias_table(dist, valid, shift):
        buckets = jnp.asarray(bucket_of[np.clip(dist, 0, FAR_DIST - 1)].astype(np.int8))
        onehot = (buckets[None] == jnp.arange(REL_BUCKETS, dtype=jnp.int8).reshape(-1, 1, 1)).astype(F32)
        vals = jnp.einsum("kh,kji->hji", rel_bias, onehot, precision=lax.Precision.HIGHEST)
        if shift:
            vals = vals - far
        return jnp.where(jnp.asarray(valid), vals * LOG2E, MASK_NEG).astype(F32)

    kj = np.arange(tq)[:, None]
    qi_ = np.arange(tq)[None, :]
    bias_diag = bias_table(qi_ - kj, qi_ >= kj, True)
    bias_near = bias_table(tq + qi_ - kj, np.ones((tq, tq), bool), True)
    shared = jnp.asarray(np.stack([np.zeros((tq, tq)), np.where(kj > qi_, 0.0, MASK_NEG), np.full((tq, tq), MASK_NEG)]), F32)
    shared = jnp.broadcast_to(shared[None], (N_HEADS, 3, tq, tq))
    bias_tiles = jnp.concatenate([shared[:, 0:1], bias_diag[:, None], bias_near[:, None], shared[:, 1:3]], axis=1)
    assert bias_tiles.shape[1] == BIAS_MASKED + 1 and WINDOW == 2 * tq
    cidx = np.arange(n_chunks)[:, None]
    dist_c = np.arange(t)[None, :] - (cidx * CMP_STRIDE + CMP_BLOCK - 1)
    bias_c = bias_table(dist_c, (dist_c >= 0) & (cidx < n_chunks - 1), False)
    cstart = np.arange(n_chunks) * CMP_STRIDE
    sj = np.arange(MAX_SBLK)
    cov = ((cstart[None, :] < (sj[:, None] + 1) * SEL_BLOCK) & (cstart[None, :] + CMP_BLOCK - 1 >= sj[:, None] * SEL_BLOCK)
           & (np.arange(n_chunks)[None, :] < n_chunks - 1))
    cover = np.zeros((LANES, n_chunks), np.float32)
    cover[SEL_ROW0:SEL_ROW0 + MAX_SBLK] = cov
    yb = _nsa_attention(qh, kc, vc, k_all, v_all, bias_c, bias_tiles, jnp.asarray(cover, BF16), proj_t, tq)

    pa = jnp.pad(proj_a[0].reshape(LRU_BLOCKS, LRU_BLOCK_W, D_MODEL), ((0, 0), (0, RNN_BW - LRU_BLOCK_W), (0, 0)))
    h = _merge_out(ya_t.reshape(t, bsz * RNN_W), yb.reshape(n_tok, Q_W), proj_t, x2,
                   pa.reshape(RNN_W, D_MODEL).astype(BF16), proj_b[0].astype(BF16), w_out[0].astype(BF16),
                   bsz, tm=512 if t % 512 == 0 else tq)

    out = _mlp(h, norm_mlp, w_mlp_in[0].astype(BF16), w_mlp_out[0].astype(BF16),
               tm=1024 if n_tok % 1024 == 0 else tq, tf=512)
    return out.reshape(bsz, t, d)
```

```python
import functools
import math

import numpy as np
import jax
import jax.numpy as jnp
from jax import lax
from jax.experimental import pallas as pl
from jax.experimental.pallas import tpu as pltpu

F32 = jnp.float32
BF16 = jnp.bfloat16

D_MODEL = 1024
D_RNN = 1344
LRU_BLOCKS = 4
LRU_BLOCK_W = D_RNN // LRU_BLOCKS
CONV_W = 4
LRU_C = 8.0
N_HEADS = 16
HEAD_DIM = 64
N_KV_GROUPS = 4
HEADS_PER_GROUP = N_HEADS // N_KV_GROUPS
CMP_BLOCK = 32
CMP_STRIDE = 16
SEL_BLOCK = 64
N_SELECT = 16
WINDOW = 512
PHI_HIDDEN = 256
SEL_FORCED = 1e4
REL_BUCKETS = 32
REL_MAX_DIST = 128
D_FF = 4 * D_MODEL
NORM_EPS = 1e-6
Q_W = N_HEADS * HEAD_DIM
KV_W = N_KV_GROUPS * HEAD_DIM

LANES = 128
VMEM_LIMIT = 56 * 1024 * 1024

RNN_BW = 384
RNN_W = LRU_BLOCKS * RNN_BW
MASK_NEG = -1e30
SEL_NEG = -1e9
SEL_ROW0 = 64
MAX_SBLK = 32
FAR_DIST = 256
LOG2E = math.log2(math.e)
V_ROWS = HEAD_DIM + 16
BIAS_DIAG, BIAS_NEAR, BIAS_WIN = range(3)
JOBS_PER_TRIP = 2

COL_U = 0
COL_GATE = RNN_W
COL_Q = 2 * RNN_W
COL_KS = COL_Q + Q_W
COL_VS = COL_KS + KV_W
COL_KW = COL_VS + KV_W
COL_VW = COL_KW + KV_W
COL_KC = COL_VW + KV_W
COL_VC = COL_KC + KV_W
COL_GN = COL_VC + KV_W
COL_GA = 6144
COL_GB = 7168
D_PROJ = 8192


def _dot(a, b):
    return jnp.dot(a, b, preferred_element_type=F32)


def _dot_nt(a, b):
    return lax.dot_general(a, b, (((1,), (1,)), ((), ())), preferred_element_type=F32)


def _gelu_tanh(x):
    return 0.5 * x * (1.0 + jnp.tanh(math.sqrt(2.0 / math.pi) * (x + 0.044715 * (x * x * x))))


def _sigmoid(x):
    return 1.0 / (1.0 + jnp.exp(-x))


def _seg_sum(x, seg_ones):
    hi = x.astype(BF16)
    lo = (x - hi.astype(F32)).astype(BF16)
    return _dot(hi, seg_ones) + _dot(lo, seg_ones)


def _params(sem, flags=None):
    return pltpu.CompilerParams(dimension_semantics=sem, vmem_limit_bytes=VMEM_LIMIT, flags=flags)


def _norm_matmul_kernel(x_ref, g_ref, w_ref, o_ref, *rest, tn, side_cols):
    side_refs, xn_ref = rest[:-1], rest[-1]
    j = pl.program_id(1)

    @pl.when(j == 0)
    def _():
        x = x_ref[...]
        y = x * lax.rsqrt(jnp.mean(x * x, axis=-1, keepdims=True) + NORM_EPS) * g_ref[...]
        xn_ref[...] = y.astype(BF16)

    res = _dot(xn_ref[...], w_ref[...]).astype(o_ref.dtype)
    o_ref[...] = res
    for (col, width), side_ref in zip(side_cols, side_refs):
        @pl.when(j == col // tn)
        def _(col=col, width=width, side_ref=side_ref):
            side_ref[...] = res[:, col % tn:col % tn + width]


def _norm_matmul(x, gain, w, bsz, tm, tn, side_cols):
    m, k = x.shape
    n = w.shape[1]
    t = m // bsz
    tiles = t // tm
    col_tiles = n // tn
    assert all(col // tn == (col + width - 1) // tn for col, width in side_cols)
    return pl.pallas_call(
        functools.partial(_norm_matmul_kernel, tn=tn, side_cols=side_cols),
        grid=(m // tm, n // tn),
        in_specs=[
            pl.BlockSpec((tm, k), lambda i, j: (i, 0)),
            pl.BlockSpec((1, k), lambda i, j: (0, 0)),
            pl.BlockSpec((k, tn), lambda i, j: (0, j)),
        ],
        out_specs=[pl.BlockSpec((tm, tn), lambda i, j: (i % tiles, (i // tiles) * col_tiles + j))]
                  + [pl.BlockSpec((tm, width), lambda i, j: (i, 0)) for _, width in side_cols],
        out_shape=[jax.ShapeDtypeStruct((t, bsz * n), BF16)]
                  + [jax.ShapeDtypeStruct((m, width), BF16) for _, width in side_cols],
        scratch_shapes=[pltpu.VMEM((tm, k), BF16)],
        compiler_params=_params(("parallel", "arbitrary")),
    )(x, gain, w)


def _rglru_kernel(u_ref, ug_ref, cw_ref, cb_ref, wa_ref, ba_ref, wx_ref, bx_ref, lam_ref,
                  y_ref, ubuf, a_scr, b_scr, hcar, *, tc, bsz):
    t = pl.program_id(1)
    rows = tc * bsz
    halo = (CONV_W - 1) * bsz

    @pl.when(t == 0)
    def _():
        ubuf[0:halo, :] = jnp.zeros((halo, RNN_BW), F32)
        hcar[...] = jnp.zeros_like(hcar)

    u = u_ref[...].astype(F32).reshape(rows, RNN_BW)
    ubuf[halo:halo + rows, :] = u
    xc = cb_ref[...]
    for k in range(CONV_W):
        xc = xc + cw_ref[k:k + 1, :] * ubuf[k * bsz:k * bsz + rows, :]
    ubuf[0:halo, :] = u[rows - halo:rows, :]

    xb = xc.astype(BF16)
    r = _sigmoid(_dot(xb, wa_ref[...]) + ba_ref[...])
    i = _sigmoid(_dot(xb, wx_ref[...]) + bx_ref[...])
    z = -lam_ref[...]
    softplus = jnp.maximum(z, 0.0) + jnp.log(1.0 + jnp.exp(-jnp.abs(z)))
    log_a = (-LRU_C) * r * softplus
    mult = jnp.sqrt(1.0 - jnp.exp(2.0 * log_a))
    row = lax.broadcasted_iota(jnp.int32, (rows, 1), 0)
    mult = jnp.where((row < bsz) & (t == 0), 1.0, mult)
    a_scr[...] = jnp.exp(log_a)
    b_scr[...] = mult * (i * xc)

    def step(s, h):
        at = pl.ds(pl.multiple_of(s * bsz, bsz), bsz)
        h = a_scr[at, :] * h + b_scr[at, :]
        b_scr[at, :] = h
        return h

    hcar[...] = lax.fori_loop(0, tc, step, hcar[...], unroll=8)
    y = b_scr[...] * _gelu_tanh(ug_ref[...].astype(F32).reshape(rows, RNN_BW))
    y_ref[...] = y.reshape(tc, bsz, RNN_BW).astype(y_ref.dtype)


def _rglru(proj3, conv_w, conv_b, wa, ba, wx, bx, lam, tc):
    t, bsz, _ = proj3.shape
    assert bsz % 8 == 0, "the recurrence advances whole sublane groups of sequences"
    nb = LRU_BLOCKS
    rows = tc * bsz
    vec = pl.BlockSpec((1, RNN_BW), lambda n, s: (0, n))
    mat = pl.BlockSpec((None, RNN_BW, RNN_BW), lambda n, s: (n, 0, 0))
    return pl.pallas_call(
        functools.partial(_rglru_kernel, tc=tc, bsz=bsz),
        grid=(nb, t // tc),
        in_specs=[
            pl.BlockSpec((tc, bsz, RNN_BW), lambda n, s: (s, 0, COL_U // RNN_BW + n)),
            pl.BlockSpec((tc, bsz, RNN_BW), lambda n, s: (s, 0, COL_GATE // RNN_BW + n)),
            pl.BlockSpec((CONV_W, RNN_BW), lambda n, s: (0, n)),
            vec, mat, vec, mat, vec, vec,
        ],
        out_specs=pl.BlockSpec((tc, bsz, RNN_BW), lambda n, s: (s, 0, n)),
        out_shape=jax.ShapeDtypeStruct((t, bsz, RNN_W), BF16),
        scratch_shapes=[pltpu.VMEM((rows + (CONV_W - 1) * bsz, RNN_BW), F32),
                        pltpu.VMEM((rows, RNN_BW), F32), pltpu.VMEM((rows, RNN_BW), F32),
                        pltpu.VMEM((bsz, RNN_BW), F32)],
        compiler_params=_params(("parallel", "arbitrary")),
    )(proj3, proj3, conv_w, conv_b, wa, ba, wx, bx, lam)


def _compress_kernel(z_ref, pe_ref, w1_ref, w2_ref, norm_ref, seg_ref, o_ref, *, n_chunks, is_key):
    z = z_ref[...].astype(F32)
    first = _dot((z + pe_ref[0:1, :]).astype(BF16), w1_ref[0])
    second = _dot((z + pe_ref[1:2, :]).astype(BF16), w1_ref[1])
    pre = first + pltpu.roll(second, n_chunks - 1, 0)
    out = _dot(_gelu_tanh(pre).astype(BF16), w2_ref[...])
    if is_key:
        ssq = _seg_sum(out * out, seg_ref[...])
        normed = out * lax.rsqrt(ssq * (1.0 / HEAD_DIM) + NORM_EPS) * norm_ref[...]
        low = lax.broadcasted_iota(jnp.int32, (n_chunks, LANES), 1) < HEAD_DIM
        for pair in range(N_KV_GROUPS // 2):
            blk = normed[:, pair * LANES:(pair + 1) * LANES]
            o_ref[2 * pair] = jnp.where(low, blk, 0.0).astype(o_ref.dtype)
            o_ref[2 * pair + 1] = jnp.where(low, pltpu.roll(blk, HEAD_DIM, 1), 0.0).astype(o_ref.dtype)
    else:
        out_t = jnp.transpose(out)
        for g in range(N_KV_GROUPS):
            o_ref[g] = out_t[g * HEAD_DIM:(g + 1) * HEAD_DIM, :].astype(o_ref.dtype)


def _compress(z, pe, w1, w2, norm, seg, is_key):
    bsz, n_chunks, width = z.shape
    hid = w1.shape[-1]
    out_block = (N_KV_GROUPS, n_chunks, LANES) if is_key else (N_KV_GROUPS, HEAD_DIM, n_chunks)
    return pl.pallas_call(
        functools.partial(_compress_kernel, n_chunks=n_chunks, is_key=is_key),
        grid=(bsz,),
        in_specs=[
            pl.BlockSpec((None, n_chunks, width), lambda b: (b, 0, 0)),
            pl.BlockSpec((2, width), lambda b: (0, 0)),
            pl.BlockSpec((2, width, hid), lambda b: (0, 0, 0)),
            pl.BlockSpec((hid, KV_W), lambda b: (0, 0)),
            pl.BlockSpec((1, KV_W), lambda b: (0, 0)),
            pl.BlockSpec((KV_W, KV_W), lambda b: (0, 0)),
        ],
        out_specs=pl.BlockSpec((None,) + out_block, lambda b: (b, 0, 0, 0)),
        out_shape=jax.ShapeDtypeStruct((bsz,) + out_block, BF16),
        compiler_params=_params(("parallel",)),
    )(z, pe, w1, w2, norm, seg)


def _nsa_prep_kernel(q_ref, ks_ref, vs_ref, kw_ref, vw_ref, qn_ref, ksn_ref, kwn_ref, seg_ref,
                     qo_ref, ko_ref, vo_ref, *, tt):
    t0 = pl.program_id(1) * tt
    lane = lax.broadcasted_iota(jnp.int32, (tt, LANES), 1)
    row = lax.broadcasted_iota(jnp.int32, (tt, LANES), 0) + t0
    low = lane < HEAD_DIM
    seg = seg_ref[...]
    onehot = jnp.where((lane - SEL_ROW0) == row // SEL_BLOCK, 1.0, 0.0)

    def normed(ref, gain_ref, blk, scale):
        x = ref[:, blk * LANES:(blk + 1) * LANES].astype(F32)
        ssq = _seg_sum(x * x, seg)
        y = x * lax.rsqrt(ssq * (1.0 / HEAD_DIM) + NORM_EPS) * gain_ref[:, blk * LANES:(blk + 1) * LANES]
        return y * scale if scale != 1.0 else y

    q_pad = jnp.zeros((LANES - HEAD_DIM, tt), BF16)
    for blk in range(Q_W // LANES):
        y_t = jnp.transpose(normed(q_ref, qn_ref, blk, HEAD_DIM ** -0.5 * LOG2E)).astype(BF16)
        for half in range(2):
            qo_ref[2 * blk + half, 0:HEAD_DIM, :] = y_t[half * HEAD_DIM:(half + 1) * HEAD_DIM, :]
            qo_ref[2 * blk + half, HEAD_DIM:LANES, :] = q_pad
    ones_row = jnp.where(lax.broadcasted_iota(jnp.int32, (V_ROWS - HEAD_DIM, tt), 0) == 0, 1.0, 0.0).astype(BF16)
    for blk in range(KV_W // LANES):
        y = normed(ks_ref, ksn_ref, blk, 1.0)
        ysw = pltpu.roll(y, HEAD_DIM, 1)
        ko_ref[2 * blk, 0] = jnp.where(low, y, onehot).astype(BF16)
        ko_ref[2 * blk + 1, 0] = jnp.where(low, ysw, onehot).astype(BF16)
        y = normed(kw_ref, kwn_ref, blk, 1.0)
        ysw = pltpu.roll(y, HEAD_DIM, 1)
        ko_ref[2 * blk, 1] = jnp.where(low, y, 0.0).astype(BF16)
        ko_ref[2 * blk + 1, 1] = jnp.where(low, ysw, 0.0).astype(BF16)
        for branch, src in enumerate((vs_ref, vw_ref)):
            v_t = jnp.transpose(src[:, blk * LANES:(blk + 1) * LANES].astype(F32))
            for half in range(2):
                g = 2 * blk + half
                vo_ref[g, branch, 0:HEAD_DIM, :] = v_t[half * HEAD_DIM:(half + 1) * HEAD_DIM, :].astype(BF16)
                vo_ref[g, branch, HEAD_DIM:V_ROWS, :] = ones_row


def _nsa_prep(proj_t, bsz, qn, ksn, kwn, seg, tt):
    t = proj_t.shape[0]

    def col(width, offset):
        return pl.BlockSpec((tt, width), lambda b, s: (s, (b * D_PROJ + offset) // width))

    def vec(width):
        return pl.BlockSpec((1, width), lambda b, s: (0, 0))

    g = N_KV_GROUPS
    return pl.pallas_call(
        functools.partial(_nsa_prep_kernel, tt=tt),
        grid=(bsz, t // tt),
        in_specs=[col(Q_W, COL_Q), col(KV_W, COL_KS), col(KV_W, COL_VS), col(KV_W, COL_KW), col(KV_W, COL_VW),
                  vec(Q_W), vec(KV_W), vec(KV_W), pl.BlockSpec((LANES, LANES), lambda b, s: (0, 0))],
        out_specs=[pl.BlockSpec((None, N_HEADS, LANES, tt), lambda b, s: (b, 0, 0, s)),
                   pl.BlockSpec((None, g, 2, tt, LANES), lambda b, s: (b, 0, 0, s, 0)),
                   pl.BlockSpec((None, g, 2, V_ROWS, tt), lambda b, s: (b, 0, 0, 0, s))],
        out_shape=[jax.ShapeDtypeStruct((bsz, N_HEADS, LANES, t), BF16),
                   jax.ShapeDtypeStruct((bsz, g, 2, t, LANES), BF16),
                   jax.ShapeDtypeStruct((bsz, g, 2, V_ROWS, t), BF16)],
        compiler_params=_params(("parallel", "parallel")),
    )(proj_t, proj_t, proj_t, proj_t, proj_t, qn, ksn, kwn, seg)


def _nsa_kernel(q_ref, kc_ref, vc_ref, k_ref, v_ref, bc_ref, bias_ref, cover_ref, gate_ref, o_ref,
                qq_ref, *scratch, tq, n_q, n_sblk):
    qi = pl.program_id(2)
    t0 = qi * tq
    rg = HEADS_PER_GROUP
    jpt = JOBS_PER_TRIP
    s_refs, smax_refs, p_refs = ((scratch[k:k + jpt], scratch[k + jpt:k + 2 * jpt]) for k in (0, 2 * jpt, 4 * jpt))
    m_refs, acc_refs = scratch[6 * jpt:6 * jpt + rg], scratch[6 * jpt + rg:]
    sel, win = 0, 1


    bias = jnp.concatenate([bc_ref[r] for r in range(rg)], axis=1)
    s = _dot(kc_ref[...], jnp.concatenate([q_ref[r] for r in range(rg)], axis=1)) + bias
    visible = bias > 0.5 * MASK_NEG
    p = jnp.exp2(s - jnp.max(s, axis=0, keepdims=True))
    p = jnp.where(visible, p / jnp.sum(p, axis=0, keepdims=True), 0.0)
    o_cmp_all = _dot(vc_ref[...], p.astype(BF16))
    o_cmp = [o_cmp_all[:, r * tq:(r + 1) * tq] for r in range(rg)]
    p_sum = p[:, 0:tq]
    for r in range(1, rg):
        p_sum = p_sum + p[:, r * tq:(r + 1) * tq]
    p_hi = p_sum.astype(BF16)
    p_lo = (p_sum - p_hi.astype(F32)).astype(BF16)
    imp_t = _dot(cover_ref[...], p_hi) + _dot(cover_ref[...], p_lo)

    score = imp_t[SEL_ROW0:SEL_ROW0 + MAX_SBLK, :]
    jrow = lax.broadcasted_iota(jnp.int32, (MAX_SBLK, tq), 0)
    qblk = (lax.broadcasted_iota(jnp.int32, (MAX_SBLK, tq), 1) + t0) // SEL_BLOCK
    causal = jrow <= qblk
    forced = causal & ((jrow == 0) | (jrow >= qblk - 1))
    score = jnp.where(forced, SEL_FORCED, jnp.where(causal, score, -1.0))
    rank = jnp.zeros((MAX_SBLK, tq), F32)
    for j in range(n_sblk):
        other = score[j:j + 1, :]
        ahead = (other > score) | ((other == score) & (jrow > j))
        rank = rank + jnp.where(ahead, 1.0, 0.0)
    n_top = min(N_SELECT, n_sblk)
    selected = (rank < n_top) & (score >= 0.0)
    neg = jnp.where(selected, 0.0, SEL_NEG).astype(BF16)
    for r in range(rg):
        qq_ref[r, 0:SEL_ROW0, :] = q_ref[r, 0:SEL_ROW0, :]
        qq_ref[r, SEL_ROW0:SEL_ROW0 + MAX_SBLK, :] = neg
        qq_ref[r, SEL_ROW0 + MAX_SBLK:LANES, :] = q_ref[r, SEL_ROW0 + MAX_SBLK:LANES, :]

    near_max = WINDOW // tq + 1
    plan = [(branch, a) for a in range(min(near_max, n_q)) for branch in (sel, win)]
    plan += [(sel, a) for a in range(near_max, n_q)]
    plan += [(sel, n_q)] * (-len(plan) % JOBS_PER_TRIP)
    groups = [plan[g:g + JOBS_PER_TRIP] for g in range(0, len(plan), JOBS_PER_TRIP)]
    n_groups = sum((qi >= group[0][1]).astype(jnp.int32) for group in groups)

    def jobs_of(i):
        jobs = []
        for branch, a in groups[i]:
            kind = (BIAS_DIAG, BIAS_NEAR, BIAS_WIN if branch == win else None)[min(a, 2)]
            real = None if a <= groups[i][0][1] else a <= qi
            tile = qi - a if real is None else jnp.where(real, qi - a, 0)
            jobs.append((branch, pl.multiple_of(tile * tq, tq), kind, real))
        return jobs

    def scores(i):
        for (branch, k0, kind, real), s_ref, smax_ref in zip(jobs_of(i), s_refs[i % 2], smax_refs[i % 2]):
            k = k_ref[branch, pl.ds(k0, tq), :]
            for r in range(rg):
                q_t = qq_ref[r] if branch == sel else q_ref[r]
                s = _dot(k, q_t)
                if kind is not None:
                    s = s + bias_ref[r, kind]
                s_ref[r] = s
                smax = jnp.max(s, axis=0, keepdims=True)
                smax_ref[r] = smax if real is None else jnp.where(real, smax, MASK_NEG)

    def softmax(i):
        jobs = jobs_of(i)
        for r in range(rg):
            for branch in sorted({b for b, _ in groups[i]}):
                mine = [j for j, (b, _) in enumerate(groups[i]) if b == branch]
                m_old = m_refs[r][branch]
                m_new = m_old
                for j in mine:
                    m_new = jnp.maximum(m_new, smax_refs[i % 2][j][r])
                for j in mine:
                    p = jnp.exp2(s_refs[i % 2][j][r] - m_new).astype(BF16)
                    real = jobs[j][3]
                    p_refs[i % 2][j][r] = p if real is None else jnp.where(real, p, jnp.zeros_like(p))
                m_refs[r][branch] = m_new
                acc_refs[r][branch] = jnp.exp2(m_old - m_new) * acc_refs[r][branch]

    def values(i):
        for (branch, k0, _, _), p_ref in zip(jobs_of(i), p_refs[i % 2]):
            v_t = v_ref[branch, :, pl.ds(k0, tq)]
            for r in range(rg):
                acc_refs[r][branch] += _dot(v_t, p_ref[r])

    for r in range(rg):
        m_refs[r][...] = jnp.full(m_refs[r].shape, MASK_NEG, F32)
        acc_refs[r][...] = jnp.zeros(acc_refs[r].shape, F32)
    scores(0)
    for i in range(len(groups)):
        if i + 1 < len(groups):
            @pl.when(i + 1 < n_groups)
            def _(i=i):
                if i > 0:
                    values(i - 1)
                softmax(i)
                scores(i + 1)

        @pl.when(i + 1 == n_groups)
        def _(i=i):
            if i > 0:
                values(i - 1)
            softmax(i)
            values(i)

    def finish(branch):
        return [acc_refs[r][branch, 0:HEAD_DIM, :] / acc_refs[r][branch, HEAD_DIM:HEAD_DIM + 1, :] for r in range(rg)]

    o_sel = finish(sel)
    o_win = finish(win)

    gates_t = jnp.transpose(_sigmoid(gate_ref[...].astype(F32)))
    outs = []
    for r in range(rg):
        outs.append(gates_t[3 * r:3 * r + 1, :] * o_cmp[r]
                    + gates_t[3 * r + 1:3 * r + 2, :] * o_sel[r]
                    + gates_t[3 * r + 2:3 * r + 3, :] * o_win[r])
    o_ref[...] = jnp.transpose(jnp.concatenate(outs, axis=0)).astype(o_ref.dtype)


def _nsa_attention(qh, kc, vc, k_all, v_all, bias_c, bias_tiles, cover, proj3, tq):
    bsz, _, _, t = qh.shape
    n_chunks = kc.shape[2]
    rg = HEADS_PER_GROUP
    n_kinds = bias_tiles.shape[1]
    return pl.pallas_call(
        functools.partial(_nsa_kernel, tq=tq, n_q=t // tq, n_sblk=t // SEL_BLOCK),
        grid=(bsz, N_KV_GROUPS, t // tq),
        in_specs=[
            pl.BlockSpec((None, rg, LANES, tq), lambda b, g, i: (b, g, 0, i)),
            pl.BlockSpec((None, None, n_chunks, LANES), lambda b, g, i: (b, g, 0, 0)),
            pl.BlockSpec((None, None, HEAD_DIM, n_chunks), lambda b, g, i: (b, g, 0, 0)),
            pl.BlockSpec((None, None, 2, t, LANES), lambda b, g, i: (b, g, 0, 0, 0)),
            pl.BlockSpec((None, None, 2, V_ROWS, t), lambda b, g, i: (b, g, 0, 0, 0)),
            pl.BlockSpec((rg, n_chunks, tq), lambda b, g, i: (g, 0, i)),
            pl.BlockSpec((rg, n_kinds, tq, tq), lambda b, g, i: (g, 0, 0, 0)),
            pl.BlockSpec((LANES, n_chunks), lambda b, g, i: (0, 0)),
            pl.BlockSpec((tq, LANES), lambda b, g, i: (i, (b * D_PROJ + COL_GN) // LANES + g)),
        ],
        out_specs=pl.BlockSpec((None, tq, rg * HEAD_DIM), lambda b, g, i: (b, i, g)),
        out_shape=jax.ShapeDtypeStruct((bsz, t, Q_W), BF16),
        scratch_shapes=[
            pltpu.VMEM((rg, LANES, tq), BF16),
        ] + [pltpu.VMEM((rg, tq, tq), F32)] * (2 * JOBS_PER_TRIP)
          + [pltpu.VMEM((rg, 1, tq), F32)] * (2 * JOBS_PER_TRIP)
          + [pltpu.VMEM((rg, tq, tq), BF16)] * (2 * JOBS_PER_TRIP)
          + [pltpu.VMEM((2, 1, tq), F32)] * rg
          + [pltpu.VMEM((2, V_ROWS, tq), F32)] * rg,
        compiler_params=_params(("parallel", "parallel", "arbitrary")),
    )(qh, kc, vc, k_all, v_all, bias_c, bias_tiles, cover, proj3)


def _merge_kernel(ya_ref, yb_ref, ga_ref, gb_ref, x_ref, pa_ref, pb_ref, wo_ref, h_ref):
    merged = (_sigmoid(ga_ref[...].astype(F32)) * _dot(ya_ref[...], pa_ref[...])
              + _sigmoid(gb_ref[...].astype(F32)) * _dot(yb_ref[...], pb_ref[...]))
    h_ref[...] = x_ref[...] + _dot(merged.astype(BF16), wo_ref[...])


def _merge_out(ya_t, yb, proj_t, x, pa, pb, wo, bsz, tm):
    t = ya_t.shape[0]
    tiles = t // tm

    def rows(width):
        return pl.BlockSpec((tm, width), lambda b, s: (b * tiles + s, 0))

    def time_major(width, per_batch, offset=0):
        return pl.BlockSpec((tm, width), lambda b, s: (s, (b * per_batch + offset) // width))

    def whole(a):
        return pl.BlockSpec(a.shape, lambda b, s: (0, 0))

    return pl.pallas_call(
        _merge_kernel,
        grid=(bsz, tiles),
        in_specs=[time_major(RNN_W, RNN_W), rows(Q_W), time_major(D_MODEL, D_PROJ, COL_GA),
                  time_major(D_MODEL, D_PROJ, COL_GB), rows(D_MODEL), whole(pa), whole(pb), whole(wo)],
        out_specs=rows(D_MODEL),
        out_shape=jax.ShapeDtypeStruct((bsz * t, D_MODEL), F32),
        compiler_params=_params(("parallel", "parallel")),
    )(ya_t, yb, proj_t, proj_t, x, pa, pb, wo)


def _mlp_kernel(h_ref, g_ref, w1_ref, w2_ref, o_ref, hn_ref, acc_ref):
    j = pl.program_id(1)

    @pl.when(j == 0)
    def _():
        h = h_ref[...]
        y = h * lax.rsqrt(jnp.mean(h * h, axis=-1, keepdims=True) + NORM_EPS) * g_ref[...]
        hn_ref[...] = y.astype(BF16)
        acc_ref[...] = h

    z = jnp.maximum(_dot(hn_ref[...], w1_ref[...]), 0.0)
    acc_ref[...] += _dot((z * z).astype(BF16), w2_ref[...])

    @pl.when(j == pl.num_programs(1) - 1)
    def _():
        o_ref[...] = acc_ref[...]


def _mlp(h, gain, w1, w2, tm, tf):
    m, d = h.shape
    ff = w1.shape[1]
    return pl.pallas_call(
        _mlp_kernel,
        grid=(m // tm, ff // tf),
        in_specs=[
            pl.BlockSpec((tm, d), lambda i, j: (i, 0)),
            pl.BlockSpec((1, d), lambda i, j: (0, 0)),
            pl.BlockSpec((d, tf), lambda i, j: (0, j)),
            pl.BlockSpec((tf, d), lambda i, j: (j, 0)),
        ],
        out_specs=pl.BlockSpec((tm, d), lambda i, j: (i, 0)),
        out_shape=jax.ShapeDtypeStruct((m, d), F32),
        scratch_shapes=[pltpu.VMEM((tm, d), BF16), pltpu.VMEM((tm, d), F32)],
        compiler_params=_params(("parallel", "arbitrary")),
    )(h, gain, w1, w2)


def _t5_bucket_table():
    max_exact = REL_BUCKETS // 2
    d = np.arange(FAR_DIST)
    df = np.maximum(d.astype(np.float32), np.float32(1.0))
    large = max_exact + (np.log(df / np.float32(max_exact)) / np.float32(math.log(REL_MAX_DIST / max_exact))
                         * np.float32(REL_BUCKETS - max_exact)).astype(np.int32)
    large = np.minimum(large, REL_BUCKETS - 1)
    return np.where(d < max_exact, d, large).astype(np.int32)


def _pad_blocks(w, axis):
    shape = w.shape
    w = w.reshape(shape[:axis] + (LRU_BLOCKS, LRU_BLOCK_W) + shape[axis + 1:])
    pad = [(0, 0)] * w.ndim
    pad[axis + 1] = (0, RNN_BW - LRU_BLOCK_W)
    w = jnp.pad(w, pad)
    return w.reshape(shape[:axis] + (RNN_W,) + shape[axis + 1:])


def _in_proj_weight(w_in):
    cuts = np.cumsum((D_RNN, D_RNN, Q_W, KV_W, KV_W, KV_W, KV_W, KV_W, KV_W, 3 * N_HEADS, D_MODEL))
    (w_u, w_gate, w_q, w_kc, w_vc, w_ks, w_vs, w_kw, w_vw, w_gn, w_ga, w_gb) = jnp.split(w_in, cuts, axis=1)
    per_group = 3 * HEADS_PER_GROUP
    w_gn = jnp.pad(w_gn.reshape(D_MODEL, N_KV_GROUPS, per_group), ((0, 0), (0, 0), (0, LANES - per_group)))
    w_gn = w_gn.reshape(D_MODEL, N_KV_GROUPS * LANES)
    gap = jnp.zeros((D_MODEL, COL_GA - COL_GN - N_KV_GROUPS * LANES), w_in.dtype)
    w = jnp.concatenate([_pad_blocks(w_u, 1), _pad_blocks(w_gate, 1), w_q, w_ks, w_vs, w_kw, w_vw, w_kc, w_vc,
                         w_gn, gap, w_ga, w_gb], axis=1)
    assert w.shape[1] == D_PROJ
    return w.astype(BF16)


def _phi_weights(pe, w1, w2):
    half = CMP_BLOCK // 2
    eye = jnp.eye(N_KV_GROUPS, dtype=w1.dtype)
    w1h = w1.reshape(2, half, HEAD_DIM, PHI_HIDDEN)
    w1e = jnp.einsum("xldh,gk->xlgdkh", w1h, eye).reshape(2, half * KV_W, N_KV_GROUPS * PHI_HIDDEN)
    w2e = jnp.einsum("hd,gk->ghkd", w2, eye).reshape(N_KV_GROUPS * PHI_HIDDEN, KV_W)
    pee = jnp.broadcast_to(pe.reshape(2, half, 1, HEAD_DIM), (2, half, N_KV_GROUPS, HEAD_DIM)).reshape(2, half * KV_W)
    return pee, w1e.astype(BF16), w2e.astype(BF16)


def kernel(x, norm_mix, w_in, conv_w, conv_b, gate_a_w, gate_a_b, gate_x_w, gate_x_b, lru_lambda, phi_k_pe, phi_k_w1, phi_k_w2, phi_v_pe, phi_v_w1, phi_v_w2, q_norm, kc_norm, ks_norm, kw_norm, rel_bias, proj_a, proj_b, w_out, norm_mlp, w_mlp_in, w_mlp_out):
    bsz, t, d = x.shape
    assert d == D_MODEL and norm_mix.shape[0] == 1
    tq = 256
    assert t % tq == 0 and t // SEL_BLOCK <= MAX_SBLK and t % CMP_STRIDE == 0
    n_tok = bsz * t
    n_chunks = t // CMP_STRIDE
    assert n_chunks % 8 == 0 and n_chunks <= LANES
    x2 = x.reshape(n_tok, d)

    proj_t, zk, zv = _norm_matmul(x2, norm_mix, _in_proj_weight(w_in[0]), bsz, tm=1024 if t % 1024 == 0 else tq,
                                  tn=1024, side_cols=((COL_KC, KV_W), (COL_VC, KV_W)))
    proj3 = proj_t.reshape(t, bsz, D_PROJ)

    pad_w = lambda w: jnp.pad(w, ((0, 0), (0, RNN_BW - LRU_BLOCK_W), (0, RNN_BW - LRU_BLOCK_W))).astype(BF16)
    ya_t = _rglru(proj3, _pad_blocks(conv_w[0], 1), _pad_blocks(conv_b, 1),
                  pad_w(gate_a_w[0]), _pad_blocks(gate_a_b.reshape(1, D_RNN), 1),
                  pad_w(gate_x_w[0]), _pad_blocks(gate_x_b.reshape(1, D_RNN), 1),
                  _pad_blocks(lru_lambda, 1), tc=128)

    zk = zk.reshape(bsz, n_chunks, CMP_STRIDE * KV_W)
    zv = zv.reshape(bsz, n_chunks, CMP_STRIDE * KV_W)
    pe_k, w1_k, w2_k = _phi_weights(phi_k_pe[0], phi_k_w1[0], phi_k_w2[0])
    pe_v, w1_v, w2_v = _phi_weights(phi_v_pe[0], phi_v_w1[0], phi_v_w2[0])
    seg256 = jnp.asarray(np.kron(np.eye(KV_W // HEAD_DIM), np.ones((HEAD_DIM, HEAD_DIM))), BF16)
    kcn = jnp.tile(kc_norm, (1, N_KV_GROUPS))
    kc = _compress(zk, pe_k, w1_k, w2_k, kcn, seg256, is_key=True)
    vc = _compress(zv, pe_v, w1_v, w2_v, kcn, seg256, is_key=False)

    seg128 = jnp.asarray(np.kron(np.eye(LANES // HEAD_DIM), np.ones((HEAD_DIM, HEAD_DIM))), BF16)
    qh, k_all, v_all = _nsa_prep(proj_t, bsz, jnp.tile(q_norm, (1, N_HEADS)), jnp.tile(ks_norm, (1, N_KV_GROUPS)),
                                 jnp.tile(kw_norm, (1, N_KV_GROUPS)), seg128, tt=tq)

    bucket_of = _t5_bucket_table()
    far = rel_bias[REL_BUCKETS - 1][:, None, None]

    def bias_table(dist, valid, shift):
        buckets = jnp.asarray(bucket_of[np.clip(dist, 0, FAR_DIST - 1)].astype(np.int8))
        onehot = (buckets[None] == jnp.arange(REL_BUCKETS, dtype=jnp.int8).reshape(-1, 1, 1)).astype(F32)
        vals = jnp.einsum("kh,kji->hji", rel_bias, onehot, precision=lax.Precision.HIGHEST)
        if shift:
            vals = vals - far
        return jnp.where(jnp.asarray(valid), vals * LOG2E, MASK_NEG).astype(F32)

    kj = np.arange(tq)[:, None]
    qi_ = np.arange(tq)[None, :]
    bias_diag = bias_table(qi_ - kj, qi_ >= kj, True)
    bias_near = bias_table(tq + qi_ - kj, np.ones((tq, tq), bool), True)
    bias_win = jnp.broadcast_to(jnp.asarray(np.where(kj > qi_, 0.0, MASK_NEG), F32), (N_HEADS, tq, tq))
    bias_tiles = jnp.stack([bias_diag, bias_near, bias_win], axis=1)
    assert WINDOW == 2 * tq
    cidx = np.arange(n_chunks)[:, None]
    dist_c = np.arange(t)[None, :] - (cidx * CMP_STRIDE + CMP_BLOCK - 1)
    bias_c = bias_table(dist_c, (dist_c >= 0) & (cidx < n_chunks - 1), False)
    cstart = np.arange(n_chunks) * CMP_STRIDE
    sj = np.arange(MAX_SBLK)
    cov = ((cstart[None, :] < (sj[:, None] + 1) * SEL_BLOCK) & (cstart[None, :] + CMP_BLOCK - 1 >= sj[:, None] * SEL_BLOCK)
           & (np.arange(n_chunks)[None, :] < n_chunks - 1))
    cover = np.zeros((LANES, n_chunks), np.float32)
    cover[SEL_ROW0:SEL_ROW0 + MAX_SBLK] = cov
    yb = _nsa_attention(qh, kc, vc, k_all, v_all, bias_c, bias_tiles, jnp.asarray(cover, BF16), proj_t, tq)

    pa = jnp.pad(proj_a[0].reshape(LRU_BLOCKS, LRU_BLOCK_W, D_MODEL), ((0, 0), (0, RNN_BW - LRU_BLOCK_W), (0, 0)))
    h = _merge_out(ya_t.reshape(t, bsz * RNN_W), yb.reshape(n_tok, Q_W), proj_t, x2,
                   pa.reshape(RNN_W, D_MODEL).astype(BF16), proj_b[0].astype(BF16), w_out[0].astype(BF16),
                   bsz, tm=512 if t % 512 == 0 else tq)

    out = _mlp(h, norm_mlp, w_mlp_in[0].astype(BF16), w_mlp_out[0].astype(BF16),
               tm=1024 if n_tok % 1024 == 0 else tq, tf=1024)
    return out.reshape(bsz, t, d)
```

```python
import functools
import math

import numpy as np
import jax
import jax.numpy as jnp
from jax import lax
from jax.experimental import pallas as pl
from jax.experimental.pallas import tpu as pltpu

F32 = jnp.float32
BF16 = jnp.bfloat16

D_MODEL = 1024
D_RNN = 1344
LRU_BLOCKS = 4
LRU_BLOCK_W = D_RNN // LRU_BLOCKS
CONV_W = 4
LRU_C = 8.0
N_HEADS = 16
HEAD_DIM = 64
N_KV_GROUPS = 4
HEADS_PER_GROUP = N_HEADS // N_KV_GROUPS
CMP_BLOCK = 32
CMP_STRIDE = 16
SEL_BLOCK = 64
N_SELECT = 16
WINDOW = 512
PHI_HIDDEN = 256
SEL_FORCED = 1e4
REL_BUCKETS = 32
REL_MAX_DIST = 128
D_FF = 4 * D_MODEL
NORM_EPS = 1e-6
Q_W = N_HEADS * HEAD_DIM
KV_W = N_KV_GROUPS * HEAD_DIM

LANES = 128
VMEM_LIMIT = 56 * 1024 * 1024

RNN_BW = 384
RNN_W = LRU_BLOCKS * RNN_BW
MASK_NEG = -1e30
SEL_NEG = -1e9
SEL_ROW0 = 64
MAX_SBLK = 32
FAR_DIST = 256
LOG2E = math.log2(math.e)
V_ROWS = HEAD_DIM + 16
BIAS_DIAG, BIAS_NEAR, BIAS_WIN = range(3)
JOBS_PER_TRIP = 2

COL_U = 0
COL_GATE = RNN_W
RNN_COLS = 2 * RNN_W
COL_Q = 0
COL_KS = COL_Q + Q_W
COL_VS = COL_KS + KV_W
COL_KW = COL_VS + KV_W
COL_VW = COL_KW + KV_W
COL_KC = COL_VW + KV_W
COL_VC = COL_KC + KV_W
COL_GN = COL_VC + KV_W
COL_GA = 3072
COL_GB = 4096
D_PROJ = 5120


def _dot(a, b):
    return jnp.dot(a, b, preferred_element_type=F32)


def _dot_nt(a, b):
    return lax.dot_general(a, b, (((1,), (1,)), ((), ())), preferred_element_type=F32)


def _gelu_tanh(x):
    return 0.5 * x * (1.0 + jnp.tanh(math.sqrt(2.0 / math.pi) * (x + 0.044715 * (x * x * x))))


def _sigmoid(x):
    return 1.0 / (1.0 + jnp.exp(-x))


def _seg_sum(x, seg_ones):
    hi = x.astype(BF16)
    lo = (x - hi.astype(F32)).astype(BF16)
    return _dot(hi, seg_ones) + _dot(lo, seg_ones)


def _params(sem, flags=None):
    return pltpu.CompilerParams(dimension_semantics=sem, vmem_limit_bytes=VMEM_LIMIT, flags=flags)


def _norm_matmul_kernel(x_ref, g_ref, w_ref, rnn_ref, o_ref, *rest, tn, rnn_tiles, side_cols):
    side_refs, xn_ref = rest[:-1], rest[-1]
    j = pl.program_id(1)

    @pl.when(j == 0)
    def _():
        x = x_ref[...]
        y = x * lax.rsqrt(jnp.mean(x * x, axis=-1, keepdims=True) + NORM_EPS) * g_ref[...]
        xn_ref[...] = y.astype(BF16)

    res = _dot(xn_ref[...], w_ref[...]).astype(o_ref.dtype)

    @pl.when(j < rnn_tiles)
    def _():
        rnn_ref[...] = res

    @pl.when(j >= rnn_tiles)
    def _():
        o_ref[...] = res

    for (col, width), side_ref in zip(side_cols, side_refs):
        @pl.when(j == rnn_tiles + col // tn)
        def _(col=col, width=width, side_ref=side_ref):
            side_ref[...] = res[:, col % tn:col % tn + width]


def _norm_matmul(x, gain, w, bsz, tm, tn, rnn_cols, side_cols):
    m, k = x.shape
    n = w.shape[1] - rnn_cols
    t = m // bsz
    tiles = t // tm
    rnn_tiles, col_tiles = rnn_cols // tn, n // tn
    assert all(col // tn == (col + width - 1) // tn for col, width in side_cols)
    return pl.pallas_call(
        functools.partial(_norm_matmul_kernel, tn=tn, rnn_tiles=rnn_tiles, side_cols=side_cols),
        grid=(m // tm, rnn_tiles + col_tiles),
        in_specs=[
            pl.BlockSpec((tm, k), lambda i, j: (i, 0)),
            pl.BlockSpec((1, k), lambda i, j: (0, 0)),
            pl.BlockSpec((k, tn), lambda i, j: (0, j)),
        ],
        out_specs=[pl.BlockSpec((tm, tn), lambda i, j: (i % tiles, (i // tiles) * rnn_tiles + jnp.minimum(j, rnn_tiles - 1))),
                   pl.BlockSpec((tm, tn), lambda i, j: (i % tiles, (i // tiles) * col_tiles + jnp.maximum(j - rnn_tiles, 0)))]
                  + [pl.BlockSpec((tm, width), lambda i, j: (i, 0)) for _, width in side_cols],
        out_shape=[jax.ShapeDtypeStruct((t, bsz * rnn_cols), BF16), jax.ShapeDtypeStruct((t, bsz * n), BF16)]
                  + [jax.ShapeDtypeStruct((m, width), BF16) for _, width in side_cols],
        scratch_shapes=[pltpu.VMEM((tm, k), BF16)],
        compiler_params=_params(("parallel", "arbitrary")),
    )(x, gain, w)


def _rglru_kernel(u_ref, ug_ref, cw_ref, cb_ref, wa_ref, ba_ref, wx_ref, bx_ref, lam_ref,
                  y_ref, ubuf, a_scr, b_scr, hcar, *, tc, bsz):
    t = pl.program_id(1)
    rows = tc * bsz
    halo = (CONV_W - 1) * bsz

    @pl.when(t == 0)
    def _():
        ubuf[0:halo, :] = jnp.zeros((halo, RNN_BW), F32)
        hcar[...] = jnp.zeros_like(hcar)

    u = u_ref[...].astype(F32).reshape(rows, RNN_BW)
    ubuf[halo:halo + rows, :] = u
    xc = cb_ref[...]
    for k in range(CONV_W):
        xc = xc + cw_ref[k:k + 1, :] * ubuf[k * bsz:k * bsz + rows, :]
    ubuf[0:halo, :] = u[rows - halo:rows, :]

    xb = xc.astype(BF16)
    r = _sigmoid(_dot(xb, wa_ref[...]) + ba_ref[...])
    i = _sigmoid(_dot(xb, wx_ref[...]) + bx_ref[...])
    z = -lam_ref[...]
    softplus = jnp.maximum(z, 0.0) + jnp.log(1.0 + jnp.exp(-jnp.abs(z)))
    log_a = (-LRU_C) * r * softplus
    mult = jnp.sqrt(1.0 - jnp.exp(2.0 * log_a))
    row = lax.broadcasted_iota(jnp.int32, (rows, 1), 0)
    mult = jnp.where((row < bsz) & (t == 0), 1.0, mult)
    a_scr[...] = jnp.exp(log_a)
    b_scr[...] = mult * (i * xc)

    def step(s, h):
        at = pl.ds(pl.multiple_of(s * bsz, bsz), bsz)
        h = a_scr[at, :] * h + b_scr[at, :]
        b_scr[at, :] = h
        return h

    hcar[...] = lax.fori_loop(0, tc, step, hcar[...], unroll=8)
    y = b_scr[...] * _gelu_tanh(ug_ref[...].astype(F32).reshape(rows, RNN_BW))
    y_ref[...] = y.reshape(tc, bsz, RNN_BW).astype(y_ref.dtype)


def _rglru(proj3, conv_w, conv_b, wa, ba, wx, bx, lam, tc):
    t, bsz, _ = proj3.shape
    assert bsz % 8 == 0, "the recurrence advances whole sublane groups of sequences"
    nb = LRU_BLOCKS
    rows = tc * bsz
    vec = pl.BlockSpec((1, RNN_BW), lambda n, s: (0, n))
    mat = pl.BlockSpec((None, RNN_BW, RNN_BW), lambda n, s: (n, 0, 0))
    return pl.pallas_call(
        functools.partial(_rglru_kernel, tc=tc, bsz=bsz),
        grid=(nb, t // tc),
        in_specs=[
            pl.BlockSpec((tc, bsz, RNN_BW), lambda n, s: (s, 0, COL_U // RNN_BW + n)),
            pl.BlockSpec((tc, bsz, RNN_BW), lambda n, s: (s, 0, COL_GATE // RNN_BW + n)),
            pl.BlockSpec((CONV_W, RNN_BW), lambda n, s: (0, n)),
            vec, mat, vec, mat, vec, vec,
        ],
        out_specs=pl.BlockSpec((tc, bsz, RNN_BW), lambda n, s: (s, 0, n)),
        out_shape=jax.ShapeDtypeStruct((t, bsz, RNN_W), BF16),
        scratch_shapes=[pltpu.VMEM((rows + (CONV_W - 1) * bsz, RNN_BW), F32),
                        pltpu.VMEM((rows, RNN_BW), F32), pltpu.VMEM((rows, RNN_BW), F32),
                        pltpu.VMEM((bsz, RNN_BW), F32)],
        compiler_params=_params(("parallel", "arbitrary")),
    )(proj3, proj3, conv_w, conv_b, wa, ba, wx, bx, lam)


def _compress_kernel(z_ref, pe_ref, w1_ref, w2_ref, norm_ref, seg_ref, o_ref, *, n_chunks, is_key):
    z = z_ref[...].astype(F32)
    first = _dot((z + pe_ref[0:1, :]).astype(BF16), w1_ref[0])
    second = _dot((z + pe_ref[1:2, :]).astype(BF16), w1_ref[1])
    pre = first + pltpu.roll(second, n_chunks - 1, 0)
    out = _dot(_gelu_tanh(pre).astype(BF16), w2_ref[...])
    if is_key:
        ssq = _seg_sum(out * out, seg_ref[...])
        normed = out * lax.rsqrt(ssq * (1.0 / HEAD_DIM) + NORM_EPS) * norm_ref[...]
        low = lax.broadcasted_iota(jnp.int32, (n_chunks, LANES), 1) < HEAD_DIM
        for pair in range(N_KV_GROUPS // 2):
            blk = normed[:, pair * LANES:(pair + 1) * LANES]
            o_ref[2 * pair] = jnp.where(low, blk, 0.0).astype(o_ref.dtype)
            o_ref[2 * pair + 1] = jnp.where(low, pltpu.roll(blk, HEAD_DIM, 1), 0.0).astype(o_ref.dtype)
    else:
        out_t = jnp.transpose(out)
        for g in range(N_KV_GROUPS):
            o_ref[g] = out_t[g * HEAD_DIM:(g + 1) * HEAD_DIM, :].astype(o_ref.dtype)


def _compress(z, pe, w1, w2, norm, seg, is_key):
    bsz, n_chunks, width = z.shape
    hid = w1.shape[-1]
    out_block = (N_KV_GROUPS, n_chunks, LANES) if is_key else (N_KV_GROUPS, HEAD_DIM, n_chunks)
    return pl.pallas_call(
        functools.partial(_compress_kernel, n_chunks=n_chunks, is_key=is_key),
        grid=(bsz,),
        in_specs=[
            pl.BlockSpec((None, n_chunks, width), lambda b: (b, 0, 0)),
            pl.BlockSpec((2, width), lambda b: (0, 0)),
            pl.BlockSpec((2, width, hid), lambda b: (0, 0, 0)),
            pl.BlockSpec((hid, KV_W), lambda b: (0, 0)),
            pl.BlockSpec((1, KV_W), lambda b: (0, 0)),
            pl.BlockSpec((KV_W, KV_W), lambda b: (0, 0)),
        ],
        out_specs=pl.BlockSpec((None,) + out_block, lambda b: (b, 0, 0, 0)),
        out_shape=jax.ShapeDtypeStruct((bsz,) + out_block, BF16),
        compiler_params=_params(("parallel",)),
    )(z, pe, w1, w2, norm, seg)


def _nsa_prep_kernel(q_ref, ks_ref, vs_ref, kw_ref, vw_ref, qn_ref, ksn_ref, kwn_ref, seg_ref,
                     qo_ref, ko_ref, vo_ref, *, tt):
    t0 = pl.program_id(1) * tt
    lane = lax.broadcasted_iota(jnp.int32, (tt, LANES), 1)
    row = lax.broadcasted_iota(jnp.int32, (tt, LANES), 0) + t0
    low = lane < HEAD_DIM
    seg = seg_ref[...]
    onehot = jnp.where((lane - SEL_ROW0) == row // SEL_BLOCK, 1.0, 0.0)

    def normed(ref, gain_ref, blk, scale):
        x = ref[:, blk * LANES:(blk + 1) * LANES].astype(F32)
        ssq = _seg_sum(x * x, seg)
        y = x * lax.rsqrt(ssq * (1.0 / HEAD_DIM) + NORM_EPS) * gain_ref[:, blk * LANES:(blk + 1) * LANES]
        return y * scale if scale != 1.0 else y

    q_pad = jnp.zeros((LANES - HEAD_DIM, tt), BF16)
    for blk in range(Q_W // LANES):
        y_t = jnp.transpose(normed(q_ref, qn_ref, blk, HEAD_DIM ** -0.5 * LOG2E)).astype(BF16)
        for half in range(2):
            qo_ref[2 * blk + half, 0:HEAD_DIM, :] = y_t[half * HEAD_DIM:(half + 1) * HEAD_DIM, :]
            qo_ref[2 * blk + half, HEAD_DIM:LANES, :] = q_pad
    ones_row = jnp.where(lax.broadcasted_iota(jnp.int32, (V_ROWS - HEAD_DIM, tt), 0) == 0, 1.0, 0.0).astype(BF16)
    for blk in range(KV_W // LANES):
        y = normed(ks_ref, ksn_ref, blk, 1.0)
        ysw = pltpu.roll(y, HEAD_DIM, 1)
        ko_ref[2 * blk, 0] = jnp.where(low, y, onehot).astype(BF16)
        ko_ref[2 * blk + 1, 0] = jnp.where(low, ysw, onehot).astype(BF16)
        y = normed(kw_ref, kwn_ref, blk, 1.0)
        ysw = pltpu.roll(y, HEAD_DIM, 1)
        ko_ref[2 * blk, 1] = jnp.where(low, y, 0.0).astype(BF16)
        ko_ref[2 * blk + 1, 1] = jnp.where(low, ysw, 0.0).astype(BF16)
        for branch, src in enumerate((vs_ref, vw_ref)):
            v_t = jnp.transpose(src[:, blk * LANES:(blk + 1) * LANES].astype(F32))
            for half in range(2):
                g = 2 * blk + half
                vo_ref[g, branch, 0:HEAD_DIM, :] = v_t[half * HEAD_DIM:(half + 1) * HEAD_DIM, :].astype(BF16)
                vo_ref[g, branch, HEAD_DIM:V_ROWS, :] = ones_row


def _nsa_prep(proj_t, bsz, qn, ksn, kwn, seg, tt):
    t = proj_t.shape[0]

    def col(width, offset):
        return pl.BlockSpec((tt, width), lambda b, s: (s, (b * D_PROJ + offset) // width))

    def vec(width):
        return pl.BlockSpec((1, width), lambda b, s: (0, 0))

    g = N_KV_GROUPS
    return pl.pallas_call(
        functools.partial(_nsa_prep_kernel, tt=tt),
        grid=(bsz, t // tt),
        in_specs=[col(Q_W, COL_Q), col(KV_W, COL_KS), col(KV_W, COL_VS), col(KV_W, COL_KW), col(KV_W, COL_VW),
                  vec(Q_W), vec(KV_W), vec(KV_W), pl.BlockSpec((LANES, LANES), lambda b, s: (0, 0))],
        out_specs=[pl.BlockSpec((None, N_HEADS, LANES, tt), lambda b, s: (b, 0, 0, s)),
                   pl.BlockSpec((None, g, 2, tt, LANES), lambda b, s: (b, 0, 0, s, 0)),
                   pl.BlockSpec((None, g, 2, V_ROWS, tt), lambda b, s: (b, 0, 0, 0, s))],
        out_shape=[jax.ShapeDtypeStruct((bsz, N_HEADS, LANES, t), BF16),
                   jax.ShapeDtypeStruct((bsz, g, 2, t, LANES), BF16),
                   jax.ShapeDtypeStruct((bsz, g, 2, V_ROWS, t), BF16)],
        compiler_params=_params(("parallel", "parallel")),
    )(proj_t, proj_t, proj_t, proj_t, proj_t, qn, ksn, kwn, seg)


def _nsa_kernel(q_ref, kc_ref, vc_ref, k_ref, v_ref, bc_ref, bias_ref, cover_ref, gate_ref, o_ref,
                qq_ref, *scratch, tq, n_q, n_sblk):
    qi = pl.program_id(2)
    t0 = qi * tq
    rg = HEADS_PER_GROUP
    jpt = JOBS_PER_TRIP
    s_refs, smax_refs, p_refs = ((scratch[k:k + jpt], scratch[k + jpt:k + 2 * jpt]) for k in (0, 2 * jpt, 4 * jpt))
    m_refs, acc_refs = scratch[6 * jpt:6 * jpt + rg], scratch[6 * jpt + rg:]
    sel, win = 0, 1


    bias = jnp.concatenate([bc_ref[r] for r in range(rg)], axis=1)
    s = _dot(kc_ref[...], jnp.concatenate([q_ref[r] for r in range(rg)], axis=1)) + bias
    visible = bias > 0.5 * MASK_NEG
    p = jnp.exp2(s - jnp.max(s, axis=0, keepdims=True))
    p = jnp.where(visible, p / jnp.sum(p, axis=0, keepdims=True), 0.0)
    o_cmp_all = _dot(vc_ref[...], p.astype(BF16))
    o_cmp = [o_cmp_all[:, r * tq:(r + 1) * tq] for r in range(rg)]
    p_sum = p[:, 0:tq]
    for r in range(1, rg):
        p_sum = p_sum + p[:, r * tq:(r + 1) * tq]
    p_hi = p_sum.astype(BF16)
    p_lo = (p_sum - p_hi.astype(F32)).astype(BF16)
    imp_t = _dot(cover_ref[...], p_hi) + _dot(cover_ref[...], p_lo)

    score = imp_t[SEL_ROW0:SEL_ROW0 + MAX_SBLK, :]
    jrow = lax.broadcasted_iota(jnp.int32, (MAX_SBLK, tq), 0)
    qblk = (lax.broadcasted_iota(jnp.int32, (MAX_SBLK, tq), 1) + t0) // SEL_BLOCK
    causal = jrow <= qblk
    forced = causal & ((jrow == 0) | (jrow >= qblk - 1))
    score = jnp.where(forced, SEL_FORCED, jnp.where(causal, score, -1.0))
    rank = jnp.zeros((MAX_SBLK, tq), F32)
    for j in range(n_sblk):
        other = score[j:j + 1, :]
        ahead = (other > score) | ((other == score) & (jrow > j))
        rank = rank + jnp.where(ahead, 1.0, 0.0)
    n_top = min(N_SELECT, n_sblk)
    selected = (rank < n_top) & (score >= 0.0)
    neg = jnp.where(selected, 0.0, SEL_NEG).astype(BF16)
    for r in range(rg):
        qq_ref[r, 0:SEL_ROW0, :] = q_ref[r, 0:SEL_ROW0, :]
        qq_ref[r, SEL_ROW0:SEL_ROW0 + MAX_SBLK, :] = neg
        qq_ref[r, SEL_ROW0 + MAX_SBLK:LANES, :] = q_ref[r, SEL_ROW0 + MAX_SBLK:LANES, :]

    near_max = WINDOW // tq + 1
    plan = [(branch, a) for a in range(min(near_max, n_q)) for branch in (sel, win)]
    plan += [(sel, a) for a in range(near_max, n_q)]
    plan += [(sel, n_q)] * (-len(plan) % JOBS_PER_TRIP)
    groups = [plan[g:g + JOBS_PER_TRIP] for g in range(0, len(plan), JOBS_PER_TRIP)]
    n_groups = sum((qi >= group[0][1]).astype(jnp.int32) for group in groups)

    def jobs_of(i):
        jobs = []
        for branch, a in groups[i]:
            kind = (BIAS_DIAG, BIAS_NEAR, BIAS_WIN if branch == win else None)[min(a, 2)]
            real = None if a <= groups[i][0][1] else a <= qi
            tile = qi - a if real is None else jnp.where(real, qi - a, 0)
            jobs.append((branch, pl.multiple_of(tile * tq, tq), kind, real))
        return jobs

    def scores(i):
        for (branch, k0, kind, real), s_ref, smax_ref in zip(jobs_of(i), s_refs[i % 2], smax_refs[i % 2]):
            k = k_ref[branch, pl.ds(k0, tq), :]
            for r in range(rg):
                q_t = qq_ref[r] if branch == sel else q_ref[r]
                s = _dot(k, q_t)
                if kind is not None:
                    s = s + bias_ref[r, kind]
                s_ref[r] = s
                smax = jnp.max(s, axis=0, keepdims=True)
                smax_ref[r] = smax if real is None else jnp.where(real, smax, MASK_NEG)

    def softmax(i):
        jobs = jobs_of(i)
        for r in range(rg):
            for branch in sorted({b for b, _ in groups[i]}):
                mine = [j for j, (b, _) in enumerate(groups[i]) if b == branch]
                m_old = m_refs[r][branch]
                m_new = m_old
                for j in mine:
                    m_new = jnp.maximum(m_new, smax_refs[i % 2][j][r])
                for j in mine:
                    p = jnp.exp2(s_refs[i % 2][j][r] - m_new).astype(BF16)
                    real = jobs[j][3]
                    p_refs[i % 2][j][r] = p if real is None else jnp.where(real, p, jnp.zeros_like(p))
                m_refs[r][branch] = m_new
                acc_refs[r][branch] = jnp.exp2(m_old - m_new) * acc_refs[r][branch]

    def values(i):
        for (branch, k0, _, _), p_ref in zip(jobs_of(i), p_refs[i % 2]):
            v_t = v_ref[branch, :, pl.ds(k0, tq)]
            for r in range(rg):
                acc_refs[r][branch] += _dot(v_t, p_ref[r])

    for r in range(rg):
        m_refs[r][...] = jnp.full(m_refs[r].shape, MASK_NEG, F32)
        acc_refs[r][...] = jnp.zeros(acc_refs[r].shape, F32)
    scores(0)
    for i in range(len(groups)):
        if i + 1 < len(groups):
            @pl.when(i + 1 < n_groups)
            def _(i=i):
                if i > 0:
                    values(i - 1)
                softmax(i)
                scores(i + 1)

        @pl.when(i + 1 == n_groups)
        def _(i=i):
            if i > 0:
                values(i - 1)
            softmax(i)
            values(i)

    def finish(branch):
        return [acc_refs[r][branch, 0:HEAD_DIM, :] / acc_refs[r][branch, HEAD_DIM:HEAD_DIM + 1, :] for r in range(rg)]

    o_sel = finish(sel)
    o_win = finish(win)

    gates_t = jnp.transpose(_sigmoid(gate_ref[...].astype(F32)))
    outs = []
    for r in range(rg):
        outs.append(gates_t[3 * r:3 * r + 1, :] * o_cmp[r]
                    + gates_t[3 * r + 1:3 * r + 2, :] * o_sel[r]
                    + gates_t[3 * r + 2:3 * r + 3, :] * o_win[r])
    o_ref[...] = jnp.transpose(jnp.concatenate(outs, axis=0)).astype(o_ref.dtype)


def _nsa_attention(qh, kc, vc, k_all, v_all, bias_c, bias_tiles, cover, proj3, tq):
    bsz, _, _, t = qh.shape
    n_chunks = kc.shape[2]
    rg = HEADS_PER_GROUP
    n_kinds = bias_tiles.shape[1]
    return pl.pallas_call(
        functools.partial(_nsa_kernel, tq=tq, n_q=t // tq, n_sblk=t // SEL_BLOCK),
        grid=(bsz, N_KV_GROUPS, t // tq),
        in_specs=[
            pl.BlockSpec((None, rg, LANES, tq), lambda b, g, i: (b, g, 0, i)),
            pl.BlockSpec((None, None, n_chunks, LANES), lambda b, g, i: (b, g, 0, 0)),
            pl.BlockSpec((None, None, HEAD_DIM, n_chunks), lambda b, g, i: (b, g, 0, 0)),
            pl.BlockSpec((None, None, 2, t, LANES), lambda b, g, i: (b, g, 0, 0, 0)),
            pl.BlockSpec((None, None, 2, V_ROWS, t), lambda b, g, i: (b, g, 0, 0, 0)),
            pl.BlockSpec((rg, n_chunks, tq), lambda b, g, i: (g, 0, i)),
            pl.BlockSpec((rg, n_kinds, tq, tq), lambda b, g, i: (g, 0, 0, 0)),
            pl.BlockSpec((LANES, n_chunks), lambda b, g, i: (0, 0)),
            pl.BlockSpec((tq, LANES), lambda b, g, i: (i, (b * D_PROJ + COL_GN) // LANES + g)),
        ],
        out_specs=pl.BlockSpec((None, tq, rg * HEAD_DIM), lambda b, g, i: (b, i, g)),
        out_shape=jax.ShapeDtypeStruct((bsz, t, Q_W), BF16),
        scratch_shapes=[
            pltpu.VMEM((rg, LANES, tq), BF16),
        ] + [pltpu.VMEM((rg, tq, tq), F32)] * (2 * JOBS_PER_TRIP)
          + [pltpu.VMEM((rg, 1, tq), F32)] * (2 * JOBS_PER_TRIP)
          + [pltpu.VMEM((rg, tq, tq), BF16)] * (2 * JOBS_PER_TRIP)
          + [pltpu.VMEM((2, 1, tq), F32)] * rg
          + [pltpu.VMEM((2, V_ROWS, tq), F32)] * rg,
        compiler_params=_params(("parallel", "parallel", "arbitrary")),
    )(qh, kc, vc, k_all, v_all, bias_c, bias_tiles, cover, proj3)


def _merge_kernel(ya_ref, yb_ref, ga_ref, gb_ref, x_ref, pa_ref, pb_ref, wo_ref, h_ref):
    merged = (_sigmoid(ga_ref[...].astype(F32)) * _dot(ya_ref[...], pa_ref[...])
              + _sigmoid(gb_ref[...].astype(F32)) * _dot(yb_ref[...], pb_ref[...]))
    h_ref[...] = x_ref[...] + _dot(merged.astype(BF16), wo_ref[...])


def _merge_out(ya_t, yb, proj_t, x, pa, pb, wo, bsz, tm):
    t = ya_t.shape[0]
    tiles = t // tm

    def rows(width):
        return pl.BlockSpec((tm, width), lambda b, s: (b * tiles + s, 0))

    def time_major(width, offset):
        return pl.BlockSpec((tm, width), lambda b, s: (s, (b * D_PROJ + offset) // width))

    def whole(a):
        return pl.BlockSpec(a.shape, lambda b, s: (0, 0))

    return pl.pallas_call(
        _merge_kernel,
        grid=(bsz, tiles),
        in_specs=[pl.BlockSpec((tm, RNN_W), lambda b, s: (s, b)), rows(Q_W), time_major(D_MODEL, COL_GA),
                  time_major(D_MODEL, COL_GB), rows(D_MODEL), whole(pa), whole(pb), whole(wo)],
        out_specs=rows(D_MODEL),
        out_shape=jax.ShapeDtypeStruct((bsz * t, D_MODEL), F32),
        compiler_params=_params(("parallel", "parallel")),
    )(ya_t, yb, proj_t, proj_t, x, pa, pb, wo)


def _mlp_kernel(h_ref, g_ref, w1_ref, w2_ref, o_ref, hn_ref, acc_ref):
    j = pl.program_id(1)

    @pl.when(j == 0)
    def _():
        h = h_ref[...]
        y = h * lax.rsqrt(jnp.mean(h * h, axis=-1, keepdims=True) + NORM_EPS) * g_ref[...]
        hn_ref[...] = y.astype(BF16)
        acc_ref[...] = h

    z = jnp.maximum(_dot(hn_ref[...], w1_ref[...]), 0.0)
    acc_ref[...] += _dot((z * z).astype(BF16), w2_ref[...])

    @pl.when(j == pl.num_programs(1) - 1)
    def _():
        o_ref[...] = acc_ref[...]


def _mlp(h, gain, w1, w2, tm, tf):
    m, d = h.shape
    ff = w1.shape[1]
    return pl.pallas_call(
        _mlp_kernel,
        grid=(m // tm, ff // tf),
        in_specs=[
            pl.BlockSpec((tm, d), lambda i, j: (i, 0)),
            pl.BlockSpec((1, d), lambda i, j: (0, 0)),
            pl.BlockSpec((d, tf), lambda i, j: (0, j)),
            pl.BlockSpec((tf, d), lambda i, j: (j, 0)),
        ],
        out_specs=pl.BlockSpec((tm, d), lambda i, j: (i, 0)),
        out_shape=jax.ShapeDtypeStruct((m, d), F32),
        scratch_shapes=[pltpu.VMEM((tm, d), BF16), pltpu.VMEM((tm, d), F32)],
        compiler_params=_params(("parallel", "arbitrary")),
    )(h, gain, w1, w2)


def _t5_bucket_table():
    max_exact = REL_BUCKETS // 2
    d = np.arange(FAR_DIST)
    df = np.maximum(d.astype(np.float32), np.float32(1.0))
    large = max_exact + (np.log(df / np.float32(max_exact)) / np.float32(math.log(REL_MAX_DIST / max_exact))
                         * np.float32(REL_BUCKETS - max_exact)).astype(np.int32)
    large = np.minimum(large, REL_BUCKETS - 1)
    return np.where(d < max_exact, d, large).astype(np.int32)


def _pad_blocks(w, axis):
    shape = w.shape
    w = w.reshape(shape[:axis] + (LRU_BLOCKS, LRU_BLOCK_W) + shape[axis + 1:])
    pad = [(0, 0)] * w.ndim
    pad[axis + 1] = (0, RNN_BW - LRU_BLOCK_W)
    w = jnp.pad(w, pad)
    return w.reshape(shape[:axis] + (RNN_W,) + shape[axis + 1:])


def _in_proj_weight(w_in):
    cuts = np.cumsum((D_RNN, D_RNN, Q_W, KV_W, KV_W, KV_W, KV_W, KV_W, KV_W, 3 * N_HEADS, D_MODEL))
    (w_u, w_gate, w_q, w_kc, w_vc, w_ks, w_vs, w_kw, w_vw, w_gn, w_ga, w_gb) = jnp.split(w_in, cuts, axis=1)
    per_group = 3 * HEADS_PER_GROUP
    w_gn = jnp.pad(w_gn.reshape(D_MODEL, N_KV_GROUPS, per_group), ((0, 0), (0, 0), (0, LANES - per_group)))
    w_gn = w_gn.reshape(D_MODEL, N_KV_GROUPS * LANES)
    gap = jnp.zeros((D_MODEL, COL_GA - COL_GN - N_KV_GROUPS * LANES), w_in.dtype)
    w = jnp.concatenate([_pad_blocks(w_u, 1), _pad_blocks(w_gate, 1), w_q, w_ks, w_vs, w_kw, w_vw, w_kc, w_vc,
                         w_gn, gap, w_ga, w_gb], axis=1)
    assert w.shape[1] == RNN_COLS + D_PROJ
    return w.astype(BF16)


def _phi_weights(pe, w1, w2):
    half = CMP_BLOCK // 2
    eye = jnp.eye(N_KV_GROUPS, dtype=w1.dtype)
    w1h = w1.reshape(2, half, HEAD_DIM, PHI_HIDDEN)
    w1e = jnp.einsum("xldh,gk->xlgdkh", w1h, eye).reshape(2, half * KV_W, N_KV_GROUPS * PHI_HIDDEN)
    w2e = jnp.einsum("hd,gk->ghkd", w2, eye).reshape(N_KV_GROUPS * PHI_HIDDEN, KV_W)
    pee = jnp.broadcast_to(pe.reshape(2, half, 1, HEAD_DIM), (2, half, N_KV_GROUPS, HEAD_DIM)).reshape(2, half * KV_W)
    return pee, w1e.astype(BF16), w2e.astype(BF16)


def kernel(x, norm_mix, w_in, conv_w, conv_b, gate_a_w, gate_a_b, gate_x_w, gate_x_b, lru_lambda, phi_k_pe, phi_k_w1, phi_k_w2, phi_v_pe, phi_v_w1, phi_v_w2, q_norm, kc_norm, ks_norm, kw_norm, rel_bias, proj_a, proj_b, w_out, norm_mlp, w_mlp_in, w_mlp_out):
    bsz, t, d = x.shape
    assert d == D_MODEL and norm_mix.shape[0] == 1
    tq = 256
    assert t % tq == 0 and t // SEL_BLOCK <= MAX_SBLK and t % CMP_STRIDE == 0
    n_tok = bsz * t
    n_chunks = t // CMP_STRIDE
    assert n_chunks % 8 == 0 and n_chunks <= LANES
    x2 = x.reshape(n_tok, d)

    rnn_t, proj_t, zk, zv = _norm_matmul(x2, norm_mix, _in_proj_weight(w_in[0]), bsz,
                                         tm=1024 if t % 1024 == 0 else tq, tn=1024, rnn_cols=RNN_COLS,
                                         side_cols=((COL_KC, KV_W), (COL_VC, KV_W)))

    pad_w = lambda w: jnp.pad(w, ((0, 0), (0, RNN_BW - LRU_BLOCK_W), (0, RNN_BW - LRU_BLOCK_W))).astype(BF16)
    ya_t = _rglru(rnn_t.reshape(t, bsz, RNN_COLS), _pad_blocks(conv_w[0], 1), _pad_blocks(conv_b, 1),
                  pad_w(gate_a_w[0]), _pad_blocks(gate_a_b.reshape(1, D_RNN), 1),
                  pad_w(gate_x_w[0]), _pad_blocks(gate_x_b.reshape(1, D_RNN), 1),
                  _pad_blocks(lru_lambda, 1), tc=128)

    zk = zk.reshape(bsz, n_chunks, CMP_STRIDE * KV_W)
    zv = zv.reshape(bsz, n_chunks, CMP_STRIDE * KV_W)
    pe_k, w1_k, w2_k = _phi_weights(phi_k_pe[0], phi_k_w1[0], phi_k_w2[0])
    pe_v, w1_v, w2_v = _phi_weights(phi_v_pe[0], phi_v_w1[0], phi_v_w2[0])
    seg256 = jnp.asarray(np.kron(np.eye(KV_W // HEAD_DIM), np.ones((HEAD_DIM, HEAD_DIM))), BF16)
    kcn = jnp.tile(kc_norm, (1, N_KV_GROUPS))
    kc = _compress(zk, pe_k, w1_k, w2_k, kcn, seg256, is_key=True)
    vc = _compress(zv, pe_v, w1_v, w2_v, kcn, seg256, is_key=False)

    seg128 = jnp.asarray(np.kron(np.eye(LANES // HEAD_DIM), np.ones((HEAD_DIM, HEAD_DIM))), BF16)
    qh, k_all, v_all = _nsa_prep(proj_t, bsz, jnp.tile(q_norm, (1, N_HEADS)), jnp.tile(ks_norm, (1, N_KV_GROUPS)),
                                 jnp.tile(kw_norm, (1, N_KV_GROUPS)), seg128, tt=tq)

    bucket_of = _t5_bucket_table()
    far = rel_bias[REL_BUCKETS - 1][:, None, None]

    def bias_table(dist, valid, shift):
        buckets = jnp.asarray(bucket_of[np.clip(dist, 0, FAR_DIST - 1)].astype(np.int8))
        onehot = (buckets[None] == jnp.arange(REL_BUCKETS, dtype=jnp.int8).reshape(-1, 1, 1)).astype(F32)
        vals = jnp.einsum("kh,kji->hji", rel_bias, onehot, precision=lax.Precision.HIGHEST)
        if shift:
            vals = vals - far
        return jnp.where(jnp.asarray(valid), vals * LOG2E, MASK_NEG).astype(F32)

    kj = np.arange(tq)[:, None]
    qi_ = np.arange(tq)[None, :]
    bias_diag = bias_table(qi_ - kj, qi_ >= kj, True)
    bias_near = bias_table(tq + qi_ - kj, np.ones((tq, tq), bool), True)
    bias_win = jnp.broadcast_to(jnp.asarray(np.where(kj > qi_, 0.0, MASK_NEG), F32), (N_HEADS, tq, tq))
    bias_tiles = jnp.stack([bias_diag, bias_near, bias_win], axis=1)
    assert WINDOW == 2 * tq
    cidx = np.arange(n_chunks)[:, None]
    dist_c = np.arange(t)[None, :] - (cidx * CMP_STRIDE + CMP_BLOCK - 1)
    bias_c = bias_table(dist_c, (dist_c >= 0) & (cidx < n_chunks - 1), False)
    cstart = np.arange(n_chunks) * CMP_STRIDE
    sj = np.arange(MAX_SBLK)
    cov = ((cstart[None, :] < (sj[:, None] + 1) * SEL_BLOCK) & (cstart[None, :] + CMP_BLOCK - 1 >= sj[:, None] * SEL_BLOCK)
           & (np.arange(n_chunks)[None, :] < n_chunks - 1))
    cover = np.zeros((LANES, n_chunks), np.float32)
    cover[SEL_ROW0:SEL_ROW0 + MAX_SBLK] = cov
    yb = _nsa_attention(qh, kc, vc, k_all, v_all, bias_c, bias_tiles, jnp.asarray(cover, BF16), proj_t, tq)

    pa = jnp.pad(proj_a[0].reshape(LRU_BLOCKS, LRU_BLOCK_W, D_MODEL), ((0, 0), (0, RNN_BW - LRU_BLOCK_W), (0, 0)))
    h = _merge_out(ya_t.reshape(t, bsz * RNN_W), yb.reshape(n_tok, Q_W), proj_t, x2,
                   pa.reshape(RNN_W, D_MODEL).astype(BF16), proj_b[0].astype(BF16), w_out[0].astype(BF16),
                   bsz, tm=512 if t % 512 == 0 else tq)

    out = _mlp(h, norm_mlp, w_mlp_in[0].astype(BF16), w_mlp_out[0].astype(BF16),
               tm=1024 if n_tok % 1024 == 0 else tq, tf=1024)
    return out.reshape(bsz, t, d)
```

```python
import functools
import math

import numpy as np
import jax
import jax.numpy as jnp
from jax import lax
from jax.experimental import pallas as pl
from jax.experimental.pallas import tpu as pltpu

F32 = jnp.float32
BF16 = jnp.bfloat16

D_MODEL = 1024
D_RNN = 1344
LRU_BLOCKS = 4
LRU_BLOCK_W = D_RNN // LRU_BLOCKS
CONV_W = 4
LRU_C = 8.0
N_HEADS = 16
HEAD_DIM = 64
N_KV_GROUPS = 4
HEADS_PER_GROUP = N_HEADS // N_KV_GROUPS
CMP_BLOCK = 32
CMP_STRIDE = 16
SEL_BLOCK = 64
N_SELECT = 16
WINDOW = 512
PHI_HIDDEN = 256
SEL_FORCED = 1e4
REL_BUCKETS = 32
REL_MAX_DIST = 128
D_FF = 4 * D_MODEL
NORM_EPS = 1e-6
Q_W = N_HEADS * HEAD_DIM
KV_W = N_KV_GROUPS * HEAD_DIM

LANES = 128
VMEM_LIMIT = 56 * 1024 * 1024

RNN_BW = 384
RNN_W = LRU_BLOCKS * RNN_BW
MASK_NEG = -1e30
SEL_NEG = -1e9
SEL_ROW0 = 64
MAX_SBLK = 32
FAR_DIST = 256
LOG2E = math.log2(math.e)
V_ROWS = HEAD_DIM + 16
BIAS_DIAG, BIAS_NEAR, BIAS_WIN = range(3)
JOBS_PER_TRIP = 2

COL_U = 0
COL_GATE = RNN_W
RNN_COLS = 2 * RNN_W
COL_Q = 0
COL_KS = COL_Q + Q_W
COL_VS = COL_KS + KV_W
COL_KW = COL_VS + KV_W
COL_VW = COL_KW + KV_W
COL_KC = COL_VW + KV_W
COL_VC = COL_KC + KV_W
COL_GN = COL_VC + KV_W
COL_GA = 3072
COL_GB = 4096
D_PROJ = 5120


def _dot(a, b):
    return jnp.dot(a, b, preferred_element_type=F32)


def _dot_nt(a, b):
    return lax.dot_general(a, b, (((1,), (1,)), ((), ())), preferred_element_type=F32)


def _gelu_tanh(x):
    return 0.5 * x * (1.0 + jnp.tanh(math.sqrt(2.0 / math.pi) * (x + 0.044715 * (x * x * x))))


def _sigmoid(x):
    return 0.5 * jnp.tanh(0.5 * x) + 0.5


def _seg_sum(x, seg_ones):
    hi = x.astype(BF16)
    lo = (x - hi.astype(F32)).astype(BF16)
    return _dot(hi, seg_ones) + _dot(lo, seg_ones)


def _params(sem, flags=None):
    return pltpu.CompilerParams(dimension_semantics=sem, vmem_limit_bytes=VMEM_LIMIT, flags=flags)


def _norm_matmul_kernel(x_ref, g_ref, w_ref, rnn_ref, o_ref, *rest, tn, rnn_tiles, side_cols):
    side_refs, xn_ref = rest[:-1], rest[-1]
    j = pl.program_id(1)

    @pl.when(j == 0)
    def _():
        x = x_ref[...]
        y = x * lax.rsqrt(jnp.mean(x * x, axis=-1, keepdims=True) + NORM_EPS) * g_ref[...]
        xn_ref[...] = y.astype(BF16)

    res = _dot(xn_ref[...], w_ref[...]).astype(o_ref.dtype)

    @pl.when(j < rnn_tiles)
    def _():
        rnn_ref[...] = res

    @pl.when(j >= rnn_tiles)
    def _():
        o_ref[...] = res

    for (col, width), side_ref in zip(side_cols, side_refs):
        @pl.when(j == rnn_tiles + col // tn)
        def _(col=col, width=width, side_ref=side_ref):
            side_ref[...] = res[:, col % tn:col % tn + width]


def _norm_matmul(x, gain, w, bsz, tm, tn, rnn_cols, side_cols):
    m, k = x.shape
    n = w.shape[1] - rnn_cols
    t = m // bsz
    tiles = t // tm
    rnn_tiles, col_tiles = rnn_cols // tn, n // tn
    assert all(col // tn == (col + width - 1) // tn for col, width in side_cols)
    return pl.pallas_call(
        functools.partial(_norm_matmul_kernel, tn=tn, rnn_tiles=rnn_tiles, side_cols=side_cols),
        grid=(m // tm, rnn_tiles + col_tiles),
        in_specs=[
            pl.BlockSpec((tm, k), lambda i, j: (i, 0)),
            pl.BlockSpec((1, k), lambda i, j: (0, 0)),
            pl.BlockSpec((k, tn), lambda i, j: (0, j)),
        ],
        out_specs=[pl.BlockSpec((tm, tn), lambda i, j: (i % tiles, (i // tiles) * rnn_tiles + jnp.minimum(j, rnn_tiles - 1))),
                   pl.BlockSpec((tm, tn), lambda i, j: (i % tiles, (i // tiles) * col_tiles + jnp.maximum(j - rnn_tiles, 0)))]
                  + [pl.BlockSpec((tm, width), lambda i, j: (i, 0)) for _, width in side_cols],
        out_shape=[jax.ShapeDtypeStruct((t, bsz * rnn_cols), BF16), jax.ShapeDtypeStruct((t, bsz * n), BF16)]
                  + [jax.ShapeDtypeStruct((m, width), BF16) for _, width in side_cols],
        scratch_shapes=[pltpu.VMEM((tm, k), BF16)],
        compiler_params=_params(("parallel", "arbitrary")),
    )(x, gain, w)


def _rglru_kernel(u_ref, ug_ref, cw_ref, cb_ref, wa_ref, ba_ref, wx_ref, bx_ref, lam_ref,
                  y_ref, ubuf, a_scr, b_scr, hcar, *, tc, bsz):
    t = pl.program_id(1)
    rows = tc * bsz
    halo = (CONV_W - 1) * bsz

    @pl.when(t == 0)
    def _():
        ubuf[0:halo, :] = jnp.zeros((halo, RNN_BW), F32)
        hcar[...] = jnp.zeros_like(hcar)

    u = u_ref[...].astype(F32).reshape(rows, RNN_BW)
    ubuf[halo:halo + rows, :] = u
    xc = cb_ref[...]
    for k in range(CONV_W):
        xc = xc + cw_ref[k:k + 1, :] * ubuf[k * bsz:k * bsz + rows, :]
    ubuf[0:halo, :] = u[rows - halo:rows, :]

    xb = xc.astype(BF16)
    r = _sigmoid(_dot(xb, wa_ref[...]) + ba_ref[...])
    i = _sigmoid(_dot(xb, wx_ref[...]) + bx_ref[...])
    z = -lam_ref[...]
    softplus = jnp.maximum(z, 0.0) + jnp.log(1.0 + jnp.exp(-jnp.abs(z)))
    a = jnp.exp(r * ((-LRU_C) * softplus))
    mult = jnp.sqrt(1.0 - a * a)
    row = lax.broadcasted_iota(jnp.int32, (rows, 1), 0)
    mult = jnp.where((row < bsz) & (t == 0), 1.0, mult)
    a_scr[...] = a
    b_scr[...] = mult * (i * xc)

    def step(s, h):
        at = pl.ds(pl.multiple_of(s * bsz, bsz), bsz)
        h = a_scr[at, :] * h + b_scr[at, :]
        b_scr[at, :] = h
        return h

    hcar[...] = lax.fori_loop(0, tc, step, hcar[...], unroll=8)
    y = b_scr[...] * _gelu_tanh(ug_ref[...].astype(F32).reshape(rows, RNN_BW))
    y_ref[...] = y.reshape(tc, bsz, RNN_BW).astype(y_ref.dtype)


def _rglru(proj3, conv_w, conv_b, wa, ba, wx, bx, lam, tc):
    t, bsz, _ = proj3.shape
    assert bsz % 8 == 0, "the recurrence advances whole sublane groups of sequences"
    nb = LRU_BLOCKS
    rows = tc * bsz
    vec = pl.BlockSpec((1, RNN_BW), lambda n, s: (0, n))
    mat = pl.BlockSpec((None, RNN_BW, RNN_BW), lambda n, s: (n, 0, 0))
    return pl.pallas_call(
        functools.partial(_rglru_kernel, tc=tc, bsz=bsz),
        grid=(nb, t // tc),
        in_specs=[
            pl.BlockSpec((tc, bsz, RNN_BW), lambda n, s: (s, 0, COL_U // RNN_BW + n)),
            pl.BlockSpec((tc, bsz, RNN_BW), lambda n, s: (s, 0, COL_GATE // RNN_BW + n)),
            pl.BlockSpec((CONV_W, RNN_BW), lambda n, s: (0, n)),
            vec, mat, vec, mat, vec, vec,
        ],
        out_specs=pl.BlockSpec((tc, bsz, RNN_BW), lambda n, s: (s, 0, n)),
        out_shape=jax.ShapeDtypeStruct((t, bsz, RNN_W), BF16),
        scratch_shapes=[pltpu.VMEM((rows + (CONV_W - 1) * bsz, RNN_BW), F32),
                        pltpu.VMEM((rows, RNN_BW), F32), pltpu.VMEM((rows, RNN_BW), F32),
                        pltpu.VMEM((bsz, RNN_BW), F32)],
        compiler_params=_params(("parallel", "arbitrary")),
    )(proj3, proj3, conv_w, conv_b, wa, ba, wx, bx, lam)


def _compress_kernel(z_ref, pe_ref, w1_ref, w2_ref, *rest, n_chunks, is_key):
    norm_ref, seg_ref, o_ref = rest if is_key else (None, None) + rest
    z = z_ref[...].astype(F32)
    first = _dot((z + pe_ref[0:1, :]).astype(BF16), w1_ref[0])
    second = _dot((z + pe_ref[1:2, :]).astype(BF16), w1_ref[1])
    pre = first + pltpu.roll(second, n_chunks - 1, 0)
    out = _dot(_gelu_tanh(pre).astype(BF16), w2_ref[...])
    if is_key:
        ssq = _seg_sum(out * out, seg_ref[...])
        normed = out * lax.rsqrt(ssq * (1.0 / HEAD_DIM) + NORM_EPS) * norm_ref[...]
        low = lax.broadcasted_iota(jnp.int32, (n_chunks, LANES), 1) < HEAD_DIM
        for pair in range(N_KV_GROUPS // 2):
            blk = normed[:, pair * LANES:(pair + 1) * LANES]
            o_ref[2 * pair] = jnp.where(low, blk, 0.0).astype(o_ref.dtype)
            o_ref[2 * pair + 1] = jnp.where(low, pltpu.roll(blk, HEAD_DIM, 1), 0.0).astype(o_ref.dtype)
    else:
        out_t = jnp.transpose(out)
        for g in range(N_KV_GROUPS):
            o_ref[g] = out_t[g * HEAD_DIM:(g + 1) * HEAD_DIM, :].astype(o_ref.dtype)


def _compress(z, pe, w1, w2, norm_and_seg=None):
    bsz, n_chunks, width = z.shape
    hid = w1.shape[-1]
    is_key = norm_and_seg is not None
    out_block = (N_KV_GROUPS, n_chunks, LANES) if is_key else (N_KV_GROUPS, HEAD_DIM, n_chunks)
    extra = norm_and_seg if is_key else ()
    return pl.pallas_call(
        functools.partial(_compress_kernel, n_chunks=n_chunks, is_key=is_key),
        grid=(bsz,),
        in_specs=[
            pl.BlockSpec((None, n_chunks, width), lambda b: (b, 0, 0)),
            pl.BlockSpec((2, width), lambda b: (0, 0)),
            pl.BlockSpec((2, width, hid), lambda b: (0, 0, 0)),
            pl.BlockSpec((hid, KV_W), lambda b: (0, 0)),
        ] + [pl.BlockSpec(a.shape, lambda b: (0, 0)) for a in extra],
        out_specs=pl.BlockSpec((None,) + out_block, lambda b: (b, 0, 0, 0)),
        out_shape=jax.ShapeDtypeStruct((bsz,) + out_block, BF16),
        compiler_params=_params(("parallel",)),
    )(z, pe, w1, w2, *extra)


def _nsa_prep_kernel(q_ref, ks_ref, vs_ref, kw_ref, vw_ref, qn_ref, ksn_ref, kwn_ref, seg_ref,
                     qo_ref, ko_ref, vo_ref, *, tt):
    t0 = pl.program_id(1) * tt
    lane = lax.broadcasted_iota(jnp.int32, (tt, LANES), 1)
    row = lax.broadcasted_iota(jnp.int32, (tt, LANES), 0) + t0
    low = lane < HEAD_DIM
    seg = seg_ref[...]
    onehot = jnp.where((lane - SEL_ROW0) == row // SEL_BLOCK, 1.0, 0.0)

    def normed(ref, gain_ref, blk, scale):
        x = ref[:, blk * LANES:(blk + 1) * LANES].astype(F32)
        ssq = _seg_sum(x * x, seg)
        y = x * lax.rsqrt(ssq * (1.0 / HEAD_DIM) + NORM_EPS) * gain_ref[:, blk * LANES:(blk + 1) * LANES]
        return y * scale if scale != 1.0 else y

    q_pad = jnp.zeros((LANES - HEAD_DIM, tt), BF16)
    for blk in range(Q_W // LANES):
        y_t = jnp.transpose(normed(q_ref, qn_ref, blk, HEAD_DIM ** -0.5 * LOG2E)).astype(BF16)
        for half in range(2):
            qo_ref[2 * blk + half, 0:HEAD_DIM, :] = y_t[half * HEAD_DIM:(half + 1) * HEAD_DIM, :]
            qo_ref[2 * blk + half, HEAD_DIM:LANES, :] = q_pad
    ones_row = jnp.where(lax.broadcasted_iota(jnp.int32, (V_ROWS - HEAD_DIM, tt), 0) == 0, 1.0, 0.0).astype(BF16)
    for blk in range(KV_W // LANES):
        y = normed(ks_ref, ksn_ref, blk, 1.0)
        ysw = pltpu.roll(y, HEAD_DIM, 1)
        ko_ref[2 * blk, 0] = jnp.where(low, y, onehot).astype(BF16)
        ko_ref[2 * blk + 1, 0] = jnp.where(low, ysw, onehot).astype(BF16)
        y = normed(kw_ref, kwn_ref, blk, 1.0)
        ysw = pltpu.roll(y, HEAD_DIM, 1)
        ko_ref[2 * blk, 1] = jnp.where(low, y, 0.0).astype(BF16)
        ko_ref[2 * blk + 1, 1] = jnp.where(low, ysw, 0.0).astype(BF16)
        for branch, src in enumerate((vs_ref, vw_ref)):
            v_t = jnp.transpose(src[:, blk * LANES:(blk + 1) * LANES].astype(F32))
            for half in range(2):
                g = 2 * blk + half
                vo_ref[g, branch, 0:HEAD_DIM, :] = v_t[half * HEAD_DIM:(half + 1) * HEAD_DIM, :].astype(BF16)
                vo_ref[g, branch, HEAD_DIM:V_ROWS, :] = ones_row


def _nsa_prep(proj_t, bsz, qn, ksn, kwn, seg, tt):
    t = proj_t.shape[0]

    def col(width, offset):
        return pl.BlockSpec((tt, width), lambda b, s: (s, (b * D_PROJ + offset) // width))

    def vec(width):
        return pl.BlockSpec((1, width), lambda b, s: (0, 0))

    g = N_KV_GROUPS
    return pl.pallas_call(
        functools.partial(_nsa_prep_kernel, tt=tt),
        grid=(bsz, t // tt),
        in_specs=[col(Q_W, COL_Q), col(KV_W, COL_KS), col(KV_W, COL_VS), col(KV_W, COL_KW), col(KV_W, COL_VW),
                  vec(Q_W), vec(KV_W), vec(KV_W), pl.BlockSpec((LANES, LANES), lambda b, s: (0, 0))],
        out_specs=[pl.BlockSpec((None, N_HEADS, LANES, tt), lambda b, s: (b, 0, 0, s)),
                   pl.BlockSpec((None, g, 2, tt, LANES), lambda b, s: (b, 0, 0, s, 0)),
                   pl.BlockSpec((None, g, 2, V_ROWS, tt), lambda b, s: (b, 0, 0, 0, s))],
        out_shape=[jax.ShapeDtypeStruct((bsz, N_HEADS, LANES, t), BF16),
                   jax.ShapeDtypeStruct((bsz, g, 2, t, LANES), BF16),
                   jax.ShapeDtypeStruct((bsz, g, 2, V_ROWS, t), BF16)],
        compiler_params=_params(("parallel", "parallel")),
    )(proj_t, proj_t, proj_t, proj_t, proj_t, qn, ksn, kwn, seg)


def _nsa_kernel(q_ref, kc_ref, vc_ref, k_ref, v_ref, bc_ref, bias_ref, cover_ref, gate_ref, o_ref,
                qq_ref, *scratch, tq, n_q, n_sblk):
    qi = pl.program_id(2)
    t0 = qi * tq
    rg = HEADS_PER_GROUP
    jpt = JOBS_PER_TRIP
    s_refs, smax_refs, p_refs = ((scratch[k:k + jpt], scratch[k + jpt:k + 2 * jpt]) for k in (0, 2 * jpt, 4 * jpt))
    m_refs, acc_refs = scratch[6 * jpt:6 * jpt + rg], scratch[6 * jpt + rg:]
    sel, win = 0, 1


    bias = jnp.concatenate([bc_ref[r] for r in range(rg)], axis=1)
    s = _dot(kc_ref[...], jnp.concatenate([q_ref[r] for r in range(rg)], axis=1)) + bias
    visible = bias > 0.5 * MASK_NEG
    p = jnp.exp2(s - jnp.max(s, axis=0, keepdims=True))
    p = jnp.where(visible, p / jnp.sum(p, axis=0, keepdims=True), 0.0)
    o_cmp_all = _dot(vc_ref[...], p.astype(BF16))
    o_cmp = [o_cmp_all[:, r * tq:(r + 1) * tq] for r in range(rg)]
    p_sum = p[:, 0:tq]
    for r in range(1, rg):
        p_sum = p_sum + p[:, r * tq:(r + 1) * tq]
    p_hi = p_sum.astype(BF16)
    p_lo = (p_sum - p_hi.astype(F32)).astype(BF16)
    imp_t = _dot(cover_ref[...], p_hi) + _dot(cover_ref[...], p_lo)

    score = imp_t[SEL_ROW0:SEL_ROW0 + MAX_SBLK, :]
    jrow = lax.broadcasted_iota(jnp.int32, (MAX_SBLK, tq), 0)
    qblk = (lax.broadcasted_iota(jnp.int32, (MAX_SBLK, tq), 1) + t0) // SEL_BLOCK
    causal = jrow <= qblk
    forced = causal & ((jrow == 0) | (jrow >= qblk - 1))
    score = jnp.where(forced, SEL_FORCED, jnp.where(causal, score, -1.0))
    rank = jnp.zeros((MAX_SBLK, tq), F32)
    for j in range(n_sblk):
        other = score[j:j + 1, :]
        ahead = (other > score) | ((other == score) & (jrow > j))
        rank = rank + jnp.where(ahead, 1.0, 0.0)
    n_top = min(N_SELECT, n_sblk)
    selected = (rank < n_top) & (score >= 0.0)
    neg = jnp.where(selected, 0.0, SEL_NEG).astype(BF16)
    for r in range(rg):
        qq_ref[r, 0:SEL_ROW0, :] = q_ref[r, 0:SEL_ROW0, :]
        qq_ref[r, SEL_ROW0:SEL_ROW0 + MAX_SBLK, :] = neg
        qq_ref[r, SEL_ROW0 + MAX_SBLK:LANES, :] = q_ref[r, SEL_ROW0 + MAX_SBLK:LANES, :]

    near_max = WINDOW // tq + 1
    plan = [(branch, a) for a in range(min(near_max, n_q)) for branch in (sel, win)]
    plan += [(sel, a) for a in range(near_max, n_q)]
    plan += [(sel, n_q)] * (-len(plan) % JOBS_PER_TRIP)
    groups = [plan[g:g + JOBS_PER_TRIP] for g in range(0, len(plan), JOBS_PER_TRIP)]
    n_groups = sum((qi >= group[0][1]).astype(jnp.int32) for group in groups)

    def jobs_of(i):
        jobs = []
        for branch, a in groups[i]:
            kind = (BIAS_DIAG, BIAS_NEAR, BIAS_WIN if branch == win else None)[min(a, 2)]
            real = None if a <= groups[i][0][1] else a <= qi
            tile = qi - a if real is None else jnp.where(real, qi - a, 0)
            jobs.append((branch, pl.multiple_of(tile * tq, tq), kind, real))
        return jobs

    def scores(i):
        for (branch, k0, kind, real), s_ref, smax_ref in zip(jobs_of(i), s_refs[i % 2], smax_refs[i % 2]):
            k = k_ref[branch, pl.ds(k0, tq), :]
            for r in range(rg):
                q_t = qq_ref[r] if branch == sel else q_ref[r]
                s = _dot(k, q_t)
                if kind is not None:
                    s = s + bias_ref[r, kind]
                s_ref[r] = s
                smax = jnp.max(s, axis=0, keepdims=True)
                smax_ref[r] = smax if real is None else jnp.where(real, smax, MASK_NEG)

    def softmax(i):
        jobs = jobs_of(i)
        for r in range(rg):
            for branch in sorted({b for b, _ in groups[i]}):
                mine = [j for j, (b, _) in enumerate(groups[i]) if b == branch]
                m_old = m_refs[r][branch]
                m_new = m_old
                for j in mine:
                    m_new = jnp.maximum(m_new, smax_refs[i % 2][j][r])
                for j in mine:
                    p = jnp.exp2(s_refs[i % 2][j][r] - m_new).astype(BF16)
                    real = jobs[j][3]
                    p_refs[i % 2][j][r] = p if real is None else jnp.where(real, p, jnp.zeros_like(p))
                m_refs[r][branch] = m_new
                acc_refs[r][branch] = jnp.exp2(m_old - m_new) * acc_refs[r][branch]

    def values(i):
        for (branch, k0, _, _), p_ref in zip(jobs_of(i), p_refs[i % 2]):
            v_t = v_ref[branch, :, pl.ds(k0, tq)]
            for r in range(rg):
                acc_refs[r][branch] += _dot(v_t, p_ref[r])

    for r in range(rg):
        m_refs[r][...] = jnp.full(m_refs[r].shape, MASK_NEG, F32)
        acc_refs[r][...] = jnp.zeros(acc_refs[r].shape, F32)
    scores(0)
    for i in range(len(groups)):
        if i + 1 < len(groups):
            @pl.when(i + 1 < n_groups)
            def _(i=i):
                if i > 0:
                    values(i - 1)
                softmax(i)
                scores(i + 1)

        @pl.when(i + 1 == n_groups)
        def _(i=i):
            if i > 0:
                values(i - 1)
            softmax(i)
            values(i)

    def finish(branch):
        return [acc_refs[r][branch, 0:HEAD_DIM, :] / acc_refs[r][branch, HEAD_DIM:HEAD_DIM + 1, :] for r in range(rg)]

    o_sel = finish(sel)
    o_win = finish(win)

    gates_t = jnp.transpose(_sigmoid(gate_ref[...].astype(F32)))
    outs = []
    for r in range(rg):
        outs.append(gates_t[3 * r:3 * r + 1, :] * o_cmp[r]
                    + gates_t[3 * r + 1:3 * r + 2, :] * o_sel[r]
                    + gates_t[3 * r + 2:3 * r + 3, :] * o_win[r])
    o_ref[...] = jnp.transpose(jnp.concatenate(outs, axis=0)).astype(o_ref.dtype)


def _nsa_attention(qh, kc, vc, k_all, v_all, bias_c, bias_tiles, cover, proj3, tq):
    bsz, _, _, t = qh.shape
    n_chunks = kc.shape[2]
    rg = HEADS_PER_GROUP
    n_kinds = bias_tiles.shape[1]
    return pl.pallas_call(
        functools.partial(_nsa_kernel, tq=tq, n_q=t // tq, n_sblk=t // SEL_BLOCK),
        grid=(bsz, N_KV_GROUPS, t // tq),
        in_specs=[
            pl.BlockSpec((None, rg, LANES, tq), lambda b, g, i: (b, g, 0, i)),
            pl.BlockSpec((None, None, n_chunks, LANES), lambda b, g, i: (b, g, 0, 0)),
            pl.BlockSpec((None, None, HEAD_DIM, n_chunks), lambda b, g, i: (b, g, 0, 0)),
            pl.BlockSpec((None, None, 2, t, LANES), lambda b, g, i: (b, g, 0, 0, 0)),
            pl.BlockSpec((None, None, 2, V_ROWS, t), lambda b, g, i: (b, g, 0, 0, 0)),
            pl.BlockSpec((rg, n_chunks, tq), lambda b, g, i: (g, 0, i)),
            pl.BlockSpec((rg, n_kinds, tq, tq), lambda b, g, i: (g, 0, 0, 0)),
            pl.BlockSpec((LANES, n_chunks), lambda b, g, i: (0, 0)),
            pl.BlockSpec((tq, LANES), lambda b, g, i: (i, (b * D_PROJ + COL_GN) // LANES + g)),
        ],
        out_specs=pl.BlockSpec((None, tq, rg * HEAD_DIM), lambda b, g, i: (b, i, g)),
        out_shape=jax.ShapeDtypeStruct((bsz, t, Q_W), BF16),
        scratch_shapes=[
            pltpu.VMEM((rg, LANES, tq), BF16),
        ] + [pltpu.VMEM((rg, tq, tq), F32)] * (2 * JOBS_PER_TRIP)
          + [pltpu.VMEM((rg, 1, tq), F32)] * (2 * JOBS_PER_TRIP)
          + [pltpu.VMEM((rg, tq, tq), BF16)] * (2 * JOBS_PER_TRIP)
          + [pltpu.VMEM((2, 1, tq), F32)] * rg
          + [pltpu.VMEM((2, V_ROWS, tq), F32)] * rg,
        compiler_params=_params(("parallel", "parallel", "arbitrary")),
    )(qh, kc, vc, k_all, v_all, bias_c, bias_tiles, cover, proj3)


def _merge_kernel(ya_ref, yb_ref, ga_ref, gb_ref, x_ref, pa_ref, pb_ref, wo_ref, h_ref):
    merged = (_sigmoid(ga_ref[...].astype(F32)) * _dot(ya_ref[...], pa_ref[...])
              + _sigmoid(gb_ref[...].astype(F32)) * _dot(yb_ref[...], pb_ref[...]))
    h_ref[...] = x_ref[...] + _dot(merged.astype(BF16), wo_ref[...])


def _merge_out(ya_t, yb, proj_t, x, pa, pb, wo, bsz, tm):
    t = ya_t.shape[0]
    tiles = t // tm

    def rows(width):
        return pl.BlockSpec((tm, width), lambda b, s: (b * tiles + s, 0))

    def time_major(width, offset):
        return pl.BlockSpec((tm, width), lambda b, s: (s, (b * D_PROJ + offset) // width))

    def whole(a):
        return pl.BlockSpec(a.shape, lambda b, s: (0, 0))

    return pl.pallas_call(
        _merge_kernel,
        grid=(bsz, tiles),
        in_specs=[pl.BlockSpec((tm, RNN_W), lambda b, s: (s, b)), rows(Q_W), time_major(D_MODEL, COL_GA),
                  time_major(D_MODEL, COL_GB), rows(D_MODEL), whole(pa), whole(pb), whole(wo)],
        out_specs=rows(D_MODEL),
        out_shape=jax.ShapeDtypeStruct((bsz * t, D_MODEL), F32),
        compiler_params=_params(("parallel", "parallel")),
    )(ya_t, yb, proj_t, proj_t, x, pa, pb, wo)


def _mlp_kernel(h_ref, g_ref, w1_ref, w2_ref, o_ref, hn_ref, acc_ref):
    j = pl.program_id(1)

    @pl.when(j == 0)
    def _():
        h = h_ref[...]
        y = h * lax.rsqrt(jnp.mean(h * h, axis=-1, keepdims=True) + NORM_EPS) * g_ref[...]
        hn_ref[...] = y.astype(BF16)
        acc_ref[...] = h

    z = jnp.maximum(_dot(hn_ref[...], w1_ref[...]), 0.0)
    acc_ref[...] += _dot((z * z).astype(BF16), w2_ref[...])

    @pl.when(j == pl.num_programs(1) - 1)
    def _():
        o_ref[...] = acc_ref[...]


def _mlp(h, gain, w1, w2, tm, tf):
    m, d = h.shape
    ff = w1.shape[1]
    return pl.pallas_call(
        _mlp_kernel,
        grid=(m // tm, ff // tf),
        in_specs=[
            pl.BlockSpec((tm, d), lambda i, j: (i, 0)),
            pl.BlockSpec((1, d), lambda i, j: (0, 0)),
            pl.BlockSpec((d, tf), lambda i, j: (0, j)),
            pl.BlockSpec((tf, d), lambda i, j: (j, 0)),
        ],
        out_specs=pl.BlockSpec((tm, d), lambda i, j: (i, 0)),
        out_shape=jax.ShapeDtypeStruct((m, d), F32),
        scratch_shapes=[pltpu.VMEM((tm, d), BF16), pltpu.VMEM((tm, d), F32)],
        compiler_params=_params(("parallel", "arbitrary")),
    )(h, gain, w1, w2)


def _t5_bucket_table():
    max_exact = REL_BUCKETS // 2
    d = np.arange(FAR_DIST)
    df = np.maximum(d.astype(np.float32), np.float32(1.0))
    large = max_exact + (np.log(df / np.float32(max_exact)) / np.float32(math.log(REL_MAX_DIST / max_exact))
                         * np.float32(REL_BUCKETS - max_exact)).astype(np.int32)
    large = np.minimum(large, REL_BUCKETS - 1)
    return np.where(d < max_exact, d, large).astype(np.int32)


def _pad_blocks(w, axis):
    shape = w.shape
    w = w.reshape(shape[:axis] + (LRU_BLOCKS, LRU_BLOCK_W) + shape[axis + 1:])
    pad = [(0, 0)] * w.ndim
    pad[axis + 1] = (0, RNN_BW - LRU_BLOCK_W)
    w = jnp.pad(w, pad)
    return w.reshape(shape[:axis] + (RNN_W,) + shape[axis + 1:])


def _in_proj_weight(w_in):
    cuts = np.cumsum((D_RNN, D_RNN, Q_W, KV_W, KV_W, KV_W, KV_W, KV_W, KV_W, 3 * N_HEADS, D_MODEL))
    w_in = w_in.astype(BF16)
    (w_u, w_gate, w_q, w_kc, w_vc, w_ks, w_vs, w_kw, w_vw, w_gn, w_ga, w_gb) = jnp.split(w_in, cuts, axis=1)
    per_group = 3 * HEADS_PER_GROUP
    w_gn = jnp.pad(w_gn.reshape(D_MODEL, N_KV_GROUPS, per_group), ((0, 0), (0, 0), (0, LANES - per_group)))
    w_gn = w_gn.reshape(D_MODEL, N_KV_GROUPS * LANES)
    gap = jnp.zeros((D_MODEL, COL_GA - COL_GN - N_KV_GROUPS * LANES), w_in.dtype)
    w = jnp.concatenate([_pad_blocks(w_u, 1), _pad_blocks(w_gate, 1), w_q, w_ks, w_vs, w_kw, w_vw, w_kc, w_vc,
                         w_gn, gap, w_ga, w_gb], axis=1)
    assert w.shape[1] == RNN_COLS + D_PROJ
    return w


def _phi_weights(pe, w1, w2):
    half, g = CMP_BLOCK // 2, N_KV_GROUPS
    w1h = w1.astype(BF16).reshape(2, half, HEAD_DIM, PHI_HIDDEN)
    w1e = jnp.zeros((2, half, g, HEAD_DIM, g, PHI_HIDDEN), BF16)
    w2e = jnp.zeros((g, PHI_HIDDEN, g, HEAD_DIM), BF16)
    for k in range(g):
        w1e = w1e.at[:, :, k, :, k, :].set(w1h)
        w2e = w2e.at[k, :, k, :].set(w2.astype(BF16))
    pee = jnp.broadcast_to(pe.reshape(2, half, 1, HEAD_DIM), (2, half, g, HEAD_DIM)).reshape(2, half * KV_W)
    return pee, w1e.reshape(2, half * KV_W, g * PHI_HIDDEN), w2e.reshape(g * PHI_HIDDEN, KV_W)


def kernel(x, norm_mix, w_in, conv_w, conv_b, gate_a_w, gate_a_b, gate_x_w, gate_x_b, lru_lambda, phi_k_pe, phi_k_w1, phi_k_w2, phi_v_pe, phi_v_w1, phi_v_w2, q_norm, kc_norm, ks_norm, kw_norm, rel_bias, proj_a, proj_b, w_out, norm_mlp, w_mlp_in, w_mlp_out):
    bsz, t, d = x.shape
    assert d == D_MODEL and norm_mix.shape[0] == 1
    tq = 256
    assert t % tq == 0 and t // SEL_BLOCK <= MAX_SBLK and t % CMP_STRIDE == 0
    n_tok = bsz * t
    n_chunks = t // CMP_STRIDE
    assert n_chunks % 8 == 0 and n_chunks <= LANES
    x2 = x.reshape(n_tok, d)

    rnn_t, proj_t, zk, zv = _norm_matmul(x2, norm_mix, _in_proj_weight(w_in[0]), bsz,
                                         tm=1024 if t % 1024 == 0 else tq, tn=1024, rnn_cols=RNN_COLS,
                                         side_cols=((COL_KC, KV_W), (COL_VC, KV_W)))

    pad_w = lambda w: jnp.pad(w, ((0, 0), (0, RNN_BW - LRU_BLOCK_W), (0, RNN_BW - LRU_BLOCK_W))).astype(BF16)
    ya_t = _rglru(rnn_t.reshape(t, bsz, RNN_COLS), _pad_blocks(conv_w[0], 1), _pad_blocks(conv_b, 1),
                  pad_w(gate_a_w[0]), _pad_blocks(gate_a_b.reshape(1, D_RNN), 1),
                  pad_w(gate_x_w[0]), _pad_blocks(gate_x_b.reshape(1, D_RNN), 1),
                  _pad_blocks(lru_lambda, 1), tc=128)

    zk = zk.reshape(bsz, n_chunks, CMP_STRIDE * KV_W)
    zv = zv.reshape(bsz, n_chunks, CMP_STRIDE * KV_W)
    pe_k, w1_k, w2_k = _phi_weights(phi_k_pe[0], phi_k_w1[0], phi_k_w2[0])
    pe_v, w1_v, w2_v = _phi_weights(phi_v_pe[0], phi_v_w1[0], phi_v_w2[0])
    seg256 = jnp.asarray(np.kron(np.eye(KV_W // HEAD_DIM), np.ones((HEAD_DIM, HEAD_DIM))), BF16)
    kcn = jnp.tile(kc_norm, (1, N_KV_GROUPS))
    kc = _compress(zk, pe_k, w1_k, w2_k, (kcn, seg256))
    vc = _compress(zv, pe_v, w1_v, w2_v)

    seg128 = jnp.asarray(np.kron(np.eye(LANES // HEAD_DIM), np.ones((HEAD_DIM, HEAD_DIM))), BF16)
    qh, k_all, v_all = _nsa_prep(proj_t, bsz, jnp.tile(q_norm, (1, N_HEADS)), jnp.tile(ks_norm, (1, N_KV_GROUPS)),
                                 jnp.tile(kw_norm, (1, N_KV_GROUPS)), seg128, tt=tq)

    bucket_of = _t5_bucket_table()
    far = rel_bias[REL_BUCKETS - 1][:, None, None]

    def bias_table(dist, valid, shift):
        buckets = jnp.asarray(bucket_of[np.clip(dist, 0, FAR_DIST - 1)].astype(np.int8))
        onehot = (buckets[None] == jnp.arange(REL_BUCKETS, dtype=jnp.int8).reshape(-1, 1, 1)).astype(F32)
        vals = jnp.einsum("kh,kji->hji", rel_bias, onehot, precision=lax.Precision.HIGHEST)
        if shift:
            vals = vals - far
        return jnp.where(jnp.asarray(valid), vals * LOG2E, MASK_NEG).astype(F32)

    kj = np.arange(tq)[:, None]
    qi_ = np.arange(tq)[None, :]
    bias_diag = bias_table(qi_ - kj, qi_ >= kj, True)
    bias_near = bias_table(tq + qi_ - kj, np.ones((tq, tq), bool), True)
    bias_win = jnp.broadcast_to(jnp.asarray(np.where(kj > qi_, 0.0, MASK_NEG), F32), (N_HEADS, tq, tq))
    bias_tiles = jnp.stack([bias_diag, bias_near, bias_win], axis=1)
    assert WINDOW == 2 * tq
    cidx = np.arange(n_chunks)[:, None]
    dist_c = np.arange(t)[None, :] - (cidx * CMP_STRIDE + CMP_BLOCK - 1)
    bias_c = bias_table(dist_c, (dist_c >= 0) & (cidx < n_chunks - 1), False)
    cstart = np.arange(n_chunks) * CMP_STRIDE
    sj = np.arange(MAX_SBLK)
    cov = ((cstart[None, :] < (sj[:, None] + 1) * SEL_BLOCK) & (cstart[None, :] + CMP_BLOCK - 1 >= sj[:, None] * SEL_BLOCK)
           & (np.arange(n_chunks)[None, :] < n_chunks - 1))
    cover = np.zeros((LANES, n_chunks), np.float32)
    cover[SEL_ROW0:SEL_ROW0 + MAX_SBLK] = cov
    yb = _nsa_attention(qh, kc, vc, k_all, v_all, bias_c, bias_tiles, jnp.asarray(cover, BF16), proj_t, tq)

    pa = jnp.pad(proj_a[0].reshape(LRU_BLOCKS, LRU_BLOCK_W, D_MODEL), ((0, 0), (0, RNN_BW - LRU_BLOCK_W), (0, 0)))
    h = _merge_out(ya_t.reshape(t, bsz * RNN_W), yb.reshape(n_tok, Q_W), proj_t, x2,
                   pa.reshape(RNN_W, D_MODEL).astype(BF16), proj_b[0].astype(BF16), w_out[0].astype(BF16),
                   bsz, tm=512 if t % 512 == 0 else tq)

    out = _mlp(h, norm_mlp, w_mlp_in[0].astype(BF16), w_mlp_out[0].astype(BF16),
               tm=1024 if n_tok % 1024 == 0 else tq, tf=1024)
    return out.reshape(bsz, t, d)
```

```python
import functools
import math

import numpy as np
import jax
import jax.numpy as jnp
from jax import lax
from jax.experimental import pallas as pl
from jax.experimental.pallas import tpu as pltpu

F32 = jnp.float32
BF16 = jnp.bfloat16

D_MODEL = 1024
D_RNN = 1344
LRU_BLOCKS = 4
LRU_BLOCK_W = D_RNN // LRU_BLOCKS
CONV_W = 4
LRU_C = 8.0
N_HEADS = 16
HEAD_DIM = 64
N_KV_GROUPS = 4
HEADS_PER_GROUP = N_HEADS // N_KV_GROUPS
CMP_BLOCK = 32
CMP_STRIDE = 16
SEL_BLOCK = 64
N_SELECT = 16
WINDOW = 512
PHI_HIDDEN = 256
SEL_FORCED = 1e4
REL_BUCKETS = 32
REL_MAX_DIST = 128
D_FF = 4 * D_MODEL
NORM_EPS = 1e-6
Q_W = N_HEADS * HEAD_DIM
KV_W = N_KV_GROUPS * HEAD_DIM

LANES = 128
VMEM_LIMIT = 56 * 1024 * 1024

RNN_BW = 384
RNN_W = LRU_BLOCKS * RNN_BW
MASK_NEG = -1e30
SEL_NEG = -1e9
SEL_ROW0 = 64
MAX_SBLK = 32
FAR_DIST = 256
LOG2E = math.log2(math.e)
V_ROWS = HEAD_DIM + 16
BIAS_DIAG, BIAS_NEAR, BIAS_WIN = range(3)
JOBS_PER_TRIP = 2

COL_U = 0
COL_GATE = RNN_W
RNN_COLS = 2 * RNN_W
COL_Q = 0
COL_KS = COL_Q + Q_W
COL_VS = COL_KS + KV_W
COL_KW = COL_VS + KV_W
COL_VW = COL_KW + KV_W
COL_KC = COL_VW + KV_W
COL_VC = COL_KC + KV_W
COL_GN = COL_VC + KV_W
COL_GA = 3072
COL_GB = 4096
D_PROJ = 5120


def _dot(a, b):
    return jnp.dot(a, b, preferred_element_type=F32)


def _dot_nt(a, b):
    return lax.dot_general(a, b, (((1,), (1,)), ((), ())), preferred_element_type=F32)


def _gelu_tanh(x):
    return 0.5 * x * (1.0 + jnp.tanh(math.sqrt(2.0 / math.pi) * (x + 0.044715 * (x * x * x))))


def _sigmoid(x):
    return 0.5 * jnp.tanh(0.5 * x) + 0.5


def _seg_sum(x, seg_ones):
    hi = x.astype(BF16)
    lo = (x - hi.astype(F32)).astype(BF16)
    return _dot(hi, seg_ones) + _dot(lo, seg_ones)


def _params(sem, flags=None):
    return pltpu.CompilerParams(dimension_semantics=sem, vmem_limit_bytes=VMEM_LIMIT, flags=flags)


def _norm_matmul_kernel(x_ref, g_ref, w_ref, rnn_ref, o_ref, *rest, tn, rnn_tiles, side_cols):
    side_refs, xn_ref = rest[:-1], rest[-1]
    j = pl.program_id(1)

    @pl.when(j == 0)
    def _():
        x = x_ref[...]
        y = x * lax.rsqrt(jnp.mean(x * x, axis=-1, keepdims=True) + NORM_EPS) * g_ref[...]
        xn_ref[...] = y.astype(BF16)

    res = _dot(xn_ref[...], w_ref[...]).astype(o_ref.dtype)

    @pl.when(j < rnn_tiles)
    def _():
        rnn_ref[...] = res

    @pl.when(j >= rnn_tiles)
    def _():
        o_ref[...] = res

    for (col, width), side_ref in zip(side_cols, side_refs):
        @pl.when(j == rnn_tiles + col // tn)
        def _(col=col, width=width, side_ref=side_ref):
            side_ref[...] = res[:, col % tn:col % tn + width]


def _norm_matmul(x, gain, w, bsz, tm, tn, rnn_cols, side_cols):
    m, k = x.shape
    n = w.shape[1] - rnn_cols
    t = m // bsz
    tiles = t // tm
    rnn_tiles, col_tiles = rnn_cols // tn, n // tn
    assert all(col // tn == (col + width - 1) // tn for col, width in side_cols)
    return pl.pallas_call(
        functools.partial(_norm_matmul_kernel, tn=tn, rnn_tiles=rnn_tiles, side_cols=side_cols),
        grid=(m // tm, rnn_tiles + col_tiles),
        in_specs=[
            pl.BlockSpec((tm, k), lambda i, j: (i, 0)),
            pl.BlockSpec((1, k), lambda i, j: (0, 0)),
            pl.BlockSpec((k, tn), lambda i, j: (0, j)),
        ],
        out_specs=[pl.BlockSpec((tm, tn), lambda i, j: (i % tiles, (i // tiles) * rnn_tiles + jnp.minimum(j, rnn_tiles - 1))),
                   pl.BlockSpec((tm, tn), lambda i, j: (i % tiles, (i // tiles) * col_tiles + jnp.maximum(j - rnn_tiles, 0)))]
                  + [pl.BlockSpec((tm, width), lambda i, j: (i, 0)) for _, width in side_cols],
        out_shape=[jax.ShapeDtypeStruct((t, bsz * rnn_cols), BF16), jax.ShapeDtypeStruct((t, bsz * n), BF16)]
                  + [jax.ShapeDtypeStruct((m, width), BF16) for _, width in side_cols],
        scratch_shapes=[pltpu.VMEM((tm, k), BF16)],
        compiler_params=_params(("parallel", "arbitrary")),
    )(x, gain, w)


def _rglru_kernel(u_ref, ug_ref, cw_ref, cb_ref, wa_ref, ba_ref, wx_ref, bx_ref, lam_ref,
                  y_ref, ubuf, a_scr, b_scr, hcar, *, tc, bsz):
    t = pl.program_id(1)
    rows = tc * bsz
    halo = (CONV_W - 1) * bsz

    @pl.when(t == 0)
    def _():
        ubuf[0:halo, :] = jnp.zeros((halo, RNN_BW), F32)
        hcar[...] = jnp.zeros_like(hcar)

    u = u_ref[...].astype(F32).reshape(rows, RNN_BW)
    ubuf[halo:halo + rows, :] = u
    xc = cb_ref[...]
    for k in range(CONV_W):
        xc = xc + cw_ref[k:k + 1, :] * ubuf[k * bsz:k * bsz + rows, :]
    ubuf[0:halo, :] = u[rows - halo:rows, :]

    xb = xc.astype(BF16)
    r = _sigmoid(_dot(xb, wa_ref[...]) + ba_ref[...])
    i = _sigmoid(_dot(xb, wx_ref[...]) + bx_ref[...])
    z = -lam_ref[...]
    softplus = jnp.maximum(z, 0.0) + jnp.log(1.0 + jnp.exp(-jnp.abs(z)))
    a = jnp.exp(r * ((-LRU_C) * softplus))
    mult = jnp.sqrt(1.0 - a * a)
    row = lax.broadcasted_iota(jnp.int32, (rows, 1), 0)
    mult = jnp.where((row < bsz) & (t == 0), 1.0, mult)
    a_scr[...] = a
    b_scr[...] = mult * (i * xc)

    def step(s, h):
        at = pl.ds(pl.multiple_of(s * bsz, bsz), bsz)
        h = a_scr[at, :] * h + b_scr[at, :]
        b_scr[at, :] = h
        return h

    hcar[...] = lax.fori_loop(0, tc, step, hcar[...], unroll=8)
    y = b_scr[...] * _gelu_tanh(ug_ref[...].astype(F32).reshape(rows, RNN_BW))
    y_ref[...] = y.reshape(tc, bsz, RNN_BW).astype(y_ref.dtype)


def _rglru(proj3, conv_w, conv_b, wa, ba, wx, bx, lam, tc):
    t, bsz, _ = proj3.shape
    assert bsz % 8 == 0, "the recurrence advances whole sublane groups of sequences"
    nb = LRU_BLOCKS
    rows = tc * bsz
    vec = pl.BlockSpec((1, RNN_BW), lambda n, s: (0, n))
    mat = pl.BlockSpec((None, RNN_BW, RNN_BW), lambda n, s: (n, 0, 0))
    return pl.pallas_call(
        functools.partial(_rglru_kernel, tc=tc, bsz=bsz),
        grid=(nb, t // tc),
        in_specs=[
            pl.BlockSpec((tc, bsz, RNN_BW), lambda n, s: (s, 0, COL_U // RNN_BW + n)),
            pl.BlockSpec((tc, bsz, RNN_BW), lambda n, s: (s, 0, COL_GATE // RNN_BW + n)),
            pl.BlockSpec((CONV_W, RNN_BW), lambda n, s: (0, n)),
            vec, mat, vec, mat, vec, vec,
        ],
        out_specs=pl.BlockSpec((tc, bsz, RNN_BW), lambda n, s: (s, 0, n)),
        out_shape=jax.ShapeDtypeStruct((t, bsz, RNN_W), BF16),
        scratch_shapes=[pltpu.VMEM((rows + (CONV_W - 1) * bsz, RNN_BW), F32),
                        pltpu.VMEM((rows, RNN_BW), F32), pltpu.VMEM((rows, RNN_BW), F32),
                        pltpu.VMEM((bsz, RNN_BW), F32)],
        compiler_params=_params(("parallel", "arbitrary")),
    )(proj3, proj3, conv_w, conv_b, wa, ba, wx, bx, lam)


def _compress_kernel(z_ref, pe_ref, w1_ref, w2_ref, *rest, n_chunks, is_key):
    norm_ref, seg_ref, o_ref = rest if is_key else (None, None) + rest
    z = z_ref[...].astype(F32)
    first = _dot((z + pe_ref[0:1, :]).astype(BF16), w1_ref[0])
    second = _dot((z + pe_ref[1:2, :]).astype(BF16), w1_ref[1])
    pre = first + pltpu.roll(second, n_chunks - 1, 0)
    out = _dot(_gelu_tanh(pre).astype(BF16), w2_ref[...])
    if is_key:
        ssq = _seg_sum(out * out, seg_ref[...])
        normed = out * lax.rsqrt(ssq * (1.0 / HEAD_DIM) + NORM_EPS) * norm_ref[...]
        low = lax.broadcasted_iota(jnp.int32, (n_chunks, LANES), 1) < HEAD_DIM
        for pair in range(N_KV_GROUPS // 2):
            blk = normed[:, pair * LANES:(pair + 1) * LANES]
            o_ref[2 * pair] = jnp.where(low, blk, 0.0).astype(o_ref.dtype)
            o_ref[2 * pair + 1] = jnp.where(low, pltpu.roll(blk, HEAD_DIM, 1), 0.0).astype(o_ref.dtype)
    else:
        out_t = jnp.transpose(out)
        for g in range(N_KV_GROUPS):
            o_ref[g] = out_t[g * HEAD_DIM:(g + 1) * HEAD_DIM, :].astype(o_ref.dtype)


def _compress(z, pe, w1, w2, norm_and_seg=None):
    bsz, n_chunks, width = z.shape
    hid = w1.shape[-1]
    is_key = norm_and_seg is not None
    out_block = (N_KV_GROUPS, n_chunks, LANES) if is_key else (N_KV_GROUPS, HEAD_DIM, n_chunks)
    extra = norm_and_seg if is_key else ()
    return pl.pallas_call(
        functools.partial(_compress_kernel, n_chunks=n_chunks, is_key=is_key),
        grid=(bsz,),
        in_specs=[
            pl.BlockSpec((None, n_chunks, width), lambda b: (b, 0, 0)),
            pl.BlockSpec((2, width), lambda b: (0, 0)),
            pl.BlockSpec((2, width, hid), lambda b: (0, 0, 0)),
            pl.BlockSpec((hid, KV_W), lambda b: (0, 0)),
        ] + [pl.BlockSpec(a.shape, lambda b: (0, 0)) for a in extra],
        out_specs=pl.BlockSpec((None,) + out_block, lambda b: (b, 0, 0, 0)),
        out_shape=jax.ShapeDtypeStruct((bsz,) + out_block, BF16),
        compiler_params=_params(("parallel",)),
    )(z, pe, w1, w2, *extra)


def _nsa_prep_kernel(q_ref, ks_ref, vs_ref, kw_ref, vw_ref, qn_ref, ksn_ref, kwn_ref, seg_ref,
                     qo_ref, ko_ref, vo_ref, *, tt):
    t0 = pl.program_id(1) * tt
    lane = lax.broadcasted_iota(jnp.int32, (tt, LANES), 1)
    row = lax.broadcasted_iota(jnp.int32, (tt, LANES), 0) + t0
    low = lane < HEAD_DIM
    seg = seg_ref[...]
    onehot = jnp.where((lane - SEL_ROW0) == row // SEL_BLOCK, 1.0, 0.0)

    def normed(ref, gain_ref, blk, scale):
        x = ref[:, blk * LANES:(blk + 1) * LANES].astype(F32)
        ssq = _seg_sum(x * x, seg)
        y = x * lax.rsqrt(ssq * (1.0 / HEAD_DIM) + NORM_EPS) * gain_ref[:, blk * LANES:(blk + 1) * LANES]
        return y * scale if scale != 1.0 else y

    q_pad = jnp.zeros((LANES - HEAD_DIM, tt), BF16)
    for blk in range(Q_W // LANES):
        y_t = jnp.transpose(normed(q_ref, qn_ref, blk, HEAD_DIM ** -0.5 * LOG2E)).astype(BF16)
        for half in range(2):
            qo_ref[2 * blk + half, 0:HEAD_DIM, :] = y_t[half * HEAD_DIM:(half + 1) * HEAD_DIM, :]
            qo_ref[2 * blk + half, HEAD_DIM:LANES, :] = q_pad
    ones_row = jnp.where(lax.broadcasted_iota(jnp.int32, (V_ROWS - HEAD_DIM, tt), 0) == 0, 1.0, 0.0).astype(BF16)
    for blk in range(KV_W // LANES):
        y = normed(ks_ref, ksn_ref, blk, 1.0)
        ysw = pltpu.roll(y, HEAD_DIM, 1)
        ko_ref[2 * blk, 0] = jnp.where(low, y, onehot).astype(BF16)
        ko_ref[2 * blk + 1, 0] = jnp.where(low, ysw, onehot).astype(BF16)
        y = normed(kw_ref, kwn_ref, blk, 1.0)
        ysw = pltpu.roll(y, HEAD_DIM, 1)
        ko_ref[2 * blk, 1] = jnp.where(low, y, 0.0).astype(BF16)
        ko_ref[2 * blk + 1, 1] = jnp.where(low, ysw, 0.0).astype(BF16)
        for branch, src in enumerate((vs_ref, vw_ref)):
            v_t = jnp.transpose(src[:, blk * LANES:(blk + 1) * LANES].astype(F32))
            for half in range(2):
                g = 2 * blk + half
                vo_ref[g, branch, 0:HEAD_DIM, :] = v_t[half * HEAD_DIM:(half + 1) * HEAD_DIM, :].astype(BF16)
                vo_ref[g, branch, HEAD_DIM:V_ROWS, :] = ones_row


def _nsa_prep(proj_t, bsz, qn, ksn, kwn, seg, tt):
    t = proj_t.shape[0]

    def col(width, offset):
        return pl.BlockSpec((tt, width), lambda b, s: (s, (b * D_PROJ + offset) // width))

    def vec(width):
        return pl.BlockSpec((1, width), lambda b, s: (0, 0))

    g = N_KV_GROUPS
    return pl.pallas_call(
        functools.partial(_nsa_prep_kernel, tt=tt),
        grid=(bsz, t // tt),
        in_specs=[col(Q_W, COL_Q), col(KV_W, COL_KS), col(KV_W, COL_VS), col(KV_W, COL_KW), col(KV_W, COL_VW),
                  vec(Q_W), vec(KV_W), vec(KV_W), pl.BlockSpec((LANES, LANES), lambda b, s: (0, 0))],
        out_specs=[pl.BlockSpec((None, N_HEADS, LANES, tt), lambda b, s: (b, 0, 0, s)),
                   pl.BlockSpec((None, g, 2, tt, LANES), lambda b, s: (b, 0, 0, s, 0)),
                   pl.BlockSpec((None, g, 2, V_ROWS, tt), lambda b, s: (b, 0, 0, 0, s))],
        out_shape=[jax.ShapeDtypeStruct((bsz, N_HEADS, LANES, t), BF16),
                   jax.ShapeDtypeStruct((bsz, g, 2, t, LANES), BF16),
                   jax.ShapeDtypeStruct((bsz, g, 2, V_ROWS, t), BF16)],
        compiler_params=_params(("parallel", "parallel")),
    )(proj_t, proj_t, proj_t, proj_t, proj_t, qn, ksn, kwn, seg)


def _nsa_kernel(q_ref, kc_ref, vc_ref, k_ref, v_ref, bc_ref, bias_ref, cover_ref, gate_ref, o_ref,
                qq_ref, *scratch, tq, n_q, n_sblk):
    qi = pl.program_id(2)
    t0 = qi * tq
    rg = HEADS_PER_GROUP
    jpt = JOBS_PER_TRIP
    s_refs, smax_refs, p_refs = ((scratch[k:k + jpt], scratch[k + jpt:k + 2 * jpt]) for k in (0, 2 * jpt, 4 * jpt))
    m_refs, acc_refs = scratch[6 * jpt:6 * jpt + rg], scratch[6 * jpt + rg:]
    sel, win = 0, 1


    bias = jnp.concatenate([bc_ref[r] for r in range(rg)], axis=1)
    s = _dot(kc_ref[...], jnp.concatenate([q_ref[r] for r in range(rg)], axis=1)) + bias
    visible = bias > 0.5 * MASK_NEG
    p = jnp.exp2(s - jnp.max(s, axis=0, keepdims=True))
    p = jnp.where(visible, p / jnp.sum(p, axis=0, keepdims=True), 0.0)
    o_cmp_all = _dot(vc_ref[...], p.astype(BF16))
    o_cmp = [o_cmp_all[:, r * tq:(r + 1) * tq] for r in range(rg)]
    p_sum = p[:, 0:tq]
    for r in range(1, rg):
        p_sum = p_sum + p[:, r * tq:(r + 1) * tq]
    p_hi = p_sum.astype(BF16)
    p_lo = (p_sum - p_hi.astype(F32)).astype(BF16)
    imp_t = _dot(cover_ref[...], p_hi) + _dot(cover_ref[...], p_lo)

    score = imp_t[SEL_ROW0:SEL_ROW0 + MAX_SBLK, :]
    jrow = lax.broadcasted_iota(jnp.int32, (MAX_SBLK, tq), 0)
    qblk = (lax.broadcasted_iota(jnp.int32, (MAX_SBLK, tq), 1) + t0) // SEL_BLOCK
    causal = jrow <= qblk
    forced = causal & ((jrow == 0) | (jrow >= qblk - 1))
    score = jnp.where(forced, SEL_FORCED, jnp.where(causal, score, -1.0))
    rank = jnp.zeros((MAX_SBLK, tq), F32)
    for j in range(n_sblk):
        other = score[j:j + 1, :]
        ahead = (other > score) | ((other == score) & (jrow > j))
        rank = rank + jnp.where(ahead, 1.0, 0.0)
    n_top = min(N_SELECT, n_sblk)
    selected = (rank < n_top) & (score >= 0.0)
    neg = jnp.where(selected, 0.0, SEL_NEG).astype(BF16)
    for r in range(rg):
        qq_ref[r, 0:SEL_ROW0, :] = q_ref[r, 0:SEL_ROW0, :]
        qq_ref[r, SEL_ROW0:SEL_ROW0 + MAX_SBLK, :] = neg
        qq_ref[r, SEL_ROW0 + MAX_SBLK:LANES, :] = q_ref[r, SEL_ROW0 + MAX_SBLK:LANES, :]

    near_max = WINDOW // tq + 1
    plan = [(branch, a) for a in range(min(near_max, n_q)) for branch in (sel, win)]
    plan += [(sel, a) for a in range(near_max, n_q)]
    plan += [(sel, n_q)] * (-len(plan) % JOBS_PER_TRIP)
    groups = [plan[g:g + JOBS_PER_TRIP] for g in range(0, len(plan), JOBS_PER_TRIP)]
    n_groups = sum((qi >= group[0][1]).astype(jnp.int32) for group in groups)

    def jobs_of(i):
        jobs = []
        for branch, a in groups[i]:
            kind = (BIAS_DIAG, BIAS_NEAR, BIAS_WIN if branch == win else None)[min(a, 2)]
            real = None if a <= groups[i][0][1] else a <= qi
            tile = qi - a if real is None else jnp.where(real, qi - a, 0)
            jobs.append((branch, pl.multiple_of(tile * tq, tq), kind, real))
        return jobs

    def scores(i):
        for (branch, k0, kind, real), s_ref, smax_ref in zip(jobs_of(i), s_refs[i % 2], smax_refs[i % 2]):
            k = k_ref[branch, pl.ds(k0, tq), :]
            for r in range(rg):
                q_t = qq_ref[r] if branch == sel else q_ref[r]
                s = _dot(k, q_t)
                if kind is not None:
                    s = s + bias_ref[r, kind]
                s_ref[r] = s
                smax = jnp.max(s, axis=0, keepdims=True)
                smax_ref[r] = smax if real is None else jnp.where(real, smax, MASK_NEG)

    def softmax(i):
        jobs = jobs_of(i)
        for r in range(rg):
            for branch in sorted({b for b, _ in groups[i]}):
                mine = [j for j, (b, _) in enumerate(groups[i]) if b == branch]
                m_old = m_refs[r][branch]
                m_new = m_old
                for j in mine:
                    m_new = jnp.maximum(m_new, smax_refs[i % 2][j][r])
                for j in mine:
                    p = jnp.exp2(s_refs[i % 2][j][r] - m_new).astype(BF16)
                    real = jobs[j][3]
                    p_refs[i % 2][j][r] = p if real is None else jnp.where(real, p, jnp.zeros_like(p))
                m_refs[r][branch] = m_new
                acc_refs[r][branch] = jnp.exp2(m_old - m_new) * acc_refs[r][branch]

    def values(i):
        for (branch, k0, _, _), p_ref in zip(jobs_of(i), p_refs[i % 2]):
            v_t = v_ref[branch, :, pl.ds(k0, tq)]
            for r in range(rg):
                acc_refs[r][branch] += _dot(v_t, p_ref[r])

    for r in range(rg):
        m_refs[r][...] = jnp.full(m_refs[r].shape, MASK_NEG, F32)
        acc_refs[r][...] = jnp.zeros(acc_refs[r].shape, F32)
    scores(0)
    for i in range(len(groups)):
        if i + 1 < len(groups):
            @pl.when(i + 1 < n_groups)
            def _(i=i):
                if i > 0:
                    values(i - 1)
                softmax(i)
                scores(i + 1)

        @pl.when(i + 1 == n_groups)
        def _(i=i):
            if i > 0:
                values(i - 1)
            softmax(i)
            values(i)

    def finish(branch):
        return [acc_refs[r][branch, 0:HEAD_DIM, :] / acc_refs[r][branch, HEAD_DIM:HEAD_DIM + 1, :] for r in range(rg)]

    o_sel = finish(sel)
    o_win = finish(win)

    gates_t = jnp.transpose(_sigmoid(gate_ref[...].astype(F32)))
    outs = []
    for r in range(rg):
        outs.append(gates_t[3 * r:3 * r + 1, :] * o_cmp[r]
                    + gates_t[3 * r + 1:3 * r + 2, :] * o_sel[r]
                    + gates_t[3 * r + 2:3 * r + 3, :] * o_win[r])
    o_ref[...] = jnp.transpose(jnp.concatenate(outs, axis=0)).astype(o_ref.dtype)


def _nsa_attention(qh, kc, vc, k_all, v_all, bias_c, bias_tiles, cover, proj3, tq):
    bsz, _, _, t = qh.shape
    n_chunks = kc.shape[2]
    rg = HEADS_PER_GROUP
    n_kinds = bias_tiles.shape[1]
    return pl.pallas_call(
        functools.partial(_nsa_kernel, tq=tq, n_q=t // tq, n_sblk=t // SEL_BLOCK),
        grid=(bsz, N_KV_GROUPS, t // tq),
        in_specs=[
            pl.BlockSpec((None, rg, LANES, tq), lambda b, g, i: (b, g, 0, i)),
            pl.BlockSpec((None, None, n_chunks, LANES), lambda b, g, i: (b, g, 0, 0)),
            pl.BlockSpec((None, None, HEAD_DIM, n_chunks), lambda b, g, i: (b, g, 0, 0)),
            pl.BlockSpec((None, None, 2, t, LANES), lambda b, g, i: (b, g, 0, 0, 0)),
            pl.BlockSpec((None, None, 2, V_ROWS, t), lambda b, g, i: (b, g, 0, 0, 0)),
            pl.BlockSpec((rg, n_chunks, tq), lambda b, g, i: (g, 0, i)),
            pl.BlockSpec((rg, n_kinds, tq, tq), lambda b, g, i: (g, 0, 0, 0)),
            pl.BlockSpec((LANES, n_chunks), lambda b, g, i: (0, 0)),
            pl.BlockSpec((tq, LANES), lambda b, g, i: (i, (b * D_PROJ + COL_GN) // LANES + g)),
        ],
        out_specs=pl.BlockSpec((None, tq, rg * HEAD_DIM), lambda b, g, i: (b, i, g)),
        out_shape=jax.ShapeDtypeStruct((bsz, t, Q_W), BF16),
        scratch_shapes=[
            pltpu.VMEM((rg, LANES, tq), BF16),
        ] + [pltpu.VMEM((rg, tq, tq), F32)] * (2 * JOBS_PER_TRIP)
          + [pltpu.VMEM((rg, 1, tq), F32)] * (2 * JOBS_PER_TRIP)
          + [pltpu.VMEM((rg, tq, tq), BF16)] * (2 * JOBS_PER_TRIP)
          + [pltpu.VMEM((2, 1, tq), F32)] * rg
          + [pltpu.VMEM((2, V_ROWS, tq), F32)] * rg,
        compiler_params=_params(("parallel", "parallel", "arbitrary")),
    )(qh, kc, vc, k_all, v_all, bias_c, bias_tiles, cover, proj3)


def _merge_kernel(ya_ref, yb_ref, ga_ref, gb_ref, x_ref, pa_ref, pb_ref, wo_ref, h_ref):
    merged = (_sigmoid(ga_ref[...].astype(F32)) * _dot(ya_ref[...], pa_ref[...])
              + _sigmoid(gb_ref[...].astype(F32)) * _dot(yb_ref[...], pb_ref[...]))
    h_ref[...] = x_ref[...] + _dot(merged.astype(BF16), wo_ref[...])


def _merge_out(ya_t, yb, proj_t, x, pa, pb, wo, bsz, tm):
    t = ya_t.shape[0]
    tiles = t // tm

    def rows(width):
        return pl.BlockSpec((tm, width), lambda b, s: (b * tiles + s, 0))

    def time_major(width, offset):
        return pl.BlockSpec((tm, width), lambda b, s: (s, (b * D_PROJ + offset) // width))

    def whole(a):
        return pl.BlockSpec(a.shape, lambda b, s: (0, 0))

    return pl.pallas_call(
        _merge_kernel,
        grid=(bsz, tiles),
        in_specs=[pl.BlockSpec((tm, RNN_W), lambda b, s: (s, b)), rows(Q_W), time_major(D_MODEL, COL_GA),
                  time_major(D_MODEL, COL_GB), rows(D_MODEL), whole(pa), whole(pb), whole(wo)],
        out_specs=rows(D_MODEL),
        out_shape=jax.ShapeDtypeStruct((bsz * t, D_MODEL), F32),
        compiler_params=_params(("parallel", "parallel")),
    )(ya_t, yb, proj_t, proj_t, x, pa, pb, wo)


def _mlp_kernel(h_ref, g_ref, w1_ref, w2_ref, o_ref, hn_ref, acc_ref):
    j = pl.program_id(1)

    @pl.when(j == 0)
    def _():
        h = h_ref[...]
        y = h * lax.rsqrt(jnp.mean(h * h, axis=-1, keepdims=True) + NORM_EPS) * g_ref[...]
        hn_ref[...] = y.astype(BF16)
        acc_ref[...] = h

    z = jnp.maximum(_dot(hn_ref[...], w1_ref[...]), 0.0)
    acc_ref[...] += _dot((z * z).astype(BF16), w2_ref[...])

    @pl.when(j == pl.num_programs(1) - 1)
    def _():
        o_ref[...] = acc_ref[...]


def _mlp(h, gain, w1, w2, tm, tf):
    m, d = h.shape
    ff = w1.shape[1]
    return pl.pallas_call(
        _mlp_kernel,
        grid=(m // tm, ff // tf),
        in_specs=[
            pl.BlockSpec((tm, d), lambda i, j: (i, 0)),
            pl.BlockSpec((1, d), lambda i, j: (0, 0)),
            pl.BlockSpec((d, tf), lambda i, j: (0, j)),
            pl.BlockSpec((tf, d), lambda i, j: (j, 0)),
        ],
        out_specs=pl.BlockSpec((tm, d), lambda i, j: (i, 0)),
        out_shape=jax.ShapeDtypeStruct((m, d), F32),
        scratch_shapes=[pltpu.VMEM((tm, d), BF16), pltpu.VMEM((tm, d), F32)],
        compiler_params=_params(("parallel", "arbitrary")),
    )(h, gain, w1, w2)


def _t5_bucket_table():
    max_exact = REL_BUCKETS // 2
    d = np.arange(FAR_DIST)
    df = np.maximum(d.astype(np.float32), np.float32(1.0))
    large = max_exact + (np.log(df / np.float32(max_exact)) / np.float32(math.log(REL_MAX_DIST / max_exact))
                         * np.float32(REL_BUCKETS - max_exact)).astype(np.int32)
    large = np.minimum(large, REL_BUCKETS - 1)
    return np.where(d < max_exact, d, large).astype(np.int32)


def _pad_blocks(w, axis):
    shape = w.shape
    w = w.reshape(shape[:axis] + (LRU_BLOCKS, LRU_BLOCK_W) + shape[axis + 1:])
    pad = [(0, 0)] * w.ndim
    pad[axis + 1] = (0, RNN_BW - LRU_BLOCK_W)
    w = jnp.pad(w, pad)
    return w.reshape(shape[:axis] + (RNN_W,) + shape[axis + 1:])


def _in_proj_weight(w_in):
    cuts = np.cumsum((D_RNN, D_RNN, Q_W, KV_W, KV_W, KV_W, KV_W, KV_W, KV_W, 3 * N_HEADS, D_MODEL))
    w_in = w_in.astype(BF16)
    (w_u, w_gate, w_q, w_kc, w_vc, w_ks, w_vs, w_kw, w_vw, w_gn, w_ga, w_gb) = jnp.split(w_in, cuts, axis=1)
    per_group = 3 * HEADS_PER_GROUP
    w_gn = jnp.pad(w_gn.reshape(D_MODEL, N_KV_GROUPS, per_group), ((0, 0), (0, 0), (0, LANES - per_group)))
    w_gn = w_gn.reshape(D_MODEL, N_KV_GROUPS * LANES)
    gap = jnp.zeros((D_MODEL, COL_GA - COL_GN - N_KV_GROUPS * LANES), w_in.dtype)
    w = jnp.concatenate([_pad_blocks(w_u, 1), _pad_blocks(w_gate, 1), w_q, w_ks, w_vs, w_kw, w_vw, w_kc, w_vc,
                         w_gn, gap, w_ga, w_gb], axis=1)
    assert w.shape[1] == RNN_COLS + D_PROJ
    return w


def _phi_weights(pe, w1, w2):
    half, g = CMP_BLOCK // 2, N_KV_GROUPS
    eye = jnp.eye(g, dtype=BF16)
    w1h = w1.astype(BF16).reshape(2, half, HEAD_DIM, PHI_HIDDEN)
    w1e = jnp.einsum("xldh,gk->xlgdkh", w1h, eye).reshape(2, half * KV_W, g * PHI_HIDDEN)
    w2e = jnp.einsum("hd,gk->ghkd", w2.astype(BF16), eye).reshape(g * PHI_HIDDEN, KV_W)
    pee = jnp.broadcast_to(pe.reshape(2, half, 1, HEAD_DIM), (2, half, g, HEAD_DIM)).reshape(2, half * KV_W)
    return pee, w1e, w2e


def kernel(x, norm_mix, w_in, conv_w, conv_b, gate_a_w, gate_a_b, gate_x_w, gate_x_b, lru_lambda, phi_k_pe, phi_k_w1, phi_k_w2, phi_v_pe, phi_v_w1, phi_v_w2, q_norm, kc_norm, ks_norm, kw_norm, rel_bias, proj_a, proj_b, w_out, norm_mlp, w_mlp_in, w_mlp_out):
    bsz, t, d = x.shape
    assert d == D_MODEL and norm_mix.shape[0] == 1
    tq = 256
    assert t % tq == 0 and t // SEL_BLOCK <= MAX_SBLK and t % CMP_STRIDE == 0
    n_tok = bsz * t
    n_chunks = t // CMP_STRIDE
    assert n_chunks % 8 == 0 and n_chunks <= LANES
    x2 = x.reshape(n_tok, d)

    rnn_t, proj_t, zk, zv = _norm_matmul(x2, norm_mix, _in_proj_weight(w_in[0]), bsz,
                                         tm=1024 if t % 1024 == 0 else tq, tn=1024, rnn_cols=RNN_COLS,
                                         side_cols=((COL_KC, KV_W), (COL_VC, KV_W)))

    pad_w = lambda w: jnp.pad(w, ((0, 0), (0, RNN_BW - LRU_BLOCK_W), (0, RNN_BW - LRU_BLOCK_W))).astype(BF16)
    ya_t = _rglru(rnn_t.reshape(t, bsz, RNN_COLS), _pad_blocks(conv_w[0], 1), _pad_blocks(conv_b, 1),
                  pad_w(gate_a_w[0]), _pad_blocks(gate_a_b.reshape(1, D_RNN), 1),
                  pad_w(gate_x_w[0]), _pad_blocks(gate_x_b.reshape(1, D_RNN), 1),
                  _pad_blocks(lru_lambda, 1), tc=128)

    zk = zk.reshape(bsz, n_chunks, CMP_STRIDE * KV_W)
    zv = zv.reshape(bsz, n_chunks, CMP_STRIDE * KV_W)
    pe_k, w1_k, w2_k = _phi_weights(phi_k_pe[0], phi_k_w1[0], phi_k_w2[0])
    pe_v, w1_v, w2_v = _phi_weights(phi_v_pe[0], phi_v_w1[0], phi_v_w2[0])
    seg256 = jnp.asarray(np.kron(np.eye(KV_W // HEAD_DIM), np.ones((HEAD_DIM, HEAD_DIM))), BF16)
    kcn = jnp.tile(kc_norm, (1, N_KV_GROUPS))
    kc = _compress(zk, pe_k, w1_k, w2_k, (kcn, seg256))
    vc = _compress(zv, pe_v, w1_v, w2_v)

    seg128 = jnp.asarray(np.kron(np.eye(LANES // HEAD_DIM), np.ones((HEAD_DIM, HEAD_DIM))), BF16)
    qh, k_all, v_all = _nsa_prep(proj_t, bsz, jnp.tile(q_norm, (1, N_HEADS)), jnp.tile(ks_norm, (1, N_KV_GROUPS)),
                                 jnp.tile(kw_norm, (1, N_KV_GROUPS)), seg128, tt=tq)

    bucket_of = _t5_bucket_table()
    far = rel_bias[REL_BUCKETS - 1][:, None, None]

    def bias_table(dist, valid, shift):
        buckets = jnp.asarray(bucket_of[np.clip(dist, 0, FAR_DIST - 1)].astype(np.int8))
        onehot = (buckets[None] == jnp.arange(REL_BUCKETS, dtype=jnp.int8).reshape(-1, 1, 1)).astype(F32)
        vals = jnp.einsum("kh,kji->hji", rel_bias, onehot, precision=lax.Precision.HIGHEST)
        if shift:
            vals = vals - far
        return jnp.where(jnp.asarray(valid), vals * LOG2E, MASK_NEG).astype(F32)

    kj = np.arange(tq)[:, None]
    qi_ = np.arange(tq)[None, :]
    bias_diag = bias_table(qi_ - kj, qi_ >= kj, True)
    bias_near = bias_table(tq + qi_ - kj, np.ones((tq, tq), bool), True)
    bias_win = jnp.broadcast_to(jnp.asarray(np.where(kj > qi_, 0.0, MASK_NEG), F32), (N_HEADS, tq, tq))
    bias_tiles = jnp.stack([bias_diag, bias_near, bias_win], axis=1)
    assert WINDOW == 2 * tq
    cidx = np.arange(n_chunks)[:, None]
    dist_c = np.arange(t)[None, :] - (cidx * CMP_STRIDE + CMP_BLOCK - 1)
    bias_c = bias_table(dist_c, (dist_c >= 0) & (cidx < n_chunks - 1), False)
    cstart = np.arange(n_chunks) * CMP_STRIDE
    sj = np.arange(MAX_SBLK)
    cov = ((cstart[None, :] < (sj[:, None] + 1) * SEL_BLOCK) & (cstart[None, :] + CMP_BLOCK - 1 >= sj[:, None] * SEL_BLOCK)
           & (np.arange(n_chunks)[None, :] < n_chunks - 1))
    cover = np.zeros((LANES, n_chunks), np.float32)
    cover[SEL_ROW0:SEL_ROW0 + MAX_SBLK] = cov
    yb = _nsa_attention(qh, kc, vc, k_all, v_all, bias_c, bias_tiles, jnp.asarray(cover, BF16), proj_t, tq)

    pa = jnp.pad(proj_a[0].reshape(LRU_BLOCKS, LRU_BLOCK_W, D_MODEL), ((0, 0), (0, RNN_BW - LRU_BLOCK_W), (0, 0)))
    h = _merge_out(ya_t.reshape(t, bsz * RNN_W), yb.reshape(n_tok, Q_W), proj_t, x2,
                   pa.reshape(RNN_W, D_MODEL).astype(BF16), proj_b[0].astype(BF16), w_out[0].astype(BF16),
                   bsz, tm=512 if t % 512 == 0 else tq)

    out = _mlp(h, norm_mlp, w_mlp_in[0].astype(BF16), w_mlp_out[0].astype(BF16),
               tm=1024 if n_tok % 1024 == 0 else tq, tf=1024)
    return out.reshape(bsz, t, d)
```

```python
import functools
import math

import numpy as np
import jax
import jax.numpy as jnp
from jax import lax
from jax.experimental import pallas as pl
from jax.experimental.pallas import tpu as pltpu

F32 = jnp.float32
BF16 = jnp.bfloat16

D_MODEL = 1024
D_RNN = 1344
LRU_BLOCKS = 4
LRU_BLOCK_W = D_RNN // LRU_BLOCKS
CONV_W = 4
LRU_C = 8.0
N_HEADS = 16
HEAD_DIM = 64
N_KV_GROUPS = 4
HEADS_PER_GROUP = N_HEADS // N_KV_GROUPS
CMP_BLOCK = 32
CMP_STRIDE = 16
SEL_BLOCK = 64
N_SELECT = 16
WINDOW = 512
PHI_HIDDEN = 256
SEL_FORCED = 1e4
REL_BUCKETS = 32
REL_MAX_DIST = 128
D_FF = 4 * D_MODEL
NORM_EPS = 1e-6
Q_W = N_HEADS * HEAD_DIM
KV_W = N_KV_GROUPS * HEAD_DIM

LANES = 128
VMEM_LIMIT = 56 * 1024 * 1024

RNN_BW = 384
RNN_W = LRU_BLOCKS * RNN_BW
MASK_NEG = -1e30
SEL_NEG = -1e9
SEL_ROW0 = 64
MAX_SBLK = 32
FAR_DIST = 256
LOG2E = math.log2(math.e)
V_ROWS = HEAD_DIM + 16
BIAS_DIAG, BIAS_NEAR, BIAS_WIN = range(3)
JOBS_PER_TRIP = 2

COL_U = 0
COL_GATE = RNN_W
RNN_COLS = 2 * RNN_W
COL_Q = 0
COL_KS = COL_Q + Q_W
COL_VS = COL_KS + KV_W
COL_KW = COL_VS + KV_W
COL_VW = COL_KW + KV_W
COL_KC = COL_VW + KV_W
COL_VC = COL_KC + KV_W
COL_GN = COL_VC + KV_W
COL_GA = 3072
COL_GB = 4096
D_PROJ = 5120


def _dot(a, b):
    return jnp.dot(a, b, preferred_element_type=F32)


def _dot_nt(a, b):
    return lax.dot_general(a, b, (((1,), (1,)), ((), ())), preferred_element_type=F32)


def _gelu_tanh(x):
    return 0.5 * x * (1.0 + jnp.tanh(math.sqrt(2.0 / math.pi) * (x + 0.044715 * (x * x * x))))


def _sigmoid(x):
    return 0.5 * jnp.tanh(0.5 * x) + 0.5


def _seg_sum(x, seg_ones):
    hi = x.astype(BF16)
    lo = (x - hi.astype(F32)).astype(BF16)
    return _dot(hi, seg_ones) + _dot(lo, seg_ones)


def _params(sem, flags=None):
    return pltpu.CompilerParams(dimension_semantics=sem, vmem_limit_bytes=VMEM_LIMIT, flags=flags)


def _norm_matmul_kernel(x_ref, g_ref, w_ref, rnn_ref, o_ref, *rest, tn, rnn_tiles, side_cols):
    side_refs, xn_ref = rest[:-1], rest[-1]
    j = pl.program_id(1)

    @pl.when(j == 0)
    def _():
        x = x_ref[...]
        y = x * lax.rsqrt(jnp.mean(x * x, axis=-1, keepdims=True) + NORM_EPS) * g_ref[...]
        xn_ref[...] = y.astype(BF16)

    res = _dot(xn_ref[...], w_ref[...]).astype(o_ref.dtype)

    @pl.when(j < rnn_tiles)
    def _():
        rnn_ref[...] = res

    @pl.when(j >= rnn_tiles)
    def _():
        o_ref[...] = res

    for (col, width), side_ref in zip(side_cols, side_refs):
        @pl.when(j == rnn_tiles + col // tn)
        def _(col=col, width=width, side_ref=side_ref):
            side_ref[...] = res[:, col % tn:col % tn + width]


def _norm_matmul(x, gain, w, bsz, tm, tn, rnn_cols, side_cols):
    m, k = x.shape
    n = w.shape[1] - rnn_cols
    t = m // bsz
    tiles = t // tm
    rnn_tiles, col_tiles = rnn_cols // tn, n // tn
    assert all(col // tn == (col + width - 1) // tn for col, width in side_cols)
    return pl.pallas_call(
        functools.partial(_norm_matmul_kernel, tn=tn, rnn_tiles=rnn_tiles, side_cols=side_cols),
        grid=(m // tm, rnn_tiles + col_tiles),
        in_specs=[
            pl.BlockSpec((tm, k), lambda i, j: (i, 0)),
            pl.BlockSpec((1, k), lambda i, j: (0, 0)),
            pl.BlockSpec((k, tn), lambda i, j: (0, j)),
        ],
        out_specs=[pl.BlockSpec((tm, tn), lambda i, j: (i % tiles, (i // tiles) * rnn_tiles + jnp.minimum(j, rnn_tiles - 1))),
                   pl.BlockSpec((tm, tn), lambda i, j: (i % tiles, (i // tiles) * col_tiles + jnp.maximum(j - rnn_tiles, 0)))]
                  + [pl.BlockSpec((tm, width), lambda i, j: (i, 0)) for _, width in side_cols],
        out_shape=[jax.ShapeDtypeStruct((t, bsz * rnn_cols), BF16), jax.ShapeDtypeStruct((t, bsz * n), BF16)]
                  + [jax.ShapeDtypeStruct((m, width), BF16) for _, width in side_cols],
        scratch_shapes=[pltpu.VMEM((tm, k), BF16)],
        compiler_params=_params(("parallel", "arbitrary")),
    )(x, gain, w)


def _rglru_kernel(u_ref, ug_ref, cw_ref, cb_ref, wa_ref, ba_ref, wx_ref, bx_ref, lam_ref,
                  y_ref, ubuf, a_scr, b_scr, hcar, *, tc, bsz):
    t = pl.program_id(1)
    rows = tc * bsz
    halo = (CONV_W - 1) * bsz

    @pl.when(t == 0)
    def _():
        ubuf[0:halo, :] = jnp.zeros((halo, RNN_BW), F32)
        hcar[...] = jnp.zeros_like(hcar)

    u = u_ref[...].astype(F32).reshape(rows, RNN_BW)
    ubuf[halo:halo + rows, :] = u
    xc = cb_ref[...]
    for k in range(CONV_W):
        xc = xc + cw_ref[k:k + 1, :] * ubuf[k * bsz:k * bsz + rows, :]
    ubuf[0:halo, :] = u[rows - halo:rows, :]

    xb = xc.astype(BF16)
    r = _sigmoid(_dot(xb, wa_ref[...]) + ba_ref[...])
    i = _sigmoid(_dot(xb, wx_ref[...]) + bx_ref[...])
    z = -lam_ref[...]
    softplus = jnp.maximum(z, 0.0) + jnp.log(1.0 + jnp.exp(-jnp.abs(z)))
    a = jnp.exp(r * ((-LRU_C) * softplus))
    mult = jnp.sqrt(1.0 - a * a)
    row = lax.broadcasted_iota(jnp.int32, (rows, 1), 0)
    mult = jnp.where((row < bsz) & (t == 0), 1.0, mult)
    a_scr[...] = a
    b_scr[...] = mult * (i * xc)

    def step(s, h):
        at = pl.ds(pl.multiple_of(s * bsz, bsz), bsz)
        h = a_scr[at, :] * h + b_scr[at, :]
        b_scr[at, :] = h
        return h

    hcar[...] = lax.fori_loop(0, tc, step, hcar[...], unroll=8)
    y = b_scr[...] * _gelu_tanh(ug_ref[...].astype(F32).reshape(rows, RNN_BW))
    y_ref[...] = y.reshape(tc, bsz, RNN_BW).astype(y_ref.dtype)


def _rglru(proj3, conv_w, conv_b, wa, ba, wx, bx, lam, tc):
    t, bsz, _ = proj3.shape
    assert bsz % 8 == 0, "the recurrence advances whole sublane groups of sequences"
    nb = LRU_BLOCKS
    rows = tc * bsz
    vec = pl.BlockSpec((1, RNN_BW), lambda n, s: (0, n))
    mat = pl.BlockSpec((None, RNN_BW, RNN_BW), lambda n, s: (n, 0, 0))
    return pl.pallas_call(
        functools.partial(_rglru_kernel, tc=tc, bsz=bsz),
        grid=(nb, t // tc),
        in_specs=[
            pl.BlockSpec((tc, bsz, RNN_BW), lambda n, s: (s, 0, COL_U // RNN_BW + n)),
            pl.BlockSpec((tc, bsz, RNN_BW), lambda n, s: (s, 0, COL_GATE // RNN_BW + n)),
            pl.BlockSpec((CONV_W, RNN_BW), lambda n, s: (0, n)),
            vec, mat, vec, mat, vec, vec,
        ],
        out_specs=pl.BlockSpec((tc, bsz, RNN_BW), lambda n, s: (s, 0, n)),
        out_shape=jax.ShapeDtypeStruct((t, bsz, RNN_W), BF16),
        scratch_shapes=[pltpu.VMEM((rows + (CONV_W - 1) * bsz, RNN_BW), F32),
                        pltpu.VMEM((rows, RNN_BW), F32), pltpu.VMEM((rows, RNN_BW), F32),
                        pltpu.VMEM((bsz, RNN_BW), F32)],
        compiler_params=_params(("parallel", "arbitrary")),
    )(proj3, proj3, conv_w, conv_b, wa, ba, wx, bx, lam)


def _compress_kernel(z_ref, pe_ref, w1_ref, w2_ref, *rest, n_chunks, is_key):
    norm_ref, seg_ref, o_ref = rest if is_key else (None, None) + rest
    z = z_ref[...].astype(F32)
    first = _dot((z + pe_ref[0:1, :]).astype(BF16), w1_ref[0])
    second = _dot((z + pe_ref[1:2, :]).astype(BF16), w1_ref[1])
    pre = first + pltpu.roll(second, n_chunks - 1, 0)
    out = _dot(_gelu_tanh(pre).astype(BF16), w2_ref[...])
    if is_key:
        ssq = _seg_sum(out * out, seg_ref[...])
        normed = out * lax.rsqrt(ssq * (1.0 / HEAD_DIM) + NORM_EPS) * norm_ref[...]
        low = lax.broadcasted_iota(jnp.int32, (n_chunks, LANES), 1) < HEAD_DIM
        for pair in range(N_KV_GROUPS // 2):
            blk = normed[:, pair * LANES:(pair + 1) * LANES]
            o_ref[2 * pair] = jnp.where(low, blk, 0.0).astype(o_ref.dtype)
            o_ref[2 * pair + 1] = jnp.where(low, pltpu.roll(blk, HEAD_DIM, 1), 0.0).astype(o_ref.dtype)
    else:
        out_t = jnp.transpose(out)
        for g in range(N_KV_GROUPS):
            o_ref[g] = out_t[g * HEAD_DIM:(g + 1) * HEAD_DIM, :].astype(o_ref.dtype)


def _compress(z, pe, w1, w2, norm_and_seg=None):
    bsz, n_chunks, width = z.shape
    hid = w1.shape[-1]
    is_key = norm_and_seg is not None
    out_block = (N_KV_GROUPS, n_chunks, LANES) if is_key else (N_KV_GROUPS, HEAD_DIM, n_chunks)
    extra = norm_and_seg if is_key else ()
    return pl.pallas_call(
        functools.partial(_compress_kernel, n_chunks=n_chunks, is_key=is_key),
        grid=(bsz,),
        in_specs=[
            pl.BlockSpec((None, n_chunks, width), lambda b: (b, 0, 0)),
            pl.BlockSpec((2, width), lambda b: (0, 0)),
            pl.BlockSpec((2, width, hid), lambda b: (0, 0, 0)),
            pl.BlockSpec((hid, KV_W), lambda b: (0, 0)),
        ] + [pl.BlockSpec(a.shape, lambda b: (0, 0)) for a in extra],
        out_specs=pl.BlockSpec((None,) + out_block, lambda b: (b, 0, 0, 0)),
        out_shape=jax.ShapeDtypeStruct((bsz,) + out_block, BF16),
        compiler_params=_params(("parallel",)),
    )(z, pe, w1, w2, *extra)


def _nsa_prep_kernel(q_ref, ks_ref, vs_ref, kw_ref, vw_ref, qn_ref, ksn_ref, kwn_ref, seg_ref,
                     qo_ref, ko_ref, vo_ref, *, tt):
    t0 = pl.program_id(1) * tt
    lane = lax.broadcasted_iota(jnp.int32, (tt, LANES), 1)
    row = lax.broadcasted_iota(jnp.int32, (tt, LANES), 0) + t0
    low = lane < HEAD_DIM
    seg = seg_ref[...]
    onehot = jnp.where((lane - SEL_ROW0) == row // SEL_BLOCK, 1.0, 0.0)

    def normed(ref, gain_ref, blk, scale):
        x = ref[:, blk * LANES:(blk + 1) * LANES].astype(F32)
        ssq = _seg_sum(x * x, seg)
        y = x * lax.rsqrt(ssq * (1.0 / HEAD_DIM) + NORM_EPS) * gain_ref[:, blk * LANES:(blk + 1) * LANES]
        return y * scale if scale != 1.0 else y

    q_pad = jnp.zeros((LANES - HEAD_DIM, tt), BF16)
    for blk in range(Q_W // LANES):
        y_t = jnp.transpose(normed(q_ref, qn_ref, blk, HEAD_DIM ** -0.5 * LOG2E)).astype(BF16)
        for half in range(2):
            qo_ref[2 * blk + half, 0:HEAD_DIM, :] = y_t[half * HEAD_DIM:(half + 1) * HEAD_DIM, :]
            qo_ref[2 * blk + half, HEAD_DIM:LANES, :] = q_pad
    ones_row = jnp.where(lax.broadcasted_iota(jnp.int32, (V_ROWS - HEAD_DIM, tt), 0) == 0, 1.0, 0.0).astype(BF16)
    for blk in range(KV_W // LANES):
        y = normed(ks_ref, ksn_ref, blk, 1.0)
        ysw = pltpu.roll(y, HEAD_DIM, 1)
        ko_ref[2 * blk, 0] = jnp.where(low, y, onehot).astype(BF16)
        ko_ref[2 * blk + 1, 0] = jnp.where(low, ysw, onehot).astype(BF16)
        y = normed(kw_ref, kwn_ref, blk, 1.0)
        ysw = pltpu.roll(y, HEAD_DIM, 1)
        ko_ref[2 * blk, 1] = jnp.where(low, y, 0.0).astype(BF16)
        ko_ref[2 * blk + 1, 1] = jnp.where(low, ysw, 0.0).astype(BF16)
        for branch, src in enumerate((vs_ref, vw_ref)):
            v_t = jnp.transpose(src[:, blk * LANES:(blk + 1) * LANES].astype(F32))
            for half in range(2):
                g = 2 * blk + half
                vo_ref[g, branch, 0:HEAD_DIM, :] = v_t[half * HEAD_DIM:(half + 1) * HEAD_DIM, :].astype(BF16)
                vo_ref[g, branch, HEAD_DIM:V_ROWS, :] = ones_row


def _nsa_prep(proj_t, bsz, qn, ksn, kwn, seg, tt):
    t = proj_t.shape[0]

    def col(width, offset):
        return pl.BlockSpec((tt, width), lambda b, s: (s, (b * D_PROJ + offset) // width))

    def vec(width):
        return pl.BlockSpec((1, width), lambda b, s: (0, 0))

    g = N_KV_GROUPS
    return pl.pallas_call(
        functools.partial(_nsa_prep_kernel, tt=tt),
        grid=(bsz, t // tt),
        in_specs=[col(Q_W, COL_Q), col(KV_W, COL_KS), col(KV_W, COL_VS), col(KV_W, COL_KW), col(KV_W, COL_VW),
                  vec(Q_W), vec(KV_W), vec(KV_W), pl.BlockSpec((LANES, LANES), lambda b, s: (0, 0))],
        out_specs=[pl.BlockSpec((None, N_HEADS, LANES, tt), lambda b, s: (b, 0, 0, s)),
                   pl.BlockSpec((None, g, 2, tt, LANES), lambda b, s: (b, 0, 0, s, 0)),
                   pl.BlockSpec((None, g, 2, V_ROWS, tt), lambda b, s: (b, 0, 0, 0, s))],
        out_shape=[jax.ShapeDtypeStruct((bsz, N_HEADS, LANES, t), BF16),
                   jax.ShapeDtypeStruct((bsz, g, 2, t, LANES), BF16),
                   jax.ShapeDtypeStruct((bsz, g, 2, V_ROWS, t), BF16)],
        compiler_params=_params(("parallel", "parallel")),
    )(proj_t, proj_t, proj_t, proj_t, proj_t, qn, ksn, kwn, seg)


def _nsa_kernel(q_ref, kc_ref, vc_ref, k_ref, v_ref, bc_ref, bias_ref, cover_ref, gate_ref, o_ref,
                qq_ref, *scratch, tq, n_q, n_sblk):
    qi = pl.program_id(2)
    t0 = qi * tq
    rg = HEADS_PER_GROUP
    jpt = JOBS_PER_TRIP
    s_refs, smax_refs, p_refs = ((scratch[k:k + jpt], scratch[k + jpt:k + 2 * jpt]) for k in (0, 2 * jpt, 4 * jpt))
    m_refs, acc_refs = scratch[6 * jpt:6 * jpt + rg], scratch[6 * jpt + rg:]
    sel, win = 0, 1


    bias = jnp.concatenate([bc_ref[r] for r in range(rg)], axis=1)
    s = _dot(kc_ref[...], jnp.concatenate([q_ref[r] for r in range(rg)], axis=1)) + bias
    visible = bias > 0.5 * MASK_NEG
    p = jnp.exp2(s - jnp.max(s, axis=0, keepdims=True))
    p = jnp.where(visible, p / jnp.sum(p, axis=0, keepdims=True), 0.0)
    o_cmp_all = _dot(vc_ref[...], p.astype(BF16))
    o_cmp = [o_cmp_all[:, r * tq:(r + 1) * tq] for r in range(rg)]
    p_sum = p[:, 0:tq]
    for r in range(1, rg):
        p_sum = p_sum + p[:, r * tq:(r + 1) * tq]
    p_hi = p_sum.astype(BF16)
    p_lo = (p_sum - p_hi.astype(F32)).astype(BF16)
    imp_t = _dot(cover_ref[...], p_hi) + _dot(cover_ref[...], p_lo)

    score = imp_t[SEL_ROW0:SEL_ROW0 + MAX_SBLK, :]
    jrow = lax.broadcasted_iota(jnp.int32, (MAX_SBLK, tq), 0)
    qblk = (lax.broadcasted_iota(jnp.int32, (MAX_SBLK, tq), 1) + t0) // SEL_BLOCK
    causal = jrow <= qblk
    forced = causal & ((jrow == 0) | (jrow >= qblk - 1))
    score = jnp.where(forced, SEL_FORCED, jnp.where(causal, score, -1.0))
    rank = jnp.zeros((MAX_SBLK, tq), F32)
    for j in range(n_sblk):
        other = score[j:j + 1, :]
        ahead = (other > score) | ((other == score) & (jrow > j))
        rank = rank + jnp.where(ahead, 1.0, 0.0)
    n_top = min(N_SELECT, n_sblk)
    selected = (rank < n_top) & (score >= 0.0)
    neg = jnp.where(selected, 0.0, SEL_NEG).astype(BF16)
    for r in range(rg):
        qq_ref[r, 0:SEL_ROW0, :] = q_ref[r, 0:SEL_ROW0, :]
        qq_ref[r, SEL_ROW0:SEL_ROW0 + MAX_SBLK, :] = neg
        qq_ref[r, SEL_ROW0 + MAX_SBLK:LANES, :] = q_ref[r, SEL_ROW0 + MAX_SBLK:LANES, :]

    near_max = WINDOW // tq + 1
    plan = [(branch, a) for a in range(min(near_max, n_q)) for branch in (sel, win)]
    plan += [(sel, a) for a in range(near_max, n_q)]
    groups = [plan[g:g + JOBS_PER_TRIP] for g in range(0, len(plan), JOBS_PER_TRIP)]
    n_groups = sum((qi >= group[0][1]).astype(jnp.int32) for group in groups)

    def count(i):
        return sum((qi >= a).astype(jnp.int32) for _, a in groups[i])

    def count_options(i):
        return sorted({sum(1 for _, a in groups[i] if a <= top) for _, top in groups[i]})

    def jobs_of(i, c):
        return [(branch, pl.multiple_of((qi - a) * tq, tq),
                 (BIAS_DIAG, BIAS_NEAR, BIAS_WIN if branch == win else None)[min(a, 2)])
                for branch, a in groups[i][:c]]

    def scores(i, c):
        for (branch, k0, kind), s_ref, smax_ref in zip(jobs_of(i, c), s_refs[i % 2], smax_refs[i % 2]):
            k = k_ref[branch, pl.ds(k0, tq), :]
            for r in range(rg):
                q_t = qq_ref[r] if branch == sel else q_ref[r]
                s = _dot(k, q_t)
                if kind is not None:
                    s = s + bias_ref[r, kind]
                s_ref[r] = s
                smax_ref[r] = jnp.max(s, axis=0, keepdims=True)

    def softmax(i, c):
        for r in range(rg):
            for branch in sorted({b for b, _ in groups[i][:c]}):
                mine = [j for j, (b, _) in enumerate(groups[i][:c]) if b == branch]
                m_old = m_refs[r][branch]
                m_new = m_old
                for j in mine:
                    m_new = jnp.maximum(m_new, smax_refs[i % 2][j][r])
                for j in mine:
                    p_refs[i % 2][j][r] = jnp.exp2(s_refs[i % 2][j][r] - m_new).astype(BF16)
                m_refs[r][branch] = m_new
                acc_refs[r][branch] = jnp.exp2(m_old - m_new) * acc_refs[r][branch]

    def values(i, c):
        for (branch, k0, _), p_ref in zip(jobs_of(i, c), p_refs[i % 2]):
            v_t = v_ref[branch, :, pl.ds(k0, tq)]
            for r in range(rg):
                acc_refs[r][branch] += _dot(v_t, p_ref[r])

    for r in range(rg):
        m_refs[r][...] = jnp.full(m_refs[r].shape, MASK_NEG, F32)
        acc_refs[r][...] = jnp.zeros(acc_refs[r].shape, F32)
    assert count_options(0) == [len(groups[0])]
    scores(0, len(groups[0]))
    for i in range(len(groups)):
        full = len(groups[i])
        if i + 1 < len(groups):
            for c_next in count_options(i + 1):
                @pl.when((i + 1 < n_groups) & (count(i + 1) == c_next))
                def _(i=i, full=full, c_next=c_next):
                    if i > 0:
                        values(i - 1, len(groups[i - 1]))
                    softmax(i, full)
                    scores(i + 1, c_next)

        for c in count_options(i):
            @pl.when((i + 1 == n_groups) & (count(i) == c))
            def _(i=i, c=c):
                if i > 0:
                    values(i - 1, len(groups[i - 1]))
                softmax(i, c)
                values(i, c)

    def finish(branch):
        return [acc_refs[r][branch, 0:HEAD_DIM, :] / acc_refs[r][branch, HEAD_DIM:HEAD_DIM + 1, :] for r in range(rg)]

    o_sel = finish(sel)
    o_win = finish(win)

    gates_t = jnp.transpose(_sigmoid(gate_ref[...].astype(F32)))
    outs = []
    for r in range(rg):
        outs.append(gates_t[3 * r:3 * r + 1, :] * o_cmp[r]
                    + gates_t[3 * r + 1:3 * r + 2, :] * o_sel[r]
                    + gates_t[3 * r + 2:3 * r + 3, :] * o_win[r])
    o_ref[...] = jnp.transpose(jnp.concatenate(outs, axis=0)).astype(o_ref.dtype)


def _nsa_attention(qh, kc, vc, k_all, v_all, bias_c, bias_tiles, cover, proj3, tq):
    bsz, _, _, t = qh.shape
    n_chunks = kc.shape[2]
    rg = HEADS_PER_GROUP
    n_kinds = bias_tiles.shape[1]
    return pl.pallas_call(
        functools.partial(_nsa_kernel, tq=tq, n_q=t // tq, n_sblk=t // SEL_BLOCK),
        grid=(bsz, N_KV_GROUPS, t // tq),
        in_specs=[
            pl.BlockSpec((None, rg, LANES, tq), lambda b, g, i: (b, g, 0, i)),
            pl.BlockSpec((None, None, n_chunks, LANES), lambda b, g, i: (b, g, 0, 0)),
            pl.BlockSpec((None, None, HEAD_DIM, n_chunks), lambda b, g, i: (b, g, 0, 0)),
            pl.BlockSpec((None, None, 2, t, LANES), lambda b, g, i: (b, g, 0, 0, 0)),
            pl.BlockSpec((None, None, 2, V_ROWS, t), lambda b, g, i: (b, g, 0, 0, 0)),
            pl.BlockSpec((rg, n_chunks, tq), lambda b, g, i: (g, 0, i)),
            pl.BlockSpec((rg, n_kinds, tq, tq), lambda b, g, i: (g, 0, 0, 0)),
            pl.BlockSpec((LANES, n_chunks), lambda b, g, i: (0, 0)),
            pl.BlockSpec((tq, LANES), lambda b, g, i: (i, (b * D_PROJ + COL_GN) // LANES + g)),
        ],
        out_specs=pl.BlockSpec((None, tq, rg * HEAD_DIM), lambda b, g, i: (b, i, g)),
        out_shape=jax.ShapeDtypeStruct((bsz, t, Q_W), BF16),
        scratch_shapes=[
            pltpu.VMEM((rg, LANES, tq), BF16),
        ] + [pltpu.VMEM((rg, tq, tq), F32)] * (2 * JOBS_PER_TRIP)
          + [pltpu.VMEM((rg, 1, tq), F32)] * (2 * JOBS_PER_TRIP)
          + [pltpu.VMEM((rg, tq, tq), BF16)] * (2 * JOBS_PER_TRIP)
          + [pltpu.VMEM((2, 1, tq), F32)] * rg
          + [pltpu.VMEM((2, V_ROWS, tq), F32)] * rg,
        compiler_params=_params(("parallel", "parallel", "arbitrary")),
    )(qh, kc, vc, k_all, v_all, bias_c, bias_tiles, cover, proj3)


def _merge_kernel(ya_ref, yb_ref, ga_ref, gb_ref, x_ref, pa_ref, pb_ref, wo_ref, h_ref):
    merged = (_sigmoid(ga_ref[...].astype(F32)) * _dot(ya_ref[...], pa_ref[...])
              + _sigmoid(gb_ref[...].astype(F32)) * _dot(yb_ref[...], pb_ref[...]))
    h_ref[...] = x_ref[...] + _dot(merged.astype(BF16), wo_ref[...])


def _merge_out(ya_t, yb, proj_t, x, pa, pb, wo, bsz, tm):
    t = ya_t.shape[0]
    tiles = t // tm

    def rows(width):
        return pl.BlockSpec((tm, width), lambda b, s: (b * tiles + s, 0))

    def time_major(width, offset):
        return pl.BlockSpec((tm, width), lambda b, s: (s, (b * D_PROJ + offset) // width))

    def whole(a):
        return pl.BlockSpec(a.shape, lambda b, s: (0, 0))

    return pl.pallas_call(
        _merge_kernel,
        grid=(bsz, tiles),
        in_specs=[pl.BlockSpec((tm, RNN_W), lambda b, s: (s, b)), rows(Q_W), time_major(D_MODEL, COL_GA),
                  time_major(D_MODEL, COL_GB), rows(D_MODEL), whole(pa), whole(pb), whole(wo)],
        out_specs=rows(D_MODEL),
        out_shape=jax.ShapeDtypeStruct((bsz * t, D_MODEL), F32),
        compiler_params=_params(("parallel", "parallel")),
    )(ya_t, yb, proj_t, proj_t, x, pa, pb, wo)


def _mlp_kernel(h_ref, g_ref, w1_ref, w2_ref, o_ref, hn_ref, acc_ref):
    j = pl.program_id(1)

    @pl.when(j == 0)
    def _():
        h = h_ref[...]
        y = h * lax.rsqrt(jnp.mean(h * h, axis=-1, keepdims=True) + NORM_EPS) * g_ref[...]
        hn_ref[...] = y.astype(BF16)
        acc_ref[...] = h

    z = jnp.maximum(_dot(hn_ref[...], w1_ref[...]), 0.0)
    acc_ref[...] += _dot((z * z).astype(BF16), w2_ref[...])

    @pl.when(j == pl.num_programs(1) - 1)
    def _():
        o_ref[...] = acc_ref[...]


def _mlp(h, gain, w1, w2, tm, tf):
    m, d = h.shape
    ff = w1.shape[1]
    return pl.pallas_call(
        _mlp_kernel,
        grid=(m // tm, ff // tf),
        in_specs=[
            pl.BlockSpec((tm, d), lambda i, j: (i, 0)),
            pl.BlockSpec((1, d), lambda i, j: (0, 0)),
            pl.BlockSpec((d, tf), lambda i, j: (0, j)),
            pl.BlockSpec((tf, d), lambda i, j: (j, 0)),
        ],
        out_specs=pl.BlockSpec((tm, d), lambda i, j: (i, 0)),
        out_shape=jax.ShapeDtypeStruct((m, d), F32),
        scratch_shapes=[pltpu.VMEM((tm, d), BF16), pltpu.VMEM((tm, d), F32)],
        compiler_params=_params(("parallel", "arbitrary")),
    )(h, gain, w1, w2)


def _t5_bucket_table():
    max_exact = REL_BUCKETS // 2
    d = np.arange(FAR_DIST)
    df = np.maximum(d.astype(np.float32), np.float32(1.0))
    large = max_exact + (np.log(df / np.float32(max_exact)) / np.float32(math.log(REL_MAX_DIST / max_exact))
                         * np.float32(REL_BUCKETS - max_exact)).astype(np.int32)
    large = np.minimum(large, REL_BUCKETS - 1)
    return np.where(d < max_exact, d, large).astype(np.int32)


def _pad_blocks(w, axis):
    shape = w.shape
    w = w.reshape(shape[:axis] + (LRU_BLOCKS, LRU_BLOCK_W) + shape[axis + 1:])
    pad = [(0, 0)] * w.ndim
    pad[axis + 1] = (0, RNN_BW - LRU_BLOCK_W)
    w = jnp.pad(w, pad)
    return w.reshape(shape[:axis] + (RNN_W,) + shape[axis + 1:])


def _in_proj_weight(w_in):
    cuts = np.cumsum((D_RNN, D_RNN, Q_W, KV_W, KV_W, KV_W, KV_W, KV_W, KV_W, 3 * N_HEADS, D_MODEL))
    w_in = w_in.astype(BF16)
    (w_u, w_gate, w_q, w_kc, w_vc, w_ks, w_vs, w_kw, w_vw, w_gn, w_ga, w_gb) = jnp.split(w_in, cuts, axis=1)
    per_group = 3 * HEADS_PER_GROUP
    w_gn = jnp.pad(w_gn.reshape(D_MODEL, N_KV_GROUPS, per_group), ((0, 0), (0, 0), (0, LANES - per_group)))
    w_gn = w_gn.reshape(D_MODEL, N_KV_GROUPS * LANES)
    gap = jnp.zeros((D_MODEL, COL_GA - COL_GN - N_KV_GROUPS * LANES), w_in.dtype)
    w = jnp.concatenate([_pad_blocks(w_u, 1), _pad_blocks(w_gate, 1), w_q, w_ks, w_vs, w_kw, w_vw, w_kc, w_vc,
                         w_gn, gap, w_ga, w_gb], axis=1)
    assert w.shape[1] == RNN_COLS + D_PROJ
    return w


def _phi_weights(pe, w1, w2):
    half, g = CMP_BLOCK // 2, N_KV_GROUPS
    eye = jnp.eye(g, dtype=BF16)
    w1h = w1.astype(BF16).reshape(2, half, HEAD_DIM, PHI_HIDDEN)
    w1e = jnp.einsum("xldh,gk->xlgdkh", w1h, eye).reshape(2, half * KV_W, g * PHI_HIDDEN)
    w2e = jnp.einsum("hd,gk->ghkd", w2.astype(BF16), eye).reshape(g * PHI_HIDDEN, KV_W)
    pee = jnp.broadcast_to(pe.reshape(2, half, 1, HEAD_DIM), (2, half, g, HEAD_DIM)).reshape(2, half * KV_W)
    return pee, w1e, w2e


def kernel(x, norm_mix, w_in, conv_w, conv_b, gate_a_w, gate_a_b, gate_x_w, gate_x_b, lru_lambda, phi_k_pe, phi_k_w1, phi_k_w2, phi_v_pe, phi_v_w1, phi_v_w2, q_norm, kc_norm, ks_norm, kw_norm, rel_bias, proj_a, proj_b, w_out, norm_mlp, w_mlp_in, w_mlp_out):
    bsz, t, d = x.shape
    assert d == D_MODEL and norm_mix.shape[0] == 1
    tq = 256
    assert t % tq == 0 and t // SEL_BLOCK <= MAX_SBLK and t % CMP_STRIDE == 0
    n_tok = bsz * t
    n_chunks = t // CMP_STRIDE
    assert n_chunks % 8 == 0 and n_chunks <= LANES
    x2 = x.reshape(n_tok, d)

    rnn_t, proj_t, zk, zv = _norm_matmul(x2, norm_mix, _in_proj_weight(w_in[0]), bsz,
                                         tm=1024 if t % 1024 == 0 else tq, tn=1024, rnn_cols=RNN_COLS,
                                         side_cols=((COL_KC, KV_W), (COL_VC, KV_W)))

    pad_w = lambda w: jnp.pad(w, ((0, 0), (0, RNN_BW - LRU_BLOCK_W), (0, RNN_BW - LRU_BLOCK_W))).astype(BF16)
    ya_t = _rglru(rnn_t.reshape(t, bsz, RNN_COLS), _pad_blocks(conv_w[0], 1), _pad_blocks(conv_b, 1),
                  pad_w(gate_a_w[0]), _pad_blocks(gate_a_b.reshape(1, D_RNN), 1),
                  pad_w(gate_x_w[0]), _pad_blocks(gate_x_b.reshape(1, D_RNN), 1),
                  _pad_blocks(lru_lambda, 1), tc=128)

    zk = zk.reshape(bsz, n_chunks, CMP_STRIDE * KV_W)
    zv = zv.reshape(bsz, n_chunks, CMP_STRIDE * KV_W)
    pe_k, w1_k, w2_k = _phi_weights(phi_k_pe[0], phi_k_w1[0], phi_k_w2[0])
    pe_v, w1_v, w2_v = _phi_weights(phi_v_pe[0], phi_v_w1[0], phi_v_w2[0])
    seg256 = jnp.asarray(np.kron(np.eye(KV_W // HEAD_DIM), np.ones((HEAD_DIM, HEAD_DIM))), BF16)
    kcn = jnp.tile(kc_norm, (1, N_KV_GROUPS))
    kc = _compress(zk, pe_k, w1_k, w2_k, (kcn, seg256))
    vc = _compress(zv, pe_v, w1_v, w2_v)

    seg128 = jnp.asarray(np.kron(np.eye(LANES // HEAD_DIM), np.ones((HEAD_DIM, HEAD_DIM))), BF16)
    qh, k_all, v_all = _nsa_prep(proj_t, bsz, jnp.tile(q_norm, (1, N_HEADS)), jnp.tile(ks_norm, (1, N_KV_GROUPS)),
                                 jnp.tile(kw_norm, (1, N_KV_GROUPS)), seg128, tt=tq)

    bucket_of = _t5_bucket_table()
    far = rel_bias[REL_BUCKETS - 1][:, None, None]

    def bias_table(dist, valid, shift):
        buckets = jnp.asarray(bucket_of[np.clip(dist, 0, FAR_DIST - 1)].astype(np.int8))
        onehot = (buckets[None] == jnp.arange(REL_BUCKETS, dtype=jnp.int8).reshape(-1, 1, 1)).astype(F32)
        vals = jnp.einsum("kh,kji->hji", rel_bias, onehot, precision=lax.Precision.HIGHEST)
        if shift:
            vals = vals - far
        return jnp.where(jnp.asarray(valid), vals * LOG2E, MASK_NEG).astype(F32)

    kj = np.arange(tq)[:, None]
    qi_ = np.arange(tq)[None, :]
    bias_diag = bias_table(qi_ - kj, qi_ >= kj, True)
    bias_near = bias_table(tq + qi_ - kj, np.ones((tq, tq), bool), True)
    bias_win = jnp.broadcast_to(jnp.asarray(np.where(kj > qi_, 0.0, MASK_NEG), F32), (N_HEADS, tq, tq))
    bias_tiles = jnp.stack([bias_diag, bias_near, bias_win], axis=1)
    assert WINDOW == 2 * tq
    cidx = np.arange(n_chunks)[:, None]
    dist_c = np.arange(t)[None, :] - (cidx * CMP_STRIDE + CMP_BLOCK - 1)
    bias_c = bias_table(dist_c, (dist_c >= 0) & (cidx < n_chunks - 1), False)
    cstart = np.arange(n_chunks) * CMP_STRIDE
    sj = np.arange(MAX_SBLK)
    cov = ((cstart[None, :] < (sj[:, None] + 1) * SEL_BLOCK) & (cstart[None, :] + CMP_BLOCK - 1 >= sj[:, None] * SEL_BLOCK)
           & (np.arange(n_chunks)[None, :] < n_chunks - 1))
    cover = np.zeros((LANES, n_chunks), np.float32)
    cover[SEL_ROW0:SEL_ROW0 + MAX_SBLK] = cov
    yb = _nsa_attention(qh, kc, vc, k_all, v_all, bias_c, bias_tiles, jnp.asarray(cover, BF16), proj_t, tq)

    pa = jnp.pad(proj_a[0].reshape(LRU_BLOCKS, LRU_BLOCK_W, D_MODEL), ((0, 0), (0, RNN_BW - LRU_BLOCK_W), (0, 0)))
    h = _merge_out(ya_t.reshape(t, bsz * RNN_W), yb.reshape(n_tok, Q_W), proj_t, x2,
                   pa.reshape(RNN_W, D_MODEL).astype(BF16), proj_b[0].astype(BF16), w_out[0].astype(BF16),
                   bsz, tm=512 if t % 512 == 0 else tq)

    out = _mlp(h, norm_mlp, w_mlp_in[0].astype(BF16), w_mlp_out[0].astype(BF16),
               tm=1024 if n_tok % 1024 == 0 else tq, tf=1024)
    return out.reshape(bsz, t, d)
```

```python
import functools
import math

import numpy as np
import jax
import jax.numpy as jnp
from jax import lax
from jax.experimental import pallas as pl
from jax.experimental.pallas import tpu as pltpu

F32 = jnp.float32
BF16 = jnp.bfloat16

D_MODEL = 1024
D_RNN = 1344
LRU_BLOCKS = 4
LRU_BLOCK_W = D_RNN // LRU_BLOCKS
CONV_W = 4
LRU_C = 8.0
N_HEADS = 16
HEAD_DIM = 64
N_KV_GROUPS = 4
HEADS_PER_GROUP = N_HEADS // N_KV_GROUPS
CMP_BLOCK = 32
CMP_STRIDE = 16
SEL_BLOCK = 64
N_SELECT = 16
WINDOW = 512
PHI_HIDDEN = 256
SEL_FORCED = 1e4
REL_BUCKETS = 32
REL_MAX_DIST = 128
D_FF = 4 * D_MODEL
NORM_EPS = 1e-6
Q_W = N_HEADS * HEAD_DIM
KV_W = N_KV_GROUPS * HEAD_DIM

LANES = 128
VMEM_LIMIT = 56 * 1024 * 1024

RNN_BW = 384
RNN_W = LRU_BLOCKS * RNN_BW
MASK_NEG = -1e30
SEL_NEG = -1e9
SEL_ROW0 = 64
MAX_SBLK = 32
FAR_DIST = 256
LOG2E = math.log2(math.e)
V_ROWS = HEAD_DIM + 16
BIAS_DIAG, BIAS_NEAR, BIAS_WIN = range(3)
JOBS_PER_TRIP = 2

COL_U = 0
COL_GATE = RNN_W
RNN_COLS = 2 * RNN_W
COL_Q = 0
COL_KS = COL_Q + Q_W
COL_VS = COL_KS + KV_W
COL_KW = COL_VS + KV_W
COL_VW = COL_KW + KV_W
COL_KC = COL_VW + KV_W
COL_VC = COL_KC + KV_W
COL_GN = COL_VC + KV_W
COL_GA = 3072
COL_GB = 4096
D_PROJ = 5120


def _dot(a, b):
    return jnp.dot(a, b, preferred_element_type=F32)


def _gelu_tanh(x):
    return 0.5 * x * (1.0 + jnp.tanh(math.sqrt(2.0 / math.pi) * (x + 0.044715 * (x * x * x))))


def _sigmoid(x):
    return 0.5 * jnp.tanh(0.5 * x) + 0.5


def _seg_sum(x, seg_ones):
    hi = x.astype(BF16)
    lo = (x - hi.astype(F32)).astype(BF16)
    return _dot(hi, seg_ones) + _dot(lo, seg_ones)


def _params(sem, flags=None):
    return pltpu.CompilerParams(dimension_semantics=sem, vmem_limit_bytes=VMEM_LIMIT, flags=flags)


def _norm_matmul_kernel(x_ref, g_ref, w_ref, rnn_ref, o_ref, *rest, tn, rnn_tiles, side_cols):
    side_refs, xn_ref = rest[:-1], rest[-1]
    j = pl.program_id(1)

    @pl.when(j == 0)
    def _():
        x = x_ref[...]
        y = x * lax.rsqrt(jnp.mean(x * x, axis=-1, keepdims=True) + NORM_EPS) * g_ref[...]
        xn_ref[...] = y.astype(BF16)

    res = _dot(xn_ref[...], w_ref[...]).astype(o_ref.dtype)

    @pl.when(j < rnn_tiles)
    def _():
        rnn_ref[...] = res

    @pl.when(j >= rnn_tiles)
    def _():
        o_ref[...] = res

    for (col, width), side_ref in zip(side_cols, side_refs):
        @pl.when(j == rnn_tiles + col // tn)
        def _(col=col, width=width, side_ref=side_ref):
            side_ref[...] = res[:, col % tn:col % tn + width]


def _norm_matmul(x, gain, w, bsz, tm, tn, rnn_cols, side_cols):
    m, k = x.shape
    n = w.shape[1] - rnn_cols
    t = m // bsz
    tiles = t // tm
    rnn_tiles, col_tiles = rnn_cols // tn, n // tn
    assert all(col // tn == (col + width - 1) // tn for col, width in side_cols)
    return pl.pallas_call(
        functools.partial(_norm_matmul_kernel, tn=tn, rnn_tiles=rnn_tiles, side_cols=side_cols),
        grid=(m // tm, rnn_tiles + col_tiles),
        in_specs=[
            pl.BlockSpec((tm, k), lambda i, j: (i, 0)),
            pl.BlockSpec((1, k), lambda i, j: (0, 0)),
            pl.BlockSpec((k, tn), lambda i, j: (0, j)),
        ],
        out_specs=[pl.BlockSpec((tm, tn), lambda i, j: (i % tiles, (i // tiles) * rnn_tiles + jnp.minimum(j, rnn_tiles - 1))),
                   pl.BlockSpec((tm, tn), lambda i, j: (i % tiles, (i // tiles) * col_tiles + jnp.maximum(j - rnn_tiles, 0)))]
                  + [pl.BlockSpec((tm, width), lambda i, j: (i, 0)) for _, width in side_cols],
        out_shape=[jax.ShapeDtypeStruct((t, bsz * rnn_cols), BF16), jax.ShapeDtypeStruct((t, bsz * n), BF16)]
                  + [jax.ShapeDtypeStruct((m, width), BF16) for _, width in side_cols],
        scratch_shapes=[pltpu.VMEM((tm, k), BF16)],
        compiler_params=_params(("parallel", "arbitrary")),
    )(x, gain, w)


def _rglru_kernel(u_ref, ug_ref, cw_ref, cb_ref, wa_ref, ba_ref, wx_ref, bx_ref, lam_ref,
                  y_ref, ubuf, a_scr, b_scr, hcar, *, tc, bsz):
    t = pl.program_id(1)
    rows = tc * bsz
    halo = (CONV_W - 1) * bsz

    @pl.when(t == 0)
    def _():
        ubuf[0:halo, :] = jnp.zeros((halo, RNN_BW), F32)
        hcar[...] = jnp.zeros_like(hcar)

    u = u_ref[...].astype(F32).reshape(rows, RNN_BW)
    ubuf[halo:halo + rows, :] = u
    xc = cb_ref[...]
    for k in range(CONV_W):
        xc = xc + cw_ref[k:k + 1, :] * ubuf[k * bsz:k * bsz + rows, :]
    ubuf[0:halo, :] = u[rows - halo:rows, :]

    xb = xc.astype(BF16)
    r = _sigmoid(_dot(xb, wa_ref[...]) + ba_ref[...])
    i = _sigmoid(_dot(xb, wx_ref[...]) + bx_ref[...])
    z = -lam_ref[...]
    softplus = jnp.maximum(z, 0.0) + jnp.log(1.0 + jnp.exp(-jnp.abs(z)))
    a = jnp.exp(r * ((-LRU_C) * softplus))
    mult = jnp.sqrt(1.0 - a * a)
    row = lax.broadcasted_iota(jnp.int32, (rows, 1), 0)
    mult = jnp.where((row < bsz) & (t == 0), 1.0, mult)
    a_scr[...] = a
    b_scr[...] = mult * (i * xc)

    def step(s, h):
        at = pl.ds(pl.multiple_of(s * bsz, bsz), bsz)
        h = a_scr[at, :] * h + b_scr[at, :]
        b_scr[at, :] = h
        return h

    hcar[...] = lax.fori_loop(0, tc, step, hcar[...], unroll=8)
    y = b_scr[...] * _gelu_tanh(ug_ref[...].astype(F32).reshape(rows, RNN_BW))
    y_ref[...] = y.reshape(tc, bsz, RNN_BW).astype(y_ref.dtype)


def _rglru(proj3, conv_w, conv_b, wa, ba, wx, bx, lam, tc):
    t, bsz, _ = proj3.shape
    assert bsz % 8 == 0, "the recurrence advances whole sublane groups of sequences"
    nb = LRU_BLOCKS
    rows = tc * bsz
    vec = pl.BlockSpec((1, RNN_BW), lambda n, s: (0, n))
    mat = pl.BlockSpec((None, RNN_BW, RNN_BW), lambda n, s: (n, 0, 0))
    return pl.pallas_call(
        functools.partial(_rglru_kernel, tc=tc, bsz=bsz),
        grid=(nb, t // tc),
        in_specs=[
            pl.BlockSpec((tc, bsz, RNN_BW), lambda n, s: (s, 0, COL_U // RNN_BW + n)),
            pl.BlockSpec((tc, bsz, RNN_BW), lambda n, s: (s, 0, COL_GATE // RNN_BW + n)),
            pl.BlockSpec((CONV_W, RNN_BW), lambda n, s: (0, n)),
            vec, mat, vec, mat, vec, vec,
        ],
        out_specs=pl.BlockSpec((tc, bsz, RNN_BW), lambda n, s: (s, 0, n)),
        out_shape=jax.ShapeDtypeStruct((t, bsz, RNN_W), BF16),
        scratch_shapes=[pltpu.VMEM((rows + (CONV_W - 1) * bsz, RNN_BW), F32),
                        pltpu.VMEM((rows, RNN_BW), F32), pltpu.VMEM((rows, RNN_BW), F32),
                        pltpu.VMEM((bsz, RNN_BW), F32)],
        compiler_params=_params(("parallel", "arbitrary")),
    )(proj3, proj3, conv_w, conv_b, wa, ba, wx, bx, lam)


def _compress_kernel(z_ref, pe_ref, w1_ref, w2_ref, *rest, n_chunks, is_key):
    norm_ref, seg_ref, o_ref = rest if is_key else (None, None) + rest
    z = z_ref[...].astype(F32)
    first = _dot((z + pe_ref[0:1, :]).astype(BF16), w1_ref[0])
    second = _dot((z + pe_ref[1:2, :]).astype(BF16), w1_ref[1])
    pre = first + pltpu.roll(second, n_chunks - 1, 0)
    out = _dot(_gelu_tanh(pre).astype(BF16), w2_ref[...])
    if is_key:
        ssq = _seg_sum(out * out, seg_ref[...])
        normed = out * lax.rsqrt(ssq * (1.0 / HEAD_DIM) + NORM_EPS) * norm_ref[...]
        low = lax.broadcasted_iota(jnp.int32, (n_chunks, LANES), 1) < HEAD_DIM
        for pair in range(N_KV_GROUPS // 2):
            blk = normed[:, pair * LANES:(pair + 1) * LANES]
            o_ref[2 * pair] = jnp.where(low, blk, 0.0).astype(o_ref.dtype)
            o_ref[2 * pair + 1] = jnp.where(low, pltpu.roll(blk, HEAD_DIM, 1), 0.0).astype(o_ref.dtype)
    else:
        out_t = jnp.transpose(out)
        for g in range(N_KV_GROUPS):
            o_ref[g] = out_t[g * HEAD_DIM:(g + 1) * HEAD_DIM, :].astype(o_ref.dtype)


def _compress(z, pe, w1, w2, norm_and_seg=None):
    bsz, n_chunks, width = z.shape
    hid = w1.shape[-1]
    is_key = norm_and_seg is not None
    out_block = (N_KV_GROUPS, n_chunks, LANES) if is_key else (N_KV_GROUPS, HEAD_DIM, n_chunks)
    extra = norm_and_seg if is_key else ()
    return pl.pallas_call(
        functools.partial(_compress_kernel, n_chunks=n_chunks, is_key=is_key),
        grid=(bsz,),
        in_specs=[
            pl.BlockSpec((None, n_chunks, width), lambda b: (b, 0, 0)),
            pl.BlockSpec((2, width), lambda b: (0, 0)),
            pl.BlockSpec((2, width, hid), lambda b: (0, 0, 0)),
            pl.BlockSpec((hid, KV_W), lambda b: (0, 0)),
        ] + [pl.BlockSpec(a.shape, lambda b: (0, 0)) for a in extra],
        out_specs=pl.BlockSpec((None,) + out_block, lambda b: (b, 0, 0, 0)),
        out_shape=jax.ShapeDtypeStruct((bsz,) + out_block, BF16),
        compiler_params=_params(("parallel",)),
    )(z, pe, w1, w2, *extra)


def _nsa_prep_kernel(q_ref, ks_ref, vs_ref, kw_ref, vw_ref, qn_ref, ksn_ref, kwn_ref, seg_ref,
                     qo_ref, ko_ref, vo_ref, *, tt):
    t0 = pl.program_id(1) * tt
    lane = lax.broadcasted_iota(jnp.int32, (tt, LANES), 1)
    row = lax.broadcasted_iota(jnp.int32, (tt, LANES), 0) + t0
    low = lane < HEAD_DIM
    seg = seg_ref[...]
    onehot = jnp.where((lane - SEL_ROW0) == row // SEL_BLOCK, 1.0, 0.0)

    def normed(ref, gain_ref, blk, scale):
        x = ref[:, blk * LANES:(blk + 1) * LANES].astype(F32)
        ssq = _seg_sum(x * x, seg)
        y = x * lax.rsqrt(ssq * (1.0 / HEAD_DIM) + NORM_EPS) * gain_ref[:, blk * LANES:(blk + 1) * LANES]
        return y * scale if scale != 1.0 else y

    q_pad = jnp.zeros((LANES - HEAD_DIM, tt), BF16)
    for blk in range(Q_W // LANES):
        y_t = jnp.transpose(normed(q_ref, qn_ref, blk, HEAD_DIM ** -0.5 * LOG2E)).astype(BF16)
        for half in range(2):
            qo_ref[2 * blk + half, 0:HEAD_DIM, :] = y_t[half * HEAD_DIM:(half + 1) * HEAD_DIM, :]
            qo_ref[2 * blk + half, HEAD_DIM:LANES, :] = q_pad
    ones_row = jnp.where(lax.broadcasted_iota(jnp.int32, (V_ROWS - HEAD_DIM, tt), 0) == 0, 1.0, 0.0).astype(BF16)
    for blk in range(KV_W // LANES):
        y = normed(ks_ref, ksn_ref, blk, 1.0)
        ysw = pltpu.roll(y, HEAD_DIM, 1)
        ko_ref[2 * blk, 0] = jnp.where(low, y, onehot).astype(BF16)
        ko_ref[2 * blk + 1, 0] = jnp.where(low, ysw, onehot).astype(BF16)
        y = normed(kw_ref, kwn_ref, blk, 1.0)
        ysw = pltpu.roll(y, HEAD_DIM, 1)
        ko_ref[2 * blk, 1] = jnp.where(low, y, 0.0).astype(BF16)
        ko_ref[2 * blk + 1, 1] = jnp.where(low, ysw, 0.0).astype(BF16)
        for branch, src in enumerate((vs_ref, vw_ref)):
            v_t = jnp.transpose(src[:, blk * LANES:(blk + 1) * LANES].astype(F32))
            for half in range(2):
                g = 2 * blk + half
                vo_ref[g, branch, 0:HEAD_DIM, :] = v_t[half * HEAD_DIM:(half + 1) * HEAD_DIM, :].astype(BF16)
                vo_ref[g, branch, HEAD_DIM:V_ROWS, :] = ones_row


def _nsa_prep(proj_t, bsz, qn, ksn, kwn, seg, tt):
    t = proj_t.shape[0]

    def col(width, offset):
        return pl.BlockSpec((tt, width), lambda b, s: (s, (b * D_PROJ + offset) // width))

    def vec(width):
        return pl.BlockSpec((1, width), lambda b, s: (0, 0))

    g = N_KV_GROUPS
    return pl.pallas_call(
        functools.partial(_nsa_prep_kernel, tt=tt),
        grid=(bsz, t // tt),
        in_specs=[col(Q_W, COL_Q), col(KV_W, COL_KS), col(KV_W, COL_VS), col(KV_W, COL_KW), col(KV_W, COL_VW),
                  vec(Q_W), vec(KV_W), vec(KV_W), pl.BlockSpec((LANES, LANES), lambda b, s: (0, 0))],
        out_specs=[pl.BlockSpec((None, N_HEADS, LANES, tt), lambda b, s: (b, 0, 0, s)),
                   pl.BlockSpec((None, g, 2, tt, LANES), lambda b, s: (b, 0, 0, s, 0)),
                   pl.BlockSpec((None, g, 2, V_ROWS, tt), lambda b, s: (b, 0, 0, 0, s))],
        out_shape=[jax.ShapeDtypeStruct((bsz, N_HEADS, LANES, t), BF16),
                   jax.ShapeDtypeStruct((bsz, g, 2, t, LANES), BF16),
                   jax.ShapeDtypeStruct((bsz, g, 2, V_ROWS, t), BF16)],
        compiler_params=_params(("parallel", "parallel")),
    )(proj_t, proj_t, proj_t, proj_t, proj_t, qn, ksn, kwn, seg)


def _nsa_kernel(q_ref, kc_ref, vc_ref, k_ref, v_ref, bc_ref, bias_ref, cover_ref, gate_ref, o_ref,
                qq_ref, alpha_ref, *scratch, tq, n_q, n_sblk):
    qi = pl.program_id(2)
    t0 = qi * tq
    rg = HEADS_PER_GROUP
    jpt = JOBS_PER_TRIP
    s_refs, smax_refs, p_refs = ((scratch[k:k + jpt], scratch[k + jpt:k + 2 * jpt]) for k in (0, 2 * jpt, 4 * jpt))
    m_refs, acc_refs = scratch[6 * jpt:6 * jpt + rg], scratch[6 * jpt + rg:]
    sel, win = 0, 1


    bias = jnp.concatenate([bc_ref[r] for r in range(rg)], axis=1)
    s = _dot(kc_ref[...], jnp.concatenate([q_ref[r] for r in range(rg)], axis=1)) + bias
    visible = bias > 0.5 * MASK_NEG
    p = jnp.exp2(s - jnp.max(s, axis=0, keepdims=True))
    p = jnp.where(visible, p / jnp.sum(p, axis=0, keepdims=True), 0.0)
    o_cmp_all = _dot(vc_ref[...], p.astype(BF16))
    o_cmp = [o_cmp_all[:, r * tq:(r + 1) * tq] for r in range(rg)]
    p_sum = p[:, 0:tq]
    for r in range(1, rg):
        p_sum = p_sum + p[:, r * tq:(r + 1) * tq]
    p_hi = p_sum.astype(BF16)
    p_lo = (p_sum - p_hi.astype(F32)).astype(BF16)
    imp_t = _dot(cover_ref[...], p_hi) + _dot(cover_ref[...], p_lo)

    score = imp_t[SEL_ROW0:SEL_ROW0 + MAX_SBLK, :]
    jrow = lax.broadcasted_iota(jnp.int32, (MAX_SBLK, tq), 0)
    qblk = (lax.broadcasted_iota(jnp.int32, (MAX_SBLK, tq), 1) + t0) // SEL_BLOCK
    causal = jrow <= qblk
    forced = causal & ((jrow == 0) | (jrow >= qblk - 1))
    score = jnp.where(forced, SEL_FORCED, jnp.where(causal, score, -1.0))
    rank = jnp.zeros((MAX_SBLK, tq), F32)
    for j in range(n_sblk):
        other = score[j:j + 1, :]
        ahead = (other > score) | ((other == score) & (jrow > j))
        rank = rank + jnp.where(ahead, 1.0, 0.0)
    n_top = min(N_SELECT, n_sblk)
    selected = (rank < n_top) & (score >= 0.0)
    neg = jnp.where(selected, 0.0, SEL_NEG).astype(BF16)
    for r in range(rg):
        qq_ref[r, 0:SEL_ROW0, :] = q_ref[r, 0:SEL_ROW0, :]
        qq_ref[r, SEL_ROW0:SEL_ROW0 + MAX_SBLK, :] = neg
        qq_ref[r, SEL_ROW0 + MAX_SBLK:LANES, :] = q_ref[r, SEL_ROW0 + MAX_SBLK:LANES, :]

    near_max = WINDOW // tq + 1
    plan = [(branch, a) for a in range(min(near_max, n_q)) for branch in (sel, win)]
    plan += [(sel, a) for a in range(near_max, n_q)]
    groups = [plan[g:g + JOBS_PER_TRIP] for g in range(0, len(plan), JOBS_PER_TRIP)]
    n_groups = sum((qi >= group[0][1]).astype(jnp.int32) for group in groups)

    def count(i):
        return sum((qi >= a).astype(jnp.int32) for _, a in groups[i])

    def count_options(i):
        return sorted({sum(1 for _, a in groups[i] if a <= top) for _, top in groups[i]})

    def jobs_of(i, c):
        return [(branch, pl.multiple_of((qi - a) * tq, tq),
                 (BIAS_DIAG, BIAS_NEAR, BIAS_WIN if branch == win else None)[min(a, 2)])
                for branch, a in groups[i][:c]]

    def scores(i, c):
        for (branch, k0, kind), s_ref, smax_ref in zip(jobs_of(i, c), s_refs[i % 2], smax_refs[i % 2]):
            k = k_ref[branch, pl.ds(k0, tq), :]
            for r in range(rg):
                q_t = qq_ref[r] if branch == sel else q_ref[r]
                s = _dot(k, q_t)
                if kind is not None:
                    s = s + bias_ref[r, kind]
                s_ref[r] = s
                smax_ref[r] = jnp.max(s, axis=0, keepdims=True)

    def softmax(i, c):
        for r in range(rg):
            for branch in sorted({b for b, _ in groups[i][:c]}):
                mine = [j for j, (b, _) in enumerate(groups[i][:c]) if b == branch]
                m_old = m_refs[r][branch]
                m_new = m_old
                for j in mine:
                    m_new = jnp.maximum(m_new, smax_refs[i % 2][j][r])
                for j in mine:
                    p_refs[i % 2][j][r] = jnp.exp2(s_refs[i % 2][j][r] - m_new).astype(BF16)
                m_refs[r][branch] = m_new
                alpha_ref[i % 2, r, branch] = jnp.exp2(m_old - m_new)

    def values(i, c):
        jobs = jobs_of(i, c)
        v_tiles = [v_ref[branch, :, pl.ds(k0, tq)] for branch, k0, _ in jobs]
        for r in range(rg):
            for branch in sorted({b for b, _, _ in jobs}):
                acc = alpha_ref[i % 2, r, branch] * acc_refs[r][branch]
                for j, (b, _, _) in enumerate(jobs):
                    if b == branch:
                        acc = acc + _dot(v_tiles[j], p_refs[i % 2][j][r])
                acc_refs[r][branch] = acc

    for r in range(rg):
        m_refs[r][...] = jnp.full(m_refs[r].shape, MASK_NEG, F32)
        acc_refs[r][...] = jnp.zeros(acc_refs[r].shape, F32)
    assert count_options(0) == [len(groups[0])]
    scores(0, len(groups[0]))
    for i in range(len(groups)):
        full = len(groups[i])
        if i + 1 < len(groups):
            for c_next in count_options(i + 1):
                @pl.when((i + 1 < n_groups) & (count(i + 1) == c_next))
                def _(i=i, full=full, c_next=c_next):
                    if i > 0:
                        values(i - 1, len(groups[i - 1]))
                    softmax(i, full)
                    scores(i + 1, c_next)

        for c in count_options(i):
            @pl.when((i + 1 == n_groups) & (count(i) == c))
            def _(i=i, c=c):
                if i > 0:
                    values(i - 1, len(groups[i - 1]))
                softmax(i, c)
                values(i, c)

    def finish(branch):
        return [acc_refs[r][branch, 0:HEAD_DIM, :] / acc_refs[r][branch, HEAD_DIM:HEAD_DIM + 1, :] for r in range(rg)]

    o_sel = finish(sel)
    o_win = finish(win)

    gates_t = jnp.transpose(_sigmoid(gate_ref[...].astype(F32)))
    outs = []
    for r in range(rg):
        outs.append(gates_t[3 * r:3 * r + 1, :] * o_cmp[r]
                    + gates_t[3 * r + 1:3 * r + 2, :] * o_sel[r]
                    + gates_t[3 * r + 2:3 * r + 3, :] * o_win[r])
    o_ref[...] = jnp.transpose(jnp.concatenate(outs, axis=0)).astype(o_ref.dtype)


def _nsa_attention(qh, kc, vc, k_all, v_all, bias_c, bias_tiles, cover, proj3, tq):
    bsz, _, _, t = qh.shape
    n_chunks = kc.shape[2]
    rg = HEADS_PER_GROUP
    n_kinds = bias_tiles.shape[1]
    return pl.pallas_call(
        functools.partial(_nsa_kernel, tq=tq, n_q=t // tq, n_sblk=t // SEL_BLOCK),
        grid=(bsz, N_KV_GROUPS, t // tq),
        in_specs=[
            pl.BlockSpec((None, rg, LANES, tq), lambda b, g, i: (b, g, 0, i)),
            pl.BlockSpec((None, None, n_chunks, LANES), lambda b, g, i: (b, g, 0, 0)),
            pl.BlockSpec((None, None, HEAD_DIM, n_chunks), lambda b, g, i: (b, g, 0, 0)),
            pl.BlockSpec((None, None, 2, t, LANES), lambda b, g, i: (b, g, 0, 0, 0)),
            pl.BlockSpec((None, None, 2, V_ROWS, t), lambda b, g, i: (b, g, 0, 0, 0)),
            pl.BlockSpec((rg, n_chunks, tq), lambda b, g, i: (g, 0, i)),
            pl.BlockSpec((rg, n_kinds, tq, tq), lambda b, g, i: (g, 0, 0, 0)),
            pl.BlockSpec((LANES, n_chunks), lambda b, g, i: (0, 0)),
            pl.BlockSpec((tq, LANES), lambda b, g, i: (i, (b * D_PROJ + COL_GN) // LANES + g)),
        ],
        out_specs=pl.BlockSpec((None, tq, rg * HEAD_DIM), lambda b, g, i: (b, i, g)),
        out_shape=jax.ShapeDtypeStruct((bsz, t, Q_W), BF16),
        scratch_shapes=[
            pltpu.VMEM((rg, LANES, tq), BF16),
            pltpu.VMEM((2, rg, 2, 1, tq), F32),
        ] + [pltpu.VMEM((rg, tq, tq), F32)] * (2 * JOBS_PER_TRIP)
          + [pltpu.VMEM((rg, 1, tq), F32)] * (2 * JOBS_PER_TRIP)
          + [pltpu.VMEM((rg, tq, tq), BF16)] * (2 * JOBS_PER_TRIP)
          + [pltpu.VMEM((2, 1, tq), F32)] * rg
          + [pltpu.VMEM((2, V_ROWS, tq), F32)] * rg,
        compiler_params=_params(("parallel", "parallel", "arbitrary")),
    )(qh, kc, vc, k_all, v_all, bias_c, bias_tiles, cover, proj3)


def _merge_kernel(ya_ref, yb_ref, ga_ref, gb_ref, x_ref, pa_ref, pb_ref, wo_ref, h_ref):
    merged = (_sigmoid(ga_ref[...].astype(F32)) * _dot(ya_ref[...], pa_ref[...])
              + _sigmoid(gb_ref[...].astype(F32)) * _dot(yb_ref[...], pb_ref[...]))
    h_ref[...] = x_ref[...] + _dot(merged.astype(BF16), wo_ref[...])


def _merge_out(ya_t, yb, proj_t, x, pa, pb, wo, bsz, tm):
    t = ya_t.shape[0]
    tiles = t // tm

    def rows(width):
        return pl.BlockSpec((tm, width), lambda b, s: (b * tiles + s, 0))

    def time_major(width, offset):
        return pl.BlockSpec((tm, width), lambda b, s: (s, (b * D_PROJ + offset) // width))

    def whole(a):
        return pl.BlockSpec(a.shape, lambda b, s: (0, 0))

    return pl.pallas_call(
        _merge_kernel,
        grid=(bsz, tiles),
        in_specs=[pl.BlockSpec((tm, RNN_W), lambda b, s: (s, b)), rows(Q_W), time_major(D_MODEL, COL_GA),
                  time_major(D_MODEL, COL_GB), rows(D_MODEL), whole(pa), whole(pb), whole(wo)],
        out_specs=rows(D_MODEL),
        out_shape=jax.ShapeDtypeStruct((bsz * t, D_MODEL), F32),
        compiler_params=_params(("parallel", "parallel")),
    )(ya_t, yb, proj_t, proj_t, x, pa, pb, wo)


def _mlp_kernel(h_ref, g_ref, w1_ref, w2_ref, o_ref, hn_ref, acc_ref):
    j = pl.program_id(1)

    @pl.when(j == 0)
    def _():
        h = h_ref[...]
        y = h * lax.rsqrt(jnp.mean(h * h, axis=-1, keepdims=True) + NORM_EPS) * g_ref[...]
        hn_ref[...] = y.astype(BF16)
        acc_ref[...] = h

    z = jnp.maximum(_dot(hn_ref[...], w1_ref[...]), 0.0)
    acc_ref[...] += _dot((z * z).astype(BF16), w2_ref[...])

    @pl.when(j == pl.num_programs(1) - 1)
    def _():
        o_ref[...] = acc_ref[...]


def _mlp(h, gain, w1, w2, tm, tf):
    m, d = h.shape
    ff = w1.shape[1]
    return pl.pallas_call(
        _mlp_kernel,
        grid=(m // tm, ff // tf),
        in_specs=[
            pl.BlockSpec((tm, d), lambda i, j: (i, 0)),
            pl.BlockSpec((1, d), lambda i, j: (0, 0)),
            pl.BlockSpec((d, tf), lambda i, j: (0, j)),
            pl.BlockSpec((tf, d), lambda i, j: (j, 0)),
        ],
        out_specs=pl.BlockSpec((tm, d), lambda i, j: (i, 0)),
        out_shape=jax.ShapeDtypeStruct((m, d), F32),
        scratch_shapes=[pltpu.VMEM((tm, d), BF16), pltpu.VMEM((tm, d), F32)],
        compiler_params=_params(("parallel", "arbitrary")),
    )(h, gain, w1, w2)


def _t5_bucket_table():
    max_exact = REL_BUCKETS // 2
    d = np.arange(FAR_DIST)
    df = np.maximum(d.astype(np.float32), np.float32(1.0))
    large = max_exact + (np.log(df / np.float32(max_exact)) / np.float32(math.log(REL_MAX_DIST / max_exact))
                         * np.float32(REL_BUCKETS - max_exact)).astype(np.int32)
    large = np.minimum(large, REL_BUCKETS - 1)
    return np.where(d < max_exact, d, large).astype(np.int32)


def _pad_blocks(w, axis):
    shape = w.shape
    w = w.reshape(shape[:axis] + (LRU_BLOCKS, LRU_BLOCK_W) + shape[axis + 1:])
    pad = [(0, 0)] * w.ndim
    pad[axis + 1] = (0, RNN_BW - LRU_BLOCK_W)
    w = jnp.pad(w, pad)
    return w.reshape(shape[:axis] + (RNN_W,) + shape[axis + 1:])


def _in_proj_weight(w_in):
    cuts = np.cumsum((D_RNN, D_RNN, Q_W, KV_W, KV_W, KV_W, KV_W, KV_W, KV_W, 3 * N_HEADS, D_MODEL))
    w_in = w_in.astype(BF16)
    (w_u, w_gate, w_q, w_kc, w_vc, w_ks, w_vs, w_kw, w_vw, w_gn, w_ga, w_gb) = jnp.split(w_in, cuts, axis=1)
    per_group = 3 * HEADS_PER_GROUP
    w_gn = jnp.pad(w_gn.reshape(D_MODEL, N_KV_GROUPS, per_group), ((0, 0), (0, 0), (0, LANES - per_group)))
    w_gn = w_gn.reshape(D_MODEL, N_KV_GROUPS * LANES)
    gap = jnp.zeros((D_MODEL, COL_GA - COL_GN - N_KV_GROUPS * LANES), w_in.dtype)
    w = jnp.concatenate([_pad_blocks(w_u, 1), _pad_blocks(w_gate, 1), w_q, w_ks, w_vs, w_kw, w_vw, w_kc, w_vc,
                         w_gn, gap, w_ga, w_gb], axis=1)
    assert w.shape[1] == RNN_COLS + D_PROJ
    return w


def _phi_weights(pe, w1, w2):
    half, g = CMP_BLOCK // 2, N_KV_GROUPS
    eye = jnp.eye(g, dtype=BF16)
    w1h = w1.astype(BF16).reshape(2, half, HEAD_DIM, PHI_HIDDEN)
    w1e = jnp.einsum("xldh,gk->xlgdkh", w1h, eye).reshape(2, half * KV_W, g * PHI_HIDDEN)
    w2e = jnp.einsum("hd,gk->ghkd", w2.astype(BF16), eye).reshape(g * PHI_HIDDEN, KV_W)
    pee = jnp.broadcast_to(pe.reshape(2, half, 1, HEAD_DIM), (2, half, g, HEAD_DIM)).reshape(2, half * KV_W)
    return pee, w1e, w2e


def kernel(x, norm_mix, w_in, conv_w, conv_b, gate_a_w, gate_a_b, gate_x_w, gate_x_b, lru_lambda, phi_k_pe, phi_k_w1, phi_k_w2, phi_v_pe, phi_v_w1, phi_v_w2, q_norm, kc_norm, ks_norm, kw_norm, rel_bias, proj_a, proj_b, w_out, norm_mlp, w_mlp_in, w_mlp_out):
    bsz, t, d = x.shape
    assert d == D_MODEL and norm_mix.shape[0] == 1
    tq = 256
    assert t % tq == 0 and t // SEL_BLOCK <= MAX_SBLK and t % CMP_STRIDE == 0
    n_tok = bsz * t
    n_chunks = t // CMP_STRIDE
    assert n_chunks % 8 == 0 and n_chunks <= LANES
    x2 = x.reshape(n_tok, d)

    rnn_t, proj_t, zk, zv = _norm_matmul(x2, norm_mix, _in_proj_weight(w_in[0]), bsz,
                                         tm=1024 if t % 1024 == 0 else tq, tn=1024, rnn_cols=RNN_COLS,
                                         side_cols=((COL_KC, KV_W), (COL_VC, KV_W)))

    pad_w = lambda w: jnp.pad(w, ((0, 0), (0, RNN_BW - LRU_BLOCK_W), (0, RNN_BW - LRU_BLOCK_W))).astype(BF16)
    ya_t = _rglru(rnn_t.reshape(t, bsz, RNN_COLS), _pad_blocks(conv_w[0], 1), _pad_blocks(conv_b, 1),
                  pad_w(gate_a_w[0]), _pad_blocks(gate_a_b.reshape(1, D_RNN), 1),
                  pad_w(gate_x_w[0]), _pad_blocks(gate_x_b.reshape(1, D_RNN), 1),
                  _pad_blocks(lru_lambda, 1), tc=128)

    zk = zk.reshape(bsz, n_chunks, CMP_STRIDE * KV_W)
    zv = zv.reshape(bsz, n_chunks, CMP_STRIDE * KV_W)
    pe_k, w1_k, w2_k = _phi_weights(phi_k_pe[0], phi_k_w1[0], phi_k_w2[0])
    pe_v, w1_v, w2_v = _phi_weights(phi_v_pe[0], phi_v_w1[0], phi_v_w2[0])
    seg256 = jnp.asarray(np.kron(np.eye(KV_W // HEAD_DIM), np.ones((HEAD_DIM, HEAD_DIM))), BF16)
    kcn = jnp.tile(kc_norm, (1, N_KV_GROUPS))
    kc = _compress(zk, pe_k, w1_k, w2_k, (kcn, seg256))
    vc = _compress(zv, pe_v, w1_v, w2_v)

    seg128 = jnp.asarray(np.kron(np.eye(LANES // HEAD_DIM), np.ones((HEAD_DIM, HEAD_DIM))), BF16)
    qh, k_all, v_all = _nsa_prep(proj_t, bsz, jnp.tile(q_norm, (1, N_HEADS)), jnp.tile(ks_norm, (1, N_KV_GROUPS)),
                                 jnp.tile(kw_norm, (1, N_KV_GROUPS)), seg128, tt=tq)

    bucket_of = _t5_bucket_table()
    far = rel_bias[REL_BUCKETS - 1][:, None, None]

    def bias_table(dist, valid, shift):
        buckets = jnp.asarray(bucket_of[np.clip(dist, 0, FAR_DIST - 1)].astype(np.int8))
        onehot = (buckets[None] == jnp.arange(REL_BUCKETS, dtype=jnp.int8).reshape(-1, 1, 1)).astype(F32)
        vals = jnp.einsum("kh,kji->hji", rel_bias, onehot, precision=lax.Precision.HIGHEST)
        if shift:
            vals = vals - far
        return jnp.where(jnp.asarray(valid), vals * LOG2E, MASK_NEG).astype(F32)

    kj = np.arange(tq)[:, None]
    qi_ = np.arange(tq)[None, :]
    bias_diag = bias_table(qi_ - kj, qi_ >= kj, True)
    bias_near = bias_table(tq + qi_ - kj, np.ones((tq, tq), bool), True)
    bias_win = jnp.broadcast_to(jnp.asarray(np.where(kj > qi_, 0.0, MASK_NEG), F32), (N_HEADS, tq, tq))
    bias_tiles = jnp.stack([bias_diag, bias_near, bias_win], axis=1)
    assert WINDOW == 2 * tq
    cidx = np.arange(n_chunks)[:, None]
    dist_c = np.arange(t)[None, :] - (cidx * CMP_STRIDE + CMP_BLOCK - 1)
    bias_c = bias_table(dist_c, (dist_c >= 0) & (cidx < n_chunks - 1), False)
    cstart = np.arange(n_chunks) * CMP_STRIDE
    sj = np.arange(MAX_SBLK)
    cov = ((cstart[None, :] < (sj[:, None] + 1) * SEL_BLOCK) & (cstart[None, :] + CMP_BLOCK - 1 >= sj[:, None] * SEL_BLOCK)
           & (np.arange(n_chunks)[None, :] < n_chunks - 1))
    cover = np.zeros((LANES, n_chunks), np.float32)
    cover[SEL_ROW0:SEL_ROW0 + MAX_SBLK] = cov
    yb = _nsa_attention(qh, kc, vc, k_all, v_all, bias_c, bias_tiles, jnp.asarray(cover, BF16), proj_t, tq)

    pa = jnp.pad(proj_a[0].reshape(LRU_BLOCKS, LRU_BLOCK_W, D_MODEL), ((0, 0), (0, RNN_BW - LRU_BLOCK_W), (0, 0)))
    h = _merge_out(ya_t.reshape(t, bsz * RNN_W), yb.reshape(n_tok, Q_W), proj_t, x2,
                   pa.reshape(RNN_W, D_MODEL).astype(BF16), proj_b[0].astype(BF16), w_out[0].astype(BF16),
                   bsz, tm=512 if t % 512 == 0 else tq)

    out = _mlp(h, norm_mlp, w_mlp_in[0].astype(BF16), w_mlp_out[0].astype(BF16),
               tm=1024 if n_tok % 1024 == 0 else tq, tf=1024)
    return out.reshape(bsz, t, d)
```

```python
import functools
import math

import numpy as np
import jax
import jax.numpy as jnp
from jax import lax
from jax.experimental import pallas as pl
from jax.experimental.pallas import tpu as pltpu

F32 = jnp.float32
BF16 = jnp.bfloat16

D_MODEL = 1024
D_RNN = 1344
LRU_BLOCKS = 4
LRU_BLOCK_W = D_RNN // LRU_BLOCKS
CONV_W = 4
LRU_C = 8.0
N_HEADS = 16
HEAD_DIM = 64
N_KV_GROUPS = 4
HEADS_PER_GROUP = N_HEADS // N_KV_GROUPS
CMP_BLOCK = 32
CMP_STRIDE = 16
SEL_BLOCK = 64
N_SELECT = 16
WINDOW = 512
PHI_HIDDEN = 256
SEL_FORCED = 1e4
REL_BUCKETS = 32
REL_MAX_DIST = 128
D_FF = 4 * D_MODEL
NORM_EPS = 1e-6
Q_W = N_HEADS * HEAD_DIM
KV_W = N_KV_GROUPS * HEAD_DIM

LANES = 128
VMEM_LIMIT = 56 * 1024 * 1024

RNN_BW = 384
RNN_W = LRU_BLOCKS * RNN_BW
MASK_NEG = -1e30
SEL_NEG = -1e9
SEL_ROW0 = 64
MAX_SBLK = 32
FAR_DIST = 256
LOG2E = math.log2(math.e)
V_ROWS = HEAD_DIM + 16
BIAS_DIAG, BIAS_NEAR, BIAS_WIN = range(3)
JOBS_PER_TRIP = 2

COL_U = 0
COL_GATE = RNN_W
RNN_COLS = 2 * RNN_W
COL_Q = 0
COL_KS = COL_Q + Q_W
COL_VS = COL_KS + KV_W
COL_KW = COL_VS + KV_W
COL_VW = COL_KW + KV_W
COL_KC = COL_VW + KV_W
COL_VC = COL_KC + KV_W
COL_GN = COL_VC + KV_W
COL_GA = 3072
COL_GB = 4096
D_PROJ = 5120


def _dot(a, b):
    return jnp.dot(a, b, preferred_element_type=F32)


def _gelu_tanh(x):
    return 0.5 * x * (1.0 + jnp.tanh(math.sqrt(2.0 / math.pi) * (x + 0.044715 * (x * x * x))))


def _sigmoid(x):
    return 0.5 * jnp.tanh(0.5 * x) + 0.5


def _seg_sum(x, seg_ones):
    hi = x.astype(BF16)
    lo = (x - hi.astype(F32)).astype(BF16)
    return _dot(hi, seg_ones) + _dot(lo, seg_ones)


def _params(sem, flags=None):
    return pltpu.CompilerParams(dimension_semantics=sem, vmem_limit_bytes=VMEM_LIMIT, flags=flags)


def _norm_matmul_kernel(x_ref, g_ref, w_ref, rnn_ref, o_ref, *rest, tn, rnn_tiles, side_cols):
    side_refs, xn_ref = rest[:-1], rest[-1]
    j = pl.program_id(1)

    @pl.when(j == 0)
    def _():
        x = x_ref[...]
        y = x * lax.rsqrt(jnp.mean(x * x, axis=-1, keepdims=True) + NORM_EPS) * g_ref[...]
        xn_ref[...] = y.astype(BF16)

    res = _dot(xn_ref[...], w_ref[...]).astype(o_ref.dtype)

    @pl.when(j < rnn_tiles)
    def _():
        rnn_ref[...] = res

    @pl.when(j >= rnn_tiles)
    def _():
        o_ref[...] = res

    for (col, width), side_ref in zip(side_cols, side_refs):
        @pl.when(j == rnn_tiles + col // tn)
        def _(col=col, width=width, side_ref=side_ref):
            side_ref[...] = res[:, col % tn:col % tn + width]


def _norm_matmul(x, gain, w, bsz, tm, tn, rnn_cols, side_cols):
    m, k = x.shape
    n = w.shape[1] - rnn_cols
    t = m // bsz
    tiles = t // tm
    rnn_tiles, col_tiles = rnn_cols // tn, n // tn
    assert all(col // tn == (col + width - 1) // tn for col, width in side_cols)
    return pl.pallas_call(
        functools.partial(_norm_matmul_kernel, tn=tn, rnn_tiles=rnn_tiles, side_cols=side_cols),
        grid=(m // tm, rnn_tiles + col_tiles),
        in_specs=[
            pl.BlockSpec((tm, k), lambda i, j: (i, 0)),
            pl.BlockSpec((1, k), lambda i, j: (0, 0)),
            pl.BlockSpec((k, tn), lambda i, j: (0, j)),
        ],
        out_specs=[pl.BlockSpec((tm, tn), lambda i, j: (i % tiles, (i // tiles) * rnn_tiles + jnp.minimum(j, rnn_tiles - 1))),
                   pl.BlockSpec((tm, tn), lambda i, j: (i % tiles, (i // tiles) * col_tiles + jnp.maximum(j - rnn_tiles, 0)))]
                  + [pl.BlockSpec((tm, width), lambda i, j: (i, 0)) for _, width in side_cols],
        out_shape=[jax.ShapeDtypeStruct((t, bsz * rnn_cols), BF16), jax.ShapeDtypeStruct((t, bsz * n), BF16)]
                  + [jax.ShapeDtypeStruct((m, width), BF16) for _, width in side_cols],
        scratch_shapes=[pltpu.VMEM((tm, k), BF16)],
        compiler_params=_params(("parallel", "arbitrary")),
    )(x, gain, w)


def _rglru_kernel(u_ref, ug_ref, cw_ref, cb_ref, wa_ref, ba_ref, wx_ref, bx_ref, lam_ref,
                  y_ref, ubuf, a_scr, b_scr, hcar, *, tc, bsz):
    t = pl.program_id(1)
    rows = tc * bsz
    halo = (CONV_W - 1) * bsz

    @pl.when(t == 0)
    def _():
        ubuf[0:halo, :] = jnp.zeros((halo, RNN_BW), F32)
        hcar[...] = jnp.zeros_like(hcar)

    u = u_ref[...].astype(F32).reshape(rows, RNN_BW)
    ubuf[halo:halo + rows, :] = u
    xc = cb_ref[...]
    for k in range(CONV_W):
        xc = xc + cw_ref[k:k + 1, :] * ubuf[k * bsz:k * bsz + rows, :]
    ubuf[0:halo, :] = u[rows - halo:rows, :]

    xb = xc.astype(BF16)
    r = _sigmoid(_dot(xb, wa_ref[...]) + ba_ref[...])
    i = _sigmoid(_dot(xb, wx_ref[...]) + bx_ref[...])
    z = -lam_ref[...]
    softplus = jnp.maximum(z, 0.0) + jnp.log(1.0 + jnp.exp(-jnp.abs(z)))
    a = jnp.exp(r * ((-LRU_C) * softplus))
    mult = jnp.sqrt(1.0 - a * a)
    row = lax.broadcasted_iota(jnp.int32, (rows, 1), 0)
    mult = jnp.where((row < bsz) & (t == 0), 1.0, mult)
    a_scr[...] = a
    b_scr[...] = mult * (i * xc)

    def step(s, h):
        at = pl.ds(pl.multiple_of(s * bsz, bsz), bsz)
        h = a_scr[at, :] * h + b_scr[at, :]
        b_scr[at, :] = h
        return h

    hcar[...] = lax.fori_loop(0, tc, step, hcar[...], unroll=8)
    y = b_scr[...] * _gelu_tanh(ug_ref[...].astype(F32).reshape(rows, RNN_BW))
    y_ref[...] = y.reshape(tc, bsz, RNN_BW).astype(y_ref.dtype)


def _rglru(proj3, conv_w, conv_b, wa, ba, wx, bx, lam, tc):
    t, bsz, _ = proj3.shape
    assert bsz % 8 == 0, "the recurrence advances whole sublane groups of sequences"
    nb = LRU_BLOCKS
    rows = tc * bsz
    vec = pl.BlockSpec((1, RNN_BW), lambda n, s: (0, n))
    mat = pl.BlockSpec((None, RNN_BW, RNN_BW), lambda n, s: (n, 0, 0))
    return pl.pallas_call(
        functools.partial(_rglru_kernel, tc=tc, bsz=bsz),
        grid=(nb, t // tc),
        in_specs=[
            pl.BlockSpec((tc, bsz, RNN_BW), lambda n, s: (s, 0, COL_U // RNN_BW + n)),
            pl.BlockSpec((tc, bsz, RNN_BW), lambda n, s: (s, 0, COL_GATE // RNN_BW + n)),
            pl.BlockSpec((CONV_W, RNN_BW), lambda n, s: (0, n)),
            vec, mat, vec, mat, vec, vec,
        ],
        out_specs=pl.BlockSpec((tc, bsz, RNN_BW), lambda n, s: (s, 0, n)),
        out_shape=jax.ShapeDtypeStruct((t, bsz, RNN_W), BF16),
        scratch_shapes=[pltpu.VMEM((rows + (CONV_W - 1) * bsz, RNN_BW), F32),
                        pltpu.VMEM((rows, RNN_BW), F32), pltpu.VMEM((rows, RNN_BW), F32),
                        pltpu.VMEM((bsz, RNN_BW), F32)],
        compiler_params=_params(("parallel", "arbitrary")),
    )(proj3, proj3, conv_w, conv_b, wa, ba, wx, bx, lam)


def _compress_kernel(*refs, n_chunks, is_key):
    z_refs, (pe_ref, w1_ref, w2_ref), rest = refs[:N_KV_GROUPS], refs[N_KV_GROUPS:N_KV_GROUPS + 3], refs[N_KV_GROUPS + 3:]
    norm_ref, o_ref = rest if is_key else (None,) + rest
    for g, z_ref in enumerate(z_refs):
        z = z_ref[...].astype(F32)
        first = _dot((z + pe_ref[0:1, :]).astype(BF16), w1_ref[0])
        second = _dot((z + pe_ref[1:2, :]).astype(BF16), w1_ref[1])
        pre = first + pltpu.roll(second, n_chunks - 1, 0)
        out = _dot(_gelu_tanh(pre).astype(BF16), w2_ref[...])
        if is_key:
            ssq = jnp.sum(out * out, axis=-1, keepdims=True)
            o_ref[g] = (out * lax.rsqrt(ssq * (1.0 / HEAD_DIM) + NORM_EPS) * norm_ref[...]).astype(o_ref.dtype)
        else:
            o_ref[g] = jnp.transpose(out)[0:HEAD_DIM, :].astype(o_ref.dtype)


def _compress(z_groups, pe, w1, w2, norm=None):
    bsz, n_chunks, width = z_groups[0].shape
    hid = w1.shape[-1]
    is_key = norm is not None
    out_block = (N_KV_GROUPS, n_chunks, LANES) if is_key else (N_KV_GROUPS, HEAD_DIM, n_chunks)
    extra = (norm,) if is_key else ()
    return pl.pallas_call(
        functools.partial(_compress_kernel, n_chunks=n_chunks, is_key=is_key),
        grid=(bsz,),
        in_specs=[pl.BlockSpec((None, n_chunks, width), lambda b: (b, 0, 0))] * N_KV_GROUPS + [
            pl.BlockSpec((2, width), lambda b: (0, 0)),
            pl.BlockSpec((2, width, hid), lambda b: (0, 0, 0)),
            pl.BlockSpec((hid, LANES), lambda b: (0, 0)),
        ] + [pl.BlockSpec(a.shape, lambda b: (0, 0)) for a in extra],
        out_specs=pl.BlockSpec((None,) + out_block, lambda b: (b, 0, 0, 0)),
        out_shape=jax.ShapeDtypeStruct((bsz,) + out_block, BF16),
        compiler_params=_params(("parallel",)),
    )(*z_groups, pe, w1, w2, *extra)


def _nsa_prep_kernel(q_ref, ks_ref, vs_ref, kw_ref, vw_ref, qn_ref, ksn_ref, kwn_ref, seg_ref,
                     qo_ref, ko_ref, vo_ref, *, tt):
    t0 = pl.program_id(1) * tt
    lane = lax.broadcasted_iota(jnp.int32, (tt, LANES), 1)
    row = lax.broadcasted_iota(jnp.int32, (tt, LANES), 0) + t0
    low = lane < HEAD_DIM
    seg = seg_ref[...]
    onehot = jnp.where((lane - SEL_ROW0) == row // SEL_BLOCK, 1.0, 0.0)

    def normed(ref, gain_ref, blk, scale):
        x = ref[:, blk * LANES:(blk + 1) * LANES].astype(F32)
        ssq = _seg_sum(x * x, seg)
        y = x * lax.rsqrt(ssq * (1.0 / HEAD_DIM) + NORM_EPS) * gain_ref[:, blk * LANES:(blk + 1) * LANES]
        return y * scale if scale != 1.0 else y

    q_pad = jnp.zeros((LANES - HEAD_DIM, tt), BF16)
    for blk in range(Q_W // LANES):
        y_t = jnp.transpose(normed(q_ref, qn_ref, blk, HEAD_DIM ** -0.5 * LOG2E)).astype(BF16)
        for half in range(2):
            qo_ref[2 * blk + half, 0:HEAD_DIM, :] = y_t[half * HEAD_DIM:(half + 1) * HEAD_DIM, :]
            qo_ref[2 * blk + half, HEAD_DIM:LANES, :] = q_pad
    ones_row = jnp.where(lax.broadcasted_iota(jnp.int32, (V_ROWS - HEAD_DIM, tt), 0) == 0, 1.0, 0.0).astype(BF16)
    for blk in range(KV_W // LANES):
        y = normed(ks_ref, ksn_ref, blk, 1.0)
        ysw = pltpu.roll(y, HEAD_DIM, 1)
        ko_ref[2 * blk, 0] = jnp.where(low, y, onehot).astype(BF16)
        ko_ref[2 * blk + 1, 0] = jnp.where(low, ysw, onehot).astype(BF16)
        y = normed(kw_ref, kwn_ref, blk, 1.0)
        ysw = pltpu.roll(y, HEAD_DIM, 1)
        ko_ref[2 * blk, 1] = jnp.where(low, y, 0.0).astype(BF16)
        ko_ref[2 * blk + 1, 1] = jnp.where(low, ysw, 0.0).astype(BF16)
        for branch, src in enumerate((vs_ref, vw_ref)):
            v_t = jnp.transpose(src[:, blk * LANES:(blk + 1) * LANES].astype(F32))
            for half in range(2):
                g = 2 * blk + half
                vo_ref[g, branch, 0:HEAD_DIM, :] = v_t[half * HEAD_DIM:(half + 1) * HEAD_DIM, :].astype(BF16)
                vo_ref[g, branch, HEAD_DIM:V_ROWS, :] = ones_row


def _nsa_prep(proj_t, bsz, qn, ksn, kwn, seg, tt):
    t = proj_t.shape[0]

    def col(width, offset):
        return pl.BlockSpec((tt, width), lambda b, s: (s, (b * D_PROJ + offset) // width))

    def vec(width):
        return pl.BlockSpec((1, width), lambda b, s: (0, 0))

    g = N_KV_GROUPS
    return pl.pallas_call(
        functools.partial(_nsa_prep_kernel, tt=tt),
        grid=(bsz, t // tt),
        in_specs=[col(Q_W, COL_Q), col(KV_W, COL_KS), col(KV_W, COL_VS), col(KV_W, COL_KW), col(KV_W, COL_VW),
                  vec(Q_W), vec(KV_W), vec(KV_W), pl.BlockSpec((LANES, LANES), lambda b, s: (0, 0))],
        out_specs=[pl.BlockSpec((None, N_HEADS, LANES, tt), lambda b, s: (b, 0, 0, s)),
                   pl.BlockSpec((None, g, 2, tt, LANES), lambda b, s: (b, 0, 0, s, 0)),
                   pl.BlockSpec((None, g, 2, V_ROWS, tt), lambda b, s: (b, 0, 0, 0, s))],
        out_shape=[jax.ShapeDtypeStruct((bsz, N_HEADS, LANES, t), BF16),
                   jax.ShapeDtypeStruct((bsz, g, 2, t, LANES), BF16),
                   jax.ShapeDtypeStruct((bsz, g, 2, V_ROWS, t), BF16)],
        compiler_params=_params(("parallel", "parallel")),
    )(proj_t, proj_t, proj_t, proj_t, proj_t, qn, ksn, kwn, seg)


def _nsa_kernel(q_ref, kc_ref, vc_ref, k_ref, v_ref, bc_ref, bias_ref, cover_ref, gate_ref, o_ref,
                qq_ref, alpha_ref, *scratch, tq, n_q, n_sblk):
    qi = pl.program_id(2)
    t0 = qi * tq
    rg = HEADS_PER_GROUP
    jpt = JOBS_PER_TRIP
    s_refs, smax_refs, p_refs = ((scratch[k:k + jpt], scratch[k + jpt:k + 2 * jpt]) for k in (0, 2 * jpt, 4 * jpt))
    m_refs, acc_refs = scratch[6 * jpt:6 * jpt + rg], scratch[6 * jpt + rg:]
    sel, win = 0, 1


    bias = jnp.concatenate([bc_ref[r] for r in range(rg)], axis=1)
    s = _dot(kc_ref[...], jnp.concatenate([q_ref[r] for r in range(rg)], axis=1)) + bias
    visible = bias > 0.5 * MASK_NEG
    p = jnp.exp2(s - jnp.max(s, axis=0, keepdims=True))
    p = jnp.where(visible, p / jnp.sum(p, axis=0, keepdims=True), 0.0)
    o_cmp_all = _dot(vc_ref[...], p.astype(BF16))
    o_cmp = [o_cmp_all[:, r * tq:(r + 1) * tq] for r in range(rg)]
    p_sum = p[:, 0:tq]
    for r in range(1, rg):
        p_sum = p_sum + p[:, r * tq:(r + 1) * tq]
    p_hi = p_sum.astype(BF16)
    p_lo = (p_sum - p_hi.astype(F32)).astype(BF16)
    imp_t = _dot(cover_ref[...], p_hi) + _dot(cover_ref[...], p_lo)

    score = imp_t[SEL_ROW0:SEL_ROW0 + MAX_SBLK, :]
    jrow = lax.broadcasted_iota(jnp.int32, (MAX_SBLK, tq), 0)
    qblk = (lax.broadcasted_iota(jnp.int32, (MAX_SBLK, tq), 1) + t0) // SEL_BLOCK
    causal = jrow <= qblk
    forced = causal & ((jrow == 0) | (jrow >= qblk - 1))
    score = jnp.where(forced, SEL_FORCED, jnp.where(causal, score, -1.0))
    rank = jnp.zeros((MAX_SBLK, tq), F32)
    for j in range(n_sblk):
        other = score[j:j + 1, :]
        ahead = (other > score) | ((other == score) & (jrow > j))
        rank = rank + jnp.where(ahead, 1.0, 0.0)
    n_top = min(N_SELECT, n_sblk)
    selected = (rank < n_top) & (score >= 0.0)
    neg = jnp.where(selected, 0.0, SEL_NEG).astype(BF16)
    for r in range(rg):
        qq_ref[r, 0:SEL_ROW0, :] = q_ref[r, 0:SEL_ROW0, :]
        qq_ref[r, SEL_ROW0:SEL_ROW0 + MAX_SBLK, :] = neg
        qq_ref[r, SEL_ROW0 + MAX_SBLK:LANES, :] = q_ref[r, SEL_ROW0 + MAX_SBLK:LANES, :]

    near_max = WINDOW // tq + 1
    plan = [(branch, a) for a in range(min(near_max, n_q)) for branch in (sel, win)]
    plan += [(sel, a) for a in range(near_max, n_q)]
    groups = [plan[g:g + JOBS_PER_TRIP] for g in range(0, len(plan), JOBS_PER_TRIP)]
    n_groups = sum((qi >= group[0][1]).astype(jnp.int32) for group in groups)

    def count(i):
        return sum((qi >= a).astype(jnp.int32) for _, a in groups[i])

    def count_options(i):
        return sorted({sum(1 for _, a in groups[i] if a <= top) for _, top in groups[i]})

    def jobs_of(i, c):
        return [(branch, pl.multiple_of((qi - a) * tq, tq),
                 (BIAS_DIAG, BIAS_NEAR, BIAS_WIN if branch == win else None)[min(a, 2)])
                for branch, a in groups[i][:c]]

    def scores(i, c):
        for (branch, k0, kind), s_ref, smax_ref in zip(jobs_of(i, c), s_refs[i % 2], smax_refs[i % 2]):
            k = k_ref[branch, pl.ds(k0, tq), :]
            for r in range(rg):
                q_t = qq_ref[r] if branch == sel else q_ref[r]
                s = _dot(k, q_t)
                if kind is not None:
                    s = s + bias_ref[r, kind]
                s_ref[r] = s
                smax_ref[r] = jnp.max(s, axis=0, keepdims=True)

    def softmax(i, c):
        for r in range(rg):
            for branch in sorted({b for b, _ in groups[i][:c]}):
                mine = [j for j, (b, _) in enumerate(groups[i][:c]) if b == branch]
                m_old = m_refs[r][branch]
                m_new = m_old
                for j in mine:
                    m_new = jnp.maximum(m_new, smax_refs[i % 2][j][r])
                for j in mine:
                    p_refs[i % 2][j][r] = jnp.exp2(s_refs[i % 2][j][r] - m_new).astype(BF16)
                m_refs[r][branch] = m_new
                alpha_ref[i % 2, r, branch] = jnp.exp2(m_old - m_new)

    def values(i, c):
        jobs = jobs_of(i, c)
        v_tiles = [v_ref[branch, :, pl.ds(k0, tq)] for branch, k0, _ in jobs]
        for r in range(rg):
            for branch in sorted({b for b, _, _ in jobs}):
                acc = alpha_ref[i % 2, r, branch] * acc_refs[r][branch]
                for j, (b, _, _) in enumerate(jobs):
                    if b == branch:
                        acc = acc + _dot(v_tiles[j], p_refs[i % 2][j][r])
                acc_refs[r][branch] = acc

    for r in range(rg):
        m_refs[r][...] = jnp.full(m_refs[r].shape, MASK_NEG, F32)
        acc_refs[r][...] = jnp.zeros(acc_refs[r].shape, F32)
    assert count_options(0) == [len(groups[0])]
    scores(0, len(groups[0]))
    for i in range(len(groups)):
        full = len(groups[i])
        if i + 1 < len(groups):
            for c_next in count_options(i + 1):
                @pl.when((i + 1 < n_groups) & (count(i + 1) == c_next))
                def _(i=i, full=full, c_next=c_next):
                    if i > 0:
                        values(i - 1, len(groups[i - 1]))
                    softmax(i, full)
                    scores(i + 1, c_next)

        for c in count_options(i):
            @pl.when((i + 1 == n_groups) & (count(i) == c))
            def _(i=i, c=c):
                if i > 0:
                    values(i - 1, len(groups[i - 1]))
                softmax(i, c)
                values(i, c)

    def finish(branch):
        return [acc_refs[r][branch, 0:HEAD_DIM, :] / acc_refs[r][branch, HEAD_DIM:HEAD_DIM + 1, :] for r in range(rg)]

    o_sel = finish(sel)
    o_win = finish(win)

    gates_t = jnp.transpose(_sigmoid(gate_ref[...].astype(F32)))
    outs = []
    for r in range(rg):
        outs.append(gates_t[3 * r:3 * r + 1, :] * o_cmp[r]
                    + gates_t[3 * r + 1:3 * r + 2, :] * o_sel[r]
                    + gates_t[3 * r + 2:3 * r + 3, :] * o_win[r])
    o_ref[...] = jnp.transpose(jnp.concatenate(outs, axis=0)).astype(o_ref.dtype)


def _nsa_attention(qh, kc, vc, k_all, v_all, bias_c, bias_tiles, cover, proj3, tq):
    bsz, _, _, t = qh.shape
    n_chunks = kc.shape[2]
    rg = HEADS_PER_GROUP
    n_kinds = bias_tiles.shape[1]
    return pl.pallas_call(
        functools.partial(_nsa_kernel, tq=tq, n_q=t // tq, n_sblk=t // SEL_BLOCK),
        grid=(bsz, N_KV_GROUPS, t // tq),
        in_specs=[
            pl.BlockSpec((None, rg, LANES, tq), lambda b, g, i: (b, g, 0, i)),
            pl.BlockSpec((None, None, n_chunks, LANES), lambda b, g, i: (b, g, 0, 0)),
            pl.BlockSpec((None, None, HEAD_DIM, n_chunks), lambda b, g, i: (b, g, 0, 0)),
            pl.BlockSpec((None, None, 2, t, LANES), lambda b, g, i: (b, g, 0, 0, 0)),
            pl.BlockSpec((None, None, 2, V_ROWS, t), lambda b, g, i: (b, g, 0, 0, 0)),
            pl.BlockSpec((rg, n_chunks, tq), lambda b, g, i: (g, 0, i)),
            pl.BlockSpec((rg, n_kinds, tq, tq), lambda b, g, i: (g, 0, 0, 0)),
            pl.BlockSpec((LANES, n_chunks), lambda b, g, i: (0, 0)),
            pl.BlockSpec((tq, LANES), lambda b, g, i: (i, (b * D_PROJ + COL_GN) // LANES + g)),
        ],
        out_specs=pl.BlockSpec((None, tq, rg * HEAD_DIM), lambda b, g, i: (b, i, g)),
        out_shape=jax.ShapeDtypeStruct((bsz, t, Q_W), BF16),
        scratch_shapes=[
            pltpu.VMEM((rg, LANES, tq), BF16),
            pltpu.VMEM((2, rg, 2, 1, tq), F32),
        ] + [pltpu.VMEM((rg, tq, tq), F32)] * (2 * JOBS_PER_TRIP)
          + [pltpu.VMEM((rg, 1, tq), F32)] * (2 * JOBS_PER_TRIP)
          + [pltpu.VMEM((rg, tq, tq), BF16)] * (2 * JOBS_PER_TRIP)
          + [pltpu.VMEM((2, 1, tq), F32)] * rg
          + [pltpu.VMEM((2, V_ROWS, tq), F32)] * rg,
        compiler_params=_params(("parallel", "parallel", "arbitrary")),
    )(qh, kc, vc, k_all, v_all, bias_c, bias_tiles, cover, proj3)


def _merge_kernel(ya_ref, yb_ref, ga_ref, gb_ref, x_ref, pa_ref, pb_ref, wo_ref, h_ref):
    merged = (_sigmoid(ga_ref[...].astype(F32)) * _dot(ya_ref[...], pa_ref[...])
              + _sigmoid(gb_ref[...].astype(F32)) * _dot(yb_ref[...], pb_ref[...]))
    h_ref[...] = x_ref[...] + _dot(merged.astype(BF16), wo_ref[...])


def _merge_out(ya_t, yb, proj_t, x, pa, pb, wo, bsz, tm):
    t = ya_t.shape[0]
    tiles = t // tm

    def rows(width):
        return pl.BlockSpec((tm, width), lambda b, s: (b * tiles + s, 0))

    def time_major(width, offset):
        return pl.BlockSpec((tm, width), lambda b, s: (s, (b * D_PROJ + offset) // width))

    def whole(a):
        return pl.BlockSpec(a.shape, lambda b, s: (0, 0))

    return pl.pallas_call(
        _merge_kernel,
        grid=(bsz, tiles),
        in_specs=[pl.BlockSpec((tm, RNN_W), lambda b, s: (s, b)), rows(Q_W), time_major(D_MODEL, COL_GA),
                  time_major(D_MODEL, COL_GB), rows(D_MODEL), whole(pa), whole(pb), whole(wo)],
        out_specs=rows(D_MODEL),
        out_shape=jax.ShapeDtypeStruct((bsz * t, D_MODEL), F32),
        compiler_params=_params(("parallel", "parallel")),
    )(ya_t, yb, proj_t, proj_t, x, pa, pb, wo)


def _mlp_kernel(h_ref, g_ref, w1_ref, w2_ref, o_ref, hn_ref, acc_ref):
    j = pl.program_id(1)

    @pl.when(j == 0)
    def _():
        h = h_ref[...]
        y = h * lax.rsqrt(jnp.mean(h * h, axis=-1, keepdims=True) + NORM_EPS) * g_ref[...]
        hn_ref[...] = y.astype(BF16)
        acc_ref[...] = h

    z = jnp.maximum(_dot(hn_ref[...], w1_ref[...]), 0.0)
    acc_ref[...] += _dot((z * z).astype(BF16), w2_ref[...])

    @pl.when(j == pl.num_programs(1) - 1)
    def _():
        o_ref[...] = acc_ref[...]


def _mlp(h, gain, w1, w2, tm, tf):
    m, d = h.shape
    ff = w1.shape[1]
    return pl.pallas_call(
        _mlp_kernel,
        grid=(m // tm, ff // tf),
        in_specs=[
            pl.BlockSpec((tm, d), lambda i, j: (i, 0)),
            pl.BlockSpec((1, d), lambda i, j: (0, 0)),
            pl.BlockSpec((d, tf), lambda i, j: (0, j)),
            pl.BlockSpec((tf, d), lambda i, j: (j, 0)),
        ],
        out_specs=pl.BlockSpec((tm, d), lambda i, j: (i, 0)),
        out_shape=jax.ShapeDtypeStruct((m, d), F32),
        scratch_shapes=[pltpu.VMEM((tm, d), BF16), pltpu.VMEM((tm, d), F32)],
        compiler_params=_params(("parallel", "arbitrary")),
    )(h, gain, w1, w2)


def _t5_bucket_table():
    max_exact = REL_BUCKETS // 2
    d = np.arange(FAR_DIST)
    df = np.maximum(d.astype(np.float32), np.float32(1.0))
    large = max_exact + (np.log(df / np.float32(max_exact)) / np.float32(math.log(REL_MAX_DIST / max_exact))
                         * np.float32(REL_BUCKETS - max_exact)).astype(np.int32)
    large = np.minimum(large, REL_BUCKETS - 1)
    return np.where(d < max_exact, d, large).astype(np.int32)


def _pad_blocks(w, axis):
    shape = w.shape
    w = w.reshape(shape[:axis] + (LRU_BLOCKS, LRU_BLOCK_W) + shape[axis + 1:])
    pad = [(0, 0)] * w.ndim
    pad[axis + 1] = (0, RNN_BW - LRU_BLOCK_W)
    w = jnp.pad(w, pad)
    return w.reshape(shape[:axis] + (RNN_W,) + shape[axis + 1:])


def _in_proj_weight(w_in):
    cuts = np.cumsum((D_RNN, D_RNN, Q_W, KV_W, KV_W, KV_W, KV_W, KV_W, KV_W, 3 * N_HEADS, D_MODEL))
    w_in = w_in.astype(BF16)
    (w_u, w_gate, w_q, w_kc, w_vc, w_ks, w_vs, w_kw, w_vw, w_gn, w_ga, w_gb) = jnp.split(w_in, cuts, axis=1)
    per_group = 3 * HEADS_PER_GROUP
    w_gn = jnp.pad(w_gn.reshape(D_MODEL, N_KV_GROUPS, per_group), ((0, 0), (0, 0), (0, LANES - per_group)))
    w_gn = w_gn.reshape(D_MODEL, N_KV_GROUPS * LANES)
    gap = jnp.zeros((D_MODEL, COL_GA - COL_GN - N_KV_GROUPS * LANES), w_in.dtype)
    w = jnp.concatenate([_pad_blocks(w_u, 1), _pad_blocks(w_gate, 1), w_q, w_ks, w_vs, w_kw, w_vw, w_kc, w_vc,
                         w_gn, gap, w_ga, w_gb], axis=1)
    assert w.shape[1] == RNN_COLS + D_PROJ
    return w


def _phi_weights(pe, w1, w2):
    half = (CMP_BLOCK // 2) * HEAD_DIM
    w2p = jnp.pad(w2.astype(BF16), ((0, 0), (0, LANES - HEAD_DIM)))
    return pe.reshape(2, half), w1.astype(BF16).reshape(2, half, PHI_HIDDEN), w2p


def kernel(x, norm_mix, w_in, conv_w, conv_b, gate_a_w, gate_a_b, gate_x_w, gate_x_b, lru_lambda, phi_k_pe, phi_k_w1, phi_k_w2, phi_v_pe, phi_v_w1, phi_v_w2, q_norm, kc_norm, ks_norm, kw_norm, rel_bias, proj_a, proj_b, w_out, norm_mlp, w_mlp_in, w_mlp_out):
    bsz, t, d = x.shape
    assert d == D_MODEL and norm_mix.shape[0] == 1
    tq = 256
    assert t % tq == 0 and t // SEL_BLOCK <= MAX_SBLK and t % CMP_STRIDE == 0
    n_tok = bsz * t
    n_chunks = t // CMP_STRIDE
    assert n_chunks % 8 == 0 and n_chunks <= LANES
    x2 = x.reshape(n_tok, d)

    side_cols = tuple((col + g * HEAD_DIM, HEAD_DIM) for col in (COL_KC, COL_VC) for g in range(N_KV_GROUPS))
    rnn_t, proj_t, *z_groups = _norm_matmul(x2, norm_mix, _in_proj_weight(w_in[0]), bsz,
                                            tm=1024 if t % 1024 == 0 else tq, tn=1024, rnn_cols=RNN_COLS,
                                            side_cols=side_cols)

    pad_w = lambda w: jnp.pad(w, ((0, 0), (0, RNN_BW - LRU_BLOCK_W), (0, RNN_BW - LRU_BLOCK_W))).astype(BF16)
    ya_t = _rglru(rnn_t.reshape(t, bsz, RNN_COLS), _pad_blocks(conv_w[0], 1), _pad_blocks(conv_b, 1),
                  pad_w(gate_a_w[0]), _pad_blocks(gate_a_b.reshape(1, D_RNN), 1),
                  pad_w(gate_x_w[0]), _pad_blocks(gate_x_b.reshape(1, D_RNN), 1),
                  _pad_blocks(lru_lambda, 1), tc=128)

    z_groups = [z.reshape(bsz, n_chunks, CMP_STRIDE * HEAD_DIM) for z in z_groups]
    kcn = jnp.pad(kc_norm, ((0, 0), (0, LANES - HEAD_DIM)))
    kc = _compress(z_groups[:N_KV_GROUPS], *_phi_weights(phi_k_pe[0], phi_k_w1[0], phi_k_w2[0]), kcn)
    vc = _compress(z_groups[N_KV_GROUPS:], *_phi_weights(phi_v_pe[0], phi_v_w1[0], phi_v_w2[0]))

    seg128 = jnp.asarray(np.kron(np.eye(LANES // HEAD_DIM), np.ones((HEAD_DIM, HEAD_DIM))), BF16)
    qh, k_all, v_all = _nsa_prep(proj_t, bsz, jnp.tile(q_norm, (1, N_HEADS)), jnp.tile(ks_norm, (1, N_KV_GROUPS)),
                                 jnp.tile(kw_norm, (1, N_KV_GROUPS)), seg128, tt=tq)

    bucket_of = _t5_bucket_table()
    far = rel_bias[REL_BUCKETS - 1][:, None, None]

    def bias_table(dist, valid, shift):
        buckets = jnp.asarray(bucket_of[np.clip(dist, 0, FAR_DIST - 1)].astype(np.int8))
        onehot = (buckets[None] == jnp.arange(REL_BUCKETS, dtype=jnp.int8).reshape(-1, 1, 1)).astype(F32)
        vals = jnp.einsum("kh,kji->hji", rel_bias, onehot, precision=lax.Precision.HIGHEST)
        if shift:
            vals = vals - far
        return jnp.where(jnp.asarray(valid), vals * LOG2E, MASK_NEG).astype(F32)

    kj = np.arange(tq)[:, None]
    qi_ = np.arange(tq)[None, :]
    bias_diag = bias_table(qi_ - kj, qi_ >= kj, True)
    bias_near = bias_table(tq + qi_ - kj, np.ones((tq, tq), bool), True)
    bias_win = jnp.broadcast_to(jnp.asarray(np.where(kj > qi_, 0.0, MASK_NEG), F32), (N_HEADS, tq, tq))
    bias_tiles = jnp.stack([bias_diag, bias_near, bias_win], axis=1)
    assert WINDOW == 2 * tq
    cidx = np.arange(n_chunks)[:, None]
    dist_c = np.arange(t)[None, :] - (cidx * CMP_STRIDE + CMP_BLOCK - 1)
    bias_c = bias_table(dist_c, (dist_c >= 0) & (cidx < n_chunks - 1), False)
    cstart = np.arange(n_chunks) * CMP_STRIDE
    sj = np.arange(MAX_SBLK)
    cov = ((cstart[None, :] < (sj[:, None] + 1) * SEL_BLOCK) & (cstart[None, :] + CMP_BLOCK - 1 >= sj[:, None] * SEL_BLOCK)
           & (np.arange(n_chunks)[None, :] < n_chunks - 1))
    cover = np.zeros((LANES, n_chunks), np.float32)
    cover[SEL_ROW0:SEL_ROW0 + MAX_SBLK] = cov
    yb = _nsa_attention(qh, kc, vc, k_all, v_all, bias_c, bias_tiles, jnp.asarray(cover, BF16), proj_t, tq)

    pa = jnp.pad(proj_a[0].reshape(LRU_BLOCKS, LRU_BLOCK_W, D_MODEL), ((0, 0), (0, RNN_BW - LRU_BLOCK_W), (0, 0)))
    h = _merge_out(ya_t.reshape(t, bsz * RNN_W), yb.reshape(n_tok, Q_W), proj_t, x2,
                   pa.reshape(RNN_W, D_MODEL).astype(BF16), proj_b[0].astype(BF16), w_out[0].astype(BF16),
                   bsz, tm=512 if t % 512 == 0 else tq)

    out = _mlp(h, norm_mlp, w_mlp_in[0].astype(BF16), w_mlp_out[0].astype(BF16),
               tm=1024 if n_tok % 1024 == 0 else tq, tf=1024)
    return out.reshape(bsz, t, d)
```

```python
import functools
import math

import numpy as np
import jax
import jax.numpy as jnp
from jax import lax
from jax.experimental import pallas as pl
from jax.experimental.pallas import tpu as pltpu

F32 = jnp.float32
BF16 = jnp.bfloat16

D_MODEL = 1024
D_RNN = 1344
LRU_BLOCKS = 4
LRU_BLOCK_W = D_RNN // LRU_BLOCKS
CONV_W = 4
LRU_C = 8.0
N_HEADS = 16
HEAD_DIM = 64
N_KV_GROUPS = 4
HEADS_PER_GROUP = N_HEADS // N_KV_GROUPS
CMP_BLOCK = 32
CMP_STRIDE = 16
SEL_BLOCK = 64
N_SELECT = 16
WINDOW = 512
PHI_HIDDEN = 256
SEL_FORCED = 1e4
REL_BUCKETS = 32
REL_MAX_DIST = 128
D_FF = 4 * D_MODEL
NORM_EPS = 1e-6
Q_W = N_HEADS * HEAD_DIM
KV_W = N_KV_GROUPS * HEAD_DIM

LANES = 128
VMEM_LIMIT = 56 * 1024 * 1024

RNN_BW = 384
RNN_W = LRU_BLOCKS * RNN_BW
MASK_NEG = -1e30
SEL_NEG = -1e9
SEL_ROW0 = 64
MAX_SBLK = 32
FAR_DIST = 256
LOG2E = math.log2(math.e)
V_ROWS = HEAD_DIM + 16
BIAS_DIAG, BIAS_NEAR, BIAS_WIN = range(3)
JOBS_PER_TRIP = 2

COL_U = 0
COL_GATE = RNN_W
RNN_COLS = 2 * RNN_W
COL_Q = 0
COL_KS = COL_Q + Q_W
COL_VS = COL_KS + KV_W
COL_KW = COL_VS + KV_W
COL_VW = COL_KW + KV_W
COL_KC = COL_VW + KV_W
COL_VC = COL_KC + KV_W
COL_GN = COL_VC + KV_W
COL_GA = 3072
COL_GB = 4096
D_PROJ = 5120


def _dot(a, b):
    return jnp.dot(a, b, preferred_element_type=F32)


def _gelu_tanh(x):
    return 0.5 * x * (1.0 + jnp.tanh(math.sqrt(2.0 / math.pi) * (x + 0.044715 * (x * x * x))))


def _sigmoid(x):
    return 0.5 * jnp.tanh(0.5 * x) + 0.5


def _seg_sum(x, seg_ones):
    hi = x.astype(BF16)
    lo = (x - hi.astype(F32)).astype(BF16)
    return _dot(hi, seg_ones) + _dot(lo, seg_ones)


def _params(sem, flags=None):
    return pltpu.CompilerParams(dimension_semantics=sem, vmem_limit_bytes=VMEM_LIMIT, flags=flags)


def _norm_matmul_kernel(x_ref, g_ref, w_ref, rnn_ref, o_ref, *rest, tn, rnn_tiles, side_cols):
    side_refs, xn_ref = rest[:-1], rest[-1]
    j = pl.program_id(1)

    @pl.when(j == 0)
    def _():
        x = x_ref[...]
        y = x * lax.rsqrt(jnp.mean(x * x, axis=-1, keepdims=True) + NORM_EPS) * g_ref[...]
        xn_ref[...] = y.astype(BF16)

    res = _dot(xn_ref[...], w_ref[...]).astype(o_ref.dtype)

    @pl.when(j < rnn_tiles)
    def _():
        rnn_ref[...] = res

    @pl.when(j >= rnn_tiles)
    def _():
        o_ref[...] = res

    for (col, width), side_ref in zip(side_cols, side_refs):
        @pl.when(j == rnn_tiles + col // tn)
        def _(col=col, width=width, side_ref=side_ref):
            side_ref[...] = res[:, col % tn:col % tn + width]


def _norm_matmul(x, gain, w, bsz, tm, tn, rnn_cols, side_cols):
    m, k = x.shape
    n = w.shape[1] - rnn_cols
    t = m // bsz
    tiles = t // tm
    rnn_tiles, col_tiles = rnn_cols // tn, n // tn
    assert all(col // tn == (col + width - 1) // tn for col, width in side_cols)
    return pl.pallas_call(
        functools.partial(_norm_matmul_kernel, tn=tn, rnn_tiles=rnn_tiles, side_cols=side_cols),
        grid=(m // tm, rnn_tiles + col_tiles),
        in_specs=[
            pl.BlockSpec((tm, k), lambda i, j: (i, 0)),
            pl.BlockSpec((1, k), lambda i, j: (0, 0)),
            pl.BlockSpec((k, tn), lambda i, j: (0, j)),
        ],
        out_specs=[pl.BlockSpec((tm, tn), lambda i, j: (i % tiles, (i // tiles) * rnn_tiles + jnp.minimum(j, rnn_tiles - 1))),
                   pl.BlockSpec((tm, tn), lambda i, j: (i % tiles, (i // tiles) * col_tiles + jnp.maximum(j - rnn_tiles, 0)))]
                  + [pl.BlockSpec((tm, width), lambda i, j: (i, 0)) for _, width in side_cols],
        out_shape=[jax.ShapeDtypeStruct((t, bsz * rnn_cols), BF16), jax.ShapeDtypeStruct((t, bsz * n), BF16)]
                  + [jax.ShapeDtypeStruct((m, width), BF16) for _, width in side_cols],
        scratch_shapes=[pltpu.VMEM((tm, k), BF16)],
        compiler_params=_params(("parallel", "arbitrary")),
    )(x, gain, w)


def _rglru_kernel(u_ref, ug_ref, cw_ref, cb_ref, wa_ref, ba_ref, wx_ref, bx_ref, lam_ref,
                  y_ref, ubuf, a_scr, b_scr, hcar, *, tc, bsz):
    t = pl.program_id(1)
    rows = tc * bsz
    halo = (CONV_W - 1) * bsz

    @pl.when(t == 0)
    def _():
        ubuf[0:halo, :] = jnp.zeros((halo, RNN_BW), F32)
        hcar[...] = jnp.zeros_like(hcar)

    u = u_ref[...].astype(F32).reshape(rows, RNN_BW)
    ubuf[halo:halo + rows, :] = u
    xc = cb_ref[...]
    for k in range(CONV_W):
        xc = xc + cw_ref[k:k + 1, :] * ubuf[k * bsz:k * bsz + rows, :]
    ubuf[0:halo, :] = u[rows - halo:rows, :]

    xb = xc.astype(BF16)
    r = _sigmoid(_dot(xb, wa_ref[...]) + ba_ref[...])
    i = _sigmoid(_dot(xb, wx_ref[...]) + bx_ref[...])
    z = -lam_ref[...]
    softplus = jnp.maximum(z, 0.0) + jnp.log(1.0 + jnp.exp(-jnp.abs(z)))
    a = jnp.exp(r * ((-LRU_C) * softplus))
    mult = jnp.sqrt(1.0 - a * a)
    row = lax.broadcasted_iota(jnp.int32, (rows, 1), 0)
    mult = jnp.where((row < bsz) & (t == 0), 1.0, mult)
    a_scr[...] = a
    b_scr[...] = mult * (i * xc)

    def step(s, h):
        at = pl.ds(pl.multiple_of(s * bsz, bsz), bsz)
        h = a_scr[at, :] * h + b_scr[at, :]
        b_scr[at, :] = h
        return h

    hcar[...] = lax.fori_loop(0, tc, step, hcar[...], unroll=8)
    y = b_scr[...] * _gelu_tanh(ug_ref[...].astype(F32).reshape(rows, RNN_BW))
    y_ref[...] = y.reshape(tc, bsz, RNN_BW).astype(y_ref.dtype)


def _rglru(proj3, conv_w, conv_b, wa, ba, wx, bx, lam, tc):
    t, bsz, _ = proj3.shape
    assert bsz % 8 == 0, "the recurrence advances whole sublane groups of sequences"
    nb = LRU_BLOCKS
    rows = tc * bsz
    vec = pl.BlockSpec((1, RNN_BW), lambda n, s: (0, n))
    mat = pl.BlockSpec((None, RNN_BW, RNN_BW), lambda n, s: (n, 0, 0))
    return pl.pallas_call(
        functools.partial(_rglru_kernel, tc=tc, bsz=bsz),
        grid=(nb, t // tc),
        in_specs=[
            pl.BlockSpec((tc, bsz, RNN_BW), lambda n, s: (s, 0, COL_U // RNN_BW + n)),
            pl.BlockSpec((tc, bsz, RNN_BW), lambda n, s: (s, 0, COL_GATE // RNN_BW + n)),
            pl.BlockSpec((CONV_W, RNN_BW), lambda n, s: (0, n)),
            vec, mat, vec, mat, vec, vec,
        ],
        out_specs=pl.BlockSpec((tc, bsz, RNN_BW), lambda n, s: (s, 0, n)),
        out_shape=jax.ShapeDtypeStruct((t, bsz, RNN_W), BF16),
        scratch_shapes=[pltpu.VMEM((rows + (CONV_W - 1) * bsz, RNN_BW), F32),
                        pltpu.VMEM((rows, RNN_BW), F32), pltpu.VMEM((rows, RNN_BW), F32),
                        pltpu.VMEM((bsz, RNN_BW), F32)],
        compiler_params=_params(("parallel", "arbitrary")),
    )(proj3, proj3, conv_w, conv_b, wa, ba, wx, bx, lam)


def _compress_kernel(*refs, n_chunks, n_pairs, is_key):
    z_refs, (pe_ref, w1_ref, w2_ref), rest = refs[:n_pairs], refs[n_pairs:n_pairs + 3], refs[n_pairs + 3:]
    norm_ref, seg_ref, o_ref = rest if is_key else (None, None) + rest
    for pair, z_ref in enumerate(z_refs):
        z = z_ref[...].astype(F32)
        first = _dot((z + pe_ref[0:1, :]).astype(BF16), w1_ref[0])
        second = _dot((z + pe_ref[1:2, :]).astype(BF16), w1_ref[1])
        pre = first + pltpu.roll(second, n_chunks - 1, 0)
        out = _dot(_gelu_tanh(pre).astype(BF16), w2_ref[...])
        if is_key:
            ssq = _seg_sum(out * out, seg_ref[...])
            normed = out * lax.rsqrt(ssq * (1.0 / HEAD_DIM) + NORM_EPS) * norm_ref[...]
            low = lax.broadcasted_iota(jnp.int32, (n_chunks, LANES), 1) < HEAD_DIM
            o_ref[2 * pair] = jnp.where(low, normed, 0.0).astype(o_ref.dtype)
            o_ref[2 * pair + 1] = jnp.where(low, pltpu.roll(normed, HEAD_DIM, 1), 0.0).astype(o_ref.dtype)
        else:
            out_t = jnp.transpose(out)
            o_ref[2 * pair] = out_t[0:HEAD_DIM, :].astype(o_ref.dtype)
            o_ref[2 * pair + 1] = out_t[HEAD_DIM:LANES, :].astype(o_ref.dtype)


def _compress(z_pairs, pe, w1, w2, norm_and_seg=None):
    bsz, n_chunks, width = z_pairs[0].shape
    hid = w1.shape[-1]
    is_key = norm_and_seg is not None
    out_block = (N_KV_GROUPS, n_chunks, LANES) if is_key else (N_KV_GROUPS, HEAD_DIM, n_chunks)
    extra = norm_and_seg if is_key else ()
    return pl.pallas_call(
        functools.partial(_compress_kernel, n_chunks=n_chunks, n_pairs=len(z_pairs), is_key=is_key),
        grid=(bsz,),
        in_specs=[pl.BlockSpec((None, n_chunks, width), lambda b: (b, 0, 0))] * len(z_pairs) + [
            pl.BlockSpec((2, width), lambda b: (0, 0)),
            pl.BlockSpec((2, width, hid), lambda b: (0, 0, 0)),
            pl.BlockSpec((hid, LANES), lambda b: (0, 0)),
        ] + [pl.BlockSpec(a.shape, lambda b: (0, 0)) for a in extra],
        out_specs=pl.BlockSpec((None,) + out_block, lambda b: (b, 0, 0, 0)),
        out_shape=jax.ShapeDtypeStruct((bsz,) + out_block, BF16),
        compiler_params=_params(("parallel",)),
    )(*z_pairs, pe, w1, w2, *extra)


def _nsa_prep_kernel(q_ref, ks_ref, vs_ref, kw_ref, vw_ref, qn_ref, ksn_ref, kwn_ref, seg_ref,
                     qo_ref, ko_ref, vo_ref, *, tt):
    t0 = pl.program_id(1) * tt
    lane = lax.broadcasted_iota(jnp.int32, (tt, LANES), 1)
    row = lax.broadcasted_iota(jnp.int32, (tt, LANES), 0) + t0
    low = lane < HEAD_DIM
    seg = seg_ref[...]
    onehot = jnp.where((lane - SEL_ROW0) == row // SEL_BLOCK, 1.0, 0.0)

    def normed(ref, gain_ref, blk, scale):
        x = ref[:, blk * LANES:(blk + 1) * LANES].astype(F32)
        ssq = _seg_sum(x * x, seg)
        y = x * lax.rsqrt(ssq * (1.0 / HEAD_DIM) + NORM_EPS) * gain_ref[:, blk * LANES:(blk + 1) * LANES]
        return y * scale if scale != 1.0 else y

    q_pad = jnp.zeros((LANES - HEAD_DIM, tt), BF16)
    for blk in range(Q_W // LANES):
        y_t = jnp.transpose(normed(q_ref, qn_ref, blk, HEAD_DIM ** -0.5 * LOG2E)).astype(BF16)
        for half in range(2):
            qo_ref[2 * blk + half, 0:HEAD_DIM, :] = y_t[half * HEAD_DIM:(half + 1) * HEAD_DIM, :]
            qo_ref[2 * blk + half, HEAD_DIM:LANES, :] = q_pad
    ones_row = jnp.where(lax.broadcasted_iota(jnp.int32, (V_ROWS - HEAD_DIM, tt), 0) == 0, 1.0, 0.0).astype(BF16)
    for blk in range(KV_W // LANES):
        y = normed(ks_ref, ksn_ref, blk, 1.0)
        ysw = pltpu.roll(y, HEAD_DIM, 1)
        ko_ref[2 * blk, 0] = jnp.where(low, y, onehot).astype(BF16)
        ko_ref[2 * blk + 1, 0] = jnp.where(low, ysw, onehot).astype(BF16)
        y = normed(kw_ref, kwn_ref, blk, 1.0)
        ysw = pltpu.roll(y, HEAD_DIM, 1)
        ko_ref[2 * blk, 1] = jnp.where(low, y, 0.0).astype(BF16)
        ko_ref[2 * blk + 1, 1] = jnp.where(low, ysw, 0.0).astype(BF16)
        for branch, src in enumerate((vs_ref, vw_ref)):
            v_t = jnp.transpose(src[:, blk * LANES:(blk + 1) * LANES].astype(F32))
            for half in range(2):
                g = 2 * blk + half
                vo_ref[g, branch, 0:HEAD_DIM, :] = v_t[half * HEAD_DIM:(half + 1) * HEAD_DIM, :].astype(BF16)
                vo_ref[g, branch, HEAD_DIM:V_ROWS, :] = ones_row


def _nsa_prep(proj_t, bsz, qn, ksn, kwn, seg, tt):
    t = proj_t.shape[0]

    def col(width, offset):
        return pl.BlockSpec((tt, width), lambda b, s: (s, (b * D_PROJ + offset) // width))

    def vec(width):
        return pl.BlockSpec((1, width), lambda b, s: (0, 0))

    g = N_KV_GROUPS
    return pl.pallas_call(
        functools.partial(_nsa_prep_kernel, tt=tt),
        grid=(bsz, t // tt),
        in_specs=[col(Q_W, COL_Q), col(KV_W, COL_KS), col(KV_W, COL_VS), col(KV_W, COL_KW), col(KV_W, COL_VW),
                  vec(Q_W), vec(KV_W), vec(KV_W), pl.BlockSpec((LANES, LANES), lambda b, s: (0, 0))],
        out_specs=[pl.BlockSpec((None, N_HEADS, LANES, tt), lambda b, s: (b, 0, 0, s)),
                   pl.BlockSpec((None, g, 2, tt, LANES), lambda b, s: (b, 0, 0, s, 0)),
                   pl.BlockSpec((None, g, 2, V_ROWS, tt), lambda b, s: (b, 0, 0, 0, s))],
        out_shape=[jax.ShapeDtypeStruct((bsz, N_HEADS, LANES, t), BF16),
                   jax.ShapeDtypeStruct((bsz, g, 2, t, LANES), BF16),
                   jax.ShapeDtypeStruct((bsz, g, 2, V_ROWS, t), BF16)],
        compiler_params=_params(("parallel", "parallel")),
    )(proj_t, proj_t, proj_t, proj_t, proj_t, qn, ksn, kwn, seg)


def _nsa_kernel(q_ref, kc_ref, vc_ref, k_ref, v_ref, bc_ref, bias_ref, cover_ref, gate_ref, o_ref,
                qq_ref, alpha_ref, *scratch, tq, n_q, n_sblk):
    qi = pl.program_id(2)
    t0 = qi * tq
    rg = HEADS_PER_GROUP
    jpt = JOBS_PER_TRIP
    s_refs, smax_refs, p_refs = ((scratch[k:k + jpt], scratch[k + jpt:k + 2 * jpt]) for k in (0, 2 * jpt, 4 * jpt))
    m_refs, acc_refs = scratch[6 * jpt:6 * jpt + rg], scratch[6 * jpt + rg:]
    sel, win = 0, 1


    bias = jnp.concatenate([bc_ref[r] for r in range(rg)], axis=1)
    s = _dot(kc_ref[...], jnp.concatenate([q_ref[r] for r in range(rg)], axis=1)) + bias
    visible = bias > 0.5 * MASK_NEG
    p = jnp.exp2(s - jnp.max(s, axis=0, keepdims=True))
    p = jnp.where(visible, p / jnp.sum(p, axis=0, keepdims=True), 0.0)
    o_cmp_all = _dot(vc_ref[...], p.astype(BF16))
    o_cmp = [o_cmp_all[:, r * tq:(r + 1) * tq] for r in range(rg)]
    p_sum = p[:, 0:tq]
    for r in range(1, rg):
        p_sum = p_sum + p[:, r * tq:(r + 1) * tq]
    p_hi = p_sum.astype(BF16)
    p_lo = (p_sum - p_hi.astype(F32)).astype(BF16)
    imp_t = _dot(cover_ref[...], p_hi) + _dot(cover_ref[...], p_lo)

    score = imp_t[SEL_ROW0:SEL_ROW0 + MAX_SBLK, :]
    jrow = lax.broadcasted_iota(jnp.int32, (MAX_SBLK, tq), 0)
    qblk = (lax.broadcasted_iota(jnp.int32, (MAX_SBLK, tq), 1) + t0) // SEL_BLOCK
    causal = jrow <= qblk
    forced = causal & ((jrow == 0) | (jrow >= qblk - 1))
    score = jnp.where(forced, SEL_FORCED, jnp.where(causal, score, -1.0))
    rank = jnp.zeros((MAX_SBLK, tq), F32)
    for j in range(n_sblk):
        other = score[j:j + 1, :]
        ahead = (other > score) | ((other == score) & (jrow > j))
        rank = rank + jnp.where(ahead, 1.0, 0.0)
    n_top = min(N_SELECT, n_sblk)
    selected = (rank < n_top) & (score >= 0.0)
    neg = jnp.where(selected, 0.0, SEL_NEG).astype(BF16)
    for r in range(rg):
        qq_ref[r, 0:SEL_ROW0, :] = q_ref[r, 0:SEL_ROW0, :]
        qq_ref[r, SEL_ROW0:SEL_ROW0 + MAX_SBLK, :] = neg
        qq_ref[r, SEL_ROW0 + MAX_SBLK:LANES, :] = q_ref[r, SEL_ROW0 + MAX_SBLK:LANES, :]

    near_max = WINDOW // tq + 1
    plan = [(branch, a) for a in range(min(near_max, n_q)) for branch in (sel, win)]
    plan += [(sel, a) for a in range(near_max, n_q)]
    groups = [plan[g:g + JOBS_PER_TRIP] for g in range(0, len(plan), JOBS_PER_TRIP)]
    n_groups = sum((qi >= group[0][1]).astype(jnp.int32) for group in groups)

    def count(i):
        return sum((qi >= a).astype(jnp.int32) for _, a in groups[i])

    def count_options(i):
        return sorted({sum(1 for _, a in groups[i] if a <= top) for _, top in groups[i]})

    def jobs_of(i, c):
        return [(branch, pl.multiple_of((qi - a) * tq, tq),
                 (BIAS_DIAG, BIAS_NEAR, BIAS_WIN if branch == win else None)[min(a, 2)])
                for branch, a in groups[i][:c]]

    def scores(i, c):
        for (branch, k0, kind), s_ref, smax_ref in zip(jobs_of(i, c), s_refs[i % 2], smax_refs[i % 2]):
            k = k_ref[branch, pl.ds(k0, tq), :]
            for r in range(rg):
                q_t = qq_ref[r] if branch == sel else q_ref[r]
                s = _dot(k, q_t)
                if kind is not None:
                    s = s + bias_ref[r, kind]
                s_ref[r] = s
                smax_ref[r] = jnp.max(s, axis=0, keepdims=True)

    def softmax(i, c):
        for r in range(rg):
            for branch in sorted({b for b, _ in groups[i][:c]}):
                mine = [j for j, (b, _) in enumerate(groups[i][:c]) if b == branch]
                m_old = m_refs[r][branch]
                m_new = m_old
                for j in mine:
                    m_new = jnp.maximum(m_new, smax_refs[i % 2][j][r])
                for j in mine:
                    p_refs[i % 2][j][r] = jnp.exp2(s_refs[i % 2][j][r] - m_new).astype(BF16)
                m_refs[r][branch] = m_new
                alpha_ref[i % 2, r, branch] = jnp.exp2(m_old - m_new)

    def values(i, c):
        jobs = jobs_of(i, c)
        v_tiles = [v_ref[branch, :, pl.ds(k0, tq)] for branch, k0, _ in jobs]
        for r in range(rg):
            for branch in sorted({b for b, _, _ in jobs}):
                acc = alpha_ref[i % 2, r, branch] * acc_refs[r][branch]
                for j, (b, _, _) in enumerate(jobs):
                    if b == branch:
                        acc = acc + _dot(v_tiles[j], p_refs[i % 2][j][r])
                acc_refs[r][branch] = acc

    for r in range(rg):
        m_refs[r][...] = jnp.full(m_refs[r].shape, MASK_NEG, F32)
        acc_refs[r][...] = jnp.zeros(acc_refs[r].shape, F32)
    assert count_options(0) == [len(groups[0])]
    scores(0, len(groups[0]))
    for i in range(len(groups)):
        full = len(groups[i])
        if i + 1 < len(groups):
            for c_next in count_options(i + 1):
                @pl.when((i + 1 < n_groups) & (count(i + 1) == c_next))
                def _(i=i, full=full, c_next=c_next):
                    if i > 0:
                        values(i - 1, len(groups[i - 1]))
                    softmax(i, full)
                    scores(i + 1, c_next)

        for c in count_options(i):
            @pl.when((i + 1 == n_groups) & (count(i) == c))
            def _(i=i, c=c):
                if i > 0:
                    values(i - 1, len(groups[i - 1]))
                softmax(i, c)
                values(i, c)

    def finish(branch):
        return [acc_refs[r][branch, 0:HEAD_DIM, :] / acc_refs[r][branch, HEAD_DIM:HEAD_DIM + 1, :] for r in range(rg)]

    o_sel = finish(sel)
    o_win = finish(win)

    gates_t = jnp.transpose(_sigmoid(gate_ref[...].astype(F32)))
    outs = []
    for r in range(rg):
        outs.append(gates_t[3 * r:3 * r + 1, :] * o_cmp[r]
                    + gates_t[3 * r + 1:3 * r + 2, :] * o_sel[r]
                    + gates_t[3 * r + 2:3 * r + 3, :] * o_win[r])
    o_ref[...] = jnp.transpose(jnp.concatenate(outs, axis=0)).astype(o_ref.dtype)


def _nsa_attention(qh, kc, vc, k_all, v_all, bias_c, bias_tiles, cover, proj3, tq):
    bsz, _, _, t = qh.shape
    n_chunks = kc.shape[2]
    rg = HEADS_PER_GROUP
    n_kinds = bias_tiles.shape[1]
    return pl.pallas_call(
        functools.partial(_nsa_kernel, tq=tq, n_q=t // tq, n_sblk=t // SEL_BLOCK),
        grid=(bsz, N_KV_GROUPS, t // tq),
        in_specs=[
            pl.BlockSpec((None, rg, LANES, tq), lambda b, g, i: (b, g, 0, i)),
            pl.BlockSpec((None, None, n_chunks, LANES), lambda b, g, i: (b, g, 0, 0)),
            pl.BlockSpec((None, None, HEAD_DIM, n_chunks), lambda b, g, i: (b, g, 0, 0)),
            pl.BlockSpec((None, None, 2, t, LANES), lambda b, g, i: (b, g, 0, 0, 0)),
            pl.BlockSpec((None, None, 2, V_ROWS, t), lambda b, g, i: (b, g, 0, 0, 0)),
            pl.BlockSpec((rg, n_chunks, tq), lambda b, g, i: (g, 0, i)),
            pl.BlockSpec((rg, n_kinds, tq, tq), lambda b, g, i: (g, 0, 0, 0)),
            pl.BlockSpec((LANES, n_chunks), lambda b, g, i: (0, 0)),
            pl.BlockSpec((tq, LANES), lambda b, g, i: (i, (b * D_PROJ + COL_GN) // LANES + g)),
        ],
        out_specs=pl.BlockSpec((None, tq, rg * HEAD_DIM), lambda b, g, i: (b, i, g)),
        out_shape=jax.ShapeDtypeStruct((bsz, t, Q_W), BF16),
        scratch_shapes=[
            pltpu.VMEM((rg, LANES, tq), BF16),
            pltpu.VMEM((2, rg, 2, 1, tq), F32),
        ] + [pltpu.VMEM((rg, tq, tq), F32)] * (2 * JOBS_PER_TRIP)
          + [pltpu.VMEM((rg, 1, tq), F32)] * (2 * JOBS_PER_TRIP)
          + [pltpu.VMEM((rg, tq, tq), BF16)] * (2 * JOBS_PER_TRIP)
          + [pltpu.VMEM((2, 1, tq), F32)] * rg
          + [pltpu.VMEM((2, V_ROWS, tq), F32)] * rg,
        compiler_params=_params(("parallel", "parallel", "arbitrary")),
    )(qh, kc, vc, k_all, v_all, bias_c, bias_tiles, cover, proj3)


def _merge_kernel(ya_ref, yb_ref, ga_ref, gb_ref, x_ref, pa_ref, pb_ref, wo_ref, h_ref):
    merged = (_sigmoid(ga_ref[...].astype(F32)) * _dot(ya_ref[...], pa_ref[...])
              + _sigmoid(gb_ref[...].astype(F32)) * _dot(yb_ref[...], pb_ref[...]))
    h_ref[...] = x_ref[...] + _dot(merged.astype(BF16), wo_ref[...])


def _merge_out(ya_t, yb, proj_t, x, pa, pb, wo, bsz, tm):
    t = ya_t.shape[0]
    tiles = t // tm

    def rows(width):
        return pl.BlockSpec((tm, width), lambda b, s: (b * tiles + s, 0))

    def time_major(width, offset):
        return pl.BlockSpec((tm, width), lambda b, s: (s, (b * D_PROJ + offset) // width))

    def whole(a):
        return pl.BlockSpec(a.shape, lambda b, s: (0, 0))

    return pl.pallas_call(
        _merge_kernel,
        grid=(bsz, tiles),
        in_specs=[pl.BlockSpec((tm, RNN_W), lambda b, s: (s, b)), rows(Q_W), time_major(D_MODEL, COL_GA),
                  time_major(D_MODEL, COL_GB), rows(D_MODEL), whole(pa), whole(pb), whole(wo)],
        out_specs=rows(D_MODEL),
        out_shape=jax.ShapeDtypeStruct((bsz * t, D_MODEL), F32),
        compiler_params=_params(("parallel", "parallel")),
    )(ya_t, yb, proj_t, proj_t, x, pa, pb, wo)


def _mlp_kernel(h_ref, g_ref, w1_ref, w2_ref, o_ref, hn_ref, acc_ref):
    j = pl.program_id(1)

    @pl.when(j == 0)
    def _():
        h = h_ref[...]
        y = h * lax.rsqrt(jnp.mean(h * h, axis=-1, keepdims=True) + NORM_EPS) * g_ref[...]
        hn_ref[...] = y.astype(BF16)
        acc_ref[...] = h

    z = jnp.maximum(_dot(hn_ref[...], w1_ref[...]), 0.0)
    acc_ref[...] += _dot((z * z).astype(BF16), w2_ref[...])

    @pl.when(j == pl.num_programs(1) - 1)
    def _():
        o_ref[...] = acc_ref[...]


def _mlp(h, gain, w1, w2, tm, tf):
    m, d = h.shape
    ff = w1.shape[1]
    return pl.pallas_call(
        _mlp_kernel,
        grid=(m // tm, ff // tf),
        in_specs=[
            pl.BlockSpec((tm, d), lambda i, j: (i, 0)),
            pl.BlockSpec((1, d), lambda i, j: (0, 0)),
            pl.BlockSpec((d, tf), lambda i, j: (0, j)),
            pl.BlockSpec((tf, d), lambda i, j: (j, 0)),
        ],
        out_specs=pl.BlockSpec((tm, d), lambda i, j: (i, 0)),
        out_shape=jax.ShapeDtypeStruct((m, d), F32),
        scratch_shapes=[pltpu.VMEM((tm, d), BF16), pltpu.VMEM((tm, d), F32)],
        compiler_params=_params(("parallel", "arbitrary")),
    )(h, gain, w1, w2)


def _t5_bucket_table():
    max_exact = REL_BUCKETS // 2
    d = np.arange(FAR_DIST)
    df = np.maximum(d.astype(np.float32), np.float32(1.0))
    large = max_exact + (np.log(df / np.float32(max_exact)) / np.float32(math.log(REL_MAX_DIST / max_exact))
                         * np.float32(REL_BUCKETS - max_exact)).astype(np.int32)
    large = np.minimum(large, REL_BUCKETS - 1)
    return np.where(d < max_exact, d, large).astype(np.int32)


def _pad_blocks(w, axis):
    shape = w.shape
    w = w.reshape(shape[:axis] + (LRU_BLOCKS, LRU_BLOCK_W) + shape[axis + 1:])
    pad = [(0, 0)] * w.ndim
    pad[axis + 1] = (0, RNN_BW - LRU_BLOCK_W)
    w = jnp.pad(w, pad)
    return w.reshape(shape[:axis] + (RNN_W,) + shape[axis + 1:])


def _in_proj_weight(w_in):
    cuts = np.cumsum((D_RNN, D_RNN, Q_W, KV_W, KV_W, KV_W, KV_W, KV_W, KV_W, 3 * N_HEADS, D_MODEL))
    w_in = w_in.astype(BF16)
    (w_u, w_gate, w_q, w_kc, w_vc, w_ks, w_vs, w_kw, w_vw, w_gn, w_ga, w_gb) = jnp.split(w_in, cuts, axis=1)
    per_group = 3 * HEADS_PER_GROUP
    w_gn = jnp.pad(w_gn.reshape(D_MODEL, N_KV_GROUPS, per_group), ((0, 0), (0, 0), (0, LANES - per_group)))
    w_gn = w_gn.reshape(D_MODEL, N_KV_GROUPS * LANES)
    gap = jnp.zeros((D_MODEL, COL_GA - COL_GN - N_KV_GROUPS * LANES), w_in.dtype)
    w = jnp.concatenate([_pad_blocks(w_u, 1), _pad_blocks(w_gate, 1), w_q, w_ks, w_vs, w_kw, w_vw, w_kc, w_vc,
                         w_gn, gap, w_ga, w_gb], axis=1)
    assert w.shape[1] == RNN_COLS + D_PROJ
    return w


def _phi_weights(pe, w1, w2):
    half, g = CMP_BLOCK // 2, LANES // HEAD_DIM
    eye = jnp.eye(g, dtype=BF16)
    w1h = w1.astype(BF16).reshape(2, half, HEAD_DIM, PHI_HIDDEN)
    w1e = jnp.einsum("xldh,gk->xlgdkh", w1h, eye).reshape(2, half * LANES, g * PHI_HIDDEN)
    w2e = jnp.einsum("hd,gk->ghkd", w2.astype(BF16), eye).reshape(g * PHI_HIDDEN, LANES)
    pee = jnp.broadcast_to(pe.reshape(2, half, 1, HEAD_DIM), (2, half, g, HEAD_DIM)).reshape(2, half * LANES)
    return pee, w1e, w2e


def kernel(x, norm_mix, w_in, conv_w, conv_b, gate_a_w, gate_a_b, gate_x_w, gate_x_b, lru_lambda, phi_k_pe, phi_k_w1, phi_k_w2, phi_v_pe, phi_v_w1, phi_v_w2, q_norm, kc_norm, ks_norm, kw_norm, rel_bias, proj_a, proj_b, w_out, norm_mlp, w_mlp_in, w_mlp_out):
    bsz, t, d = x.shape
    assert d == D_MODEL and norm_mix.shape[0] == 1
    tq = 256
    assert t % tq == 0 and t // SEL_BLOCK <= MAX_SBLK and t % CMP_STRIDE == 0
    n_tok = bsz * t
    n_chunks = t // CMP_STRIDE
    assert n_chunks % 8 == 0 and n_chunks <= LANES
    x2 = x.reshape(n_tok, d)

    side_cols = tuple((col + p * LANES, LANES) for col in (COL_KC, COL_VC) for p in range(KV_W // LANES))
    rnn_t, proj_t, *z_groups = _norm_matmul(x2, norm_mix, _in_proj_weight(w_in[0]), bsz,
                                            tm=1024 if t % 1024 == 0 else tq, tn=1024, rnn_cols=RNN_COLS,
                                            side_cols=side_cols)

    pad_w = lambda w: jnp.pad(w, ((0, 0), (0, RNN_BW - LRU_BLOCK_W), (0, RNN_BW - LRU_BLOCK_W))).astype(BF16)
    ya_t = _rglru(rnn_t.reshape(t, bsz, RNN_COLS), _pad_blocks(conv_w[0], 1), _pad_blocks(conv_b, 1),
                  pad_w(gate_a_w[0]), _pad_blocks(gate_a_b.reshape(1, D_RNN), 1),
                  pad_w(gate_x_w[0]), _pad_blocks(gate_x_b.reshape(1, D_RNN), 1),
                  _pad_blocks(lru_lambda, 1), tc=128)

    z_pairs = [z.reshape(bsz, n_chunks, CMP_STRIDE * LANES) for z in z_groups]
    n_pairs = KV_W // LANES
    seg128 = jnp.asarray(np.kron(np.eye(LANES // HEAD_DIM), np.ones((HEAD_DIM, HEAD_DIM))), BF16)
    kcn = jnp.tile(kc_norm, (1, LANES // HEAD_DIM))
    kc = _compress(z_pairs[:n_pairs], *_phi_weights(phi_k_pe[0], phi_k_w1[0], phi_k_w2[0]), (kcn, seg128))
    vc = _compress(z_pairs[n_pairs:], *_phi_weights(phi_v_pe[0], phi_v_w1[0], phi_v_w2[0]))

    qh, k_all, v_all = _nsa_prep(proj_t, bsz, jnp.tile(q_norm, (1, N_HEADS)), jnp.tile(ks_norm, (1, N_KV_GROUPS)),
                                 jnp.tile(kw_norm, (1, N_KV_GROUPS)), seg128, tt=tq)

    bucket_of = _t5_bucket_table()
    far = rel_bias[REL_BUCKETS - 1][:, None, None]

    def bias_table(dist, valid, shift):
        buckets = jnp.asarray(bucket_of[np.clip(dist, 0, FAR_DIST - 1)].astype(np.int8))
        onehot = (buckets[None] == jnp.arange(REL_BUCKETS, dtype=jnp.int8).reshape(-1, 1, 1)).astype(F32)
        vals = jnp.einsum("kh,kji->hji", rel_bias, onehot, precision=lax.Precision.HIGHEST)
        if shift:
            vals = vals - far
        return jnp.where(jnp.asarray(valid), vals * LOG2E, MASK_NEG).astype(F32)

    kj = np.arange(tq)[:, None]
    qi_ = np.arange(tq)[None, :]
    bias_diag = bias_table(qi_ - kj, qi_ >= kj, True)
    bias_near = bias_table(tq + qi_ - kj, np.ones((tq, tq), bool), True)
    bias_win = jnp.broadcast_to(jnp.asarray(np.where(kj > qi_, 0.0, MASK_NEG), F32), (N_HEADS, tq, tq))
    bias_tiles = jnp.stack([bias_diag, bias_near, bias_win], axis=1)
    assert WINDOW == 2 * tq
    cidx = np.arange(n_chunks)[:, None]
    dist_c = np.arange(t)[None, :] - (cidx * CMP_STRIDE + CMP_BLOCK - 1)
    bias_c = bias_table(dist_c, (dist_c >= 0) & (cidx < n_chunks - 1), False)
    cstart = np.arange(n_chunks) * CMP_STRIDE
    sj = np.arange(MAX_SBLK)
    cov = ((cstart[None, :] < (sj[:, None] + 1) * SEL_BLOCK) & (cstart[None, :] + CMP_BLOCK - 1 >= sj[:, None] * SEL_BLOCK)
           & (np.arange(n_chunks)[None, :] < n_chunks - 1))
    cover = np.zeros((LANES, n_chunks), np.float32)
    cover[SEL_ROW0:SEL_ROW0 + MAX_SBLK] = cov
    yb = _nsa_attention(qh, kc, vc, k_all, v_all, bias_c, bias_tiles, jnp.asarray(cover, BF16), proj_t, tq)

    pa = jnp.pad(proj_a[0].reshape(LRU_BLOCKS, LRU_BLOCK_W, D_MODEL), ((0, 0), (0, RNN_BW - LRU_BLOCK_W), (0, 0)))
    h = _merge_out(ya_t.reshape(t, bsz * RNN_W), yb.reshape(n_tok, Q_W), proj_t, x2,
                   pa.reshape(RNN_W, D_MODEL).astype(BF16), proj_b[0].astype(BF16), w_out[0].astype(BF16),
                   bsz, tm=512 if t % 512 == 0 else tq)

    out = _mlp(h, norm_mlp, w_mlp_in[0].astype(BF16), w_mlp_out[0].astype(BF16),
               tm=1024 if n_tok % 1024 == 0 else tq, tf=1024)
    return out.reshape(bsz, t, d)
```

```python
import functools
import math

import numpy as np
import jax
import jax.numpy as jnp
from jax import lax
from jax.experimental import pallas as pl
from jax.experimental.pallas import tpu as pltpu

F32 = jnp.float32
BF16 = jnp.bfloat16

D_MODEL = 1024
D_RNN = 1344
LRU_BLOCKS = 4
LRU_BLOCK_W = D_RNN // LRU_BLOCKS
CONV_W = 4
LRU_C = 8.0
N_HEADS = 16
HEAD_DIM = 64
N_KV_GROUPS = 4
HEADS_PER_GROUP = N_HEADS // N_KV_GROUPS
CMP_BLOCK = 32
CMP_STRIDE = 16
SEL_BLOCK = 64
N_SELECT = 16
WINDOW = 512
PHI_HIDDEN = 256
SEL_FORCED = 1e4
REL_BUCKETS = 32
REL_MAX_DIST = 128
D_FF = 4 * D_MODEL
NORM_EPS = 1e-6
Q_W = N_HEADS * HEAD_DIM
KV_W = N_KV_GROUPS * HEAD_DIM

LANES = 128
VMEM_LIMIT = 56 * 1024 * 1024

RNN_BW = 384
RNN_W = LRU_BLOCKS * RNN_BW
MASK_NEG = -1e30
SEL_NEG = -1e9
SEL_ROW0 = 64
MAX_SBLK = 32
FAR_DIST = 256
LOG2E = math.log2(math.e)
V_ROWS = HEAD_DIM + 16
BIAS_DIAG, BIAS_NEAR, BIAS_WIN = range(3)
JOBS_PER_TRIP = 2

COL_U = 0
COL_GATE = RNN_W
RNN_COLS = 2 * RNN_W
COL_Q = 0
COL_KS = COL_Q + Q_W
COL_VS = COL_KS + KV_W
COL_KW = COL_VS + KV_W
COL_VW = COL_KW + KV_W
COL_KC = COL_VW + KV_W
COL_VC = COL_KC + KV_W
COL_GN = COL_VC + KV_W
COL_GA = 3072
COL_GB = 4096
D_PROJ = 5120


def _dot(a, b):
    return jnp.dot(a, b, preferred_element_type=F32)


def _gelu_tanh(x):
    return 0.5 * x * (1.0 + jnp.tanh(math.sqrt(2.0 / math.pi) * (x + 0.044715 * (x * x * x))))


def _sigmoid(x):
    return 0.5 * jnp.tanh(0.5 * x) + 0.5


def _seg_sum(x, seg_ones):
    hi = x.astype(BF16)
    lo = (x - hi.astype(F32)).astype(BF16)
    return _dot(hi, seg_ones) + _dot(lo, seg_ones)


def _params(sem, flags=None):
    return pltpu.CompilerParams(dimension_semantics=sem, vmem_limit_bytes=VMEM_LIMIT, flags=flags)


def _norm_matmul_kernel(x_ref, g_ref, w_ref, rnn_ref, o_ref, *rest, tn, rnn_tiles, side_cols):
    side_refs, xn_ref = rest[:-1], rest[-1]
    j = pl.program_id(1)

    @pl.when(j == 0)
    def _():
        x = x_ref[...]
        y = x * lax.rsqrt(jnp.mean(x * x, axis=-1, keepdims=True) + NORM_EPS) * g_ref[...]
        xn_ref[...] = y.astype(BF16)

    res = _dot(xn_ref[...], w_ref[...]).astype(o_ref.dtype)

    @pl.when(j < rnn_tiles)
    def _():
        rnn_ref[...] = res

    @pl.when(j >= rnn_tiles)
    def _():
        o_ref[...] = res

    for (col, width), side_ref in zip(side_cols, side_refs):
        @pl.when(j == rnn_tiles + col // tn)
        def _(col=col, width=width, side_ref=side_ref):
            side_ref[...] = res[:, col % tn:col % tn + width]


def _norm_matmul(x, gain, w, bsz, tm, tn, rnn_cols, side_cols):
    m, k = x.shape
    n = w.shape[1] - rnn_cols
    t = m // bsz
    tiles = t // tm
    rnn_tiles, col_tiles = rnn_cols // tn, n // tn
    assert all(col // tn == (col + width - 1) // tn for col, width in side_cols)
    return pl.pallas_call(
        functools.partial(_norm_matmul_kernel, tn=tn, rnn_tiles=rnn_tiles, side_cols=side_cols),
        grid=(m // tm, rnn_tiles + col_tiles),
        in_specs=[
            pl.BlockSpec((tm, k), lambda i, j: (i, 0)),
            pl.BlockSpec((1, k), lambda i, j: (0, 0)),
            pl.BlockSpec((k, tn), lambda i, j: (0, j)),
        ],
        out_specs=[pl.BlockSpec((tm, tn), lambda i, j: (i % tiles, (i // tiles) * rnn_tiles + jnp.minimum(j, rnn_tiles - 1))),
                   pl.BlockSpec((tm, tn), lambda i, j: (i % tiles, (i // tiles) * col_tiles + jnp.maximum(j - rnn_tiles, 0)))]
                  + [pl.BlockSpec((tm, width), lambda i, j: (i, 0)) for _, width in side_cols],
        out_shape=[jax.ShapeDtypeStruct((t, bsz * rnn_cols), BF16), jax.ShapeDtypeStruct((t, bsz * n), BF16)]
                  + [jax.ShapeDtypeStruct((m, width), BF16) for _, width in side_cols],
        scratch_shapes=[pltpu.VMEM((tm, k), BF16)],
        compiler_params=_params(("parallel", "arbitrary")),
    )(x, gain, w)


def _rglru_kernel(u_ref, ug_ref, cw_ref, cb_ref, wa_ref, ba_ref, wx_ref, bx_ref, lam_ref,
                  y_ref, ubuf, a_scr, b_scr, hcar, *, tc, bsz):
    t = pl.program_id(1)
    rows = tc * bsz
    halo = (CONV_W - 1) * bsz

    @pl.when(t == 0)
    def _():
        ubuf[0:halo, :] = jnp.zeros((halo, RNN_BW), F32)
        hcar[...] = jnp.zeros_like(hcar)

    u = u_ref[...].astype(F32).reshape(rows, RNN_BW)
    ubuf[halo:halo + rows, :] = u
    xc = cb_ref[...]
    for k in range(CONV_W):
        xc = xc + cw_ref[k:k + 1, :] * ubuf[k * bsz:k * bsz + rows, :]
    ubuf[0:halo, :] = u[rows - halo:rows, :]

    xb = xc.astype(BF16)
    r = _sigmoid(_dot(xb, wa_ref[...]) + ba_ref[...])
    i = _sigmoid(_dot(xb, wx_ref[...]) + bx_ref[...])
    z = -lam_ref[...]
    softplus = jnp.maximum(z, 0.0) + jnp.log(1.0 + jnp.exp(-jnp.abs(z)))
    a = jnp.exp(r * ((-LRU_C) * softplus))
    mult = jnp.sqrt(1.0 - a * a)
    row = lax.broadcasted_iota(jnp.int32, (rows, 1), 0)
    mult = jnp.where((row < bsz) & (t == 0), 1.0, mult)
    a_scr[...] = a
    b_scr[...] = mult * (i * xc)

    def step(s, h):
        at = pl.ds(pl.multiple_of(s * bsz, bsz), bsz)
        h = a_scr[at, :] * h + b_scr[at, :]
        b_scr[at, :] = h
        return h

    hcar[...] = lax.fori_loop(0, tc, step, hcar[...], unroll=8)
    y = b_scr[...] * _gelu_tanh(ug_ref[...].astype(F32).reshape(rows, RNN_BW))
    y_ref[...] = y.reshape(tc, bsz, RNN_BW).astype(y_ref.dtype)


def _rglru(proj3, conv_w, conv_b, wa, ba, wx, bx, lam, tc):
    t, bsz, _ = proj3.shape
    assert bsz % 8 == 0, "the recurrence advances whole sublane groups of sequences"
    nb = LRU_BLOCKS
    rows = tc * bsz
    vec = pl.BlockSpec((1, RNN_BW), lambda n, s: (0, n))
    mat = pl.BlockSpec((None, RNN_BW, RNN_BW), lambda n, s: (n, 0, 0))
    return pl.pallas_call(
        functools.partial(_rglru_kernel, tc=tc, bsz=bsz),
        grid=(nb, t // tc),
        in_specs=[
            pl.BlockSpec((tc, bsz, RNN_BW), lambda n, s: (s, 0, COL_U // RNN_BW + n)),
            pl.BlockSpec((tc, bsz, RNN_BW), lambda n, s: (s, 0, COL_GATE // RNN_BW + n)),
            pl.BlockSpec((CONV_W, RNN_BW), lambda n, s: (0, n)),
            vec, mat, vec, mat, vec, vec,
        ],
        out_specs=pl.BlockSpec((tc, bsz, RNN_BW), lambda n, s: (s, 0, n)),
        out_shape=jax.ShapeDtypeStruct((t, bsz, RNN_W), BF16),
        scratch_shapes=[pltpu.VMEM((rows + (CONV_W - 1) * bsz, RNN_BW), F32),
                        pltpu.VMEM((rows, RNN_BW), F32), pltpu.VMEM((rows, RNN_BW), F32),
                        pltpu.VMEM((bsz, RNN_BW), F32)],
        compiler_params=_params(("parallel", "arbitrary")),
    )(proj3, proj3, conv_w, conv_b, wa, ba, wx, bx, lam)


def _compress_kernel(*refs, n_chunks, n_pairs, is_key):
    z_refs, (pe_ref, w1_ref, w2_ref), rest = refs[:n_pairs], refs[n_pairs:n_pairs + 3], refs[n_pairs + 3:]
    norm_ref, seg_ref, o_ref = rest if is_key else (None, None) + rest
    for pair, z_ref in enumerate(z_refs):
        z = z_ref[...].astype(F32)
        first = _dot((z + pe_ref[0:1, :]).astype(BF16), w1_ref[0])
        second = _dot((z + pe_ref[1:2, :]).astype(BF16), w1_ref[1])
        pre = first + pltpu.roll(second, n_chunks - 1, 0)
        out = _dot(_gelu_tanh(pre).astype(BF16), w2_ref[...])
        if is_key:
            ssq = _seg_sum(out * out, seg_ref[...])
            normed = out * lax.rsqrt(ssq * (1.0 / HEAD_DIM) + NORM_EPS) * norm_ref[...]
            low = lax.broadcasted_iota(jnp.int32, (n_chunks, LANES), 1) < HEAD_DIM
            o_ref[2 * pair] = jnp.where(low, normed, 0.0).astype(o_ref.dtype)
            o_ref[2 * pair + 1] = jnp.where(low, pltpu.roll(normed, HEAD_DIM, 1), 0.0).astype(o_ref.dtype)
        else:
            out_t = jnp.transpose(out)
            o_ref[2 * pair] = out_t[0:HEAD_DIM, :].astype(o_ref.dtype)
            o_ref[2 * pair + 1] = out_t[HEAD_DIM:LANES, :].astype(o_ref.dtype)


def _compress(z_pairs, pe, w1, w2, norm_and_seg=None):
    bsz, n_chunks, width = z_pairs[0].shape
    hid = w1.shape[-1]
    is_key = norm_and_seg is not None
    out_block = (N_KV_GROUPS, n_chunks, LANES) if is_key else (N_KV_GROUPS, HEAD_DIM, n_chunks)
    extra = norm_and_seg if is_key else ()
    return pl.pallas_call(
        functools.partial(_compress_kernel, n_chunks=n_chunks, n_pairs=len(z_pairs), is_key=is_key),
        grid=(bsz,),
        in_specs=[pl.BlockSpec((None, n_chunks, width), lambda b: (b, 0, 0))] * len(z_pairs) + [
            pl.BlockSpec((2, width), lambda b: (0, 0)),
            pl.BlockSpec((2, width, hid), lambda b: (0, 0, 0)),
            pl.BlockSpec((hid, LANES), lambda b: (0, 0)),
        ] + [pl.BlockSpec(a.shape, lambda b: (0, 0)) for a in extra],
        out_specs=pl.BlockSpec((None,) + out_block, lambda b: (b, 0, 0, 0)),
        out_shape=jax.ShapeDtypeStruct((bsz,) + out_block, BF16),
        compiler_params=_params(("parallel",)),
    )(*z_pairs, pe, w1, w2, *extra)


def _nsa_prep_kernel(q_ref, ks_ref, vs_ref, kw_ref, vw_ref, qn_ref, ksn_ref, kwn_ref, seg_ref,
                     qo_ref, ko_ref, vo_ref, *, tt):
    t0 = pl.program_id(1) * tt
    lane = lax.broadcasted_iota(jnp.int32, (tt, LANES), 1)
    row = lax.broadcasted_iota(jnp.int32, (tt, LANES), 0) + t0
    low = lane < HEAD_DIM
    seg = seg_ref[...]
    onehot = jnp.where((lane - SEL_ROW0) == row // SEL_BLOCK, 1.0, 0.0)

    def normed(ref, gain_ref, blk, scale):
        x = ref[:, blk * LANES:(blk + 1) * LANES].astype(F32)
        ssq = _seg_sum(x * x, seg)
        y = x * lax.rsqrt(ssq * (1.0 / HEAD_DIM) + NORM_EPS) * gain_ref[:, blk * LANES:(blk + 1) * LANES]
        return y * scale if scale != 1.0 else y

    q_pad = jnp.zeros((LANES - HEAD_DIM, tt), BF16)
    for blk in range(Q_W // LANES):
        y_t = jnp.transpose(normed(q_ref, qn_ref, blk, HEAD_DIM ** -0.5 * LOG2E)).astype(BF16)
        for half in range(2):
            qo_ref[2 * blk + half, 0:HEAD_DIM, :] = y_t[half * HEAD_DIM:(half + 1) * HEAD_DIM, :]
            qo_ref[2 * blk + half, HEAD_DIM:LANES, :] = q_pad
    ones_row = jnp.where(lax.broadcasted_iota(jnp.int32, (V_ROWS - HEAD_DIM, tt), 0) == 0, 1.0, 0.0).astype(BF16)
    for blk in range(KV_W // LANES):
        y = normed(ks_ref, ksn_ref, blk, 1.0)
        ysw = pltpu.roll(y, HEAD_DIM, 1)
        ko_ref[2 * blk, 0] = jnp.where(low, y, onehot).astype(BF16)
        ko_ref[2 * blk + 1, 0] = jnp.where(low, ysw, onehot).astype(BF16)
        y = normed(kw_ref, kwn_ref, blk, 1.0)
        ysw = pltpu.roll(y, HEAD_DIM, 1)
        ko_ref[2 * blk, 1] = jnp.where(low, y, 0.0).astype(BF16)
        ko_ref[2 * blk + 1, 1] = jnp.where(low, ysw, 0.0).astype(BF16)
        for branch, src in enumerate((vs_ref, vw_ref)):
            v_t = jnp.transpose(src[:, blk * LANES:(blk + 1) * LANES].astype(F32))
            for half in range(2):
                g = 2 * blk + half
                vo_ref[g, branch, 0:HEAD_DIM, :] = v_t[half * HEAD_DIM:(half + 1) * HEAD_DIM, :].astype(BF16)
                vo_ref[g, branch, HEAD_DIM:V_ROWS, :] = ones_row


def _nsa_prep(proj_t, bsz, qn, ksn, kwn, seg, tt):
    t = proj_t.shape[0]

    def col(width, offset):
        return pl.BlockSpec((tt, width), lambda b, s: (s, (b * D_PROJ + offset) // width))

    def vec(width):
        return pl.BlockSpec((1, width), lambda b, s: (0, 0))

    g = N_KV_GROUPS
    return pl.pallas_call(
        functools.partial(_nsa_prep_kernel, tt=tt),
        grid=(bsz, t // tt),
        in_specs=[col(Q_W, COL_Q), col(KV_W, COL_KS), col(KV_W, COL_VS), col(KV_W, COL_KW), col(KV_W, COL_VW),
                  vec(Q_W), vec(KV_W), vec(KV_W), pl.BlockSpec((LANES, LANES), lambda b, s: (0, 0))],
        out_specs=[pl.BlockSpec((None, N_HEADS, LANES, tt), lambda b, s: (b, 0, 0, s)),
                   pl.BlockSpec((None, g, 2, tt, LANES), lambda b, s: (b, 0, 0, s, 0)),
                   pl.BlockSpec((None, g, 2, V_ROWS, tt), lambda b, s: (b, 0, 0, 0, s))],
        out_shape=[jax.ShapeDtypeStruct((bsz, N_HEADS, LANES, t), BF16),
                   jax.ShapeDtypeStruct((bsz, g, 2, t, LANES), BF16),
                   jax.ShapeDtypeStruct((bsz, g, 2, V_ROWS, t), BF16)],
        compiler_params=_params(("parallel", "parallel")),
    )(proj_t, proj_t, proj_t, proj_t, proj_t, qn, ksn, kwn, seg)


def _nsa_kernel(q_ref, kc_ref, vc_ref, k_ref, v_ref, bc_ref, bias_ref, cover_ref, gate_ref, o_ref,
                qq_ref, alpha_ref, *scratch, tq, n_q, n_sblk):
    qi = pl.program_id(2)
    t0 = qi * tq
    rg = HEADS_PER_GROUP
    jpt = JOBS_PER_TRIP
    s_refs, smax_refs, p_refs = ((scratch[k:k + jpt], scratch[k + jpt:k + 2 * jpt]) for k in (0, 2 * jpt, 4 * jpt))
    m_refs, acc_refs = scratch[6 * jpt:6 * jpt + rg], scratch[6 * jpt + rg:]
    sel, win = 0, 1


    bias = jnp.concatenate([bc_ref[r] for r in range(rg)], axis=1)
    s = _dot(kc_ref[...], jnp.concatenate([q_ref[r] for r in range(rg)], axis=1)) + bias
    visible = bias > 0.5 * MASK_NEG
    p = jnp.exp2(s - jnp.max(s, axis=0, keepdims=True))
    p = jnp.where(visible, p / jnp.sum(p, axis=0, keepdims=True), 0.0)
    o_cmp_all = _dot(vc_ref[...], p.astype(BF16))
    o_cmp = [o_cmp_all[:, r * tq:(r + 1) * tq] for r in range(rg)]
    p_sum = p[:, 0:tq]
    for r in range(1, rg):
        p_sum = p_sum + p[:, r * tq:(r + 1) * tq]
    p_hi = p_sum.astype(BF16)
    p_lo = (p_sum - p_hi.astype(F32)).astype(BF16)
    imp_t = _dot(cover_ref[...], p_hi) + _dot(cover_ref[...], p_lo)

    score = imp_t[SEL_ROW0:SEL_ROW0 + MAX_SBLK, :]
    jrow = lax.broadcasted_iota(jnp.int32, (MAX_SBLK, tq), 0)
    qblk = (lax.broadcasted_iota(jnp.int32, (MAX_SBLK, tq), 1) + t0) // SEL_BLOCK
    causal = jrow <= qblk
    forced = causal & ((jrow == 0) | (jrow >= qblk - 1))
    score = jnp.where(forced, SEL_FORCED, jnp.where(causal, score, -1.0))
    rank = jnp.zeros((MAX_SBLK, tq), F32)
    for j in range(n_sblk):
        other = score[j:j + 1, :]
        ahead = (other > score) | ((other == score) & (jrow > j))
        rank = rank + jnp.where(ahead, 1.0, 0.0)
    n_top = min(N_SELECT, n_sblk)
    selected = (rank < n_top) & (score >= 0.0)
    neg = jnp.where(selected, 0.0, SEL_NEG).astype(BF16)
    for r in range(rg):
        qq_ref[r, 0:SEL_ROW0, :] = q_ref[r, 0:SEL_ROW0, :]
        qq_ref[r, SEL_ROW0:SEL_ROW0 + MAX_SBLK, :] = neg
        qq_ref[r, SEL_ROW0 + MAX_SBLK:LANES, :] = q_ref[r, SEL_ROW0 + MAX_SBLK:LANES, :]

    near_max = WINDOW // tq + 1
    plan = [(branch, a) for a in range(min(near_max, n_q)) for branch in (sel, win)]
    plan += [(sel, a) for a in range(near_max, n_q)]
    groups = [plan[g:g + JOBS_PER_TRIP] for g in range(0, len(plan), JOBS_PER_TRIP)]
    n_groups = sum((qi >= group[0][1]).astype(jnp.int32) for group in groups)

    def count(i):
        return sum((qi >= a).astype(jnp.int32) for _, a in groups[i])

    def count_options(i):
        return sorted({sum(1 for _, a in groups[i] if a <= top) for _, top in groups[i]})

    def jobs_of(i, c):
        return [(branch, pl.multiple_of((qi - a) * tq, tq),
                 (BIAS_DIAG, BIAS_NEAR, BIAS_WIN if branch == win else None)[min(a, 2)])
                for branch, a in groups[i][:c]]

    def scores(i, c):
        for (branch, k0, kind), s_ref, smax_ref in zip(jobs_of(i, c), s_refs[i % 2], smax_refs[i % 2]):
            k = k_ref[branch, pl.ds(k0, tq), :]
            for r in range(rg):
                q_t = qq_ref[r] if branch == sel else q_ref[r]
                s = _dot(k, q_t)
                if kind is not None:
                    s = s + bias_ref[r, kind]
                s_ref[r] = s
                smax_ref[r] = jnp.max(s, axis=0, keepdims=True)

    def softmax(i, c):
        for r in range(rg):
            for branch in sorted({b for b, _ in groups[i][:c]}):
                mine = [j for j, (b, _) in enumerate(groups[i][:c]) if b == branch]
                m_old = m_refs[r][branch]
                m_new = m_old
                for j in mine:
                    m_new = jnp.maximum(m_new, smax_refs[i % 2][j][r])
                for j in mine:
                    p_refs[i % 2][j][r] = jnp.exp2((s_refs[i % 2][j][r] - m_new).astype(BF16))
                m_refs[r][branch] = m_new
                alpha_ref[i % 2, r, branch] = jnp.exp2(m_old - m_new)

    def values(i, c):
        jobs = jobs_of(i, c)
        v_tiles = [v_ref[branch, :, pl.ds(k0, tq)] for branch, k0, _ in jobs]
        for r in range(rg):
            for branch in sorted({b for b, _, _ in jobs}):
                acc = alpha_ref[i % 2, r, branch] * acc_refs[r][branch]
                for j, (b, _, _) in enumerate(jobs):
                    if b == branch:
                        acc = acc + _dot(v_tiles[j], p_refs[i % 2][j][r])
                acc_refs[r][branch] = acc

    for r in range(rg):
        m_refs[r][...] = jnp.full(m_refs[r].shape, MASK_NEG, F32)
        acc_refs[r][...] = jnp.zeros(acc_refs[r].shape, F32)
    assert count_options(0) == [len(groups[0])]
    scores(0, len(groups[0]))
    for i in range(len(groups)):
        full = len(groups[i])
        if i + 1 < len(groups):
            for c_next in count_options(i + 1):
                @pl.when((i + 1 < n_groups) & (count(i + 1) == c_next))
                def _(i=i, full=full, c_next=c_next):
                    if i > 0:
                        values(i - 1, len(groups[i - 1]))
                    softmax(i, full)
                    scores(i + 1, c_next)

        for c in count_options(i):
            @pl.when((i + 1 == n_groups) & (count(i) == c))
            def _(i=i, c=c):
                if i > 0:
                    values(i - 1, len(groups[i - 1]))
                softmax(i, c)
                values(i, c)

    def finish(branch):
        return [acc_refs[r][branch, 0:HEAD_DIM, :] / acc_refs[r][branch, HEAD_DIM:HEAD_DIM + 1, :] for r in range(rg)]

    o_sel = finish(sel)
    o_win = finish(win)

    gates_t = jnp.transpose(_sigmoid(gate_ref[...].astype(F32)))
    outs = []
    for r in range(rg):
        outs.append(gates_t[3 * r:3 * r + 1, :] * o_cmp[r]
                    + gates_t[3 * r + 1:3 * r + 2, :] * o_sel[r]
                    + gates_t[3 * r + 2:3 * r + 3, :] * o_win[r])
    o_ref[...] = jnp.transpose(jnp.concatenate(outs, axis=0)).astype(o_ref.dtype)


def _nsa_attention(qh, kc, vc, k_all, v_all, bias_c, bias_tiles, cover, proj3, tq):
    bsz, _, _, t = qh.shape
    n_chunks = kc.shape[2]
    rg = HEADS_PER_GROUP
    n_kinds = bias_tiles.shape[1]
    return pl.pallas_call(
        functools.partial(_nsa_kernel, tq=tq, n_q=t // tq, n_sblk=t // SEL_BLOCK),
        grid=(bsz, N_KV_GROUPS, t // tq),
        in_specs=[
            pl.BlockSpec((None, rg, LANES, tq), lambda b, g, i: (b, g, 0, i)),
            pl.BlockSpec((None, None, n_chunks, LANES), lambda b, g, i: (b, g, 0, 0)),
            pl.BlockSpec((None, None, HEAD_DIM, n_chunks), lambda b, g, i: (b, g, 0, 0)),
            pl.BlockSpec((None, None, 2, t, LANES), lambda b, g, i: (b, g, 0, 0, 0)),
            pl.BlockSpec((None, None, 2, V_ROWS, t), lambda b, g, i: (b, g, 0, 0, 0)),
            pl.BlockSpec((rg, n_chunks, tq), lambda b, g, i: (g, 0, i)),
            pl.BlockSpec((rg, n_kinds, tq, tq), lambda b, g, i: (g, 0, 0, 0)),
            pl.BlockSpec((LANES, n_chunks), lambda b, g, i: (0, 0)),
            pl.BlockSpec((tq, LANES), lambda b, g, i: (i, (b * D_PROJ + COL_GN) // LANES + g)),
        ],
        out_specs=pl.BlockSpec((None, tq, rg * HEAD_DIM), lambda b, g, i: (b, i, g)),
        out_shape=jax.ShapeDtypeStruct((bsz, t, Q_W), BF16),
        scratch_shapes=[
            pltpu.VMEM((rg, LANES, tq), BF16),
            pltpu.VMEM((2, rg, 2, 1, tq), F32),
        ] + [pltpu.VMEM((rg, tq, tq), F32)] * (2 * JOBS_PER_TRIP)
          + [pltpu.VMEM((rg, 1, tq), F32)] * (2 * JOBS_PER_TRIP)
          + [pltpu.VMEM((rg, tq, tq), BF16)] * (2 * JOBS_PER_TRIP)
          + [pltpu.VMEM((2, 1, tq), F32)] * rg
          + [pltpu.VMEM((2, V_ROWS, tq), F32)] * rg,
        compiler_params=_params(("parallel", "parallel", "arbitrary")),
    )(qh, kc, vc, k_all, v_all, bias_c, bias_tiles, cover, proj3)


def _merge_kernel(ya_ref, yb_ref, ga_ref, gb_ref, x_ref, pa_ref, pb_ref, wo_ref, h_ref):
    merged = (_sigmoid(ga_ref[...].astype(F32)) * _dot(ya_ref[...], pa_ref[...])
              + _sigmoid(gb_ref[...].astype(F32)) * _dot(yb_ref[...], pb_ref[...]))
    h_ref[...] = x_ref[...] + _dot(merged.astype(BF16), wo_ref[...])


def _merge_out(ya_t, yb, proj_t, x, pa, pb, wo, bsz, tm):
    t = ya_t.shape[0]
    tiles = t // tm

    def rows(width):
        return pl.BlockSpec((tm, width), lambda b, s: (b * tiles + s, 0))

    def time_major(width, offset):
        return pl.BlockSpec((tm, width), lambda b, s: (s, (b * D_PROJ + offset) // width))

    def whole(a):
        return pl.BlockSpec(a.shape, lambda b, s: (0, 0))

    return pl.pallas_call(
        _merge_kernel,
        grid=(bsz, tiles),
        in_specs=[pl.BlockSpec((tm, RNN_W), lambda b, s: (s, b)), rows(Q_W), time_major(D_MODEL, COL_GA),
                  time_major(D_MODEL, COL_GB), rows(D_MODEL), whole(pa), whole(pb), whole(wo)],
        out_specs=rows(D_MODEL),
        out_shape=jax.ShapeDtypeStruct((bsz * t, D_MODEL), F32),
        compiler_params=_params(("parallel", "parallel")),
    )(ya_t, yb, proj_t, proj_t, x, pa, pb, wo)


def _mlp_kernel(h_ref, g_ref, w1_ref, w2_ref, o_ref, hn_ref, acc_ref):
    j = pl.program_id(1)

    @pl.when(j == 0)
    def _():
        h = h_ref[...]
        y = h * lax.rsqrt(jnp.mean(h * h, axis=-1, keepdims=True) + NORM_EPS) * g_ref[...]
        hn_ref[...] = y.astype(BF16)
        acc_ref[...] = h

    z = jnp.maximum(_dot(hn_ref[...], w1_ref[...]), 0.0)
    acc_ref[...] += _dot((z * z).astype(BF16), w2_ref[...])

    @pl.when(j == pl.num_programs(1) - 1)
    def _():
        o_ref[...] = acc_ref[...]


def _mlp(h, gain, w1, w2, tm, tf):
    m, d = h.shape
    ff = w1.shape[1]
    return pl.pallas_call(
        _mlp_kernel,
        grid=(m // tm, ff // tf),
        in_specs=[
            pl.BlockSpec((tm, d), lambda i, j: (i, 0)),
            pl.BlockSpec((1, d), lambda i, j: (0, 0)),
            pl.BlockSpec((d, tf), lambda i, j: (0, j)),
            pl.BlockSpec((tf, d), lambda i, j: (j, 0)),
        ],
        out_specs=pl.BlockSpec((tm, d), lambda i, j: (i, 0)),
        out_shape=jax.ShapeDtypeStruct((m, d), F32),
        scratch_shapes=[pltpu.VMEM((tm, d), BF16), pltpu.VMEM((tm, d), F32)],
        compiler_params=_params(("parallel", "arbitrary")),
    )(h, gain, w1, w2)


def _t5_bucket_table():
    max_exact = REL_BUCKETS // 2
    d = np.arange(FAR_DIST)
    df = np.maximum(d.astype(np.float32), np.float32(1.0))
    large = max_exact + (np.log(df / np.float32(max_exact)) / np.float32(math.log(REL_MAX_DIST / max_exact))
                         * np.float32(REL_BUCKETS - max_exact)).astype(np.int32)
    large = np.minimum(large, REL_BUCKETS - 1)
    return np.where(d < max_exact, d, large).astype(np.int32)


def _pad_blocks(w, axis):
    shape = w.shape
    w = w.reshape(shape[:axis] + (LRU_BLOCKS, LRU_BLOCK_W) + shape[axis + 1:])
    pad = [(0, 0)] * w.ndim
    pad[axis + 1] = (0, RNN_BW - LRU_BLOCK_W)
    w = jnp.pad(w, pad)
    return w.reshape(shape[:axis] + (RNN_W,) + shape[axis + 1:])


def _in_proj_weight(w_in):
    cuts = np.cumsum((D_RNN, D_RNN, Q_W, KV_W, KV_W, KV_W, KV_W, KV_W, KV_W, 3 * N_HEADS, D_MODEL))
    w_in = w_in.astype(BF16)
    (w_u, w_gate, w_q, w_kc, w_vc, w_ks, w_vs, w_kw, w_vw, w_gn, w_ga, w_gb) = jnp.split(w_in, cuts, axis=1)
    per_group = 3 * HEADS_PER_GROUP
    w_gn = jnp.pad(w_gn.reshape(D_MODEL, N_KV_GROUPS, per_group), ((0, 0), (0, 0), (0, LANES - per_group)))
    w_gn = w_gn.reshape(D_MODEL, N_KV_GROUPS * LANES)
    gap = jnp.zeros((D_MODEL, COL_GA - COL_GN - N_KV_GROUPS * LANES), w_in.dtype)
    w = jnp.concatenate([_pad_blocks(w_u, 1), _pad_blocks(w_gate, 1), w_q, w_ks, w_vs, w_kw, w_vw, w_kc, w_vc,
                         w_gn, gap, w_ga, w_gb], axis=1)
    assert w.shape[1] == RNN_COLS + D_PROJ
    return w


def _phi_weights(pe, w1, w2):
    half, g = CMP_BLOCK // 2, LANES // HEAD_DIM
    eye = jnp.eye(g, dtype=BF16)
    w1h = w1.astype(BF16).reshape(2, half, HEAD_DIM, PHI_HIDDEN)
    w1e = jnp.einsum("xldh,gk->xlgdkh", w1h, eye).reshape(2, half * LANES, g * PHI_HIDDEN)
    w2e = jnp.einsum("hd,gk->ghkd", w2.astype(BF16), eye).reshape(g * PHI_HIDDEN, LANES)
    pee = jnp.broadcast_to(pe.reshape(2, half, 1, HEAD_DIM), (2, half, g, HEAD_DIM)).reshape(2, half * LANES)
    return pee, w1e, w2e


def kernel(x, norm_mix, w_in, conv_w, conv_b, gate_a_w, gate_a_b, gate_x_w, gate_x_b, lru_lambda, phi_k_pe, phi_k_w1, phi_k_w2, phi_v_pe, phi_v_w1, phi_v_w2, q_norm, kc_norm, ks_norm, kw_norm, rel_bias, proj_a, proj_b, w_out, norm_mlp, w_mlp_in, w_mlp_out):
    bsz, t, d = x.shape
    assert d == D_MODEL and norm_mix.shape[0] == 1
    tq = 256
    assert t % tq == 0 and t // SEL_BLOCK <= MAX_SBLK and t % CMP_STRIDE == 0
    n_tok = bsz * t
    n_chunks = t // CMP_STRIDE
    assert n_chunks % 8 == 0 and n_chunks <= LANES
    x2 = x.reshape(n_tok, d)

    side_cols = tuple((col + p * LANES, LANES) for col in (COL_KC, COL_VC) for p in range(KV_W // LANES))
    rnn_t, proj_t, *z_groups = _norm_matmul(x2, norm_mix, _in_proj_weight(w_in[0]), bsz,
                                            tm=1024 if t % 1024 == 0 else tq, tn=1024, rnn_cols=RNN_COLS,
                                            side_cols=side_cols)

    pad_w = lambda w: jnp.pad(w, ((0, 0), (0, RNN_BW - LRU_BLOCK_W), (0, RNN_BW - LRU_BLOCK_W))).astype(BF16)
    ya_t = _rglru(rnn_t.reshape(t, bsz, RNN_COLS), _pad_blocks(conv_w[0], 1), _pad_blocks(conv_b, 1),
                  pad_w(gate_a_w[0]), _pad_blocks(gate_a_b.reshape(1, D_RNN), 1),
                  pad_w(gate_x_w[0]), _pad_blocks(gate_x_b.reshape(1, D_RNN), 1),
                  _pad_blocks(lru_lambda, 1), tc=128)

    z_pairs = [z.reshape(bsz, n_chunks, CMP_STRIDE * LANES) for z in z_groups]
    n_pairs = KV_W // LANES
    seg128 = jnp.asarray(np.kron(np.eye(LANES // HEAD_DIM), np.ones((HEAD_DIM, HEAD_DIM))), BF16)
    kcn = jnp.tile(kc_norm, (1, LANES // HEAD_DIM))
    kc = _compress(z_pairs[:n_pairs], *_phi_weights(phi_k_pe[0], phi_k_w1[0], phi_k_w2[0]), (kcn, seg128))
    vc = _compress(z_pairs[n_pairs:], *_phi_weights(phi_v_pe[0], phi_v_w1[0], phi_v_w2[0]))

    qh, k_all, v_all = _nsa_prep(proj_t, bsz, jnp.tile(q_norm, (1, N_HEADS)), jnp.tile(ks_norm, (1, N_KV_GROUPS)),
                                 jnp.tile(kw_norm, (1, N_KV_GROUPS)), seg128, tt=tq)

    bucket_of = _t5_bucket_table()
    far = rel_bias[REL_BUCKETS - 1][:, None, None]

    def bias_table(dist, valid, shift):
        buckets = jnp.asarray(bucket_of[np.clip(dist, 0, FAR_DIST - 1)].astype(np.int8))
        onehot = (buckets[None] == jnp.arange(REL_BUCKETS, dtype=jnp.int8).reshape(-1, 1, 1)).astype(F32)
        vals = jnp.einsum("kh,kji->hji", rel_bias, onehot, precision=lax.Precision.HIGHEST)
        if shift:
            vals = vals - far
        return jnp.where(jnp.asarray(valid), vals * LOG2E, MASK_NEG).astype(F32)

    kj = np.arange(tq)[:, None]
    qi_ = np.arange(tq)[None, :]
    bias_diag = bias_table(qi_ - kj, qi_ >= kj, True)
    bias_near = bias_table(tq + qi_ - kj, np.ones((tq, tq), bool), True)
    bias_win = jnp.broadcast_to(jnp.asarray(np.where(kj > qi_, 0.0, MASK_NEG), F32), (N_HEADS, tq, tq))
    bias_tiles = jnp.stack([bias_diag, bias_near, bias_win], axis=1)
    assert WINDOW == 2 * tq
    cidx = np.arange(n_chunks)[:, None]
    dist_c = np.arange(t)[None, :] - (cidx * CMP_STRIDE + CMP_BLOCK - 1)
    bias_c = bias_table(dist_c, (dist_c >= 0) & (cidx < n_chunks - 1), False)
    cstart = np.arange(n_chunks) * CMP_STRIDE
    sj = np.arange(MAX_SBLK)
    cov = ((cstart[None, :] < (sj[:, None] + 1) * SEL_BLOCK) & (cstart[None, :] + CMP_BLOCK - 1 >= sj[:, None] * SEL_BLOCK)
           & (np.arange(n_chunks)[None, :] < n_chunks - 1))
    cover = np.zeros((LANES, n_chunks), np.float32)
    cover[SEL_ROW0:SEL_ROW0 + MAX_SBLK] = cov
    yb = _nsa_attention(qh, kc, vc, k_all, v_all, bias_c, bias_tiles, jnp.asarray(cover, BF16), proj_t, tq)

    pa = jnp.pad(proj_a[0].reshape(LRU_BLOCKS, LRU_BLOCK_W, D_MODEL), ((0, 0), (0, RNN_BW - LRU_BLOCK_W), (0, 0)))
    h = _merge_out(ya_t.reshape(t, bsz * RNN_W), yb.reshape(n_tok, Q_W), proj_t, x2,
                   pa.reshape(RNN_W, D_MODEL).astype(BF16), proj_b[0].astype(BF16), w_out[0].astype(BF16),
                   bsz, tm=512 if t % 512 == 0 else tq)

    out = _mlp(h, norm_mlp, w_mlp_in[0].astype(BF16), w_mlp_out[0].astype(BF16),
               tm=1024 if n_tok % 1024 == 0 else tq, tf=1024)
    return out.reshape(bsz, t, d)
```

```python
import functools
import math

import numpy as np
import jax
import jax.numpy as jnp
from jax import lax
from jax.experimental import pallas as pl
from jax.experimental.pallas import tpu as pltpu

F32 = jnp.float32
BF16 = jnp.bfloat16

D_MODEL = 1024
D_RNN = 1344
LRU_BLOCKS = 4
LRU_BLOCK_W = D_RNN // LRU_BLOCKS
CONV_W = 4
LRU_C = 8.0
N_HEADS = 16
HEAD_DIM = 64
N_KV_GROUPS = 4
HEADS_PER_GROUP = N_HEADS // N_KV_GROUPS
CMP_BLOCK = 32
CMP_STRIDE = 16
SEL_BLOCK = 64
N_SELECT = 16
WINDOW = 512
PHI_HIDDEN = 256
SEL_FORCED = 1e4
REL_BUCKETS = 32
REL_MAX_DIST = 128
D_FF = 4 * D_MODEL
NORM_EPS = 1e-6
Q_W = N_HEADS * HEAD_DIM
KV_W = N_KV_GROUPS * HEAD_DIM

LANES = 128
VMEM_LIMIT = 56 * 1024 * 1024

RNN_BW = 384
RNN_W = LRU_BLOCKS * RNN_BW
MASK_NEG = -1e30
SEL_NEG = -1e9
SEL_ROW0 = 64
MAX_SBLK = 32
FAR_DIST = 256
LOG2E = math.log2(math.e)
V_ROWS = HEAD_DIM + 16
BIAS_DIAG, BIAS_NEAR, BIAS_WIN = range(3)
JOBS_PER_TRIP = 2

COL_U = 0
COL_GATE = RNN_W
RNN_COLS = 2 * RNN_W
COL_Q = 0
COL_KS = COL_Q + Q_W
COL_VS = COL_KS + KV_W
COL_KW = COL_VS + KV_W
COL_VW = COL_KW + KV_W
COL_KC = COL_VW + KV_W
COL_VC = COL_KC + KV_W
COL_GN = COL_VC + KV_W
COL_GA = 3072
COL_GB = 4096
D_PROJ = 5120


def _dot(a, b):
    return jnp.dot(a, b, preferred_element_type=F32)


def _gelu_tanh(x):
    return 0.5 * x * (1.0 + jnp.tanh(math.sqrt(2.0 / math.pi) * (x + 0.044715 * (x * x * x))))


def _sigmoid(x):
    return 0.5 * jnp.tanh(0.5 * x) + 0.5


def _seg_sum(x, seg_ones):
    hi = x.astype(BF16)
    lo = (x - hi.astype(F32)).astype(BF16)
    return _dot(hi, seg_ones) + _dot(lo, seg_ones)


def _params(sem, flags=None):
    return pltpu.CompilerParams(dimension_semantics=sem, vmem_limit_bytes=VMEM_LIMIT, flags=flags)


def _norm_matmul_kernel(x_ref, g_ref, w_ref, rnn_ref, o_ref, *rest, tn, rnn_tiles, side_cols):
    side_refs, xn_ref = rest[:-1], rest[-1]
    j = pl.program_id(1)

    @pl.when(j == 0)
    def _():
        x = x_ref[...]
        y = x * lax.rsqrt(jnp.mean(x * x, axis=-1, keepdims=True) + NORM_EPS) * g_ref[...]
        xn_ref[...] = y.astype(BF16)

    res = _dot(xn_ref[...], w_ref[...]).astype(o_ref.dtype)

    @pl.when(j < rnn_tiles)
    def _():
        rnn_ref[...] = res

    @pl.when(j >= rnn_tiles)
    def _():
        o_ref[...] = res

    for (col, width), side_ref in zip(side_cols, side_refs):
        @pl.when(j == rnn_tiles + col // tn)
        def _(col=col, width=width, side_ref=side_ref):
            side_ref[...] = res[:, col % tn:col % tn + width]


def _norm_matmul(x, gain, w, bsz, tm, tn, rnn_cols, side_cols):
    m, k = x.shape
    n = w.shape[1] - rnn_cols
    t = m // bsz
    tiles = t // tm
    rnn_tiles, col_tiles = rnn_cols // tn, n // tn
    assert all(col // tn == (col + width - 1) // tn for col, width in side_cols)
    return pl.pallas_call(
        functools.partial(_norm_matmul_kernel, tn=tn, rnn_tiles=rnn_tiles, side_cols=side_cols),
        grid=(m // tm, rnn_tiles + col_tiles),
        in_specs=[
            pl.BlockSpec((tm, k), lambda i, j: (i, 0)),
            pl.BlockSpec((1, k), lambda i, j: (0, 0)),
            pl.BlockSpec((k, tn), lambda i, j: (0, j)),
        ],
        out_specs=[pl.BlockSpec((tm, tn), lambda i, j: (i % tiles, (i // tiles) * rnn_tiles + jnp.minimum(j, rnn_tiles - 1))),
                   pl.BlockSpec((tm, tn), lambda i, j: (i % tiles, (i // tiles) * col_tiles + jnp.maximum(j - rnn_tiles, 0)))]
                  + [pl.BlockSpec((tm, width), lambda i, j: (i, 0)) for _, width in side_cols],
        out_shape=[jax.ShapeDtypeStruct((t, bsz * rnn_cols), BF16), jax.ShapeDtypeStruct((t, bsz * n), BF16)]
                  + [jax.ShapeDtypeStruct((m, width), BF16) for _, width in side_cols],
        scratch_shapes=[pltpu.VMEM((tm, k), BF16)],
        compiler_params=_params(("parallel", "arbitrary")),
    )(x, gain, w)


def _rglru_kernel(u_ref, ug_ref, cw_ref, cb_ref, wa_ref, ba_ref, wx_ref, bx_ref, lam_ref,
                  y_ref, ubuf, a_scr, b_scr, hcar, *, tc, bsz):
    t = pl.program_id(1)
    rows = tc * bsz
    halo = (CONV_W - 1) * bsz

    @pl.when(t == 0)
    def _():
        ubuf[0:halo, :] = jnp.zeros((halo, RNN_BW), F32)
        hcar[...] = jnp.zeros_like(hcar)

    u = u_ref[...].astype(F32).reshape(rows, RNN_BW)
    ubuf[halo:halo + rows, :] = u
    xc = cb_ref[...]
    for k in range(CONV_W):
        xc = xc + cw_ref[k:k + 1, :] * ubuf[k * bsz:k * bsz + rows, :]
    ubuf[0:halo, :] = u[rows - halo:rows, :]

    xb = xc.astype(BF16)
    r = _sigmoid(_dot(xb, wa_ref[...]) + ba_ref[...])
    i = _sigmoid(_dot(xb, wx_ref[...]) + bx_ref[...])
    z = -lam_ref[...]
    softplus = jnp.maximum(z, 0.0) + jnp.log(1.0 + jnp.exp(-jnp.abs(z)))
    a = jnp.exp(r * ((-LRU_C) * softplus))
    mult = jnp.sqrt(1.0 - a * a)
    row = lax.broadcasted_iota(jnp.int32, (rows, 1), 0)
    mult = jnp.where((row < bsz) & (t == 0), 1.0, mult)
    a_scr[...] = a
    b_scr[...] = mult * (i * xc)

    def step(s, h):
        at = pl.ds(pl.multiple_of(s * bsz, bsz), bsz)
        h = a_scr[at, :] * h + b_scr[at, :]
        b_scr[at, :] = h
        return h

    hcar[...] = lax.fori_loop(0, tc, step, hcar[...], unroll=8)
    y = b_scr[...] * _gelu_tanh(ug_ref[...].astype(F32).reshape(rows, RNN_BW))
    y_ref[...] = y.reshape(tc, bsz, RNN_BW).astype(y_ref.dtype)


def _rglru(proj3, conv_w, conv_b, wa, ba, wx, bx, lam, tc):
    t, bsz, _ = proj3.shape
    assert bsz % 8 == 0, "the recurrence advances whole sublane groups of sequences"
    nb = LRU_BLOCKS
    rows = tc * bsz
    vec = pl.BlockSpec((1, RNN_BW), lambda n, s: (0, n))
    mat = pl.BlockSpec((None, RNN_BW, RNN_BW), lambda n, s: (n, 0, 0))
    return pl.pallas_call(
        functools.partial(_rglru_kernel, tc=tc, bsz=bsz),
        grid=(nb, t // tc),
        in_specs=[
            pl.BlockSpec((tc, bsz, RNN_BW), lambda n, s: (s, 0, COL_U // RNN_BW + n)),
            pl.BlockSpec((tc, bsz, RNN_BW), lambda n, s: (s, 0, COL_GATE // RNN_BW + n)),
            pl.BlockSpec((CONV_W, RNN_BW), lambda n, s: (0, n)),
            vec, mat, vec, mat, vec, vec,
        ],
        out_specs=pl.BlockSpec((tc, bsz, RNN_BW), lambda n, s: (s, 0, n)),
        out_shape=jax.ShapeDtypeStruct((t, bsz, RNN_W), BF16),
        scratch_shapes=[pltpu.VMEM((rows + (CONV_W - 1) * bsz, RNN_BW), F32),
                        pltpu.VMEM((rows, RNN_BW), F32), pltpu.VMEM((rows, RNN_BW), F32),
                        pltpu.VMEM((bsz, RNN_BW), F32)],
        compiler_params=_params(("parallel", "arbitrary")),
    )(proj3, proj3, conv_w, conv_b, wa, ba, wx, bx, lam)


def _compress_kernel(*refs, n_chunks, n_pairs, is_key):
    z_refs, (pe_ref, w1_ref, w2_ref), rest = refs[:n_pairs], refs[n_pairs:n_pairs + 3], refs[n_pairs + 3:]
    norm_ref, seg_ref, o_ref = rest if is_key else (None, None) + rest
    for pair, z_ref in enumerate(z_refs):
        z = z_ref[...].astype(F32)
        first = _dot((z + pe_ref[0:1, :]).astype(BF16), w1_ref[0])
        second = _dot((z + pe_ref[1:2, :]).astype(BF16), w1_ref[1])
        pre = first + pltpu.roll(second, n_chunks - 1, 0)
        out = _dot(_gelu_tanh(pre).astype(BF16), w2_ref[...])
        if is_key:
            ssq = _seg_sum(out * out, seg_ref[...])
            normed = out * lax.rsqrt(ssq * (1.0 / HEAD_DIM) + NORM_EPS) * norm_ref[...]
            low = lax.broadcasted_iota(jnp.int32, (n_chunks, LANES), 1) < HEAD_DIM
            o_ref[2 * pair] = jnp.where(low, normed, 0.0).astype(o_ref.dtype)
            o_ref[2 * pair + 1] = jnp.where(low, pltpu.roll(normed, HEAD_DIM, 1), 0.0).astype(o_ref.dtype)
        else:
            out_t = jnp.transpose(out)
            o_ref[2 * pair] = out_t[0:HEAD_DIM, :].astype(o_ref.dtype)
            o_ref[2 * pair + 1] = out_t[HEAD_DIM:LANES, :].astype(o_ref.dtype)


def _compress(z_pairs, pe, w1, w2, norm_and_seg=None):
    bsz, n_chunks, width = z_pairs[0].shape
    hid = w1.shape[-1]
    is_key = norm_and_seg is not None
    out_block = (N_KV_GROUPS, n_chunks, LANES) if is_key else (N_KV_GROUPS, HEAD_DIM, n_chunks)
    extra = norm_and_seg if is_key else ()
    return pl.pallas_call(
        functools.partial(_compress_kernel, n_chunks=n_chunks, n_pairs=len(z_pairs), is_key=is_key),
        grid=(bsz,),
        in_specs=[pl.BlockSpec((None, n_chunks, width), lambda b: (b, 0, 0))] * len(z_pairs) + [
            pl.BlockSpec((2, width), lambda b: (0, 0)),
            pl.BlockSpec((2, width, hid), lambda b: (0, 0, 0)),
            pl.BlockSpec((hid, LANES), lambda b: (0, 0)),
        ] + [pl.BlockSpec(a.shape, lambda b: (0, 0)) for a in extra],
        out_specs=pl.BlockSpec((None,) + out_block, lambda b: (b, 0, 0, 0)),
        out_shape=jax.ShapeDtypeStruct((bsz,) + out_block, BF16),
        compiler_params=_params(("parallel",)),
    )(*z_pairs, pe, w1, w2, *extra)


def _nsa_prep_kernel(q_ref, ks_ref, vs_ref, kw_ref, vw_ref, qn_ref, ksn_ref, kwn_ref, seg_ref,
                     qo_ref, ko_ref, vo_ref, *, tt):
    t0 = pl.program_id(1) * tt
    lane = lax.broadcasted_iota(jnp.int32, (tt, LANES), 1)
    row = lax.broadcasted_iota(jnp.int32, (tt, LANES), 0) + t0
    low = lane < HEAD_DIM
    seg = seg_ref[...]
    onehot = jnp.where((lane - SEL_ROW0) == row // SEL_BLOCK, 1.0, 0.0)

    def normed(ref, gain_ref, blk, scale):
        x = ref[:, blk * LANES:(blk + 1) * LANES].astype(F32)
        ssq = _seg_sum(x * x, seg)
        y = x * lax.rsqrt(ssq * (1.0 / HEAD_DIM) + NORM_EPS) * gain_ref[:, blk * LANES:(blk + 1) * LANES]
        return y * scale if scale != 1.0 else y

    q_pad = jnp.zeros((LANES - HEAD_DIM, tt), BF16)
    for blk in range(Q_W // LANES):
        y_t = jnp.transpose(normed(q_ref, qn_ref, blk, HEAD_DIM ** -0.5 * LOG2E)).astype(BF16)
        for half in range(2):
            qo_ref[2 * blk + half, 0:HEAD_DIM, :] = y_t[half * HEAD_DIM:(half + 1) * HEAD_DIM, :]
            qo_ref[2 * blk + half, HEAD_DIM:LANES, :] = q_pad
    ones_row = jnp.where(lax.broadcasted_iota(jnp.int32, (V_ROWS - HEAD_DIM, tt), 0) == 0, 1.0, 0.0).astype(BF16)
    for blk in range(KV_W // LANES):
        y = normed(ks_ref, ksn_ref, blk, 1.0)
        ysw = pltpu.roll(y, HEAD_DIM, 1)
        ko_ref[2 * blk, 0] = jnp.where(low, y, onehot).astype(BF16)
        ko_ref[2 * blk + 1, 0] = jnp.where(low, ysw, onehot).astype(BF16)
        y = normed(kw_ref, kwn_ref, blk, 1.0)
        ysw = pltpu.roll(y, HEAD_DIM, 1)
        ko_ref[2 * blk, 1] = jnp.where(low, y, 0.0).astype(BF16)
        ko_ref[2 * blk + 1, 1] = jnp.where(low, ysw, 0.0).astype(BF16)
        for branch, src in enumerate((vs_ref, vw_ref)):
            v_t = jnp.transpose(src[:, blk * LANES:(blk + 1) * LANES].astype(F32))
            for half in range(2):
                g = 2 * blk + half
                vo_ref[g, branch, 0:HEAD_DIM, :] = v_t[half * HEAD_DIM:(half + 1) * HEAD_DIM, :].astype(BF16)
                vo_ref[g, branch, HEAD_DIM:V_ROWS, :] = ones_row


def _nsa_prep(proj_t, bsz, qn, ksn, kwn, seg, tt):
    t = proj_t.shape[0]

    def col(width, offset):
        return pl.BlockSpec((tt, width), lambda b, s: (s, (b * D_PROJ + offset) // width))

    def vec(width):
        return pl.BlockSpec((1, width), lambda b, s: (0, 0))

    g = N_KV_GROUPS
    return pl.pallas_call(
        functools.partial(_nsa_prep_kernel, tt=tt),
        grid=(bsz, t // tt),
        in_specs=[col(Q_W, COL_Q), col(KV_W, COL_KS), col(KV_W, COL_VS), col(KV_W, COL_KW), col(KV_W, COL_VW),
                  vec(Q_W), vec(KV_W), vec(KV_W), pl.BlockSpec((LANES, LANES), lambda b, s: (0, 0))],
        out_specs=[pl.BlockSpec((None, N_HEADS, LANES, tt), lambda b, s: (b, 0, 0, s)),
                   pl.BlockSpec((None, g, 2, tt, LANES), lambda b, s: (b, 0, 0, s, 0)),
                   pl.BlockSpec((None, g, 2, V_ROWS, tt), lambda b, s: (b, 0, 0, 0, s))],
        out_shape=[jax.ShapeDtypeStruct((bsz, N_HEADS, LANES, t), BF16),
                   jax.ShapeDtypeStruct((bsz, g, 2, t, LANES), BF16),
                   jax.ShapeDtypeStruct((bsz, g, 2, V_ROWS, t), BF16)],
        compiler_params=_params(("parallel", "parallel")),
    )(proj_t, proj_t, proj_t, proj_t, proj_t, qn, ksn, kwn, seg)


def _nsa_kernel(q_ref, kc_ref, vc_ref, k_ref, v_ref, bc_ref, bias_ref, cover_ref, gate_ref, o_ref,
                qq_ref, alpha_ref, *scratch, tq, n_q, n_sblk):
    qi = pl.program_id(2)
    t0 = qi * tq
    rg = HEADS_PER_GROUP
    jpt = JOBS_PER_TRIP
    s_refs, smax_refs, p_refs = ((scratch[k:k + jpt], scratch[k + jpt:k + 2 * jpt]) for k in (0, 2 * jpt, 4 * jpt))
    m_refs, acc_refs = scratch[6 * jpt:6 * jpt + rg], scratch[6 * jpt + rg:]
    sel, win = 0, 1


    bias = jnp.concatenate([bc_ref[r] for r in range(rg)], axis=1)
    s = _dot(kc_ref[...], jnp.concatenate([q_ref[r] for r in range(rg)], axis=1)) + bias
    visible = bias > 0.5 * MASK_NEG
    p = jnp.exp2(s - jnp.max(s, axis=0, keepdims=True))
    p = jnp.where(visible, p / jnp.sum(p, axis=0, keepdims=True), 0.0)
    o_cmp_all = _dot(vc_ref[...], p.astype(BF16))
    o_cmp = [o_cmp_all[:, r * tq:(r + 1) * tq] for r in range(rg)]
    p_sum = p[:, 0:tq]
    for r in range(1, rg):
        p_sum = p_sum + p[:, r * tq:(r + 1) * tq]
    p_hi = p_sum.astype(BF16)
    p_lo = (p_sum - p_hi.astype(F32)).astype(BF16)
    imp_t = _dot(cover_ref[...], p_hi) + _dot(cover_ref[...], p_lo)

    score = imp_t[SEL_ROW0:SEL_ROW0 + MAX_SBLK, :]
    jrow = lax.broadcasted_iota(jnp.int32, (MAX_SBLK, tq), 0)
    qblk = (lax.broadcasted_iota(jnp.int32, (MAX_SBLK, tq), 1) + t0) // SEL_BLOCK
    causal = jrow <= qblk
    forced = causal & ((jrow == 0) | (jrow >= qblk - 1))
    score = jnp.where(forced, SEL_FORCED, jnp.where(causal, score, -1.0))
    rank = jnp.zeros((MAX_SBLK, tq), F32)
    for j in range(n_sblk):
        other = score[j:j + 1, :]
        ahead = (other > score) | ((other == score) & (jrow > j))
        rank = rank + jnp.where(ahead, 1.0, 0.0)
    n_top = min(N_SELECT, n_sblk)
    selected = (rank < n_top) & (score >= 0.0)
    neg = jnp.where(selected, 0.0, SEL_NEG).astype(BF16)
    for r in range(rg):
        qq_ref[r, 0:SEL_ROW0, :] = q_ref[r, 0:SEL_ROW0, :]
        qq_ref[r, SEL_ROW0:SEL_ROW0 + MAX_SBLK, :] = neg
        qq_ref[r, SEL_ROW0 + MAX_SBLK:LANES, :] = q_ref[r, SEL_ROW0 + MAX_SBLK:LANES, :]

    near_max = WINDOW // tq + 1
    plan = [(branch, a) for a in range(min(near_max, n_q)) for branch in (sel, win)]
    plan += [(sel, a) for a in range(near_max, n_q)]
    groups = [plan[g:g + JOBS_PER_TRIP] for g in range(0, len(plan), JOBS_PER_TRIP)]
    n_groups = sum((qi >= group[0][1]).astype(jnp.int32) for group in groups)

    def count(i):
        return sum((qi >= a).astype(jnp.int32) for _, a in groups[i])

    def count_options(i):
        return sorted({sum(1 for _, a in groups[i] if a <= top) for _, top in groups[i]})

    def jobs_of(i, c):
        return [(branch, pl.multiple_of((qi - a) * tq, tq),
                 (BIAS_DIAG, BIAS_NEAR, BIAS_WIN if branch == win else None)[min(a, 2)])
                for branch, a in groups[i][:c]]

    def scores(i, c):
        for (branch, k0, kind), s_ref, smax_ref in zip(jobs_of(i, c), s_refs[i % 2], smax_refs[i % 2]):
            k = k_ref[branch, pl.ds(k0, tq), :]
            for r in range(rg):
                q_t = qq_ref[r] if branch == sel else q_ref[r]
                s = _dot(k, q_t)
                if kind is not None:
                    s = s + bias_ref[r, kind]
                s_ref[r] = s
                smax_ref[r] = jnp.max(s, axis=0, keepdims=True)

    def softmax(i, c):
        for r in range(rg):
            for branch in sorted({b for b, _ in groups[i][:c]}):
                mine = [j for j, (b, _) in enumerate(groups[i][:c]) if b == branch]
                m_old = m_refs[r][branch]
                m_new = m_old
                for j in mine:
                    m_new = jnp.maximum(m_new, smax_refs[i % 2][j][r])
                for j in mine:
                    p_refs[i % 2][j][r] = jnp.exp2(s_refs[i % 2][j][r] - m_new).astype(BF16)
                m_refs[r][branch] = m_new
                alpha_ref[i % 2, r, branch] = jnp.exp2(m_old - m_new)

    def values(i, c):
        jobs = jobs_of(i, c)
        v_tiles = [v_ref[branch, :, pl.ds(k0, tq)] for branch, k0, _ in jobs]
        for r in range(rg):
            for branch in sorted({b for b, _, _ in jobs}):
                acc = None if i == 0 else alpha_ref[i % 2, r, branch] * acc_refs[r][branch]
                for j, (b, _, _) in enumerate(jobs):
                    if b == branch:
                        pv = _dot(v_tiles[j], p_refs[i % 2][j][r])
                        acc = pv if acc is None else acc + pv
                acc_refs[r][branch] = acc

    for r in range(rg):
        m_refs[r][...] = jnp.full(m_refs[r].shape, MASK_NEG, F32)
    assert count_options(0) == [len(groups[0])]
    assert sorted(b for b, _ in groups[0]) == [sel, win]
    scores(0, len(groups[0]))
    for i in range(len(groups)):
        full = len(groups[i])
        if i + 1 < len(groups):
            for c_next in count_options(i + 1):
                @pl.when((i + 1 < n_groups) & (count(i + 1) == c_next))
                def _(i=i, full=full, c_next=c_next):
                    if i > 0:
                        values(i - 1, len(groups[i - 1]))
                    softmax(i, full)
                    scores(i + 1, c_next)

        for c in count_options(i):
            @pl.when((i + 1 == n_groups) & (count(i) == c))
            def _(i=i, c=c):
                if i > 0:
                    values(i - 1, len(groups[i - 1]))
                softmax(i, c)
                values(i, c)

    def finish(branch):
        return [acc_refs[r][branch, 0:HEAD_DIM, :] / acc_refs[r][branch, HEAD_DIM:HEAD_DIM + 1, :] for r in range(rg)]

    o_sel = finish(sel)
    o_win = finish(win)

    gates_t = jnp.transpose(_sigmoid(gate_ref[...].astype(F32)))
    outs = []
    for r in range(rg):
        outs.append(gates_t[3 * r:3 * r + 1, :] * o_cmp[r]
                    + gates_t[3 * r + 1:3 * r + 2, :] * o_sel[r]
                    + gates_t[3 * r + 2:3 * r + 3, :] * o_win[r])
    o_ref[...] = jnp.transpose(jnp.concatenate(outs, axis=0)).astype(o_ref.dtype)


def _nsa_attention(qh, kc, vc, k_all, v_all, bias_c, bias_tiles, cover, proj3, tq):
    bsz, _, _, t = qh.shape
    n_chunks = kc.shape[2]
    rg = HEADS_PER_GROUP
    n_kinds = bias_tiles.shape[1]
    return pl.pallas_call(
        functools.partial(_nsa_kernel, tq=tq, n_q=t // tq, n_sblk=t // SEL_BLOCK),
        grid=(bsz, N_KV_GROUPS, t // tq),
        in_specs=[
            pl.BlockSpec((None, rg, LANES, tq), lambda b, g, i: (b, g, 0, i)),
            pl.BlockSpec((None, None, n_chunks, LANES), lambda b, g, i: (b, g, 0, 0)),
            pl.BlockSpec((None, None, HEAD_DIM, n_chunks), lambda b, g, i: (b, g, 0, 0)),
            pl.BlockSpec((None, None, 2, t, LANES), lambda b, g, i: (b, g, 0, 0, 0)),
            pl.BlockSpec((None, None, 2, V_ROWS, t), lambda b, g, i: (b, g, 0, 0, 0)),
            pl.BlockSpec((rg, n_chunks, tq), lambda b, g, i: (g, 0, i)),
            pl.BlockSpec((rg, n_kinds, tq, tq), lambda b, g, i: (g, 0, 0, 0)),
            pl.BlockSpec((LANES, n_chunks), lambda b, g, i: (0, 0)),
            pl.BlockSpec((tq, LANES), lambda b, g, i: (i, (b * D_PROJ + COL_GN) // LANES + g)),
        ],
        out_specs=pl.BlockSpec((None, tq, rg * HEAD_DIM), lambda b, g, i: (b, i, g)),
        out_shape=jax.ShapeDtypeStruct((bsz, t, Q_W), BF16),
        scratch_shapes=[
            pltpu.VMEM((rg, LANES, tq), BF16),
            pltpu.VMEM((2, rg, 2, 1, tq), F32),
        ] + [pltpu.VMEM((rg, tq, tq), F32)] * (2 * JOBS_PER_TRIP)
          + [pltpu.VMEM((rg, 1, tq), F32)] * (2 * JOBS_PER_TRIP)
          + [pltpu.VMEM((rg, tq, tq), BF16)] * (2 * JOBS_PER_TRIP)
          + [pltpu.VMEM((2, 1, tq), F32)] * rg
          + [pltpu.VMEM((2, V_ROWS, tq), F32)] * rg,
        compiler_params=_params(("parallel", "parallel", "arbitrary")),
    )(qh, kc, vc, k_all, v_all, bias_c, bias_tiles, cover, proj3)


def _merge_kernel(ya_ref, yb_ref, ga_ref, gb_ref, x_ref, pa_ref, pb_ref, wo_ref, h_ref):
    merged = (_sigmoid(ga_ref[...].astype(F32)) * _dot(ya_ref[...], pa_ref[...])
              + _sigmoid(gb_ref[...].astype(F32)) * _dot(yb_ref[...], pb_ref[...]))
    h_ref[...] = x_ref[...] + _dot(merged.astype(BF16), wo_ref[...])


def _merge_out(ya_t, yb, proj_t, x, pa, pb, wo, bsz, tm):
    t = ya_t.shape[0]
    tiles = t // tm

    def rows(width):
        return pl.BlockSpec((tm, width), lambda b, s: (b * tiles + s, 0))

    def time_major(width, offset):
        return pl.BlockSpec((tm, width), lambda b, s: (s, (b * D_PROJ + offset) // width))

    def whole(a):
        return pl.BlockSpec(a.shape, lambda b, s: (0, 0))

    return pl.pallas_call(
        _merge_kernel,
        grid=(bsz, tiles),
        in_specs=[pl.BlockSpec((tm, RNN_W), lambda b, s: (s, b)), rows(Q_W), time_major(D_MODEL, COL_GA),
                  time_major(D_MODEL, COL_GB), rows(D_MODEL), whole(pa), whole(pb), whole(wo)],
        out_specs=rows(D_MODEL),
        out_shape=jax.ShapeDtypeStruct((bsz * t, D_MODEL), F32),
        compiler_params=_params(("parallel", "parallel")),
    )(ya_t, yb, proj_t, proj_t, x, pa, pb, wo)


def _mlp_kernel(h_ref, g_ref, w1_ref, w2_ref, o_ref, hn_ref, acc_ref):
    j = pl.program_id(1)

    @pl.when(j == 0)
    def _():
        h = h_ref[...]
        y = h * lax.rsqrt(jnp.mean(h * h, axis=-1, keepdims=True) + NORM_EPS) * g_ref[...]
        hn_ref[...] = y.astype(BF16)
        acc_ref[...] = h

    z = jnp.maximum(_dot(hn_ref[...], w1_ref[...]), 0.0)
    acc_ref[...] += _dot((z * z).astype(BF16), w2_ref[...])

    @pl.when(j == pl.num_programs(1) - 1)
    def _():
        o_ref[...] = acc_ref[...]


def _mlp(h, gain, w1, w2, tm, tf):
    m, d = h.shape
    ff = w1.shape[1]
    return pl.pallas_call(
        _mlp_kernel,
        grid=(m // tm, ff // tf),
        in_specs=[
            pl.BlockSpec((tm, d), lambda i, j: (i, 0)),
            pl.BlockSpec((1, d), lambda i, j: (0, 0)),
            pl.BlockSpec((d, tf), lambda i, j: (0, j)),
            pl.BlockSpec((tf, d), lambda i, j: (j, 0)),
        ],
        out_specs=pl.BlockSpec((tm, d), lambda i, j: (i, 0)),
        out_shape=jax.ShapeDtypeStruct((m, d), F32),
        scratch_shapes=[pltpu.VMEM((tm, d), BF16), pltpu.VMEM((tm, d), F32)],
        compiler_params=_params(("parallel", "arbitrary")),
    )(h, gain, w1, w2)


def _t5_bucket_table():
    max_exact = REL_BUCKETS // 2
    d = np.arange(FAR_DIST)
    df = np.maximum(d.astype(np.float32), np.float32(1.0))
    large = max_exact + (np.log(df / np.float32(max_exact)) / np.float32(math.log(REL_MAX_DIST / max_exact))
                         * np.float32(REL_BUCKETS - max_exact)).astype(np.int32)
    large = np.minimum(large, REL_BUCKETS - 1)
    return np.where(d < max_exact, d, large).astype(np.int32)


def _pad_blocks(w, axis):
    shape = w.shape
    w = w.reshape(shape[:axis] + (LRU_BLOCKS, LRU_BLOCK_W) + shape[axis + 1:])
    pad = [(0, 0)] * w.ndim
    pad[axis + 1] = (0, RNN_BW - LRU_BLOCK_W)
    w = jnp.pad(w, pad)
    return w.reshape(shape[:axis] + (RNN_W,) + shape[axis + 1:])


def _in_proj_weight(w_in):
    cuts = np.cumsum((D_RNN, D_RNN, Q_W, KV_W, KV_W, KV_W, KV_W, KV_W, KV_W, 3 * N_HEADS, D_MODEL))
    w_in = w_in.astype(BF16)
    (w_u, w_gate, w_q, w_kc, w_vc, w_ks, w_vs, w_kw, w_vw, w_gn, w_ga, w_gb) = jnp.split(w_in, cuts, axis=1)
    per_group = 3 * HEADS_PER_GROUP
    w_gn = jnp.pad(w_gn.reshape(D_MODEL, N_KV_GROUPS, per_group), ((0, 0), (0, 0), (0, LANES - per_group)))
    w_gn = w_gn.reshape(D_MODEL, N_KV_GROUPS * LANES)
    gap = jnp.zeros((D_MODEL, COL_GA - COL_GN - N_KV_GROUPS * LANES), w_in.dtype)
    w = jnp.concatenate([_pad_blocks(w_u, 1), _pad_blocks(w_gate, 1), w_q, w_ks, w_vs, w_kw, w_vw, w_kc, w_vc,
                         w_gn, gap, w_ga, w_gb], axis=1)
    assert w.shape[1] == RNN_COLS + D_PROJ
    return w


def _phi_weights(pe, w1, w2):
    half, g = CMP_BLOCK // 2, LANES // HEAD_DIM
    eye = jnp.eye(g, dtype=BF16)
    w1h = w1.astype(BF16).reshape(2, half, HEAD_DIM, PHI_HIDDEN)
    w1e = jnp.einsum("xldh,gk->xlgdkh", w1h, eye).reshape(2, half * LANES, g * PHI_HIDDEN)
    w2e = jnp.einsum("hd,gk->ghkd", w2.astype(BF16), eye).reshape(g * PHI_HIDDEN, LANES)
    pee = jnp.broadcast_to(pe.reshape(2, half, 1, HEAD_DIM), (2, half, g, HEAD_DIM)).reshape(2, half * LANES)
    return pee, w1e, w2e


def kernel(x, norm_mix, w_in, conv_w, conv_b, gate_a_w, gate_a_b, gate_x_w, gate_x_b, lru_lambda, phi_k_pe, phi_k_w1, phi_k_w2, phi_v_pe, phi_v_w1, phi_v_w2, q_norm, kc_norm, ks_norm, kw_norm, rel_bias, proj_a, proj_b, w_out, norm_mlp, w_mlp_in, w_mlp_out):
    bsz, t, d = x.shape
    assert d == D_MODEL and norm_mix.shape[0] == 1
    tq = 256
    assert t % tq == 0 and t // SEL_BLOCK <= MAX_SBLK and t % CMP_STRIDE == 0
    n_tok = bsz * t
    n_chunks = t // CMP_STRIDE
    assert n_chunks % 8 == 0 and n_chunks <= LANES
    x2 = x.reshape(n_tok, d)

    side_cols = tuple((col + p * LANES, LANES) for col in (COL_KC, COL_VC) for p in range(KV_W // LANES))
    rnn_t, proj_t, *z_groups = _norm_matmul(x2, norm_mix, _in_proj_weight(w_in[0]), bsz,
                                            tm=1024 if t % 1024 == 0 else tq, tn=1024, rnn_cols=RNN_COLS,
                                            side_cols=side_cols)

    pad_w = lambda w: jnp.pad(w, ((0, 0), (0, RNN_BW - LRU_BLOCK_W), (0, RNN_BW - LRU_BLOCK_W))).astype(BF16)
    ya_t = _rglru(rnn_t.reshape(t, bsz, RNN_COLS), _pad_blocks(conv_w[0], 1), _pad_blocks(conv_b, 1),
                  pad_w(gate_a_w[0]), _pad_blocks(gate_a_b.reshape(1, D_RNN), 1),
                  pad_w(gate_x_w[0]), _pad_blocks(gate_x_b.reshape(1, D_RNN), 1),
                  _pad_blocks(lru_lambda, 1), tc=128)

    z_pairs = [z.reshape(bsz, n_chunks, CMP_STRIDE * LANES) for z in z_groups]
    n_pairs = KV_W // LANES
    seg128 = jnp.asarray(np.kron(np.eye(LANES // HEAD_DIM), np.ones((HEAD_DIM, HEAD_DIM))), BF16)
    kcn = jnp.tile(kc_norm, (1, LANES // HEAD_DIM))
    kc = _compress(z_pairs[:n_pairs], *_phi_weights(phi_k_pe[0], phi_k_w1[0], phi_k_w2[0]), (kcn, seg128))
    vc = _compress(z_pairs[n_pairs:], *_phi_weights(phi_v_pe[0], phi_v_w1[0], phi_v_w2[0]))

    qh, k_all, v_all = _nsa_prep(proj_t, bsz, jnp.tile(q_norm, (1, N_HEADS)), jnp.tile(ks_norm, (1, N_KV_GROUPS)),
                                 jnp.tile(kw_norm, (1, N_KV_GROUPS)), seg128, tt=tq)

    bucket_of = _t5_bucket_table()
    far = rel_bias[REL_BUCKETS - 1][:, None, None]

    def bias_table(dist, valid, shift):
        buckets = jnp.asarray(bucket_of[np.clip(dist, 0, FAR_DIST - 1)].astype(np.int8))
        onehot = (buckets[None] == jnp.arange(REL_BUCKETS, dtype=jnp.int8).reshape(-1, 1, 1)).astype(F32)
        vals = jnp.einsum("kh,kji->hji", rel_bias, onehot, precision=lax.Precision.HIGHEST)
        if shift:
            vals = vals - far
        return jnp.where(jnp.asarray(valid), vals * LOG2E, MASK_NEG).astype(F32)

    kj = np.arange(tq)[:, None]
    qi_ = np.arange(tq)[None, :]
    bias_diag = bias_table(qi_ - kj, qi_ >= kj, True)
    bias_near = bias_table(tq + qi_ - kj, np.ones((tq, tq), bool), True)
    bias_win = jnp.broadcast_to(jnp.asarray(np.where(kj > qi_, 0.0, MASK_NEG), F32), (N_HEADS, tq, tq))
    bias_tiles = jnp.stack([bias_diag, bias_near, bias_win], axis=1)
    assert WINDOW == 2 * tq
    cidx = np.arange(n_chunks)[:, None]
    dist_c = np.arange(t)[None, :] - (cidx * CMP_STRIDE + CMP_BLOCK - 1)
    bias_c = bias_table(dist_c, (dist_c >= 0) & (cidx < n_chunks - 1), False)
    cstart = np.arange(n_chunks) * CMP_STRIDE
    sj = np.arange(MAX_SBLK)
    cov = ((cstart[None, :] < (sj[:, None] + 1) * SEL_BLOCK) & (cstart[None, :] + CMP_BLOCK - 1 >= sj[:, None] * SEL_BLOCK)
           & (np.arange(n_chunks)[None, :] < n_chunks - 1))
    cover = np.zeros((LANES, n_chunks), np.float32)
    cover[SEL_ROW0:SEL_ROW0 + MAX_SBLK] = cov
    yb = _nsa_attention(qh, kc, vc, k_all, v_all, bias_c, bias_tiles, jnp.asarray(cover, BF16), proj_t, tq)

    pa = jnp.pad(proj_a[0].reshape(LRU_BLOCKS, LRU_BLOCK_W, D_MODEL), ((0, 0), (0, RNN_BW - LRU_BLOCK_W), (0, 0)))
    h = _merge_out(ya_t.reshape(t, bsz * RNN_W), yb.reshape(n_tok, Q_W), proj_t, x2,
                   pa.reshape(RNN_W, D_MODEL).astype(BF16), proj_b[0].astype(BF16), w_out[0].astype(BF16),
                   bsz, tm=512 if t % 512 == 0 else tq)

    out = _mlp(h, norm_mlp, w_mlp_in[0].astype(BF16), w_mlp_out[0].astype(BF16),
               tm=1024 if n_tok % 1024 == 0 else tq, tf=1024)
    return out.reshape(bsz, t, d)
```

```python
import functools
import math

import numpy as np
import jax
import jax.numpy as jnp
from jax import lax
from jax.experimental import pallas as pl
from jax.experimental.pallas import tpu as pltpu

F32 = jnp.float32
BF16 = jnp.bfloat16

D_MODEL = 1024
D_RNN = 1344
LRU_BLOCKS = 4
LRU_BLOCK_W = D_RNN // LRU_BLOCKS
CONV_W = 4
LRU_C = 8.0
N_HEADS = 16
HEAD_DIM = 64
N_KV_GROUPS = 4
HEADS_PER_GROUP = N_HEADS // N_KV_GROUPS
CMP_BLOCK = 32
CMP_STRIDE = 16
SEL_BLOCK = 64
N_SELECT = 16
WINDOW = 512
PHI_HIDDEN = 256
SEL_FORCED = 1e4
REL_BUCKETS = 32
REL_MAX_DIST = 128
D_FF = 4 * D_MODEL
NORM_EPS = 1e-6
Q_W = N_HEADS * HEAD_DIM
KV_W = N_KV_GROUPS * HEAD_DIM

LANES = 128
VMEM_LIMIT = 56 * 1024 * 1024

RNN_BW = 384
RNN_W = LRU_BLOCKS * RNN_BW
MASK_NEG = -1e30
SEL_NEG = -1e9
SEL_ROW0 = 64
MAX_SBLK = 32
FAR_DIST = 256
LOG2E = math.log2(math.e)
V_ROWS = HEAD_DIM + 16
BIAS_DIAG, BIAS_NEAR, BIAS_WIN = range(3)
JOBS_PER_TRIP = 2

COL_U = 0
COL_GATE = RNN_W
RNN_COLS = 2 * RNN_W
COL_Q = 0
COL_KS = COL_Q + Q_W
COL_VS = COL_KS + KV_W
COL_KW = COL_VS + KV_W
COL_VW = COL_KW + KV_W
COL_KC = COL_VW + KV_W
COL_VC = COL_KC + KV_W
COL_GN = COL_VC + KV_W
COL_GA = 3072
COL_GB = 4096
D_PROJ = 5120


def _dot(a, b):
    return jnp.dot(a, b, preferred_element_type=F32)


def _gelu_tanh(x):
    return 0.5 * x * (1.0 + jnp.tanh(math.sqrt(2.0 / math.pi) * (x + 0.044715 * (x * x * x))))


def _sigmoid(x):
    return 0.5 * jnp.tanh(0.5 * x) + 0.5


def _seg_sum(x, seg_ones):
    hi = x.astype(BF16)
    lo = (x - hi.astype(F32)).astype(BF16)
    return _dot(hi, seg_ones) + _dot(lo, seg_ones)


def _params(sem, flags=None):
    return pltpu.CompilerParams(dimension_semantics=sem, vmem_limit_bytes=VMEM_LIMIT, flags=flags)


def _norm_matmul_kernel(x_ref, g_ref, w_ref, rnn_ref, o_ref, *rest, tn, rnn_tiles, side_cols):
    side_refs, xn_ref = rest[:-1], rest[-1]
    j = pl.program_id(1)

    @pl.when(j == 0)
    def _():
        x = x_ref[...]
        y = x * lax.rsqrt(jnp.mean(x * x, axis=-1, keepdims=True) + NORM_EPS) * g_ref[...]
        xn_ref[...] = y.astype(BF16)

    res = _dot(xn_ref[...], w_ref[...]).astype(o_ref.dtype)

    @pl.when(j < rnn_tiles)
    def _():
        rnn_ref[...] = res

    @pl.when(j >= rnn_tiles)
    def _():
        o_ref[...] = res

    for (col, width), side_ref in zip(side_cols, side_refs):
        @pl.when(j == rnn_tiles + col // tn)
        def _(col=col, width=width, side_ref=side_ref):
            side_ref[...] = res[:, col % tn:col % tn + width]


def _norm_matmul(x, gain, w, bsz, tm, tn, rnn_cols, side_cols):
    m, k = x.shape
    n = w.shape[1] - rnn_cols
    t = m // bsz
    tiles = t // tm
    rnn_tiles, col_tiles = rnn_cols // tn, n // tn
    assert all(col // tn == (col + width - 1) // tn for col, width in side_cols)
    return pl.pallas_call(
        functools.partial(_norm_matmul_kernel, tn=tn, rnn_tiles=rnn_tiles, side_cols=side_cols),
        grid=(m // tm, rnn_tiles + col_tiles),
        in_specs=[
            pl.BlockSpec((tm, k), lambda i, j: (i, 0)),
            pl.BlockSpec((1, k), lambda i, j: (0, 0)),
            pl.BlockSpec((k, tn), lambda i, j: (0, j)),
        ],
        out_specs=[pl.BlockSpec((tm, tn), lambda i, j: (i % tiles, (i // tiles) * rnn_tiles + jnp.minimum(j, rnn_tiles - 1))),
                   pl.BlockSpec((tm, tn), lambda i, j: (i % tiles, (i // tiles) * col_tiles + jnp.maximum(j - rnn_tiles, 0)))]
                  + [pl.BlockSpec((tm, width), lambda i, j: (i, 0)) for _, width in side_cols],
        out_shape=[jax.ShapeDtypeStruct((t, bsz * rnn_cols), BF16), jax.ShapeDtypeStruct((t, bsz * n), BF16)]
                  + [jax.ShapeDtypeStruct((m, width), BF16) for _, width in side_cols],
        scratch_shapes=[pltpu.VMEM((tm, k), BF16)],
        compiler_params=_params(("parallel", "arbitrary")),
    )(x, gain, w)


def _rglru_kernel(u_ref, ug_ref, cw_ref, cb_ref, wa_ref, ba_ref, wx_ref, bx_ref, lam_ref,
                  y_ref, ubuf, a_scr, b_scr, hcar, *, tc, bsz):
    t = pl.program_id(1)
    rows = tc * bsz
    halo = (CONV_W - 1) * bsz

    @pl.when(t == 0)
    def _():
        ubuf[0:halo, :] = jnp.zeros((halo, RNN_BW), F32)
        hcar[...] = jnp.zeros_like(hcar)

    u = u_ref[...].astype(F32).reshape(rows, RNN_BW)
    ubuf[halo:halo + rows, :] = u
    xc = cb_ref[...]
    for k in range(CONV_W):
        xc = xc + cw_ref[k:k + 1, :] * ubuf[k * bsz:k * bsz + rows, :]
    ubuf[0:halo, :] = u[rows - halo:rows, :]

    xb = xc.astype(BF16)
    r = _sigmoid(_dot(xb, wa_ref[...]) + ba_ref[...])
    i = _sigmoid(_dot(xb, wx_ref[...]) + bx_ref[...])
    z = -lam_ref[...]
    softplus = jnp.maximum(z, 0.0) + jnp.log(1.0 + jnp.exp(-jnp.abs(z)))
    a = jnp.exp(r * ((-LRU_C) * softplus))
    mult = jnp.sqrt(1.0 - a * a)
    row = lax.broadcasted_iota(jnp.int32, (rows, 1), 0)
    mult = jnp.where((row < bsz) & (t == 0), 1.0, mult)
    a_scr[...] = a
    b_scr[...] = mult * (i * xc)

    def step(s, h):
        at = pl.ds(pl.multiple_of(s * bsz, bsz), bsz)
        h = a_scr[at, :] * h + b_scr[at, :]
        b_scr[at, :] = h
        return h

    hcar[...] = lax.fori_loop(0, tc, step, hcar[...], unroll=8)
    y = b_scr[...] * _gelu_tanh(ug_ref[...].astype(F32).reshape(rows, RNN_BW))
    y_ref[...] = y.reshape(tc, bsz, RNN_BW).astype(y_ref.dtype)


def _rglru(proj3, conv_w, conv_b, wa, ba, wx, bx, lam, tc):
    t, bsz, _ = proj3.shape
    assert bsz % 8 == 0, "the recurrence advances whole sublane groups of sequences"
    nb = LRU_BLOCKS
    rows = tc * bsz
    vec = pl.BlockSpec((1, RNN_BW), lambda n, s: (0, n))
    mat = pl.BlockSpec((None, RNN_BW, RNN_BW), lambda n, s: (n, 0, 0))
    return pl.pallas_call(
        functools.partial(_rglru_kernel, tc=tc, bsz=bsz),
        grid=(nb, t // tc),
        in_specs=[
            pl.BlockSpec((tc, bsz, RNN_BW), lambda n, s: (s, 0, COL_U // RNN_BW + n)),
            pl.BlockSpec((tc, bsz, RNN_BW), lambda n, s: (s, 0, COL_GATE // RNN_BW + n)),
            pl.BlockSpec((CONV_W, RNN_BW), lambda n, s: (0, n)),
            vec, mat, vec, mat, vec, vec,
        ],
        out_specs=pl.BlockSpec((tc, bsz, RNN_BW), lambda n, s: (s, 0, n)),
        out_shape=jax.ShapeDtypeStruct((t, bsz, RNN_W), BF16),
        scratch_shapes=[pltpu.VMEM((rows + (CONV_W - 1) * bsz, RNN_BW), F32),
                        pltpu.VMEM((rows, RNN_BW), F32), pltpu.VMEM((rows, RNN_BW), F32),
                        pltpu.VMEM((bsz, RNN_BW), F32)],
        compiler_params=_params(("parallel", "arbitrary")),
    )(proj3, proj3, conv_w, conv_b, wa, ba, wx, bx, lam)


def _compress_kernel(*refs, n_chunks, n_pairs, is_key):
    z_refs, (pe_ref, w1_ref, w2_ref), rest = refs[:n_pairs], refs[n_pairs:n_pairs + 3], refs[n_pairs + 3:]
    norm_ref, seg_ref, o_ref = rest if is_key else (None, None) + rest
    for pair, z_ref in enumerate(z_refs):
        z = z_ref[...].astype(F32)
        first = _dot((z + pe_ref[0:1, :]).astype(BF16), w1_ref[0])
        second = _dot((z + pe_ref[1:2, :]).astype(BF16), w1_ref[1])
        pre = first + pltpu.roll(second, n_chunks - 1, 0)
        out = _dot(_gelu_tanh(pre).astype(BF16), w2_ref[...])
        if is_key:
            ssq = _seg_sum(out * out, seg_ref[...])
            normed = out * lax.rsqrt(ssq * (1.0 / HEAD_DIM) + NORM_EPS) * norm_ref[...]
            low = lax.broadcasted_iota(jnp.int32, (n_chunks, LANES), 1) < HEAD_DIM
            o_ref[2 * pair] = jnp.where(low, normed, 0.0).astype(o_ref.dtype)
            o_ref[2 * pair + 1] = jnp.where(low, pltpu.roll(normed, HEAD_DIM, 1), 0.0).astype(o_ref.dtype)
        else:
            out_t = jnp.transpose(out)
            o_ref[2 * pair] = out_t[0:HEAD_DIM, :].astype(o_ref.dtype)
            o_ref[2 * pair + 1] = out_t[HEAD_DIM:LANES, :].astype(o_ref.dtype)


def _compress(z_pairs, pe, w1, w2, norm_and_seg=None):
    bsz, n_chunks, width = z_pairs[0].shape
    hid = w1.shape[-1]
    is_key = norm_and_seg is not None
    out_block = (N_KV_GROUPS, n_chunks, LANES) if is_key else (N_KV_GROUPS, HEAD_DIM, n_chunks)
    extra = norm_and_seg if is_key else ()
    return pl.pallas_call(
        functools.partial(_compress_kernel, n_chunks=n_chunks, n_pairs=len(z_pairs), is_key=is_key),
        grid=(bsz,),
        in_specs=[pl.BlockSpec((None, n_chunks, width), lambda b: (b, 0, 0))] * len(z_pairs) + [
            pl.BlockSpec((2, width), lambda b: (0, 0)),
            pl.BlockSpec((2, width, hid), lambda b: (0, 0, 0)),
            pl.BlockSpec((hid, LANES), lambda b: (0, 0)),
        ] + [pl.BlockSpec(a.shape, lambda b: (0, 0)) for a in extra],
        out_specs=pl.BlockSpec((None,) + out_block, lambda b: (b, 0, 0, 0)),
        out_shape=jax.ShapeDtypeStruct((bsz,) + out_block, BF16),
        compiler_params=_params(("parallel",)),
    )(*z_pairs, pe, w1, w2, *extra)


def _nsa_prep_kernel(q_ref, ks_ref, vs_ref, kw_ref, vw_ref, qn_ref, ksn_ref, kwn_ref, seg_ref,
                     qo_ref, ko_ref, vo_ref, *, tt):
    t0 = pl.program_id(1) * tt
    lane = lax.broadcasted_iota(jnp.int32, (tt, LANES), 1)
    row = lax.broadcasted_iota(jnp.int32, (tt, LANES), 0) + t0
    low = lane < HEAD_DIM
    seg = seg_ref[...]
    onehot = jnp.where((lane - SEL_ROW0) == row // SEL_BLOCK, 1.0, 0.0)

    def normed(ref, gain_ref, blk, scale):
        x = ref[:, blk * LANES:(blk + 1) * LANES].astype(F32)
        ssq = _seg_sum(x * x, seg)
        y = x * lax.rsqrt(ssq * (1.0 / HEAD_DIM) + NORM_EPS) * gain_ref[:, blk * LANES:(blk + 1) * LANES]
        return y * scale if scale != 1.0 else y

    q_pad = jnp.zeros((LANES - HEAD_DIM, tt), BF16)
    for blk in range(Q_W // LANES):
        y_t = jnp.transpose(normed(q_ref, qn_ref, blk, HEAD_DIM ** -0.5 * LOG2E)).astype(BF16)
        for half in range(2):
            qo_ref[2 * blk + half, 0:HEAD_DIM, :] = y_t[half * HEAD_DIM:(half + 1) * HEAD_DIM, :]
            qo_ref[2 * blk + half, HEAD_DIM:LANES, :] = q_pad
    ones_row = jnp.where(lax.broadcasted_iota(jnp.int32, (V_ROWS - HEAD_DIM, tt), 0) == 0, 1.0, 0.0).astype(BF16)
    for blk in range(KV_W // LANES):
        y = normed(ks_ref, ksn_ref, blk, 1.0)
        ysw = pltpu.roll(y, HEAD_DIM, 1)
        ko_ref[2 * blk, 0] = jnp.where(low, y, onehot).astype(BF16)
        ko_ref[2 * blk + 1, 0] = jnp.where(low, ysw, onehot).astype(BF16)
        y = normed(kw_ref, kwn_ref, blk, 1.0)
        ysw = pltpu.roll(y, HEAD_DIM, 1)
        ko_ref[2 * blk, 1] = jnp.where(low, y, 0.0).astype(BF16)
        ko_ref[2 * blk + 1, 1] = jnp.where(low, ysw, 0.0).astype(BF16)
        for branch, src in enumerate((vs_ref, vw_ref)):
            v_t = jnp.transpose(src[:, blk * LANES:(blk + 1) * LANES].astype(F32))
            for half in range(2):
                g = 2 * blk + half
                vo_ref[g, branch, 0:HEAD_DIM, :] = v_t[half * HEAD_DIM:(half + 1) * HEAD_DIM, :].astype(BF16)
                vo_ref[g, branch, HEAD_DIM:V_ROWS, :] = ones_row


def _nsa_prep(proj_t, bsz, qn, ksn, kwn, seg, tt):
    t = proj_t.shape[0]

    def col(width, offset):
        return pl.BlockSpec((tt, width), lambda b, s: (s, (b * D_PROJ + offset) // width))

    def vec(width):
        return pl.BlockSpec((1, width), lambda b, s: (0, 0))

    g = N_KV_GROUPS
    return pl.pallas_call(
        functools.partial(_nsa_prep_kernel, tt=tt),
        grid=(bsz, t // tt),
        in_specs=[col(Q_W, COL_Q), col(KV_W, COL_KS), col(KV_W, COL_VS), col(KV_W, COL_KW), col(KV_W, COL_VW),
                  vec(Q_W), vec(KV_W), vec(KV_W), pl.BlockSpec((LANES, LANES), lambda b, s: (0, 0))],
        out_specs=[pl.BlockSpec((None, N_HEADS, LANES, tt), lambda b, s: (b, 0, 0, s)),
                   pl.BlockSpec((None, g, 2, tt, LANES), lambda b, s: (b, 0, 0, s, 0)),
                   pl.BlockSpec((None, g, 2, V_ROWS, tt), lambda b, s: (b, 0, 0, 0, s))],
        out_shape=[jax.ShapeDtypeStruct((bsz, N_HEADS, LANES, t), BF16),
                   jax.ShapeDtypeStruct((bsz, g, 2, t, LANES), BF16),
                   jax.ShapeDtypeStruct((bsz, g, 2, V_ROWS, t), BF16)],
        compiler_params=_params(("parallel", "parallel")),
    )(proj_t, proj_t, proj_t, proj_t, proj_t, qn, ksn, kwn, seg)


def _nsa_kernel(q_ref, kc_ref, vc_ref, k_ref, v_ref, bc_ref, bias_ref, cover_ref, gate_ref, o_ref,
                qq_ref, alpha_ref, *scratch, tq, n_q, n_sblk):
    qi = pl.program_id(2)
    t0 = qi * tq
    rg = HEADS_PER_GROUP
    jpt = JOBS_PER_TRIP
    s_refs, smax_refs, p_refs = ((scratch[k:k + jpt], scratch[k + jpt:k + 2 * jpt]) for k in (0, 2 * jpt, 4 * jpt))
    m_refs, acc_refs = scratch[6 * jpt:6 * jpt + rg], scratch[6 * jpt + rg:]
    sel, win = 0, 1


    bias = jnp.concatenate([bc_ref[r] for r in range(rg)], axis=1)
    s = _dot(kc_ref[...], jnp.concatenate([q_ref[r] for r in range(rg)], axis=1)) + bias
    visible = bias > 0.5 * MASK_NEG
    p = jnp.exp2(s - jnp.max(s, axis=0, keepdims=True))
    p = jnp.where(visible, p / jnp.sum(p, axis=0, keepdims=True), 0.0)
    o_cmp_all = _dot(vc_ref[...], p.astype(BF16))
    o_cmp = [o_cmp_all[:, r * tq:(r + 1) * tq] for r in range(rg)]
    p_sum = p[:, 0:tq]
    for r in range(1, rg):
        p_sum = p_sum + p[:, r * tq:(r + 1) * tq]
    p_hi = p_sum.astype(BF16)
    p_lo = (p_sum - p_hi.astype(F32)).astype(BF16)
    imp_t = _dot(cover_ref[...], p_hi) + _dot(cover_ref[...], p_lo)

    score = imp_t[SEL_ROW0:SEL_ROW0 + MAX_SBLK, :]
    jrow = lax.broadcasted_iota(jnp.int32, (MAX_SBLK, tq), 0)
    qblk = (lax.broadcasted_iota(jnp.int32, (MAX_SBLK, tq), 1) + t0) // SEL_BLOCK
    causal = jrow <= qblk
    forced = causal & ((jrow == 0) | (jrow >= qblk - 1))
    score = jnp.where(forced, SEL_FORCED, jnp.where(causal, score, -1.0))
    rank = jnp.zeros((MAX_SBLK, tq), F32)
    for j in range(n_sblk):
        other = score[j:j + 1, :]
        ahead = (other > score) | ((other == score) & (jrow > j))
        rank = rank + jnp.where(ahead, 1.0, 0.0)
    n_top = min(N_SELECT, n_sblk)
    selected = (rank < n_top) & (score >= 0.0)
    neg = jnp.where(selected, 0.0, SEL_NEG).astype(BF16)
    for r in range(rg):
        qq_ref[r, 0:SEL_ROW0, :] = q_ref[r, 0:SEL_ROW0, :]
        qq_ref[r, SEL_ROW0:SEL_ROW0 + MAX_SBLK, :] = neg
        qq_ref[r, SEL_ROW0 + MAX_SBLK:LANES, :] = q_ref[r, SEL_ROW0 + MAX_SBLK:LANES, :]

    near_max = WINDOW // tq + 1
    plan = [(branch, a) for a in range(min(near_max, n_q)) for branch in (sel, win)]
    plan += [(sel, a) for a in range(near_max, n_q)]
    groups = [plan[g:g + JOBS_PER_TRIP] for g in range(0, len(plan), JOBS_PER_TRIP)]
    n_groups = sum((qi >= group[0][1]).astype(jnp.int32) for group in groups)

    def count(i):
        return sum((qi >= a).astype(jnp.int32) for _, a in groups[i])

    def count_options(i):
        return sorted({sum(1 for _, a in groups[i] if a <= top) for _, top in groups[i]})

    def jobs_of(i, c):
        return [(branch, pl.multiple_of((qi - a) * tq, tq),
                 (BIAS_DIAG, BIAS_NEAR, BIAS_WIN if branch == win else None)[min(a, 2)])
                for branch, a in groups[i][:c]]

    def scores(i, c):
        for (branch, k0, kind), s_ref, smax_ref in zip(jobs_of(i, c), s_refs[i % 2], smax_refs[i % 2]):
            k = k_ref[branch, pl.ds(k0, tq), :]
            for r in range(rg):
                q_t = qq_ref[r] if branch == sel else q_ref[r]
                s = _dot(k, q_t)
                if kind is not None:
                    s = s + bias_ref[r, kind]
                s_ref[r] = s
                smax_ref[r] = jnp.max(s, axis=0, keepdims=True)

    def softmax(i, c):
        for r in range(rg):
            for branch in sorted({b for b, _ in groups[i][:c]}):
                mine = [j for j, (b, _) in enumerate(groups[i][:c]) if b == branch]
                m_old = m_refs[r][branch]
                m_new = m_old
                for j in mine:
                    m_new = jnp.maximum(m_new, smax_refs[i % 2][j][r])
                for j in mine:
                    p_refs[i % 2][j][r] = jnp.exp2(s_refs[i % 2][j][r] - m_new).astype(BF16)
                m_refs[r][branch] = m_new
                alpha_ref[i % 2, r, branch] = jnp.exp2(m_old - m_new)

    def values(i, c):
        jobs = jobs_of(i, c)
        v_tiles = [v_ref[branch, :, pl.ds(k0, tq)] for branch, k0, _ in jobs]
        for r in range(rg):
            for branch in sorted({b for b, _, _ in jobs}):
                acc = None if i == 0 else alpha_ref[i % 2, r, branch] * acc_refs[r][branch]
                for j, (b, _, _) in enumerate(jobs):
                    if b == branch:
                        pv = _dot(v_tiles[j], p_refs[i % 2][j][r])
                        acc = pv if acc is None else acc + pv
                acc_refs[r][branch] = acc

    for r in range(rg):
        m_refs[r][...] = jnp.full(m_refs[r].shape, MASK_NEG, F32)
    assert count_options(0) == [len(groups[0])]
    assert sorted(b for b, _ in groups[0]) == [sel, win]
    scores(0, len(groups[0]))
    for i in range(len(groups)):
        full = len(groups[i])
        if i + 1 < len(groups):
            for c_next in count_options(i + 1):
                @pl.when((i + 1 < n_groups) & (count(i + 1) == c_next))
                def _(i=i, full=full, c_next=c_next):
                    if i > 0:
                        values(i - 1, len(groups[i - 1]))
                    softmax(i, full)
                    scores(i + 1, c_next)

        for c in count_options(i):
            @pl.when((i + 1 == n_groups) & (count(i) == c))
            def _(i=i, c=c):
                if i > 0:
                    values(i - 1, len(groups[i - 1]))
                softmax(i, c)
                values(i, c)

    def finish(branch):
        return [acc_refs[r][branch, 0:HEAD_DIM, :] / acc_refs[r][branch, HEAD_DIM:HEAD_DIM + 1, :] for r in range(rg)]

    o_sel = finish(sel)
    o_win = finish(win)

    gates_t = jnp.transpose(_sigmoid(gate_ref[...].astype(F32)))
    outs = []
    for r in range(rg):
        outs.append(gates_t[3 * r:3 * r + 1, :] * o_cmp[r]
                    + gates_t[3 * r + 1:3 * r + 2, :] * o_sel[r]
                    + gates_t[3 * r + 2:3 * r + 3, :] * o_win[r])
    o_ref[...] = jnp.transpose(jnp.concatenate(outs, axis=0)).astype(o_ref.dtype)


def _nsa_attention(qh, kc, vc, k_all, v_all, bias_c, bias_tiles, cover, proj3, tq):
    bsz, _, _, t = qh.shape
    n_chunks = kc.shape[2]
    rg = HEADS_PER_GROUP
    n_kinds = bias_tiles.shape[1]
    return pl.pallas_call(
        functools.partial(_nsa_kernel, tq=tq, n_q=t // tq, n_sblk=t // SEL_BLOCK),
        grid=(bsz, N_KV_GROUPS, t // tq),
        in_specs=[
            pl.BlockSpec((None, rg, LANES, tq), lambda b, g, i: (b, g, 0, i)),
            pl.BlockSpec((None, None, n_chunks, LANES), lambda b, g, i: (b, g, 0, 0)),
            pl.BlockSpec((None, None, HEAD_DIM, n_chunks), lambda b, g, i: (b, g, 0, 0)),
            pl.BlockSpec((None, None, 2, t, LANES), lambda b, g, i: (b, g, 0, 0, 0)),
            pl.BlockSpec((None, None, 2, V_ROWS, t), lambda b, g, i: (b, g, 0, 0, 0)),
            pl.BlockSpec((rg, n_chunks, tq), lambda b, g, i: (g, 0, i)),
            pl.BlockSpec((rg, n_kinds, tq, tq), lambda b, g, i: (g, 0, 0, 0)),
            pl.BlockSpec((LANES, n_chunks), lambda b, g, i: (0, 0)),
            pl.BlockSpec((tq, LANES), lambda b, g, i: (i, (b * D_PROJ + COL_GN) // LANES + g)),
        ],
        out_specs=pl.BlockSpec((None, tq, rg * HEAD_DIM), lambda b, g, i: (b, i, g)),
        out_shape=jax.ShapeDtypeStruct((bsz, t, Q_W), BF16),
        scratch_shapes=[
            pltpu.VMEM((rg, LANES, tq), BF16),
            pltpu.VMEM((2, rg, 2, 1, tq), F32),
        ] + [pltpu.VMEM((rg, tq, tq), F32)] * (2 * JOBS_PER_TRIP)
          + [pltpu.VMEM((rg, 1, tq), F32)] * (2 * JOBS_PER_TRIP)
          + [pltpu.VMEM((rg, tq, tq), BF16)] * (2 * JOBS_PER_TRIP)
          + [pltpu.VMEM((2, 1, tq), F32)] * rg
          + [pltpu.VMEM((2, V_ROWS, tq), F32)] * rg,
        compiler_params=_params(("parallel", "parallel", "arbitrary")),
    )(qh, kc, vc, k_all, v_all, bias_c, bias_tiles, cover, proj3)


def _merge_kernel(ya_ref, yb_ref, ga_ref, gb_ref, x_ref, pa_ref, pb_ref, wo_ref, h_ref):
    merged = (_sigmoid(ga_ref[...].astype(F32)) * _dot(ya_ref[...], pa_ref[...])
              + _sigmoid(gb_ref[...].astype(F32)) * _dot(yb_ref[...], pb_ref[...]))
    h_ref[...] = x_ref[...] + _dot(merged.astype(BF16), wo_ref[...])


def _merge_out(ya_t, yb, proj_t, x, pa, pb, wo, bsz, tm):
    t = ya_t.shape[0]
    tiles = t // tm

    def rows(width):
        return pl.BlockSpec((tm, width), lambda b, s: (b * tiles + s, 0))

    def time_major(width, offset):
        return pl.BlockSpec((tm, width), lambda b, s: (s, (b * D_PROJ + offset) // width))

    def whole(a):
        return pl.BlockSpec(a.shape, lambda b, s: (0, 0))

    return pl.pallas_call(
        _merge_kernel,
        grid=(bsz, tiles),
        in_specs=[pl.BlockSpec((tm, RNN_W), lambda b, s: (s, b)), rows(Q_W), time_major(D_MODEL, COL_GA),
                  time_major(D_MODEL, COL_GB), rows(D_MODEL), whole(pa), whole(pb), whole(wo)],
        out_specs=rows(D_MODEL),
        out_shape=jax.ShapeDtypeStruct((bsz * t, D_MODEL), F32),
        compiler_params=_params(("parallel", "parallel")),
    )(ya_t, yb, proj_t, proj_t, x, pa, pb, wo)


def _mlp_kernel(h_ref, g_ref, w1_ref, w2_ref, o_ref, hn_ref, acc_ref):
    j = pl.program_id(1)

    @pl.when(j == 0)
    def _():
        h = h_ref[...]
        y = h * lax.rsqrt(jnp.mean(h * h, axis=-1, keepdims=True) + NORM_EPS) * g_ref[...]
        hn_ref[...] = y.astype(BF16)
        acc_ref[...] = h

    z = jnp.maximum(_dot(hn_ref[...], w1_ref[...]), 0.0)
    acc_ref[...] += _dot((z * z).astype(BF16), w2_ref[...])

    @pl.when(j == pl.num_programs(1) - 1)
    def _():
        o_ref[...] = acc_ref[...]


def _mlp(h, gain, w1, w2, tm, tf):
    m, d = h.shape
    ff = w1.shape[1]
    return pl.pallas_call(
        _mlp_kernel,
        grid=(m // tm, ff // tf),
        in_specs=[
            pl.BlockSpec((tm, d), lambda i, j: (i, 0)),
            pl.BlockSpec((1, d), lambda i, j: (0, 0)),
            pl.BlockSpec((d, tf), lambda i, j: (0, j)),
            pl.BlockSpec((tf, d), lambda i, j: (j, 0)),
        ],
        out_specs=pl.BlockSpec((tm, d), lambda i, j: (i, 0)),
        out_shape=jax.ShapeDtypeStruct((m, d), F32),
        scratch_shapes=[pltpu.VMEM((tm, d), BF16), pltpu.VMEM((tm, d), F32)],
        compiler_params=_params(("parallel", "arbitrary")),
    )(h, gain, w1, w2)


def _t5_bucket_table():
    max_exact = REL_BUCKETS // 2
    d = np.arange(FAR_DIST)
    df = np.maximum(d.astype(np.float32), np.float32(1.0))
    large = max_exact + (np.log(df / np.float32(max_exact)) / np.float32(math.log(REL_MAX_DIST / max_exact))
                         * np.float32(REL_BUCKETS - max_exact)).astype(np.int32)
    large = np.minimum(large, REL_BUCKETS - 1)
    return np.where(d < max_exact, d, large).astype(np.int32)


def _pad_blocks(w, axis):
    shape = w.shape
    w = w.reshape(shape[:axis] + (LRU_BLOCKS, LRU_BLOCK_W) + shape[axis + 1:])
    pad = [(0, 0)] * w.ndim
    pad[axis + 1] = (0, RNN_BW - LRU_BLOCK_W)
    w = jnp.pad(w, pad)
    return w.reshape(shape[:axis] + (RNN_W,) + shape[axis + 1:])


def _in_proj_weight(w_in):
    cuts = np.cumsum((D_RNN, D_RNN, Q_W, KV_W, KV_W, KV_W, KV_W, KV_W, KV_W, 3 * N_HEADS, D_MODEL))
    w_in = w_in.astype(BF16)
    (w_u, w_gate, w_q, w_kc, w_vc, w_ks, w_vs, w_kw, w_vw, w_gn, w_ga, w_gb) = jnp.split(w_in, cuts, axis=1)
    per_group = 3 * HEADS_PER_GROUP
    w_gn = jnp.pad(w_gn.reshape(D_MODEL, N_KV_GROUPS, per_group), ((0, 0), (0, 0), (0, LANES - per_group)))
    w_gn = w_gn.reshape(D_MODEL, N_KV_GROUPS * LANES)
    gap = jnp.zeros((D_MODEL, COL_GA - COL_GN - N_KV_GROUPS * LANES), w_in.dtype)
    w = jnp.concatenate([_pad_blocks(w_u, 1), _pad_blocks(w_gate, 1), w_q, w_ks, w_vs, w_kw, w_vw, w_kc, w_vc,
                         w_gn, gap, w_ga, w_gb], axis=1)
    assert w.shape[1] == RNN_COLS + D_PROJ
    return w


def _phi_weights(pe, w1, w2):
    half, g = CMP_BLOCK // 2, LANES // HEAD_DIM
    eye = jnp.eye(g, dtype=BF16)
    w1h = w1.astype(BF16).reshape(2, half, HEAD_DIM, PHI_HIDDEN)
    w1e = jnp.einsum("xldh,gk->xlgdkh", w1h, eye).reshape(2, half * LANES, g * PHI_HIDDEN)
    w2e = jnp.einsum("hd,gk->ghkd", w2.astype(BF16), eye).reshape(g * PHI_HIDDEN, LANES)
    pee = jnp.broadcast_to(pe.reshape(2, half, 1, HEAD_DIM), (2, half, g, HEAD_DIM)).reshape(2, half * LANES)
    return pee, w1e, w2e


def kernel(x, norm_mix, w_in, conv_w, conv_b, gate_a_w, gate_a_b, gate_x_w, gate_x_b, lru_lambda, phi_k_pe, phi_k_w1, phi_k_w2, phi_v_pe, phi_v_w1, phi_v_w2, q_norm, kc_norm, ks_norm, kw_norm, rel_bias, proj_a, proj_b, w_out, norm_mlp, w_mlp_in, w_mlp_out):
    bsz, t, d = x.shape
    assert d == D_MODEL and norm_mix.shape[0] == 1
    tq = 256
    assert t % tq == 0 and t // SEL_BLOCK <= MAX_SBLK and t % CMP_STRIDE == 0
    n_tok = bsz * t
    n_chunks = t // CMP_STRIDE
    assert n_chunks % 8 == 0 and n_chunks <= LANES
    x2 = x.reshape(n_tok, d)

    side_cols = tuple((col + p * LANES, LANES) for col in (COL_KC, COL_VC) for p in range(KV_W // LANES))
    rnn_t, proj_t, *z_groups = _norm_matmul(x2, norm_mix, _in_proj_weight(w_in[0]), bsz,
                                            tm=1024 if t % 1024 == 0 else tq, tn=1024, rnn_cols=RNN_COLS,
                                            side_cols=side_cols)

    pad_w = lambda w: jnp.pad(w, ((0, 0), (0, RNN_BW - LRU_BLOCK_W), (0, RNN_BW - LRU_BLOCK_W))).astype(BF16)
    ya_t = _rglru(rnn_t.reshape(t, bsz, RNN_COLS), _pad_blocks(conv_w[0], 1), _pad_blocks(conv_b, 1),
                  pad_w(gate_a_w[0]), _pad_blocks(gate_a_b.reshape(1, D_RNN), 1),
                  pad_w(gate_x_w[0]), _pad_blocks(gate_x_b.reshape(1, D_RNN), 1),
                  _pad_blocks(lru_lambda, 1), tc=128)

    z_pairs = [z.reshape(bsz, n_chunks, CMP_STRIDE * LANES) for z in z_groups]
    n_pairs = KV_W // LANES
    seg128 = jnp.asarray(np.kron(np.eye(LANES // HEAD_DIM), np.ones((HEAD_DIM, HEAD_DIM))), BF16)
    kcn = jnp.tile(kc_norm, (1, LANES // HEAD_DIM))
    kc = _compress(z_pairs[:n_pairs], *_phi_weights(phi_k_pe[0], phi_k_w1[0], phi_k_w2[0]), (kcn, seg128))
    vc = _compress(z_pairs[n_pairs:], *_phi_weights(phi_v_pe[0], phi_v_w1[0], phi_v_w2[0]))

    qh, k_all, v_all = _nsa_prep(proj_t, bsz, jnp.tile(q_norm, (1, N_HEADS)), jnp.tile(ks_norm, (1, N_KV_GROUPS)),
                                 jnp.tile(kw_norm, (1, N_KV_GROUPS)), seg128, tt=512 if t % 512 == 0 else tq)

    bucket_of = _t5_bucket_table()
    far = rel_bias[REL_BUCKETS - 1][:, None, None]

    def bias_table(dist, valid, shift):
        buckets = jnp.asarray(bucket_of[np.clip(dist, 0, FAR_DIST - 1)].astype(np.int8))
        onehot = (buckets[None] == jnp.arange(REL_BUCKETS, dtype=jnp.int8).reshape(-1, 1, 1)).astype(F32)
        vals = jnp.einsum("kh,kji->hji", rel_bias, onehot, precision=lax.Precision.HIGHEST)
        if shift:
            vals = vals - far
        return jnp.where(jnp.asarray(valid), vals * LOG2E, MASK_NEG).astype(F32)

    kj = np.arange(tq)[:, None]
    qi_ = np.arange(tq)[None, :]
    bias_diag = bias_table(qi_ - kj, qi_ >= kj, True)
    bias_near = bias_table(tq + qi_ - kj, np.ones((tq, tq), bool), True)
    bias_win = jnp.broadcast_to(jnp.asarray(np.where(kj > qi_, 0.0, MASK_NEG), F32), (N_HEADS, tq, tq))
    bias_tiles = jnp.stack([bias_diag, bias_near, bias_win], axis=1)
    assert WINDOW == 2 * tq
    cidx = np.arange(n_chunks)[:, None]
    dist_c = np.arange(t)[None, :] - (cidx * CMP_STRIDE + CMP_BLOCK - 1)
    bias_c = bias_table(dist_c, (dist_c >= 0) & (cidx < n_chunks - 1), False)
    cstart = np.arange(n_chunks) * CMP_STRIDE
    sj = np.arange(MAX_SBLK)
    cov = ((cstart[None, :] < (sj[:, None] + 1) * SEL_BLOCK) & (cstart[None, :] + CMP_BLOCK - 1 >= sj[:, None] * SEL_BLOCK)
           & (np.arange(n_chunks)[None, :] < n_chunks - 1))
    cover = np.zeros((LANES, n_chunks), np.float32)
    cover[SEL_ROW0:SEL_ROW0 + MAX_SBLK] = cov
    yb = _nsa_attention(qh, kc, vc, k_all, v_all, bias_c, bias_tiles, jnp.asarray(cover, BF16), proj_t, tq)

    pa = jnp.pad(proj_a[0].reshape(LRU_BLOCKS, LRU_BLOCK_W, D_MODEL), ((0, 0), (0, RNN_BW - LRU_BLOCK_W), (0, 0)))
    h = _merge_out(ya_t.reshape(t, bsz * RNN_W), yb.reshape(n_tok, Q_W), proj_t, x2,
                   pa.reshape(RNN_W, D_MODEL).astype(BF16), proj_b[0].astype(BF16), w_out[0].astype(BF16),
                   bsz, tm=1024 if t % 1024 == 0 else tq)

    out = _mlp(h, norm_mlp, w_mlp_in[0].astype(BF16), w_mlp_out[0].astype(BF16),
               tm=1024 if n_tok % 1024 == 0 else tq, tf=1024)
    return out.reshape(bsz, t, d)
```

```python
import functools
import math

import numpy as np
import jax
import jax.numpy as jnp
from jax import lax
from jax.experimental import pallas as pl
from jax.experimental.pallas import tpu as pltpu

F32 = jnp.float32
BF16 = jnp.bfloat16

D_MODEL = 1024
D_RNN = 1344
LRU_BLOCKS = 4
LRU_BLOCK_W = D_RNN // LRU_BLOCKS
CONV_W = 4
LRU_C = 8.0
N_HEADS = 16
HEAD_DIM = 64
N_KV_GROUPS = 4
HEADS_PER_GROUP = N_HEADS // N_KV_GROUPS
CMP_BLOCK = 32
CMP_STRIDE = 16
SEL_BLOCK = 64
N_SELECT = 16
WINDOW = 512
PHI_HIDDEN = 256
SEL_FORCED = 1e4
REL_BUCKETS = 32
REL_MAX_DIST = 128
D_FF = 4 * D_MODEL
NORM_EPS = 1e-6
Q_W = N_HEADS * HEAD_DIM
KV_W = N_KV_GROUPS * HEAD_DIM

LANES = 128
VMEM_LIMIT = 56 * 1024 * 1024

RNN_BW = 384
RNN_W = LRU_BLOCKS * RNN_BW
MASK_NEG = -1e30
SEL_NEG = -1e9
SEL_ROW0 = 64
MAX_SBLK = 32
FAR_DIST = 256
LOG2E = math.log2(math.e)
V_ROWS = HEAD_DIM + 16
BIAS_DIAG, BIAS_NEAR, BIAS_WIN = range(3)
JOBS_PER_TRIP = 2

COL_U = 0
COL_GATE = RNN_W
RNN_COLS = 2 * RNN_W
COL_Q = 0
COL_KS = COL_Q + Q_W
COL_VS = COL_KS + KV_W
COL_KW = COL_VS + KV_W
COL_VW = COL_KW + KV_W
COL_KC = COL_VW + KV_W
COL_VC = COL_KC + KV_W
COL_GN = COL_VC + KV_W
COL_GA = 3072
COL_GB = 4096
D_PROJ = 5120


def _dot(a, b):
    return jnp.dot(a, b, preferred_element_type=F32)


def _gelu_tanh(x):
    return 0.5 * x * (1.0 + jnp.tanh(math.sqrt(2.0 / math.pi) * (x + 0.044715 * (x * x * x))))


def _sigmoid(x):
    return 0.5 * jnp.tanh(0.5 * x) + 0.5


def _seg_sum(x, seg_ones):
    hi = x.astype(BF16)
    lo = (x - hi.astype(F32)).astype(BF16)
    return _dot(hi, seg_ones) + _dot(lo, seg_ones)


def _params(sem, flags=None):
    return pltpu.CompilerParams(dimension_semantics=sem, vmem_limit_bytes=VMEM_LIMIT, flags=flags)


def _norm_matmul_kernel(x_ref, g_ref, w_ref, rnn_ref, o_ref, *rest, tn, rnn_tiles, side_cols):
    side_refs, xn_ref = rest[:-1], rest[-1]
    j = pl.program_id(1)

    @pl.when(j == 0)
    def _():
        x = x_ref[...]
        y = x * lax.rsqrt(jnp.mean(x * x, axis=-1, keepdims=True) + NORM_EPS) * g_ref[...]
        xn_ref[...] = y.astype(BF16)

    res = _dot(xn_ref[...], w_ref[...]).astype(o_ref.dtype)

    @pl.when(j < rnn_tiles)
    def _():
        rnn_ref[...] = res

    @pl.when(j >= rnn_tiles)
    def _():
        o_ref[...] = res

    for (col, width), side_ref in zip(side_cols, side_refs):
        @pl.when(j == rnn_tiles + col // tn)
        def _(col=col, width=width, side_ref=side_ref):
            side_ref[...] = res[:, col % tn:col % tn + width]


def _norm_matmul(x, gain, w, bsz, tm, tn, rnn_cols, side_cols):
    m, k = x.shape
    n = w.shape[1] - rnn_cols
    t = m // bsz
    tiles = t // tm
    rnn_tiles, col_tiles = rnn_cols // tn, n // tn
    assert all(col // tn == (col + width - 1) // tn for col, width in side_cols)
    return pl.pallas_call(
        functools.partial(_norm_matmul_kernel, tn=tn, rnn_tiles=rnn_tiles, side_cols=side_cols),
        grid=(m // tm, rnn_tiles + col_tiles),
        in_specs=[
            pl.BlockSpec((tm, k), lambda i, j: (i, 0)),
            pl.BlockSpec((1, k), lambda i, j: (0, 0)),
            pl.BlockSpec((k, tn), lambda i, j: (0, j)),
        ],
        out_specs=[pl.BlockSpec((tm, tn), lambda i, j: (i % tiles, (i // tiles) * rnn_tiles + jnp.minimum(j, rnn_tiles - 1))),
                   pl.BlockSpec((tm, tn), lambda i, j: (i % tiles, (i // tiles) * col_tiles + jnp.maximum(j - rnn_tiles, 0)))]
                  + [pl.BlockSpec((tm, width), lambda i, j: (i, 0)) for _, width in side_cols],
        out_shape=[jax.ShapeDtypeStruct((t, bsz * rnn_cols), BF16), jax.ShapeDtypeStruct((t, bsz * n), BF16)]
                  + [jax.ShapeDtypeStruct((m, width), BF16) for _, width in side_cols],
        scratch_shapes=[pltpu.VMEM((tm, k), BF16)],
        compiler_params=_params(("parallel", "arbitrary")),
    )(x, gain, w)


def _rglru_kernel(u_ref, ug_ref, cw_ref, cb_ref, wa_ref, ba_ref, wx_ref, bx_ref, lam_ref,
                  y_ref, ubuf, a_scr, b_scr, hcar, *, tc, bsz):
    t = pl.program_id(1)
    rows = tc * bsz
    halo = (CONV_W - 1) * bsz

    @pl.when(t == 0)
    def _():
        ubuf[0:halo, :] = jnp.zeros((halo, RNN_BW), F32)
        hcar[...] = jnp.zeros_like(hcar)

    u = u_ref[...].astype(F32).reshape(rows, RNN_BW)
    ubuf[halo:halo + rows, :] = u
    xc = cb_ref[...]
    for k in range(CONV_W):
        xc = xc + cw_ref[k:k + 1, :] * ubuf[k * bsz:k * bsz + rows, :]
    ubuf[0:halo, :] = u[rows - halo:rows, :]

    xb = xc.astype(BF16)
    r = _sigmoid(_dot(xb, wa_ref[...]) + ba_ref[...])
    i = _sigmoid(_dot(xb, wx_ref[...]) + bx_ref[...])
    z = -lam_ref[...]
    softplus = jnp.maximum(z, 0.0) + jnp.log(1.0 + jnp.exp(-jnp.abs(z)))
    a = jnp.exp(r * ((-LRU_C) * softplus))
    mult = jnp.sqrt(1.0 - a * a)
    row = lax.broadcasted_iota(jnp.int32, (rows, 1), 0)
    mult = jnp.where((row < bsz) & (t == 0), 1.0, mult)
    a_scr[...] = a
    b_scr[...] = mult * (i * xc)

    def step(s, h):
        at = pl.ds(pl.multiple_of(s * bsz, bsz), bsz)
        h = a_scr[at, :] * h + b_scr[at, :]
        b_scr[at, :] = h
        return h

    hcar[...] = lax.fori_loop(0, tc, step, hcar[...], unroll=8)
    y = b_scr[...] * _gelu_tanh(ug_ref[...].astype(F32).reshape(rows, RNN_BW))
    y_ref[...] = y.reshape(tc, bsz, RNN_BW).astype(y_ref.dtype)


def _rglru(proj3, conv_w, conv_b, wa, ba, wx, bx, lam, tc):
    t, bsz, _ = proj3.shape
    assert bsz % 8 == 0, "the recurrence advances whole sublane groups of sequences"
    nb = LRU_BLOCKS
    rows = tc * bsz
    vec = pl.BlockSpec((1, RNN_BW), lambda n, s: (0, n))
    mat = pl.BlockSpec((None, RNN_BW, RNN_BW), lambda n, s: (n, 0, 0))
    return pl.pallas_call(
        functools.partial(_rglru_kernel, tc=tc, bsz=bsz),
        grid=(nb, t // tc),
        in_specs=[
            pl.BlockSpec((tc, bsz, RNN_BW), lambda n, s: (s, 0, COL_U // RNN_BW + n)),
            pl.BlockSpec((tc, bsz, RNN_BW), lambda n, s: (s, 0, COL_GATE // RNN_BW + n)),
            pl.BlockSpec((CONV_W, RNN_BW), lambda n, s: (0, n)),
            vec, mat, vec, mat, vec, vec,
        ],
        out_specs=pl.BlockSpec((tc, bsz, RNN_BW), lambda n, s: (s, 0, n)),
        out_shape=jax.ShapeDtypeStruct((t, bsz, RNN_W), BF16),
        scratch_shapes=[pltpu.VMEM((rows + (CONV_W - 1) * bsz, RNN_BW), F32),
                        pltpu.VMEM((rows, RNN_BW), F32), pltpu.VMEM((rows, RNN_BW), F32),
                        pltpu.VMEM((bsz, RNN_BW), F32)],
        compiler_params=_params(("parallel", "arbitrary")),
    )(proj3, proj3, conv_w, conv_b, wa, ba, wx, bx, lam)


def _compress_kernel(*refs, n_chunks, n_pairs, is_key):
    z_refs, (pe_ref, w1_ref, w2_ref), rest = refs[:n_pairs], refs[n_pairs:n_pairs + 3], refs[n_pairs + 3:]
    norm_ref, seg_ref, o_ref = rest if is_key else (None, None) + rest
    for pair, z_ref in enumerate(z_refs):
        z = z_ref[...].astype(F32)
        first = _dot((z + pe_ref[0:1, :]).astype(BF16), w1_ref[0])
        second = _dot((z + pe_ref[1:2, :]).astype(BF16), w1_ref[1])
        pre = first + pltpu.roll(second, n_chunks - 1, 0)
        out = _dot(_gelu_tanh(pre).astype(BF16), w2_ref[...])
        if is_key:
            ssq = _seg_sum(out * out, seg_ref[...])
            normed = out * lax.rsqrt(ssq * (1.0 / HEAD_DIM) + NORM_EPS) * norm_ref[...]
            low = lax.broadcasted_iota(jnp.int32, (n_chunks, LANES), 1) < HEAD_DIM
            o_ref[2 * pair] = jnp.where(low, normed, 0.0).astype(o_ref.dtype)
            o_ref[2 * pair + 1] = jnp.where(low, pltpu.roll(normed, HEAD_DIM, 1), 0.0).astype(o_ref.dtype)
        else:
            out_t = jnp.transpose(out)
            o_ref[2 * pair] = out_t[0:HEAD_DIM, :].astype(o_ref.dtype)
            o_ref[2 * pair + 1] = out_t[HEAD_DIM:LANES, :].astype(o_ref.dtype)


def _compress(z_pairs, pe, w1, w2, norm_and_seg=None):
    bsz, n_chunks, width = z_pairs[0].shape
    hid = w1.shape[-1]
    is_key = norm_and_seg is not None
    out_block = (N_KV_GROUPS, n_chunks, LANES) if is_key else (N_KV_GROUPS, HEAD_DIM, n_chunks)
    extra = norm_and_seg if is_key else ()
    return pl.pallas_call(
        functools.partial(_compress_kernel, n_chunks=n_chunks, n_pairs=len(z_pairs), is_key=is_key),
        grid=(bsz,),
        in_specs=[pl.BlockSpec((None, n_chunks, width), lambda b: (b, 0, 0))] * len(z_pairs) + [
            pl.BlockSpec((2, width), lambda b: (0, 0)),
            pl.BlockSpec((2, width, hid), lambda b: (0, 0, 0)),
            pl.BlockSpec((hid, LANES), lambda b: (0, 0)),
        ] + [pl.BlockSpec(a.shape, lambda b: (0, 0)) for a in extra],
        out_specs=pl.BlockSpec((None,) + out_block, lambda b: (b, 0, 0, 0)),
        out_shape=jax.ShapeDtypeStruct((bsz,) + out_block, BF16),
        compiler_params=_params(("parallel",)),
    )(*z_pairs, pe, w1, w2, *extra)


def _nsa_prep_kernel(q_ref, ks_ref, vs_ref, kw_ref, vw_ref, qn_ref, ksn_ref, kwn_ref, seg_ref,
                     qo_ref, ko_ref, vo_ref, *, tt):
    t0 = pl.program_id(1) * tt
    lane = lax.broadcasted_iota(jnp.int32, (tt, LANES), 1)
    row = lax.broadcasted_iota(jnp.int32, (tt, LANES), 0) + t0
    low = lane < HEAD_DIM
    seg = seg_ref[...]
    onehot = jnp.where((lane - SEL_ROW0) == row // SEL_BLOCK, 1.0, 0.0)

    def normed(ref, gain_ref, blk, scale):
        x = ref[:, blk * LANES:(blk + 1) * LANES].astype(F32)
        ssq = _seg_sum(x * x, seg)
        y = x * lax.rsqrt(ssq * (1.0 / HEAD_DIM) + NORM_EPS) * gain_ref[:, blk * LANES:(blk + 1) * LANES]
        return y * scale if scale != 1.0 else y

    q_pad = jnp.zeros((LANES - HEAD_DIM, tt), BF16)
    for blk in range(Q_W // LANES):
        y_t = jnp.transpose(normed(q_ref, qn_ref, blk, HEAD_DIM ** -0.5 * LOG2E)).astype(BF16)
        for half in range(2):
            qo_ref[2 * blk + half, 0:HEAD_DIM, :] = y_t[half * HEAD_DIM:(half + 1) * HEAD_DIM, :]
            qo_ref[2 * blk + half, HEAD_DIM:LANES, :] = q_pad
    ones_row = jnp.where(lax.broadcasted_iota(jnp.int32, (V_ROWS - HEAD_DIM, tt), 0) == 0, 1.0, 0.0).astype(BF16)
    for blk in range(KV_W // LANES):
        y = normed(ks_ref, ksn_ref, blk, 1.0)
        ysw = pltpu.roll(y, HEAD_DIM, 1)
        ko_ref[2 * blk, 0] = jnp.where(low, y, onehot).astype(BF16)
        ko_ref[2 * blk + 1, 0] = jnp.where(low, ysw, onehot).astype(BF16)
        y = normed(kw_ref, kwn_ref, blk, 1.0)
        ysw = pltpu.roll(y, HEAD_DIM, 1)
        ko_ref[2 * blk, 1] = jnp.where(low, y, 0.0).astype(BF16)
        ko_ref[2 * blk + 1, 1] = jnp.where(low, ysw, 0.0).astype(BF16)
        for branch, src in enumerate((vs_ref, vw_ref)):
            v_t = jnp.transpose(src[:, blk * LANES:(blk + 1) * LANES].astype(F32))
            for half in range(2):
                g = 2 * blk + half
                vo_ref[g, branch, 0:HEAD_DIM, :] = v_t[half * HEAD_DIM:(half + 1) * HEAD_DIM, :].astype(BF16)
                vo_ref[g, branch, HEAD_DIM:V_ROWS, :] = ones_row


def _nsa_prep(proj_t, bsz, qn, ksn, kwn, seg, tt):
    t = proj_t.shape[0]

    def col(width, offset):
        return pl.BlockSpec((tt, width), lambda b, s: (s, (b * D_PROJ + offset) // width))

    def vec(width):
        return pl.BlockSpec((1, width), lambda b, s: (0, 0))

    g = N_KV_GROUPS
    return pl.pallas_call(
        functools.partial(_nsa_prep_kernel, tt=tt),
        grid=(bsz, t // tt),
        in_specs=[col(Q_W, COL_Q), col(KV_W, COL_KS), col(KV_W, COL_VS), col(KV_W, COL_KW), col(KV_W, COL_VW),
                  vec(Q_W), vec(KV_W), vec(KV_W), pl.BlockSpec((LANES, LANES), lambda b, s: (0, 0))],
        out_specs=[pl.BlockSpec((None, N_HEADS, LANES, tt), lambda b, s: (b, 0, 0, s)),
                   pl.BlockSpec((None, g, 2, tt, LANES), lambda b, s: (b, 0, 0, s, 0)),
                   pl.BlockSpec((None, g, 2, V_ROWS, tt), lambda b, s: (b, 0, 0, 0, s))],
        out_shape=[jax.ShapeDtypeStruct((bsz, N_HEADS, LANES, t), BF16),
                   jax.ShapeDtypeStruct((bsz, g, 2, t, LANES), BF16),
                   jax.ShapeDtypeStruct((bsz, g, 2, V_ROWS, t), BF16)],
        compiler_params=_params(("parallel", "parallel")),
    )(proj_t, proj_t, proj_t, proj_t, proj_t, qn, ksn, kwn, seg)


def _nsa_kernel(q_ref, kc_ref, vc_ref, k_ref, v_ref, bc_ref, bias_ref, cover_ref, gate_ref, o_ref,
                qq_ref, alpha_ref, *scratch, tq, n_q, n_sblk):
    qi = pl.program_id(2)
    t0 = qi * tq
    rg = HEADS_PER_GROUP
    jpt = JOBS_PER_TRIP
    s_refs, smax_refs, p_refs = ((scratch[k:k + jpt], scratch[k + jpt:k + 2 * jpt]) for k in (0, 2 * jpt, 4 * jpt))
    m_refs, acc_refs = scratch[6 * jpt:6 * jpt + rg], scratch[6 * jpt + rg:]
    sel, win = 0, 1


    bias = jnp.concatenate([bc_ref[r] for r in range(rg)], axis=1)
    s = _dot(kc_ref[...], jnp.concatenate([q_ref[r] for r in range(rg)], axis=1)) + bias
    visible = bias > 0.5 * MASK_NEG
    p = jnp.exp2(s - jnp.max(s, axis=0, keepdims=True))
    p = jnp.where(visible, p / jnp.sum(p, axis=0, keepdims=True), 0.0)
    o_cmp_all = _dot(vc_ref[...], p.astype(BF16))
    o_cmp = [o_cmp_all[:, r * tq:(r + 1) * tq] for r in range(rg)]
    p_sum = p[:, 0:tq]
    for r in range(1, rg):
        p_sum = p_sum + p[:, r * tq:(r + 1) * tq]
    p_hi = p_sum.astype(BF16)
    p_lo = (p_sum - p_hi.astype(F32)).astype(BF16)
    imp_t = _dot(cover_ref[...], p_hi) + _dot(cover_ref[...], p_lo)

    score = imp_t[SEL_ROW0:SEL_ROW0 + MAX_SBLK, :]
    jrow = lax.broadcasted_iota(jnp.int32, (MAX_SBLK, tq), 0)
    qblk = (lax.broadcasted_iota(jnp.int32, (MAX_SBLK, tq), 1) + t0) // SEL_BLOCK
    causal = jrow <= qblk
    forced = causal & ((jrow == 0) | (jrow >= qblk - 1))
    score = jnp.where(forced, SEL_FORCED, jnp.where(causal, score, -1.0))
    rank = jnp.zeros((MAX_SBLK, tq), F32)
    for j in range(n_sblk):
        other = score[j:j + 1, :]
        ahead = (other > score) | ((other == score) & (jrow > j))
        rank = rank + jnp.where(ahead, 1.0, 0.0)
    n_top = min(N_SELECT, n_sblk)
    selected = (rank < n_top) & (score >= 0.0)
    neg = jnp.where(selected, 0.0, SEL_NEG).astype(BF16)
    for r in range(rg):
        qq_ref[r, 0:SEL_ROW0, :] = q_ref[r, 0:SEL_ROW0, :]
        qq_ref[r, SEL_ROW0:SEL_ROW0 + MAX_SBLK, :] = neg
        qq_ref[r, SEL_ROW0 + MAX_SBLK:LANES, :] = q_ref[r, SEL_ROW0 + MAX_SBLK:LANES, :]

    near_max = WINDOW // tq + 1
    plan = [(branch, a) for a in range(min(near_max, n_q)) for branch in (sel, win)]
    plan += [(sel, a) for a in range(near_max, n_q)]
    groups = [plan[g:g + JOBS_PER_TRIP] for g in range(0, len(plan), JOBS_PER_TRIP)]
    n_groups = sum((qi >= group[0][1]).astype(jnp.int32) for group in groups)

    def count(i):
        return sum((qi >= a).astype(jnp.int32) for _, a in groups[i])

    def count_options(i):
        return sorted({sum(1 for _, a in groups[i] if a <= top) for _, top in groups[i]})

    def jobs_of(i, c):
        return [(branch, pl.multiple_of((qi - a) * tq, tq),
                 (BIAS_DIAG, BIAS_NEAR, BIAS_WIN if branch == win else None)[min(a, 2)])
                for branch, a in groups[i][:c]]

    def scores(i, c):
        for (branch, k0, kind), s_ref, smax_ref in zip(jobs_of(i, c), s_refs[i % 2], smax_refs[i % 2]):
            k = k_ref[branch, pl.ds(k0, tq), :]
            for r in range(rg):
                q_t = qq_ref[r] if branch == sel else q_ref[r]
                s = _dot(k, q_t)
                if kind is not None:
                    s = s + bias_ref[r, kind]
                s_ref[r] = s
                smax_ref[r] = jnp.max(s, axis=0, keepdims=True)

    def softmax(i, c):
        for r in range(rg):
            for branch in sorted({b for b, _ in groups[i][:c]}):
                mine = [j for j, (b, _) in enumerate(groups[i][:c]) if b == branch]
                m_old = m_refs[r][branch]
                m_new = m_old
                for j in mine:
                    m_new = jnp.maximum(m_new, smax_refs[i % 2][j][r])
                for j in mine:
                    p_refs[i % 2][j][r] = jnp.exp2(s_refs[i % 2][j][r] - m_new).astype(BF16)
                m_refs[r][branch] = m_new
                alpha_ref[i % 2, r, branch] = jnp.exp2(m_old - m_new)

    def values(i, c):
        jobs = jobs_of(i, c)
        v_tiles = [v_ref[branch, :, pl.ds(k0, tq)] for branch, k0, _ in jobs]
        for r in range(rg):
            for branch in sorted({b for b, _, _ in jobs}):
                acc = None if i == 0 else alpha_ref[i % 2, r, branch] * acc_refs[r][branch]
                for j, (b, _, _) in enumerate(jobs):
                    if b == branch:
                        pv = _dot(v_tiles[j], p_refs[i % 2][j][r])
                        acc = pv if acc is None else acc + pv
                acc_refs[r][branch] = acc

    for r in range(rg):
        m_refs[r][...] = jnp.full(m_refs[r].shape, MASK_NEG, F32)
    assert count_options(0) == [len(groups[0])]
    assert sorted(b for b, _ in groups[0]) == [sel, win]
    scores(0, len(groups[0]))
    for i in range(len(groups)):
        full = len(groups[i])
        if i + 1 < len(groups):
            for c_next in count_options(i + 1):
                @pl.when((i + 1 < n_groups) & (count(i + 1) == c_next))
                def _(i=i, full=full, c_next=c_next):
                    if i > 0:
                        values(i - 1, len(groups[i - 1]))
                    softmax(i, full)
                    scores(i + 1, c_next)

        for c in count_options(i):
            @pl.when((i + 1 == n_groups) & (count(i) == c))
            def _(i=i, c=c):
                if i > 0:
                    values(i - 1, len(groups[i - 1]))
                softmax(i, c)
                values(i, c)

    def finish(branch):
        return [acc_refs[r][branch, 0:HEAD_DIM, :] / acc_refs[r][branch, HEAD_DIM:HEAD_DIM + 1, :] for r in range(rg)]

    o_sel = finish(sel)
    o_win = finish(win)

    gates_t = jnp.transpose(_sigmoid(gate_ref[...].astype(F32)))
    outs = []
    for r in range(rg):
        outs.append(gates_t[3 * r:3 * r + 1, :] * o_cmp[r]
                    + gates_t[3 * r + 1:3 * r + 2, :] * o_sel[r]
                    + gates_t[3 * r + 2:3 * r + 3, :] * o_win[r])
    o_ref[...] = jnp.transpose(jnp.concatenate(outs, axis=0)).astype(o_ref.dtype)


def _nsa_attention(qh, kc, vc, k_all, v_all, bias_c, bias_tiles, cover, proj3, tq):
    bsz, _, _, t = qh.shape
    n_chunks = kc.shape[2]
    rg = HEADS_PER_GROUP
    n_kinds = bias_tiles.shape[1]
    return pl.pallas_call(
        functools.partial(_nsa_kernel, tq=tq, n_q=t // tq, n_sblk=t // SEL_BLOCK),
        grid=(bsz, N_KV_GROUPS, t // tq),
        in_specs=[
            pl.BlockSpec((None, rg, LANES, tq), lambda b, g, i: (b, g, 0, i)),
            pl.BlockSpec((None, None, n_chunks, LANES), lambda b, g, i: (b, g, 0, 0)),
            pl.BlockSpec((None, None, HEAD_DIM, n_chunks), lambda b, g, i: (b, g, 0, 0)),
            pl.BlockSpec((None, None, 2, t, LANES), lambda b, g, i: (b, g, 0, 0, 0)),
            pl.BlockSpec((None, None, 2, V_ROWS, t), lambda b, g, i: (b, g, 0, 0, 0)),
            pl.BlockSpec((rg, n_chunks, tq), lambda b, g, i: (g, 0, i)),
            pl.BlockSpec((rg, n_kinds, tq, tq), lambda b, g, i: (g, 0, 0, 0)),
            pl.BlockSpec((LANES, n_chunks), lambda b, g, i: (0, 0)),
            pl.BlockSpec((tq, LANES), lambda b, g, i: (i, (b * D_PROJ + COL_GN) // LANES + g)),
        ],
        out_specs=pl.BlockSpec((None, tq, rg * HEAD_DIM), lambda b, g, i: (b, i, g)),
        out_shape=jax.ShapeDtypeStruct((bsz, t, Q_W), BF16),
        scratch_shapes=[
            pltpu.VMEM((rg, LANES, tq), BF16),
            pltpu.VMEM((2, rg, 2, 1, tq), F32),
        ] + [pltpu.VMEM((rg, tq, tq), F32)] * (2 * JOBS_PER_TRIP)
          + [pltpu.VMEM((rg, 1, tq), F32)] * (2 * JOBS_PER_TRIP)
          + [pltpu.VMEM((rg, tq, tq), BF16)] * (2 * JOBS_PER_TRIP)
          + [pltpu.VMEM((2, 1, tq), F32)] * rg
          + [pltpu.VMEM((2, V_ROWS, tq), F32)] * rg,
        compiler_params=_params(("parallel", "parallel", "arbitrary")),
    )(qh, kc, vc, k_all, v_all, bias_c, bias_tiles, cover, proj3)


def _merge_kernel(ya_ref, yb_ref, ga_ref, gb_ref, x_ref, pa_ref, pb_ref, wo_ref, h_ref):
    merged = (_sigmoid(ga_ref[...].astype(F32)) * _dot(ya_ref[...], pa_ref[...])
              + _sigmoid(gb_ref[...].astype(F32)) * _dot(yb_ref[...], pb_ref[...]))
    h_ref[...] = x_ref[...] + _dot(merged.astype(BF16), wo_ref[...])


def _merge_out(ya_t, yb, proj_t, x, pa, pb, wo, bsz, tm):
    t = ya_t.shape[0]
    tiles = t // tm

    def rows(width):
        return pl.BlockSpec((tm, width), lambda b, s: (b * tiles + s, 0))

    def time_major(width, offset):
        return pl.BlockSpec((tm, width), lambda b, s: (s, (b * D_PROJ + offset) // width))

    def whole(a):
        return pl.BlockSpec(a.shape, lambda b, s: (0, 0))

    return pl.pallas_call(
        _merge_kernel,
        grid=(bsz, tiles),
        in_specs=[pl.BlockSpec((tm, RNN_W), lambda b, s: (s, b)), rows(Q_W), time_major(D_MODEL, COL_GA),
                  time_major(D_MODEL, COL_GB), rows(D_MODEL), whole(pa), whole(pb), whole(wo)],
        out_specs=rows(D_MODEL),
        out_shape=jax.ShapeDtypeStruct((bsz * t, D_MODEL), F32),
        compiler_params=_params(("parallel", "parallel")),
    )(ya_t, yb, proj_t, proj_t, x, pa, pb, wo)


def _mlp_kernel(h_ref, g_ref, w1_ref, w2_ref, o_ref, hn_ref, acc_ref):
    j = pl.program_id(1)

    @pl.when(j == 0)
    def _():
        h = h_ref[...]
        y = h * lax.rsqrt(jnp.mean(h * h, axis=-1, keepdims=True) + NORM_EPS) * g_ref[...]
        hn_ref[...] = y.astype(BF16)
        acc_ref[...] = h

    z = jnp.maximum(_dot(hn_ref[...], w1_ref[...]), 0.0)
    acc_ref[...] += _dot((z * z).astype(BF16), w2_ref[...])

    @pl.when(j == pl.num_programs(1) - 1)
    def _():
        o_ref[...] = acc_ref[...]


def _mlp(h, gain, w1, w2, tm, tf):
    m, d = h.shape
    ff = w1.shape[1]
    return pl.pallas_call(
        _mlp_kernel,
        grid=(m // tm, ff // tf),
        in_specs=[
            pl.BlockSpec((tm, d), lambda i, j: (i, 0)),
            pl.BlockSpec((1, d), lambda i, j: (0, 0)),
            pl.BlockSpec((d, tf), lambda i, j: (0, j)),
            pl.BlockSpec((tf, d), lambda i, j: (j, 0)),
        ],
        out_specs=pl.BlockSpec((tm, d), lambda i, j: (i, 0)),
        out_shape=jax.ShapeDtypeStruct((m, d), F32),
        scratch_shapes=[pltpu.VMEM((tm, d), BF16), pltpu.VMEM((tm, d), F32)],
        compiler_params=_params(("parallel", "arbitrary")),
    )(h, gain, w1, w2)


def _t5_bucket_table():
    max_exact = REL_BUCKETS // 2
    d = np.arange(FAR_DIST)
    df = np.maximum(d.astype(np.float32), np.float32(1.0))
    large = max_exact + (np.log(df / np.float32(max_exact)) / np.float32(math.log(REL_MAX_DIST / max_exact))
                         * np.float32(REL_BUCKETS - max_exact)).astype(np.int32)
    large = np.minimum(large, REL_BUCKETS - 1)
    return np.where(d < max_exact, d, large).astype(np.int32)


def _pad_blocks(w, axis):
    shape = w.shape
    w = w.reshape(shape[:axis] + (LRU_BLOCKS, LRU_BLOCK_W) + shape[axis + 1:])
    pad = [(0, 0)] * w.ndim
    pad[axis + 1] = (0, RNN_BW - LRU_BLOCK_W)
    w = jnp.pad(w, pad)
    return w.reshape(shape[:axis] + (RNN_W,) + shape[axis + 1:])


def _in_proj_weight(w_in):
    cuts = np.cumsum((D_RNN, D_RNN, Q_W, KV_W, KV_W, KV_W, KV_W, KV_W, KV_W, 3 * N_HEADS, D_MODEL))
    w_in = w_in.astype(BF16)
    (w_u, w_gate, w_q, w_kc, w_vc, w_ks, w_vs, w_kw, w_vw, w_gn, w_ga, w_gb) = jnp.split(w_in, cuts, axis=1)
    per_group = 3 * HEADS_PER_GROUP
    w_gn = jnp.pad(w_gn.reshape(D_MODEL, N_KV_GROUPS, per_group), ((0, 0), (0, 0), (0, LANES - per_group)))
    w_gn = w_gn.reshape(D_MODEL, N_KV_GROUPS * LANES)
    gap = jnp.zeros((D_MODEL, COL_GA - COL_GN - N_KV_GROUPS * LANES), w_in.dtype)
    w = jnp.concatenate([_pad_blocks(w_u, 1), _pad_blocks(w_gate, 1), w_q, w_ks, w_vs, w_kw, w_vw, w_kc, w_vc,
                         w_gn, gap, w_ga, w_gb], axis=1)
    assert w.shape[1] == RNN_COLS + D_PROJ
    return w


def _phi_weights(pe, w1, w2):
    half, g = CMP_BLOCK // 2, LANES // HEAD_DIM
    eye = jnp.eye(g, dtype=BF16)
    w1h = w1.astype(BF16).reshape(2, half, HEAD_DIM, PHI_HIDDEN)
    w1e = jnp.einsum("xldh,gk->xlgdkh", w1h, eye).reshape(2, half * LANES, g * PHI_HIDDEN)
    w2e = jnp.einsum("hd,gk->ghkd", w2.astype(BF16), eye).reshape(g * PHI_HIDDEN, LANES)
    pee = jnp.broadcast_to(pe.reshape(2, half, 1, HEAD_DIM), (2, half, g, HEAD_DIM)).reshape(2, half * LANES)
    return pee, w1e, w2e


def kernel(x, norm_mix, w_in, conv_w, conv_b, gate_a_w, gate_a_b, gate_x_w, gate_x_b, lru_lambda, phi_k_pe, phi_k_w1, phi_k_w2, phi_v_pe, phi_v_w1, phi_v_w2, q_norm, kc_norm, ks_norm, kw_norm, rel_bias, proj_a, proj_b, w_out, norm_mlp, w_mlp_in, w_mlp_out):
    bsz, t, d = x.shape
    assert d == D_MODEL and norm_mix.shape[0] == 1
    tq = 256
    assert t % tq == 0 and t // SEL_BLOCK <= MAX_SBLK and t % CMP_STRIDE == 0
    n_tok = bsz * t
    n_chunks = t // CMP_STRIDE
    assert n_chunks % 8 == 0 and n_chunks <= LANES
    x2 = x.reshape(n_tok, d)

    side_cols = tuple((col + p * LANES, LANES) for col in (COL_KC, COL_VC) for p in range(KV_W // LANES))
    rnn_t, proj_t, *z_groups = _norm_matmul(x2, norm_mix, _in_proj_weight(w_in[0]), bsz,
                                            tm=1024 if t % 1024 == 0 else tq, tn=1024, rnn_cols=RNN_COLS,
                                            side_cols=side_cols)

    pad_w = lambda w: jnp.pad(w, ((0, 0), (0, RNN_BW - LRU_BLOCK_W), (0, RNN_BW - LRU_BLOCK_W))).astype(BF16)
    ya_t = _rglru(rnn_t.reshape(t, bsz, RNN_COLS), _pad_blocks(conv_w[0], 1), _pad_blocks(conv_b, 1),
                  pad_w(gate_a_w[0]), _pad_blocks(gate_a_b.reshape(1, D_RNN), 1),
                  pad_w(gate_x_w[0]), _pad_blocks(gate_x_b.reshape(1, D_RNN), 1),
                  _pad_blocks(lru_lambda, 1), tc=256 if t % 256 == 0 else 128)

    z_pairs = [z.reshape(bsz, n_chunks, CMP_STRIDE * LANES) for z in z_groups]
    n_pairs = KV_W // LANES
    seg128 = jnp.asarray(np.kron(np.eye(LANES // HEAD_DIM), np.ones((HEAD_DIM, HEAD_DIM))), BF16)
    kcn = jnp.tile(kc_norm, (1, LANES // HEAD_DIM))
    kc = _compress(z_pairs[:n_pairs], *_phi_weights(phi_k_pe[0], phi_k_w1[0], phi_k_w2[0]), (kcn, seg128))
    vc = _compress(z_pairs[n_pairs:], *_phi_weights(phi_v_pe[0], phi_v_w1[0], phi_v_w2[0]))

    qh, k_all, v_all = _nsa_prep(proj_t, bsz, jnp.tile(q_norm, (1, N_HEADS)), jnp.tile(ks_norm, (1, N_KV_GROUPS)),
                                 jnp.tile(kw_norm, (1, N_KV_GROUPS)), seg128, tt=1024 if t % 1024 == 0 else tq)

    bucket_of = _t5_bucket_table()
    far = rel_bias[REL_BUCKETS - 1][:, None, None]

    def bias_table(dist, valid, shift):
        buckets = jnp.asarray(bucket_of[np.clip(dist, 0, FAR_DIST - 1)].astype(np.int8))
        onehot = (buckets[None] == jnp.arange(REL_BUCKETS, dtype=jnp.int8).reshape(-1, 1, 1)).astype(F32)
        vals = jnp.einsum("kh,kji->hji", rel_bias, onehot, precision=lax.Precision.HIGHEST)
        if shift:
            vals = vals - far
        return jnp.where(jnp.asarray(valid), vals * LOG2E, MASK_NEG).astype(F32)

    kj = np.arange(tq)[:, None]
    qi_ = np.arange(tq)[None, :]
    bias_diag = bias_table(qi_ - kj, qi_ >= kj, True)
    bias_near = bias_table(tq + qi_ - kj, np.ones((tq, tq), bool), True)
    bias_win = jnp.broadcast_to(jnp.asarray(np.where(kj > qi_, 0.0, MASK_NEG), F32), (N_HEADS, tq, tq))
    bias_tiles = jnp.stack([bias_diag, bias_near, bias_win], axis=1)
    assert WINDOW == 2 * tq
    cidx = np.arange(n_chunks)[:, None]
    dist_c = np.arange(t)[None, :] - (cidx * CMP_STRIDE + CMP_BLOCK - 1)
    bias_c = bias_table(dist_c, (dist_c >= 0) & (cidx < n_chunks - 1), False)
    cstart = np.arange(n_chunks) * CMP_STRIDE
    sj = np.arange(MAX_SBLK)
    cov = ((cstart[None, :] < (sj[:, None] + 1) * SEL_BLOCK) & (cstart[None, :] + CMP_BLOCK - 1 >= sj[:, None] * SEL_BLOCK)
           & (np.arange(n_chunks)[None, :] < n_chunks - 1))
    cover = np.zeros((LANES, n_chunks), np.float32)
    cover[SEL_ROW0:SEL_ROW0 + MAX_SBLK] = cov
    yb = _nsa_attention(qh, kc, vc, k_all, v_all, bias_c, bias_tiles, jnp.asarray(cover, BF16), proj_t, tq)

    pa = jnp.pad(proj_a[0].reshape(LRU_BLOCKS, LRU_BLOCK_W, D_MODEL), ((0, 0), (0, RNN_BW - LRU_BLOCK_W), (0, 0)))
    h = _merge_out(ya_t.reshape(t, bsz * RNN_W), yb.reshape(n_tok, Q_W), proj_t, x2,
                   pa.reshape(RNN_W, D_MODEL).astype(BF16), proj_b[0].astype(BF16), w_out[0].astype(BF16),
                   bsz, tm=1024 if t % 1024 == 0 else tq)

    out = _mlp(h, norm_mlp, w_mlp_in[0].astype(BF16), w_mlp_out[0].astype(BF16),
               tm=1024 if n_tok % 1024 == 0 else tq, tf=1024)
    return out.reshape(bsz, t, d)
```

```python
import functools
import math

import numpy as np
import jax
import jax.numpy as jnp
from jax import lax
from jax.experimental import pallas as pl
from jax.experimental.pallas import tpu as pltpu

F32 = jnp.float32
BF16 = jnp.bfloat16

D_MODEL = 1024
D_RNN = 1344
LRU_BLOCKS = 4
LRU_BLOCK_W = D_RNN // LRU_BLOCKS
CONV_W = 4
LRU_C = 8.0
N_HEADS = 16
HEAD_DIM = 64
N_KV_GROUPS = 4
HEADS_PER_GROUP = N_HEADS // N_KV_GROUPS
CMP_BLOCK = 32
CMP_STRIDE = 16
SEL_BLOCK = 64
N_SELECT = 16
WINDOW = 512
PHI_HIDDEN = 256
SEL_FORCED = 1e4
REL_BUCKETS = 32
REL_MAX_DIST = 128
D_FF = 4 * D_MODEL
NORM_EPS = 1e-6
Q_W = N_HEADS * HEAD_DIM
KV_W = N_KV_GROUPS * HEAD_DIM

LANES = 128
VMEM_LIMIT = 56 * 1024 * 1024

RNN_BW = 384
RNN_W = LRU_BLOCKS * RNN_BW
MASK_NEG = -1e30
SEL_NEG = -1e9
SEL_ROW0 = 64
MAX_SBLK = 32
FAR_DIST = 256
LOG2E = math.log2(math.e)
V_ROWS = HEAD_DIM + 16
BIAS_DIAG, BIAS_NEAR, BIAS_WIN = range(3)
JOBS_PER_TRIP = 2

COL_U = 0
COL_GATE = RNN_W
RNN_COLS = 2 * RNN_W
COL_Q = 0
COL_KS = COL_Q + Q_W
COL_VS = COL_KS + KV_W
COL_KW = COL_VS + KV_W
COL_VW = COL_KW + KV_W
COL_KC = COL_VW + KV_W
COL_VC = COL_KC + KV_W
COL_GN = COL_VC + KV_W
COL_GA = 3072
COL_GB = 4096
D_PROJ = 5120


def _dot(a, b):
    return jnp.dot(a, b, preferred_element_type=F32)


def _gelu_tanh(x):
    return 0.5 * x * (1.0 + jnp.tanh(math.sqrt(2.0 / math.pi) * (x + 0.044715 * (x * x * x))))


def _sigmoid(x):
    return 0.5 * jnp.tanh(0.5 * x) + 0.5


def _seg_sum(x, seg_ones):
    hi = x.astype(BF16)
    lo = (x - hi.astype(F32)).astype(BF16)
    return _dot(hi, seg_ones) + _dot(lo, seg_ones)


def _params(sem, flags=None):
    return pltpu.CompilerParams(dimension_semantics=sem, vmem_limit_bytes=VMEM_LIMIT, flags=flags)


def _norm_matmul_kernel(x_ref, g_ref, w_ref, rnn_ref, o_ref, *rest, tn, rnn_tiles, side_cols):
    side_refs, xn_ref = rest[:-1], rest[-1]
    j = pl.program_id(1)

    @pl.when(j == 0)
    def _():
        x = x_ref[...]
        y = x * lax.rsqrt(jnp.mean(x * x, axis=-1, keepdims=True) + NORM_EPS) * g_ref[...]
        xn_ref[...] = y.astype(BF16)

    res = _dot(xn_ref[...], w_ref[...]).astype(o_ref.dtype)

    @pl.when(j < rnn_tiles)
    def _():
        rnn_ref[...] = res

    @pl.when(j >= rnn_tiles)
    def _():
        o_ref[...] = res

    for (col, width), side_ref in zip(side_cols, side_refs):
        @pl.when(j == rnn_tiles + col // tn)
        def _(col=col, width=width, side_ref=side_ref):
            side_ref[...] = res[:, col % tn:col % tn + width]


def _norm_matmul(x, gain, w, bsz, tm, tn, rnn_cols, side_cols):
    m, k = x.shape
    n = w.shape[1] - rnn_cols
    t = m // bsz
    tiles = t // tm
    rnn_tiles, col_tiles = rnn_cols // tn, n // tn
    assert all(col // tn == (col + width - 1) // tn for col, width in side_cols)
    return pl.pallas_call(
        functools.partial(_norm_matmul_kernel, tn=tn, rnn_tiles=rnn_tiles, side_cols=side_cols),
        grid=(m // tm, rnn_tiles + col_tiles),
        in_specs=[
            pl.BlockSpec((tm, k), lambda i, j: (i, 0)),
            pl.BlockSpec((1, k), lambda i, j: (0, 0)),
            pl.BlockSpec((k, tn), lambda i, j: (0, j)),
        ],
        out_specs=[pl.BlockSpec((tm, tn), lambda i, j: (i % tiles, (i // tiles) * rnn_tiles + jnp.minimum(j, rnn_tiles - 1))),
                   pl.BlockSpec((tm, tn), lambda i, j: (i % tiles, (i // tiles) * col_tiles + jnp.maximum(j - rnn_tiles, 0)))]
                  + [pl.BlockSpec((tm, width), lambda i, j: (i, 0)) for _, width in side_cols],
        out_shape=[jax.ShapeDtypeStruct((t, bsz * rnn_cols), BF16), jax.ShapeDtypeStruct((t, bsz * n), BF16)]
                  + [jax.ShapeDtypeStruct((m, width), BF16) for _, width in side_cols],
        scratch_shapes=[pltpu.VMEM((tm, k), BF16)],
        compiler_params=_params(("parallel", "arbitrary")),
    )(x, gain, w)


def _rglru_kernel(u_ref, ug_ref, cw_ref, cb_ref, wa_ref, ba_ref, wx_ref, bx_ref, lam_ref,
                  y_ref, ubuf, a_scr, b_scr, hcar, *, tc, bsz):
    t = pl.program_id(1)
    rows = tc * bsz
    halo = (CONV_W - 1) * bsz

    @pl.when(t == 0)
    def _():
        ubuf[0:halo, :] = jnp.zeros((halo, RNN_BW), F32)
        hcar[...] = jnp.zeros_like(hcar)

    u = u_ref[...].astype(F32).reshape(rows, RNN_BW)
    ubuf[halo:halo + rows, :] = u
    xc = cb_ref[...]
    for k in range(CONV_W):
        xc = xc + cw_ref[k:k + 1, :] * ubuf[k * bsz:k * bsz + rows, :]
    ubuf[0:halo, :] = u[rows - halo:rows, :]

    xb = xc.astype(BF16)
    r = _sigmoid(_dot(xb, wa_ref[...]) + ba_ref[...])
    i = _sigmoid(_dot(xb, wx_ref[...]) + bx_ref[...])
    z = -lam_ref[...]
    softplus = jnp.maximum(z, 0.0) + jnp.log(1.0 + jnp.exp(-jnp.abs(z)))
    a = jnp.exp(r * ((-LRU_C) * softplus))
    mult = jnp.sqrt(1.0 - a * a)
    row = lax.broadcasted_iota(jnp.int32, (rows, 1), 0)
    mult = jnp.where((row < bsz) & (t == 0), 1.0, mult)
    a_scr[...] = a
    b_scr[...] = mult * (i * xc)

    def step(s, h):
        at = pl.ds(pl.multiple_of(s * bsz, bsz), bsz)
        h = a_scr[at, :] * h + b_scr[at, :]
        b_scr[at, :] = h
        return h

    hcar[...] = lax.fori_loop(0, tc, step, hcar[...], unroll=8)
    y = b_scr[...] * _gelu_tanh(ug_ref[...].astype(F32).reshape(rows, RNN_BW))
    y_ref[...] = y.reshape(tc, bsz, RNN_BW).astype(y_ref.dtype)


def _rglru(proj3, conv_w, conv_b, wa, ba, wx, bx, lam, tc):
    t, bsz, _ = proj3.shape
    assert bsz % 8 == 0, "the recurrence advances whole sublane groups of sequences"
    nb = LRU_BLOCKS
    rows = tc * bsz
    vec = pl.BlockSpec((1, RNN_BW), lambda n, s: (0, n))
    mat = pl.BlockSpec((None, RNN_BW, RNN_BW), lambda n, s: (n, 0, 0))
    return pl.pallas_call(
        functools.partial(_rglru_kernel, tc=tc, bsz=bsz),
        grid=(nb, t // tc),
        in_specs=[
            pl.BlockSpec((tc, bsz, RNN_BW), lambda n, s: (s, 0, COL_U // RNN_BW + n)),
            pl.BlockSpec((tc, bsz, RNN_BW), lambda n, s: (s, 0, COL_GATE // RNN_BW + n)),
            pl.BlockSpec((CONV_W, RNN_BW), lambda n, s: (0, n)),
            vec, mat, vec, mat, vec, vec,
        ],
        out_specs=pl.BlockSpec((tc, bsz, RNN_BW), lambda n, s: (s, 0, n)),
        out_shape=jax.ShapeDtypeStruct((t, bsz, RNN_W), BF16),
        scratch_shapes=[pltpu.VMEM((rows + (CONV_W - 1) * bsz, RNN_BW), F32),
                        pltpu.VMEM((rows, RNN_BW), F32), pltpu.VMEM((rows, RNN_BW), F32),
                        pltpu.VMEM((bsz, RNN_BW), F32)],
        compiler_params=_params(("parallel", "arbitrary")),
    )(proj3, proj3, conv_w, conv_b, wa, ba, wx, bx, lam)


def _compress_kernel(*refs, n_chunks, n_pairs, is_key):
    z_refs, (pe_ref, w1_ref, w2_ref), rest = refs[:n_pairs], refs[n_pairs:n_pairs + 3], refs[n_pairs + 3:]
    norm_ref, seg_ref, o_ref = rest if is_key else (None, None) + rest
    for pair, z_ref in enumerate(z_refs):
        z = z_ref[...].astype(F32)
        first = _dot((z + pe_ref[0:1, :]).astype(BF16), w1_ref[0])
        second = _dot((z + pe_ref[1:2, :]).astype(BF16), w1_ref[1])
        pre = first + pltpu.roll(second, n_chunks - 1, 0)
        out = _dot(_gelu_tanh(pre).astype(BF16), w2_ref[...])
        if is_key:
            ssq = _seg_sum(out * out, seg_ref[...])
            normed = out * lax.rsqrt(ssq * (1.0 / HEAD_DIM) + NORM_EPS) * norm_ref[...]
            low = lax.broadcasted_iota(jnp.int32, (n_chunks, LANES), 1) < HEAD_DIM
            o_ref[2 * pair] = jnp.where(low, normed, 0.0).astype(o_ref.dtype)
            o_ref[2 * pair + 1] = jnp.where(low, pltpu.roll(normed, HEAD_DIM, 1), 0.0).astype(o_ref.dtype)
        else:
            out_t = jnp.transpose(out)
            o_ref[2 * pair] = out_t[0:HEAD_DIM, :].astype(o_ref.dtype)
            o_ref[2 * pair + 1] = out_t[HEAD_DIM:LANES, :].astype(o_ref.dtype)


def _compress(z_pairs, pe, w1, w2, norm_and_seg=None):
    bsz, n_chunks, width = z_pairs[0].shape
    hid = w1.shape[-1]
    is_key = norm_and_seg is not None
    out_block = (N_KV_GROUPS, n_chunks, LANES) if is_key else (N_KV_GROUPS, HEAD_DIM, n_chunks)
    extra = norm_and_seg if is_key else ()
    return pl.pallas_call(
        functools.partial(_compress_kernel, n_chunks=n_chunks, n_pairs=len(z_pairs), is_key=is_key),
        grid=(bsz,),
        in_specs=[pl.BlockSpec((None, n_chunks, width), lambda b: (b, 0, 0))] * len(z_pairs) + [
            pl.BlockSpec((2, width), lambda b: (0, 0)),
            pl.BlockSpec((2, width, hid), lambda b: (0, 0, 0)),
            pl.BlockSpec((hid, LANES), lambda b: (0, 0)),
        ] + [pl.BlockSpec(a.shape, lambda b: (0, 0)) for a in extra],
        out_specs=pl.BlockSpec((None,) + out_block, lambda b: (b, 0, 0, 0)),
        out_shape=jax.ShapeDtypeStruct((bsz,) + out_block, BF16),
        compiler_params=_params(("parallel",)),
    )(*z_pairs, pe, w1, w2, *extra)


def _nsa_prep_kernel(q_ref, ks_ref, vs_ref, kw_ref, vw_ref, qn_ref, ksn_ref, kwn_ref, seg_ref,
                     qo_ref, ko_ref, vo_ref, *, tt):
    t0 = pl.program_id(1) * tt
    lane = lax.broadcasted_iota(jnp.int32, (tt, LANES), 1)
    row = lax.broadcasted_iota(jnp.int32, (tt, LANES), 0) + t0
    low = lane < HEAD_DIM
    seg = seg_ref[...]
    onehot = jnp.where((lane - SEL_ROW0) == row // SEL_BLOCK, 1.0, 0.0)

    def normed(ref, gain_ref, blk, scale):
        x = ref[:, blk * LANES:(blk + 1) * LANES].astype(F32)
        ssq = _seg_sum(x * x, seg)
        y = x * lax.rsqrt(ssq * (1.0 / HEAD_DIM) + NORM_EPS) * gain_ref[:, blk * LANES:(blk + 1) * LANES]
        return y * scale if scale != 1.0 else y

    q_pad = jnp.zeros((LANES - HEAD_DIM, tt), BF16)
    for blk in range(Q_W // LANES):
        y_t = jnp.transpose(normed(q_ref, qn_ref, blk, HEAD_DIM ** -0.5 * LOG2E)).astype(BF16)
        for half in range(2):
            qo_ref[2 * blk + half, 0:HEAD_DIM, :] = y_t[half * HEAD_DIM:(half + 1) * HEAD_DIM, :]
            qo_ref[2 * blk + half, HEAD_DIM:LANES, :] = q_pad
    ones_row = jnp.where(lax.broadcasted_iota(jnp.int32, (V_ROWS - HEAD_DIM, tt), 0) == 0, 1.0, 0.0).astype(BF16)
    for blk in range(KV_W // LANES):
        y = normed(ks_ref, ksn_ref, blk, 1.0)
        ysw = pltpu.roll(y, HEAD_DIM, 1)
        ko_ref[2 * blk, 0] = jnp.where(low, y, onehot).astype(BF16)
        ko_ref[2 * blk + 1, 0] = jnp.where(low, ysw, onehot).astype(BF16)
        y = normed(kw_ref, kwn_ref, blk, 1.0)
        ysw = pltpu.roll(y, HEAD_DIM, 1)
        ko_ref[2 * blk, 1] = jnp.where(low, y, 0.0).astype(BF16)
        ko_ref[2 * blk + 1, 1] = jnp.where(low, ysw, 0.0).astype(BF16)
        for branch, src in enumerate((vs_ref, vw_ref)):
            v_t = jnp.transpose(src[:, blk * LANES:(blk + 1) * LANES].astype(F32))
            for half in range(2):
                g = 2 * blk + half
                vo_ref[g, branch, 0:HEAD_DIM, :] = v_t[half * HEAD_DIM:(half + 1) * HEAD_DIM, :].astype(BF16)
                vo_ref[g, branch, HEAD_DIM:V_ROWS, :] = ones_row


def _nsa_prep(proj_t, bsz, qn, ksn, kwn, seg, tt):
    t = proj_t.shape[0]

    def col(width, offset):
        return pl.BlockSpec((tt, width), lambda b, s: (s, (b * D_PROJ + offset) // width))

    def vec(width):
        return pl.BlockSpec((1, width), lambda b, s: (0, 0))

    g = N_KV_GROUPS
    return pl.pallas_call(
        functools.partial(_nsa_prep_kernel, tt=tt),
        grid=(bsz, t // tt),
        in_specs=[col(Q_W, COL_Q), col(KV_W, COL_KS), col(KV_W, COL_VS), col(KV_W, COL_KW), col(KV_W, COL_VW),
                  vec(Q_W), vec(KV_W), vec(KV_W), pl.BlockSpec((LANES, LANES), lambda b, s: (0, 0))],
        out_specs=[pl.BlockSpec((None, N_HEADS, LANES, tt), lambda b, s: (b, 0, 0, s)),
                   pl.BlockSpec((None, g, 2, tt, LANES), lambda b, s: (b, 0, 0, s, 0)),
                   pl.BlockSpec((None, g, 2, V_ROWS, tt), lambda b, s: (b, 0, 0, 0, s))],
        out_shape=[jax.ShapeDtypeStruct((bsz, N_HEADS, LANES, t), BF16),
                   jax.ShapeDtypeStruct((bsz, g, 2, t, LANES), BF16),
                   jax.ShapeDtypeStruct((bsz, g, 2, V_ROWS, t), BF16)],
        compiler_params=_params(("parallel", "parallel")),
    )(proj_t, proj_t, proj_t, proj_t, proj_t, qn, ksn, kwn, seg)


def _nsa_kernel(q_ref, kc_ref, vc_ref, k_ref, v_ref, bc_ref, bias_ref, cover_ref, gate_ref, o_ref,
                qq_ref, alpha_ref, *scratch, tq, n_q, n_sblk):
    qi = pl.program_id(2)
    t0 = qi * tq
    rg = HEADS_PER_GROUP
    jpt = JOBS_PER_TRIP
    s_refs, smax_refs, p_refs = ((scratch[k:k + jpt], scratch[k + jpt:k + 2 * jpt]) for k in (0, 2 * jpt, 4 * jpt))
    m_refs, acc_refs = scratch[6 * jpt:6 * jpt + rg], scratch[6 * jpt + rg:]
    sel, win = 0, 1


    bias = jnp.concatenate([bc_ref[r] for r in range(rg)], axis=1)
    s = _dot(kc_ref[...], jnp.concatenate([q_ref[r] for r in range(rg)], axis=1)) + bias
    visible = bias > 0.5 * MASK_NEG
    p = jnp.exp2(s - jnp.max(s, axis=0, keepdims=True))
    p = jnp.where(visible, p / jnp.sum(p, axis=0, keepdims=True), 0.0)
    o_cmp_all = _dot(vc_ref[...], p.astype(BF16))
    o_cmp = [o_cmp_all[:, r * tq:(r + 1) * tq] for r in range(rg)]
    p_sum = p[:, 0:tq]
    for r in range(1, rg):
        p_sum = p_sum + p[:, r * tq:(r + 1) * tq]
    p_hi = p_sum.astype(BF16)
    p_lo = (p_sum - p_hi.astype(F32)).astype(BF16)
    imp_t = _dot(cover_ref[...], p_hi) + _dot(cover_ref[...], p_lo)

    score = imp_t[SEL_ROW0:SEL_ROW0 + MAX_SBLK, :]
    jrow = lax.broadcasted_iota(jnp.int32, (MAX_SBLK, tq), 0)
    qblk = (lax.broadcasted_iota(jnp.int32, (MAX_SBLK, tq), 1) + t0) // SEL_BLOCK
    causal = jrow <= qblk
    forced = causal & ((jrow == 0) | (jrow >= qblk - 1))
    score = jnp.where(forced, SEL_FORCED, jnp.where(causal, score, -1.0))
    rank = jnp.zeros((MAX_SBLK, tq), F32)
    for j in range(n_sblk):
        other = score[j:j + 1, :]
        ahead = (other > score) | ((other == score) & (jrow > j))
        rank = rank + jnp.where(ahead, 1.0, 0.0)
    n_top = min(N_SELECT, n_sblk)
    selected = (rank < n_top) & (score >= 0.0)
    neg = jnp.where(selected, 0.0, SEL_NEG).astype(BF16)
    for r in range(rg):
        qq_ref[r, 0:SEL_ROW0, :] = q_ref[r, 0:SEL_ROW0, :]
        qq_ref[r, SEL_ROW0:SEL_ROW0 + MAX_SBLK, :] = neg
        qq_ref[r, SEL_ROW0 + MAX_SBLK:LANES, :] = q_ref[r, SEL_ROW0 + MAX_SBLK:LANES, :]

    near_max = WINDOW // tq + 1
    plan = [(branch, a) for a in range(min(near_max, n_q)) for branch in (sel, win)]
    plan += [(sel, a) for a in range(near_max, n_q)]
    groups = [plan[g:g + JOBS_PER_TRIP] for g in range(0, len(plan), JOBS_PER_TRIP)]
    n_groups = sum((qi >= group[0][1]).astype(jnp.int32) for group in groups)

    def count(i):
        return sum((qi >= a).astype(jnp.int32) for _, a in groups[i])

    def count_options(i):
        return sorted({sum(1 for _, a in groups[i] if a <= top) for _, top in groups[i]})

    def jobs_of(i, c):
        return [(branch, pl.multiple_of((qi - a) * tq, tq),
                 (BIAS_DIAG, BIAS_NEAR, BIAS_WIN if branch == win else None)[min(a, 2)])
                for branch, a in groups[i][:c]]

    def scores(i, c):
        for (branch, k0, kind), s_ref, smax_ref in zip(jobs_of(i, c), s_refs[i % 2], smax_refs[i % 2]):
            k = k_ref[branch, pl.ds(k0, tq), :]
            for r in range(rg):
                q_t = qq_ref[r] if branch == sel else q_ref[r]
                s = _dot(k, q_t)
                if kind is not None:
                    s = s + bias_ref[r, kind]
                s_ref[r] = s
                smax_ref[r] = jnp.max(s, axis=0, keepdims=True)

    def softmax(i, c):
        for r in range(rg):
            for branch in sorted({b for b, _ in groups[i][:c]}):
                mine = [j for j, (b, _) in enumerate(groups[i][:c]) if b == branch]
                m_old = m_refs[r][branch]
                m_new = m_old
                for j in mine:
                    m_new = jnp.maximum(m_new, smax_refs[i % 2][j][r])
                for j in mine:
                    p_refs[i % 2][j][r] = jnp.exp2(s_refs[i % 2][j][r] - m_new).astype(BF16)
                m_refs[r][branch] = m_new
                alpha_ref[i % 2, r, branch] = jnp.exp2(m_old - m_new)

    def values(i, c):
        jobs = jobs_of(i, c)
        v_tiles = [v_ref[branch, :, pl.ds(k0, tq)] for branch, k0, _ in jobs]
        for r in range(rg):
            for branch in sorted({b for b, _, _ in jobs}):
                acc = None if i == 0 else alpha_ref[i % 2, r, branch] * acc_refs[r][branch]
                for j, (b, _, _) in enumerate(jobs):
                    if b == branch:
                        pv = _dot(v_tiles[j], p_refs[i % 2][j][r])
                        acc = pv if acc is None else acc + pv
                acc_refs[r][branch] = acc

    for r in range(rg):
        m_refs[r][...] = jnp.full(m_refs[r].shape, MASK_NEG, F32)
    assert count_options(0) == [len(groups[0])]
    assert sorted(b for b, _ in groups[0]) == [sel, win]
    scores(0, len(groups[0]))
    for i in range(len(groups)):
        full = len(groups[i])
        if i + 1 < len(groups):
            for c_next in count_options(i + 1):
                @pl.when((i + 1 < n_groups) & (count(i + 1) == c_next))
                def _(i=i, full=full, c_next=c_next):
                    if i > 0:
                        values(i - 1, len(groups[i - 1]))
                    softmax(i, full)
                    scores(i + 1, c_next)

        for c in count_options(i):
            @pl.when((i + 1 == n_groups) & (count(i) == c))
            def _(i=i, c=c):
                if i > 0:
                    values(i - 1, len(groups[i - 1]))
                softmax(i, c)
                values(i, c)

    def finish(branch):
        return [acc_refs[r][branch, 0:HEAD_DIM, :] / acc_refs[r][branch, HEAD_DIM:HEAD_DIM + 1, :] for r in range(rg)]

    o_sel = finish(sel)
    o_win = finish(win)

    gates_t = jnp.transpose(_sigmoid(gate_ref[...].astype(F32)))
    outs = []
    for r in range(rg):
        outs.append(gates_t[3 * r:3 * r + 1, :] * o_cmp[r]
                    + gates_t[3 * r + 1:3 * r + 2, :] * o_sel[r]
                    + gates_t[3 * r + 2:3 * r + 3, :] * o_win[r])
    o_ref[...] = jnp.transpose(jnp.concatenate(outs, axis=0)).astype(o_ref.dtype)


def _nsa_attention(qh, kc, vc, k_all, v_all, bias_c, bias_tiles, cover, proj3, tq):
    bsz, _, _, t = qh.shape
    n_chunks = kc.shape[2]
    rg = HEADS_PER_GROUP
    n_kinds = bias_tiles.shape[1]
    return pl.pallas_call(
        functools.partial(_nsa_kernel, tq=tq, n_q=t // tq, n_sblk=t // SEL_BLOCK),
        grid=(N_KV_GROUPS, bsz, t // tq),
        in_specs=[
            pl.BlockSpec((None, rg, LANES, tq), lambda g, b, i: (b, g, 0, i)),
            pl.BlockSpec((None, None, n_chunks, LANES), lambda g, b, i: (b, g, 0, 0)),
            pl.BlockSpec((None, None, HEAD_DIM, n_chunks), lambda g, b, i: (b, g, 0, 0)),
            pl.BlockSpec((None, None, 2, t, LANES), lambda g, b, i: (b, g, 0, 0, 0)),
            pl.BlockSpec((None, None, 2, V_ROWS, t), lambda g, b, i: (b, g, 0, 0, 0)),
            pl.BlockSpec((rg, n_chunks, tq), lambda g, b, i: (g, 0, i)),
            pl.BlockSpec((rg, n_kinds, tq, tq), lambda g, b, i: (g, 0, 0, 0)),
            pl.BlockSpec((LANES, n_chunks), lambda g, b, i: (0, 0)),
            pl.BlockSpec((tq, LANES), lambda g, b, i: (i, (b * D_PROJ + COL_GN) // LANES + g)),
        ],
        out_specs=pl.BlockSpec((None, tq, rg * HEAD_DIM), lambda g, b, i: (b, i, g)),
        out_shape=jax.ShapeDtypeStruct((bsz, t, Q_W), BF16),
        scratch_shapes=[
            pltpu.VMEM((rg, LANES, tq), BF16),
            pltpu.VMEM((2, rg, 2, 1, tq), F32),
        ] + [pltpu.VMEM((rg, tq, tq), F32)] * (2 * JOBS_PER_TRIP)
          + [pltpu.VMEM((rg, 1, tq), F32)] * (2 * JOBS_PER_TRIP)
          + [pltpu.VMEM((rg, tq, tq), BF16)] * (2 * JOBS_PER_TRIP)
          + [pltpu.VMEM((2, 1, tq), F32)] * rg
          + [pltpu.VMEM((2, V_ROWS, tq), F32)] * rg,
        compiler_params=_params(("parallel", "parallel", "arbitrary")),
    )(qh, kc, vc, k_all, v_all, bias_c, bias_tiles, cover, proj3)


def _merge_kernel(ya_ref, yb_ref, ga_ref, gb_ref, x_ref, pa_ref, pb_ref, wo_ref, h_ref):
    merged = (_sigmoid(ga_ref[...].astype(F32)) * _dot(ya_ref[...], pa_ref[...])
              + _sigmoid(gb_ref[...].astype(F32)) * _dot(yb_ref[...], pb_ref[...]))
    h_ref[...] = x_ref[...] + _dot(merged.astype(BF16), wo_ref[...])


def _merge_out(ya_t, yb, proj_t, x, pa, pb, wo, bsz, tm):
    t = ya_t.shape[0]
    tiles = t // tm

    def rows(width):
        return pl.BlockSpec((tm, width), lambda b, s: (b * tiles + s, 0))

    def time_major(width, offset):
        return pl.BlockSpec((tm, width), lambda b, s: (s, (b * D_PROJ + offset) // width))

    def whole(a):
        return pl.BlockSpec(a.shape, lambda b, s: (0, 0))

    return pl.pallas_call(
        _merge_kernel,
        grid=(bsz, tiles),
        in_specs=[pl.BlockSpec((tm, RNN_W), lambda b, s: (s, b)), rows(Q_W), time_major(D_MODEL, COL_GA),
                  time_major(D_MODEL, COL_GB), rows(D_MODEL), whole(pa), whole(pb), whole(wo)],
        out_specs=rows(D_MODEL),
        out_shape=jax.ShapeDtypeStruct((bsz * t, D_MODEL), F32),
        compiler_params=_params(("parallel", "parallel")),
    )(ya_t, yb, proj_t, proj_t, x, pa, pb, wo)


def _mlp_kernel(h_ref, g_ref, w1_ref, w2_ref, o_ref, hn_ref, acc_ref):
    j = pl.program_id(1)

    @pl.when(j == 0)
    def _():
        h = h_ref[...]
        y = h * lax.rsqrt(jnp.mean(h * h, axis=-1, keepdims=True) + NORM_EPS) * g_ref[...]
        hn_ref[...] = y.astype(BF16)
        acc_ref[...] = h

    z = jnp.maximum(_dot(hn_ref[...], w1_ref[...]), 0.0)
    acc_ref[...] += _dot((z * z).astype(BF16), w2_ref[...])

    @pl.when(j == pl.num_programs(1) - 1)
    def _():
        o_ref[...] = acc_ref[...]


def _mlp(h, gain, w1, w2, tm, tf):
    m, d = h.shape
    ff = w1.shape[1]
    return pl.pallas_call(
        _mlp_kernel,
        grid=(m // tm, ff // tf),
        in_specs=[
            pl.BlockSpec((tm, d), lambda i, j: (i, 0)),
            pl.BlockSpec((1, d), lambda i, j: (0, 0)),
            pl.BlockSpec((d, tf), lambda i, j: (0, j)),
            pl.BlockSpec((tf, d), lambda i, j: (j, 0)),
        ],
        out_specs=pl.BlockSpec((tm, d), lambda i, j: (i, 0)),
        out_shape=jax.ShapeDtypeStruct((m, d), F32),
        scratch_shapes=[pltpu.VMEM((tm, d), BF16), pltpu.VMEM((tm, d), F32)],
        compiler_params=_params(("parallel", "arbitrary")),
    )(h, gain, w1, w2)


def _t5_bucket_table():
    max_exact = REL_BUCKETS // 2
    d = np.arange(FAR_DIST)
    df = np.maximum(d.astype(np.float32), np.float32(1.0))
    large = max_exact + (np.log(df / np.float32(max_exact)) / np.float32(math.log(REL_MAX_DIST / max_exact))
                         * np.float32(REL_BUCKETS - max_exact)).astype(np.int32)
    large = np.minimum(large, REL_BUCKETS - 1)
    return np.where(d < max_exact, d, large).astype(np.int32)


def _pad_blocks(w, axis):
    shape = w.shape
    w = w.reshape(shape[:axis] + (LRU_BLOCKS, LRU_BLOCK_W) + shape[axis + 1:])
    pad = [(0, 0)] * w.ndim
    pad[axis + 1] = (0, RNN_BW - LRU_BLOCK_W)
    w = jnp.pad(w, pad)
    return w.reshape(shape[:axis] + (RNN_W,) + shape[axis + 1:])


def _in_proj_weight(w_in):
    cuts = np.cumsum((D_RNN, D_RNN, Q_W, KV_W, KV_W, KV_W, KV_W, KV_W, KV_W, 3 * N_HEADS, D_MODEL))
    w_in = w_in.astype(BF16)
    (w_u, w_gate, w_q, w_kc, w_vc, w_ks, w_vs, w_kw, w_vw, w_gn, w_ga, w_gb) = jnp.split(w_in, cuts, axis=1)
    per_group = 3 * HEADS_PER_GROUP
    w_gn = jnp.pad(w_gn.reshape(D_MODEL, N_KV_GROUPS, per_group), ((0, 0), (0, 0), (0, LANES - per_group)))
    w_gn = w_gn.reshape(D_MODEL, N_KV_GROUPS * LANES)
    gap = jnp.zeros((D_MODEL, COL_GA - COL_GN - N_KV_GROUPS * LANES), w_in.dtype)
    w = jnp.concatenate([_pad_blocks(w_u, 1), _pad_blocks(w_gate, 1), w_q, w_ks, w_vs, w_kw, w_vw, w_kc, w_vc,
                         w_gn, gap, w_ga, w_gb], axis=1)
    assert w.shape[1] == RNN_COLS + D_PROJ
    return w


def _phi_weights(pe, w1, w2):
    half, g = CMP_BLOCK // 2, LANES // HEAD_DIM
    eye = jnp.eye(g, dtype=BF16)
    w1h = w1.astype(BF16).reshape(2, half, HEAD_DIM, PHI_HIDDEN)
    w1e = jnp.einsum("xldh,gk->xlgdkh", w1h, eye).reshape(2, half * LANES, g * PHI_HIDDEN)
    w2e = jnp.einsum("hd,gk->ghkd", w2.astype(BF16), eye).reshape(g * PHI_HIDDEN, LANES)
    pee = jnp.broadcast_to(pe.reshape(2, half, 1, HEAD_DIM), (2, half, g, HEAD_DIM)).reshape(2, half * LANES)
    return pee, w1e, w2e


def kernel(x, norm_mix, w_in, conv_w, conv_b, gate_a_w, gate_a_b, gate_x_w, gate_x_b, lru_lambda, phi_k_pe, phi_k_w1, phi_k_w2, phi_v_pe, phi_v_w1, phi_v_w2, q_norm, kc_norm, ks_norm, kw_norm, rel_bias, proj_a, proj_b, w_out, norm_mlp, w_mlp_in, w_mlp_out):
    bsz, t, d = x.shape
    assert d == D_MODEL and norm_mix.shape[0] == 1
    tq = 256
    assert t % tq == 0 and t // SEL_BLOCK <= MAX_SBLK and t % CMP_STRIDE == 0
    n_tok = bsz * t
    n_chunks = t // CMP_STRIDE
    assert n_chunks % 8 == 0 and n_chunks <= LANES
    x2 = x.reshape(n_tok, d)

    side_cols = tuple((col + p * LANES, LANES) for col in (COL_KC, COL_VC) for p in range(KV_W // LANES))
    rnn_t, proj_t, *z_groups = _norm_matmul(x2, norm_mix, _in_proj_weight(w_in[0]), bsz,
                                            tm=1024 if t % 1024 == 0 else tq, tn=1024, rnn_cols=RNN_COLS,
                                            side_cols=side_cols)

    pad_w = lambda w: jnp.pad(w, ((0, 0), (0, RNN_BW - LRU_BLOCK_W), (0, RNN_BW - LRU_BLOCK_W))).astype(BF16)
    ya_t = _rglru(rnn_t.reshape(t, bsz, RNN_COLS), _pad_blocks(conv_w[0], 1), _pad_blocks(conv_b, 1),
                  pad_w(gate_a_w[0]), _pad_blocks(gate_a_b.reshape(1, D_RNN), 1),
                  pad_w(gate_x_w[0]), _pad_blocks(gate_x_b.reshape(1, D_RNN), 1),
                  _pad_blocks(lru_lambda, 1), tc=256 if t % 256 == 0 else 128)

    z_pairs = [z.reshape(bsz, n_chunks, CMP_STRIDE * LANES) for z in z_groups]
    n_pairs = KV_W // LANES
    seg128 = jnp.asarray(np.kron(np.eye(LANES // HEAD_DIM), np.ones((HEAD_DIM, HEAD_DIM))), BF16)
    kcn = jnp.tile(kc_norm, (1, LANES // HEAD_DIM))
    kc = _compress(z_pairs[:n_pairs], *_phi_weights(phi_k_pe[0], phi_k_w1[0], phi_k_w2[0]), (kcn, seg128))
    vc = _compress(z_pairs[n_pairs:], *_phi_weights(phi_v_pe[0], phi_v_w1[0], phi_v_w2[0]))

    qh, k_all, v_all = _nsa_prep(proj_t, bsz, jnp.tile(q_norm, (1, N_HEADS)), jnp.tile(ks_norm, (1, N_KV_GROUPS)),
                                 jnp.tile(kw_norm, (1, N_KV_GROUPS)), seg128, tt=1024 if t % 1024 == 0 else tq)

    bucket_of = _t5_bucket_table()
    far = rel_bias[REL_BUCKETS - 1][:, None, None]

    def bias_table(dist, valid, shift):
        buckets = jnp.asarray(bucket_of[np.clip(dist, 0, FAR_DIST - 1)].astype(np.int8))
        onehot = (buckets[None] == jnp.arange(REL_BUCKETS, dtype=jnp.int8).reshape(-1, 1, 1)).astype(F32)
        vals = jnp.einsum("kh,kji->hji", rel_bias, onehot, precision=lax.Precision.HIGHEST)
        if shift:
            vals = vals - far
        return jnp.where(jnp.asarray(valid), vals * LOG2E, MASK_NEG).astype(F32)

    kj = np.arange(tq)[:, None]
    qi_ = np.arange(tq)[None, :]
    bias_diag = bias_table(qi_ - kj, qi_ >= kj, True)
    bias_near = bias_table(tq + qi_ - kj, np.ones((tq, tq), bool), True)
    bias_win = jnp.broadcast_to(jnp.asarray(np.where(kj > qi_, 0.0, MASK_NEG), F32), (N_HEADS, tq, tq))
    bias_tiles = jnp.stack([bias_diag, bias_near, bias_win], axis=1)
    assert WINDOW == 2 * tq
    cidx = np.arange(n_chunks)[:, None]
    dist_c = np.arange(t)[None, :] - (cidx * CMP_STRIDE + CMP_BLOCK - 1)
    bias_c = bias_table(dist_c, (dist_c >= 0) & (cidx < n_chunks - 1), False)
    cstart = np.arange(n_chunks) * CMP_STRIDE
    sj = np.arange(MAX_SBLK)
    cov = ((cstart[None, :] < (sj[:, None] + 1) * SEL_BLOCK) & (cstart[None, :] + CMP_BLOCK - 1 >= sj[:, None] * SEL_BLOCK)
           & (np.arange(n_chunks)[None, :] < n_chunks - 1))
    cover = np.zeros((LANES, n_chunks), np.float32)
    cover[SEL_ROW0:SEL_ROW0 + MAX_SBLK] = cov
    yb = _nsa_attention(qh, kc, vc, k_all, v_all, bias_c, bias_tiles, jnp.asarray(cover, BF16), proj_t, tq)

    pa = jnp.pad(proj_a[0].reshape(LRU_BLOCKS, LRU_BLOCK_W, D_MODEL), ((0, 0), (0, RNN_BW - LRU_BLOCK_W), (0, 0)))
    h = _merge_out(ya_t.reshape(t, bsz * RNN_W), yb.reshape(n_tok, Q_W), proj_t, x2,
                   pa.reshape(RNN_W, D_MODEL).astype(BF16), proj_b[0].astype(BF16), w_out[0].astype(BF16),
                   bsz, tm=1024 if t % 1024 == 0 else tq)

    out = _mlp(h, norm_mlp, w_mlp_in[0].astype(BF16), w_mlp_out[0].astype(BF16),
               tm=1024 if n_tok % 1024 == 0 else tq, tf=1024)
    return out.reshape(bsz, t, d)
```
